```python
import math
import jax, jax.numpy as jnp
from jax import lax
import numpy as np

D_MODEL = 1024
BATCH = 8
SEQ = 4096
DEPTH = 2

CTX_LEN = 256
GRID_W = 64

HY_WIDTH = 512
SSM_WIDTH = 512
ATT_WIDTH = 512
D_MIX = HY_WIDTH + SSM_WIDTH + ATT_WIDTH

HY_SHORT = 3
HY_BANDS = 16
HY_EMB = 1 + 2 * HY_BANDS
HY_FILTER_HIDDEN = 64
HY_DECAY_TARGET = 1e-2
HY_FAST_PCT = 0.3
HY_SLOW_PCT = 1.5
HY_FILTER_OUT_SCALE = 0.05

SSM_HEADS = 8
SSM_HEADDIM = 64
SSM_GROUPS = 2
SSM_STATE = 128
SSM_CONV = 3
SSM_CHUNK = 128
SSM_CONV_DIM = SSM_WIDTH + 2 * SSM_GROUPS * SSM_STATE

ATT_HEADS = 8
ATT_KV_HEADS = 2
ATT_HEADDIM = 64
ATT_GROUP = ATT_HEADS // ATT_KV_HEADS
ATT_KV = ATT_KV_HEADS * ATT_HEADDIM
WINDOW = 128
ROPE_BASE = 10000.0

HY_IN = 4 * HY_WIDTH
SSM_IN = SSM_CONV_DIM + SSM_WIDTH + 2 * SSM_HEADS
ATT_IN = ATT_WIDTH + 2 * ATT_KV + ATT_WIDTH
N_IN = HY_IN + SSM_IN + ATT_IN

DEEPNORM_ALPHA = (2 * DEPTH) ** 0.25
DEEPNORM_BETA = (8 * DEPTH) ** -0.25
LN_EPS = 1e-6
RMS_EPS = 1e-5

kernel_name = "hymba_hyena_ssd_swa_prefix_dit"


def _layernorm(x):
    xf = x.astype(jnp.float32)
    mu = jnp.mean(xf, -1, keepdims=True)
    var = jnp.mean(jnp.square(xf - mu), -1, keepdims=True)
    return ((xf - mu) * lax.rsqrt(var + LN_EPS)).astype(x.dtype)


def _centred_dwconv(u, w, b):
    k = w.shape[0]
    pad = k // 2
    L = u.shape[1]
    up = jnp.pad(u, ((0, 0), (pad, pad), (0, 0)))
    out = up[:, 0:L] * w[0]
    for j in range(1, k):
        out = out + up[:, j:j + L] * w[j]
    return out + b


def _hyena_filters(L, w1, b1, w2, b2, w3, b3, freq, w_out):
    t = jnp.linspace(0.0, 1.0, L, dtype=jnp.float32)[:, None]
    w = 2.0 * math.pi * jnp.arange(L, dtype=jnp.float32)[:, None] / L
    f = jnp.linspace(1e-4, HY_BANDS - 1, HY_BANDS, dtype=jnp.float32)[None]
    z = jnp.concatenate([t, jnp.cos(f * w), -jnp.sin(f * w)], -1)
    h = jnp.sin(freq * (z @ w1 + b1))
    h = jnp.sin(freq * (h @ w2 + b2))
    h = jnp.sin(freq * (h @ w3 + b3))
    h = (h @ w_out).reshape(L, 2, 2, HY_WIDTH)
    max_decay = math.log(HY_DECAY_TARGET) / HY_FAST_PCT
    min_decay = math.log(HY_DECAY_TARGET) / HY_SLOW_PCT
    deltas = jnp.linspace(min_decay, max_decay, HY_WIDTH, dtype=jnp.float32)
    window = jnp.exp(-t * jnp.abs(deltas))
    h = h.astype(jnp.float32) * window[:, None, None, :]
    fwd, bwd = h[:, :, 0], h[:, :, 1]
    k_full = jnp.concatenate([fwd.at[0].add(bwd[0]), jnp.zeros_like(fwd[:1]), bwd[1:][::-1]], 0)
    return jnp.fft.rfft(k_full, axis=0)


def _long_conv(u, k_f, bias):
    L = u.shape[1]
    uf = u.astype(jnp.float32)
    y = jnp.fft.irfft(jnp.fft.rfft(uf, n=2 * L, axis=1) * k_f, n=2 * L, axis=1)[:, :L]
    return (y + uf * bias).astype(u.dtype)


def _hyena(u, conv_w, conv_b, filt_f, hy_bias):
    vxx = _centred_dwconv(u[..., :3 * HY_WIDTH], conv_w, conv_b)
    v, x1, x2 = jnp.split(vxx, 3, axis=-1)
    gate = u[..., 3 * HY_WIDTH:]
    y = x1 * _long_conv(v, filt_f[:, 0], hy_bias[0])
    y = x2 * _long_conv(y, filt_f[:, 1], hy_bias[1])
    return y * jax.nn.silu(gate)


def _ssd_scan(x, dt, a, B, C, init):
    b, L, h, p = x.shape
    n = B.shape[-1]
    nc = L // SSM_CHUNK
    f32 = jnp.float32
    xd = (x.astype(f32) * dt[..., None]).reshape(b, nc, SSM_CHUNK, h, p)
    da = (dt * a).reshape(b, nc, SSM_CHUNK, h)
    Bc = B.astype(f32).reshape(b, nc, SSM_CHUNK, h, n)
    Cc = C.astype(f32).reshape(b, nc, SSM_CHUNK, h, n)
    acs = jnp.cumsum(da, axis=2)
    seg = acs[:, :, :, None, :] - acs[:, :, None, :, :]
    lower = jnp.tril(jnp.ones((SSM_CHUNK, SSM_CHUNK), bool))[None, None, :, :, None]
    decay_ls = jnp.exp(jnp.where(lower, seg, -jnp.inf))
    y_diag = jnp.einsum("bclhn,bcshn,bclsh,bcshp->bclhp", Cc, Bc, decay_ls, xd)
    decay_to_end = jnp.exp(acs[:, :, -1:, :] - acs)
    chunk_states = jnp.einsum("bclhn,bclh,bclhp->bchpn", Bc, decay_to_end, xd)
    chunk_decay = jnp.exp(acs[:, :, -1, :])

    def step(s, inp):
        dec, st = inp
        return s * dec[:, :, None, None] + st, s

    final, states_in = lax.scan(step, init.astype(f32),
                                (jnp.moveaxis(chunk_decay, 1, 0), jnp.moveaxis(chunk_states, 1, 0)))
    states_in = jnp.moveaxis(states_in, 0, 1)
    y_off = jnp.einsum("bclhn,bchpn,bclh->bclhp", Cc, states_in, jnp.exp(acs))
    return (y_diag + y_off).reshape(b, L, h, p), final


def _ssd_prep(u, conv_w, conv_b, dt_bias, a_log):
    b, L, _ = u.shape
    gn = SSM_GROUPS * SSM_STATE
    xbc = jax.nn.silu(_centred_dwconv(u[..., :SSM_CONV_DIM], conv_w, conv_b))
    xs = xbc[..., :SSM_WIDTH].reshape(b, L, SSM_HEADS, SSM_HEADDIM)
    rep = SSM_HEADS // SSM_GROUPS
    Bh = jnp.repeat(xbc[..., SSM_WIDTH:SSM_WIDTH + gn].reshape(b, L, SSM_GROUPS, SSM_STATE), rep, axis=2)
    Ch = jnp.repeat(xbc[..., SSM_WIDTH + gn:].reshape(b, L, SSM_GROUPS, SSM_STATE), rep, axis=2)
    gate = u[..., SSM_CONV_DIM:SSM_CONV_DIM + SSM_WIDTH]
    dt_raw = u[..., SSM_CONV_DIM + SSM_WIDTH:].reshape(b, L, 2, SSM_HEADS)
    dt = jax.nn.softplus(dt_raw.astype(jnp.float32) + dt_bias.astype(jnp.float32))
    a = -jnp.exp(a_log.astype(jnp.float32))
    return xs, Bh, Ch, gate, dt, a


def _ssd_bidir(xs, Bh, Ch, dt, a, init_f, init_b):
    flip = lambda t: t[:, ::-1]
    y_f, s_f = _ssd_scan(xs, dt[:, :, 0], a[0], Bh, Ch, init_f)
    y_b, s_b = _ssd_scan(flip(xs), flip(dt[:, :, 1]), a[1], flip(Bh), flip(Ch), init_b)
    return y_f + flip(y_b), s_f, s_b


def _ssd_finish(y, xs, gate, d_skip, norm_w):
    b, L = xs.shape[:2]
    y = (y + xs.astype(jnp.float32) * d_skip[:, None]).reshape(b, L, SSM_WIDTH)
    y = y * jax.nn.silu(gate.astype(jnp.float32))
    yg = y.reshape(b, L, SSM_GROUPS, SSM_WIDTH // SSM_GROUPS)
    yg = yg * lax.rsqrt(jnp.mean(jnp.square(yg), -1, keepdims=True) + RMS_EPS)
    return yg.reshape(b, L, SSM_WIDTH).astype(gate.dtype) * norm_w


def _axial_angles(L):
    rows = L // GRID_W
    row = jnp.broadcast_to(jnp.arange(rows)[:, None], (rows, GRID_W)).reshape(L)
    col = jnp.broadcast_to(jnp.arange(GRID_W)[None, :], (rows, GRID_W)).reshape(L)
    nf = ATT_HEADDIM // 4
    inv = ROPE_BASE ** (-jnp.arange(nf, dtype=jnp.float32) / nf)
    ang = jnp.stack([row, col], -1).astype(jnp.float32)[:, :, None] * inv
    return jnp.cos(ang), jnp.sin(ang)


def _apply_rope(x, cos, sin):
    b, L, h, d = x.shape
    xr = x.astype(jnp.float32).reshape(b, L, h, 2, 2, d // 4)
    x1, x2 = xr[..., 0, :], xr[..., 1, :]
    c = cos[None, :, None]
    s = sin[None, :, None]
    out = jnp.stack([x1 * c - x2 * s, x1 * s + x2 * c], axis=-2)
    return out.reshape(b, L, h, d).astype(x.dtype)


def _attn_split(u):
    b, L, _ = u.shape
    q = u[..., :ATT_WIDTH].reshape(b, L, ATT_HEADS, ATT_HEADDIM)
    k = u[..., ATT_WIDTH:ATT_WIDTH + ATT_KV].reshape(b, L, ATT_KV_HEADS, ATT_HEADDIM)
    v = u[..., ATT_WIDTH + ATT_KV:ATT_WIDTH + 2 * ATT_KV].reshape(b, L, ATT_KV_HEADS, ATT_HEADDIM)
    gate = u[..., ATT_WIDTH + 2 * ATT_KV:]
    return q, k, v, gate


def _ctx_attention(q, k, v, sinks):
    b, L = q.shape[:2]
    scale = ATT_HEADDIM ** -0.5
    qg = q.reshape(b, L, ATT_KV_HEADS, ATT_GROUP, ATT_HEADDIM)
    s = jnp.einsum("bqhgd,bkhd->bhgqk", qg, k).astype(jnp.float32) * scale
    sink = jnp.broadcast_to(sinks.astype(jnp.float32).reshape(ATT_KV_HEADS, ATT_GROUP, 1, 1),
                            (b, ATT_KV_HEADS, ATT_GROUP, L, 1))
    p = jax.nn.softmax(jnp.concatenate([s, sink], -1), axis=-1)[..., :-1]
    o = jnp.einsum("bhgqk,bkhd->bqhgd", p.astype(v.dtype), v)
    return o.reshape(b, L, ATT_WIDTH)


def _window_attention(q, k, v, kc, vc, sinks):
    b, L = q.shape[:2]
    nb = L // WINDOW
    nloc = 3 * WINDOW
    lc = kc.shape[1]
    scale = ATT_HEADDIM ** -0.5
    qb = q.reshape(b, nb, WINDOW, ATT_KV_HEADS, ATT_GROUP, ATT_HEADDIM)
    pad = ((0, 0), (WINDOW, WINDOW), (0, 0), (0, 0))
    kp = jnp.pad(k, pad).reshape(b, nb + 2, WINDOW, ATT_KV_HEADS, ATT_HEADDIM)
    vp = jnp.pad(v, pad).reshape(b, nb + 2, WINDOW, ATT_KV_HEADS, ATT_HEADDIM)
    kband = jnp.concatenate([kp[:, :-2], kp[:, 1:-1], kp[:, 2:]], axis=2)
    vband = jnp.concatenate([vp[:, :-2], vp[:, 1:-1], vp[:, 2:]], axis=2)
    qi = jnp.arange(WINDOW)[:, None]
    kj = jnp.arange(nloc)[None, :] - WINDOW
    kpos = jnp.arange(nb)[:, None, None] * WINDOW + kj[None]
    mask = (jnp.abs(qi - kj) <= WINDOW)[None] & (kpos >= 0) & (kpos < L)
    s_loc = jnp.einsum("bnqhgd,bnkhd->bnhgqk", qb, kband).astype(jnp.float32) * scale
    s_loc = jnp.where(mask[None, :, None, None], s_loc, -jnp.inf)
    s_ctx = jnp.einsum("bnqhgd,bkhd->bnhgqk", qb, kc).astype(jnp.float32) * scale
    sink = jnp.broadcast_to(sinks.astype(jnp.float32).reshape(1, 1, ATT_KV_HEADS, ATT_GROUP, 1, 1),
                            (b, nb, ATT_KV_HEADS, ATT_GROUP, WINDOW, 1))
    p = jax.nn.softmax(jnp.concatenate([s_loc, s_ctx, sink], -1), axis=-1).astype(v.dtype)
    o = (jnp.einsum("bnhgqk,bnkhd->bnqhgd", p[..., :nloc], vband)
         + jnp.einsum("bnhgqk,bkhd->bnqhgd", p[..., nloc:nloc + lc], vc))
    return o.reshape(b, L, ATT_WIDTH)


def _modulate(x, shift, scale):
    return _layernorm(x) * (1.0 + scale) + shift


def setup_inputs(seed: int = 0) -> dict:
    key = jax.random.key(seed)
    ks = jax.random.split(key, 32)
    f32 = jnp.float32

    def nrm(k, shape, s):
        return jax.random.normal(k, shape, f32) * s

    dt0 = jnp.exp(jax.random.uniform(ks[20], (DEPTH, 2, SSM_HEADS), f32, math.log(1e-3), math.log(1e-1)))
    return {
        "x": nrm(ks[0], (BATCH, SEQ, D_MODEL), 1.0),
        "c": nrm(ks[1], (BATCH, D_MODEL), 1.0),
        "ctx": nrm(ks[2], (BATCH, CTX_LEN, D_MODEL), 1.0),
        "c_ctx": nrm(ks[3], (D_MODEL,), 1.0),
        "w_mod": nrm(ks[4], (DEPTH, D_MODEL, 3 * D_MODEL), D_MODEL ** -0.5),
        "b_mod": nrm(ks[5], (DEPTH, 3 * D_MODEL), 0.02),
        "w_in": nrm(ks[6], (DEPTH, D_MODEL, N_IN), D_MODEL ** -0.5),
        "hy_conv_w": nrm(ks[7], (DEPTH, HY_SHORT, 3 * HY_WIDTH), HY_SHORT ** -0.5),
        "hy_conv_b": nrm(ks[8], (DEPTH, 3 * HY_WIDTH), 0.02),
        "hy_f_w1": nrm(ks[9], (DEPTH, HY_EMB, HY_FILTER_HIDDEN), HY_EMB ** -0.5),
        "hy_f_b1": nrm(ks[10], (DEPTH, HY_FILTER_HIDDEN), 0.02),
        "hy_f_w2": nrm(ks[11], (DEPTH, HY_FILTER_HIDDEN, HY_FILTER_HIDDEN), HY_FILTER_HIDDEN ** -0.5),
        "hy_f_b2": nrm(ks[12], (DEPTH, HY_FILTER_HIDDEN), 0.02),
        "hy_f_w3": nrm(ks[13], (DEPTH, HY_FILTER_HIDDEN, HY_FILTER_HIDDEN), HY_FILTER_HIDDEN ** -0.5),
        "hy_f_b3": nrm(ks[14], (DEPTH, HY_FILTER_HIDDEN), 0.02),
        "hy_f_freq": 1.0 + nrm(ks[15], (DEPTH, HY_FILTER_HIDDEN), 0.05),
        "hy_f_wout": nrm(ks[16], (DEPTH, HY_FILTER_HIDDEN, 4 * HY_WIDTH), HY_FILTER_OUT_SCALE * HY_FILTER_HIDDEN ** -0.5),
        "hy_bias": nrm(ks[17], (DEPTH, 2, HY_WIDTH), 0.1),
        "ssm_conv_w": nrm(ks[18], (DEPTH, SSM_CONV, SSM_CONV_DIM), SSM_CONV ** -0.5),
        "ssm_conv_b": nrm(ks[19], (DEPTH, SSM_CONV_DIM), 0.02),
        "ssm_dt_bias": dt0 + jnp.log(-jnp.expm1(-dt0)),
        "ssm_a_log": jnp.log(jax.random.uniform(ks[21], (DEPTH, 2, SSM_HEADS), f32, 1.0, 16.0)),
        "ssm_d": 1.0 + nrm(ks[22], (DEPTH, SSM_HEADS), 0.1),
        "ssm_norm_w": 1.0 + nrm(ks[23], (DEPTH, SSM_WIDTH), 0.05),
        "attn_sinks": nrm(ks[24], (DEPTH, ATT_HEADS), 0.5),
        "w_out": nrm(ks[25], (DEPTH, D_MIX, D_MODEL), DEEPNORM_BETA * D_MIX ** -0.5),
        "ln_g": 1.0 + nrm(ks[26], (DEPTH, D_MODEL), 0.05),
        "ln_b": nrm(ks[27], (DEPTH, D_MODEL), 0.02),
    }


def reference(x, c, ctx, c_ctx, w_mod, b_mod, w_in, hy_conv_w, hy_conv_b, hy_f_w1, hy_f_b1,
              hy_f_w2, hy_f_b2, hy_f_w3, hy_f_b3, hy_f_freq, hy_f_wout, hy_bias, ssm_conv_w,
              ssm_conv_b, ssm_dt_bias, ssm_a_log, ssm_d, ssm_norm_w, attn_sinks, w_out, ln_g, ln_b):
    b, L, _ = x.shape
    lc = ctx.shape[1]
    cos, sin = _axial_angles(L)
    h_lat, h_ctx = x, ctx
    for i in range(DEPTH):
        ctx_needed = i < DEPTH - 1
        mod = jax.nn.silu(c) @ w_mod[i] + b_mod[i]
        mod_c = jax.nn.silu(c_ctx) @ w_mod[i] + b_mod[i]
        sh, sc, g = jnp.split(mod[:, None, :], 3, axis=-1)
        sh_c, sc_c, g_c = jnp.split(mod_c, 3, axis=-1)
        u = _modulate(h_lat, sh, sc) @ w_in[i]
        uc = _modulate(h_ctx, sh_c, sc_c) @ w_in[i]
        u_hy, u_ss, u_at = u[..., :HY_IN], u[..., HY_IN:HY_IN + SSM_IN], u[..., HY_IN + SSM_IN:]
        uc_hy, uc_ss, uc_at = uc[..., :HY_IN], uc[..., HY_IN:HY_IN + SSM_IN], uc[..., HY_IN + SSM_IN:]

        filt_args = (hy_f_w1[i], hy_f_b1[i], hy_f_w2[i], hy_f_b2[i], hy_f_w3[i], hy_f_b3[i],
                     hy_f_freq[i], hy_f_wout[i])
        y_hy = _hyena(u_hy, hy_conv_w[i], hy_conv_b[i], _hyena_filters(L, *filt_args), hy_bias[i])

        xs_c, B_c, C_c, z_c, dt_c, a = _ssd_prep(uc_ss, ssm_conv_w[i], ssm_conv_b[i], ssm_dt_bias[i], ssm_a_log[i])
        zero_state = jnp.zeros((b, SSM_HEADS, SSM_HEADDIM, SSM_STATE), jnp.float32)
        y_ss_c, s_f, s_b = _ssd_bidir(xs_c, B_c, C_c, dt_c, a, zero_state, zero_state)
        xs, Bh, Ch, z_s, dt, a = _ssd_prep(u_ss, ssm_conv_w[i], ssm_conv_b[i], ssm_dt_bias[i], ssm_a_log[i])
        y_ss, _, _ = _ssd_bidir(xs, Bh, Ch, dt, a, s_f, s_b)
        y_ss = _ssd_finish(y_ss, xs, z_s, ssm_d[i], ssm_norm_w[i])

        q, k, v, z_a = _attn_split(u_at)
        q, k = _apply_rope(q, cos, sin), _apply_rope(k, cos, sin)
        qc, kc, vc, z_ac = _attn_split(uc_at)
        y_at = _window_attention(q, k, v, kc, vc, attn_sinks[i]) * jax.nn.silu(z_a)

        out = jnp.concatenate([y_hy, y_ss, y_at], axis=-1) @ w_out[i]
        new_lat = _layernorm(DEEPNORM_ALPHA * h_lat + g * out) * ln_g[i] + ln_b[i]

        if ctx_needed:
            y_hy_c = _hyena(uc_hy, hy_conv_w[i], hy_conv_b[i], _hyena_filters(lc, *filt_args), hy_bias[i])
            y_ss_c = _ssd_finish(y_ss_c, xs_c, z_c, ssm_d[i], ssm_norm_w[i])
            y_at_c = _ctx_attention(qc, kc, vc, attn_sinks[i]) * jax.nn.silu(z_ac)
            out_c = jnp.concatenate([y_hy_c, y_ss_c, y_at_c], axis=-1) @ w_out[i]
            h_ctx = _layernorm(DEEPNORM_ALPHA * h_ctx + g_c * out_c) * ln_g[i] + ln_b[i]
        h_lat = new_lat
    return h_lat
```

```python
import functools
import math

import numpy as np
import jax
import jax.numpy as jnp
from jax import lax
from jax.experimental import pallas as pl
from jax.experimental.pallas import tpu as pltpu

F32 = jnp.float32
BF16 = jnp.bfloat16
HI = lax.Precision.HIGHEST

D_MODEL = 1024
DEPTH = 2
GRID_W = 64
HY_WIDTH = 512
HY_BANDS = 16
HY_EMB = 1 + 2 * HY_BANDS
HY_FILTER_HIDDEN = 64
HY_DECAY_TARGET = 1e-2
HY_FAST_PCT = 0.3
HY_SLOW_PCT = 1.5
SSM_WIDTH = 512
SSM_HEADS = 8
SSM_HEADDIM = 64
SSM_GROUPS = 2
SSM_STATE = 128
SSM_CHUNK = 128
SSM_CONV_DIM = SSM_WIDTH + 2 * SSM_GROUPS * SSM_STATE
ATT_WIDTH = 512
ATT_HEADS = 8
ATT_KV_HEADS = 2
ATT_HEADDIM = 64
ATT_GROUP = ATT_HEADS // ATT_KV_HEADS
ATT_KV = ATT_KV_HEADS * ATT_HEADDIM
WINDOW = 128
ROPE_BASE = 10000.0
HY_IN = 4 * HY_WIDTH
SSM_IN = SSM_CONV_DIM + SSM_WIDTH + 2 * SSM_HEADS
DEEPNORM_ALPHA = (2 * DEPTH) ** 0.25
LN_EPS = 1e-6
RMS_EPS = 1e-5

LANES = 128
VMEM_LIMIT = 56 * 1024 * 1024
NEG = -1e30


def _cp(*sem):
    return pltpu.CompilerParams(dimension_semantics=sem, vmem_limit_bytes=VMEM_LIMIT)


def _silu(x):
    return x / (1.0 + jnp.exp(-x))


def _dot(a, b, precision=None):
    return jnp.dot(a, b, preferred_element_type=F32, precision=precision)


def _dot_t(a, b):
    return lax.dot_general(a, b, (((1,), (1,)), ((), ())), preferred_element_type=F32)


def _mod_body(c_ref, w_ref, b_ref, o_ref):
    o_ref[...] = _dot(_silu(c_ref[...]), w_ref[...], HI) + b_ref[...]


def _modulation(cc, w, b):
    rows, d = cc.shape
    n = w.shape[1]
    tn = 1024
    return pl.pallas_call(
        _mod_body,
        grid=(n // tn,),
        in_specs=[pl.BlockSpec((rows, d), lambda j: (0, 0)),
                  pl.BlockSpec((d, tn), lambda j: (0, j)),
                  pl.BlockSpec((1, tn), lambda j: (0, j))],
        out_specs=pl.BlockSpec((rows, tn), lambda j: (0, j)),
        out_shape=jax.ShapeDtypeStruct((rows, n), F32),
        compiler_params=_cp("arbitrary"),
        name="modulation",
    )(cc, w, b.reshape(1, n))


IN_SEGS = (3 * HY_WIDTH, HY_WIDTH, SSM_CONV_DIM, SSM_WIDTH, ATT_WIDTH, 2 * ATT_KV, ATT_WIDTH, LANES)
IN_CHUNK = 512


def _pack_w_in(w):
    o_ss = HY_IN
    o_at = HY_IN + SSM_IN
    dt = w[:, o_ss + SSM_CONV_DIM + SSM_WIDTH:o_at]
    parts = [w[:, :HY_IN], w[:, o_ss:o_ss + SSM_CONV_DIM + SSM_WIDTH], w[:, o_at:],
             dt, jnp.zeros((w.shape[0], LANES - dt.shape[1]), w.dtype)]
    return jnp.concatenate(parts, axis=1).astype(BF16)


def _inproj_body(h_ref, sh_ref, sc_ref, w_ref, *o_refs):
    x = h_ref[0]
    mu = jnp.mean(x, -1, keepdims=True)
    xc = x - mu
    var = jnp.mean(xc * xc, -1, keepdims=True)
    xm = (xc * lax.rsqrt(var + LN_EPS) * (1.0 + sc_ref[0]) + sh_ref[0]).astype(BF16)
    off = 0
    for o_ref, n in zip(o_refs, IN_SEGS):
        for j in range(0, n, IN_CHUNK):
            w = min(IN_CHUNK, n - j)
            o_ref[0, :, j:j + w] = _dot(xm, w_ref[:, off + j:off + j + w])
        off += n


def _in_projection(h, shift, scale, w_packed):
    b, L, d = h.shape
    tm = 256
    n_all = w_packed.shape[1]
    row = lambda bi, i: (bi, i, 0)
    vec = lambda bi, i: (bi, 0, 0)
    return pl.pallas_call(
        _inproj_body,
        grid=(b, L // tm),
        in_specs=[pl.BlockSpec((1, tm, d), row), pl.BlockSpec((1, 1, d), vec), pl.BlockSpec((1, 1, d), vec),
                  pl.BlockSpec((d, n_all), lambda bi, i: (0, 0))],
        out_specs=[pl.BlockSpec((1, tm, n), row) for n in IN_SEGS],
        out_shape=[jax.ShapeDtypeStruct((b, L, n), F32) for n in IN_SEGS],
        compiler_params=_cp("parallel", "arbitrary"),
        name="in_projection",
    )(h, shift, scale, w_packed)


def _dwconv_body(u_ref, w_ref, b_ref, o_ref, *, act):
    x = u_ref[0]
    L = x.shape[0]
    row = lax.broadcasted_iota(jnp.int32, x.shape, 0)
    prev = jnp.where(row == 0, 0.0, pltpu.roll(x, 1, 0))
    nxt = jnp.where(row == L - 1, 0.0, pltpu.roll(x, L - 1, 0))
    y = prev * w_ref[0:1, :] + x * w_ref[1:2, :] + nxt * w_ref[2:3, :] + b_ref[...]
    if act:
        y = _silu(y)
    o_ref[0, 0] = y


def _dwconv(u, w, bias, *, act, split):
    b, L, c = u.shape
    tc = 256
    per = split // tc
    return pl.pallas_call(
        functools.partial(_dwconv_body, act=act),
        grid=(b, c // tc),
        in_specs=[pl.BlockSpec((1, L, tc), lambda bi, j: (bi, 0, j)),
                  pl.BlockSpec((3, tc), lambda bi, j: (0, j)),
                  pl.BlockSpec((1, tc), lambda bi, j: (0, j))],
        out_specs=pl.BlockSpec((1, 1, L, tc), lambda bi, j: (j // per, bi, 0, j % per)),
        out_shape=jax.ShapeDtypeStruct((c // split, b, L, split), F32),
        compiler_params=_cp("parallel", "arbitrary"),
        name="dwconv",
    )(u, w, bias.reshape(1, c))


def _filter_features(L):
    t = jnp.linspace(0.0, 1.0, L, dtype=F32)[:, None]
    w = 2.0 * math.pi * jnp.arange(L, dtype=F32)[:, None] / L
    f = jnp.linspace(1e-4, HY_BANDS - 1, HY_BANDS, dtype=F32)[None]
    z = jnp.concatenate([t, jnp.cos(f * w), -jnp.sin(f * w)], -1)
    return jnp.pad(z, ((0, 0), (0, LANES - HY_EMB)))


def _pad_to(a, rows, cols):
    return jnp.pad(a, ((0, rows - a.shape[0]), (0, cols - a.shape[1])))


def _filter_body(z_ref, w1_ref, w2_ref, w3_ref, b_ref, fr_ref, wo_ref, ad_ref, o_ref):
    z = z_ref[...]
    fr = fr_ref[...]
    h = jnp.sin(fr * (_dot(z, w1_ref[...], HI) + b_ref[0:1, :]))
    h = jnp.sin(fr * (_dot(h, w2_ref[...], HI) + b_ref[1:2, :]))
    h = jnp.sin(fr * (_dot(h, w3_ref[...], HI) + b_ref[2:3, :]))
    win = jnp.exp(-z[:, 0:1] * ad_ref[...])
    for j in range(4):
        lo = j * HY_WIDTH
        o_ref[:, lo:lo + HY_WIDTH] = _dot(h, wo_ref[:, lo:lo + HY_WIDTH], HI) * win


def _hyena_filter_taps(L, w1, b1, w2, b2, w3, b3, freq, w_out):
    z = _filter_features(L)
    hp = LANES
    bias = jnp.stack([jnp.pad(b, (0, hp - b.shape[0])) for b in (b1, b2, b3)])
    bias = jnp.pad(bias, ((0, 5), (0, 0)))
    fr = jnp.pad(freq, (0, hp - freq.shape[0])).reshape(1, hp)
    max_decay = math.log(HY_DECAY_TARGET) / HY_FAST_PCT
    min_decay = math.log(HY_DECAY_TARGET) / HY_SLOW_PCT
    absd = jnp.abs(jnp.linspace(min_decay, max_decay, HY_WIDTH, dtype=F32)).reshape(1, HY_WIDTH)
    tl = 256
    n = 4 * HY_WIDTH
    full = lambda i: (0, 0)
    return pl.pallas_call(
        _filter_body,
        grid=(L // tl,),
        in_specs=[pl.BlockSpec((tl, hp), lambda i: (i, 0)),
                  pl.BlockSpec((hp, hp), full), pl.BlockSpec((hp, hp), full), pl.BlockSpec((hp, hp), full),
                  pl.BlockSpec((8, hp), full), pl.BlockSpec((1, hp), full),
                  pl.BlockSpec((hp, n), full), pl.BlockSpec((1, HY_WIDTH), full)],
        out_specs=pl.BlockSpec((tl, n), lambda i: (i, 0)),
        out_shape=jax.ShapeDtypeStruct((L, n), F32),
        compiler_params=_cp("arbitrary"),
        name="hyena_filter",
    )(z, _pad_to(w1, hp, hp), _pad_to(w2, hp, hp), _pad_to(w3, hp, hp), bias, fr, _pad_to(w_out, hp, n), absd)


@functools.lru_cache(maxsize=None)
def _dft_tables(L, n2):
    n = 2 * L
    n1 = n // n2
    nh = n1 // 2
    k1n = n1 // 2 + 1
    k1p = -(-k1n // 8) * 8
    a_n1 = np.arange(n1)
    a_k1 = np.arange(k1n)
    th = 2 * np.pi * np.outer(a_k1, a_n1) / n1
    f1 = np.zeros((2 * k1p, n1))
    f1[0:2 * k1n:2] = np.cos(th)
    f1[1:2 * k1n:2] = -np.sin(th)
    a_n2 = np.arange(n2)
    m1 = np.zeros((k1p, 2 * n2, 2 * n2))
    m2 = np.zeros_like(m1)
    for k in range(k1n):
        f = np.exp(-2j * np.pi * (np.outer(a_n2, a_n2) / n2 + a_n2[None, :] * k / n))
        g = np.conj(f).T
        m1[k] = np.block([[f.real, -f.imag], [f.imag, f.real]])
        m2[k] = np.block([[g.real, -g.imag], [g.imag, g.real]])
    ck = np.full(k1n, 2.0)
    ck[0] = 1.0
    ck[-1] = 1.0
    th6 = 2 * np.pi * np.outer(np.arange(nh), a_k1) / n1
    f6 = np.zeros((nh, 2 * k1p))
    f6[:, 0:2 * k1n:2] = ck * np.cos(th6) / n
    f6[:, 1:2 * k1n:2] = -ck * np.sin(th6) / n
    as32 = lambda a: np.asarray(a, np.float32)
    return dict(n1=n1, nh=nh, k1p=k1p, f1=as32(f1), m1=as32(m1), m2=as32(m2), f6=as32(f6))


def _lmat_body(f_ref, x_ref, o_ref, *, precision):
    x = x_ref[0]
    if precision is None:
        x = x.astype(BF16)
    o_ref[0] = _dot(f_ref[...], x, precision)


def _left_matmul(f, x, *, precision, sel=None):
    r, k = f.shape
    bn, cols = x.shape[-3], x.shape[-1]
    tn = min(cols, 8192)
    if sel is None:
        x_spec = pl.BlockSpec((1, k, tn), lambda bi, j: (bi, 0, j))
    else:
        x_spec = pl.BlockSpec((None, 1, k, tn), lambda bi, j: (sel, bi, 0, j))
    if precision is None:
        f = f.astype(BF16)
    return pl.pallas_call(
        functools.partial(_lmat_body, precision=precision),
        grid=(bn, cols // tn),
        in_specs=[pl.BlockSpec((r, k), lambda bi, j: (0, 0)), x_spec],
        out_specs=pl.BlockSpec((1, r, tn), lambda bi, j: (bi, 0, j)),
        out_shape=jax.ShapeDtypeStruct((bn, r, cols), F32),
        compiler_params=_cp("parallel", "arbitrary"),
        name="dft_stage1",
    )(f, x)


def _spectrum_body(m_ref, a_ref, o_ref):
    o_ref[0] = _dot(m_ref[0], a_ref[0], HI)


def _filter_spectrum(m1, a):
    k1p, r, cols = a.shape
    return pl.pallas_call(
        _spectrum_body,
        grid=(k1p,),
        in_specs=[pl.BlockSpec((1, r, r), lambda k: (k, 0, 0)), pl.BlockSpec((1, r, cols), lambda k: (k, 0, 0))],
        out_specs=pl.BlockSpec((1, r, cols), lambda k: (k, 0, 0)),
        out_shape=jax.ShapeDtypeStruct((k1p, r, cols), F32),
        compiler_params=_cp("arbitrary"),
        name="filter_spectrum",
    )(m1, a)


def _mid_body(a_ref, m1_ref, kf_ref, m2_ref, o_ref, *, n2):
    x = _dot(m1_ref[0], a_ref[0, 0].astype(BF16))
    xr, xi = x[:n2], x[n2:]
    kr, ki = kf_ref[0, :n2], kf_ref[0, n2:]
    yr = (xr * kr - xi * ki).astype(BF16)
    yi = (xr * ki + xi * kr).astype(BF16)
    o_ref[0, 0] = _dot(m2_ref[0, :, :n2], yr) + _dot(m2_ref[0, :, n2:], yi)


def _dft_mid(a, m1, kf, m2, conv, n2):
    b, k1p, r, c = a.shape
    return pl.pallas_call(
        functools.partial(_mid_body, n2=n2),
        grid=(b, k1p),
        in_specs=[pl.BlockSpec((1, 1, r, c), lambda bi, k: (bi, k, 0, 0)),
                  pl.BlockSpec((1, r, r), lambda bi, k: (k, 0, 0)),
                  pl.BlockSpec((1, r, c), lambda bi, k: (k, 0, conv)),
                  pl.BlockSpec((1, r, r), lambda bi, k: (k, 0, 0))],
        out_specs=pl.BlockSpec((1, 1, r, c), lambda bi, k: (bi, k, 0, 0)),
        out_shape=jax.ShapeDtypeStruct((b, k1p, r, c), F32),
        compiler_params=_cp("parallel", "arbitrary"),
        name="dft_mid",
    )(a, m1, kf, m2)


def _stage6_body(f_ref, b_ref, xa_ref, xb_ref, bias_ref, *rest, gated):
    o_ref = rest[-1]
    y = _dot(f_ref[...], b_ref[0].astype(BF16))
    out = xa_ref[0] * (y + xb_ref[0] * bias_ref[...])
    if gated:
        out = out * _silu(rest[0][0])
    o_ref[0] = out


def _dft_stage6(f6, bm, xa, xa_sel, xb, xb_sel, bias_t, gate):
    b, r2, cols = bm.shape
    nh = f6.shape[0]
    tn = min(cols, 8192)
    blk = lambda bi, j: (bi, 0, j)
    stacked = lambda s: pl.BlockSpec((None, 1, nh, tn), lambda bi, j: (s, bi, 0, j))
    in_specs = [pl.BlockSpec((nh, r2), lambda bi, j: (0, 0)), pl.BlockSpec((1, r2, tn), blk),
                stacked(xa_sel),
                stacked(xb_sel) if xb_sel is not None else pl.BlockSpec((1, nh, tn), blk),
                pl.BlockSpec((1, tn), lambda bi, j: (0, j))]
    args = [f6.astype(BF16), bm, xa, xb, bias_t]
    if gate is not None:
        in_specs.append(pl.BlockSpec((1, nh, tn), blk))
        args.append(gate)
    return pl.pallas_call(
        functools.partial(_stage6_body, gated=gate is not None),
        grid=(b, cols // tn),
        in_specs=in_specs,
        out_specs=pl.BlockSpec((1, nh, tn), blk),
        out_shape=jax.ShapeDtypeStruct((b, nh, cols), F32),
        compiler_params=_cp("parallel", "arbitrary"),
        name="dft_stage6",
    )(*args)


def _hyena_n2(L):
    return 128 if L >= 2048 else 16


def _hyena_spectrum(L, taps):
    c = HY_WIDTH
    n2 = _hyena_n2(L)
    tb = _dft_tables(L, n2)
    h4 = taps.reshape(L, 2, 2, c)
    fwd, bwd = h4[:, :, 0], h4[:, :, 1]
    k_full = jnp.concatenate([fwd.at[0].add(bwd[0]), jnp.zeros_like(fwd[:1]), bwd[1:][::-1]], 0)
    kx = k_full.reshape(1, tb["n1"], n2 * 2 * c)
    ka = _left_matmul(jnp.asarray(tb["f1"]), kx, precision=HI)
    return _filter_spectrum(jnp.asarray(tb["m1"]), ka.reshape(tb["k1p"], 2 * n2, 2 * c))


def _hyena(vxx, gate, kf, hy_bias, L):
    c = HY_WIDTH
    b = gate.shape[0]
    n2 = _hyena_n2(L)
    tb = _dft_tables(L, n2)
    nh, k1p = tb["nh"], tb["k1p"]
    f1h = jnp.asarray(tb["f1"][:, :nh])
    m1 = jnp.asarray(tb["m1"]).astype(BF16)
    m2 = jnp.asarray(tb["m2"]).astype(BF16)
    f6 = jnp.asarray(tb["f6"])
    vx = vxx.reshape(3, b, nh, n2 * c)
    bias_t = jnp.tile(hy_bias, (1, n2))

    def conv(x, sel, idx):
        a = _left_matmul(f1h, x, precision=None, sel=sel)
        bm = _dft_mid(a.reshape(b, k1p, 2 * n2, c), m1, kf, m2, idx, n2)
        return bm.reshape(b, 2 * k1p, n2 * c)

    y1 = _dft_stage6(f6, conv(vx, 0, 0), vx, 1, vx, 0, bias_t[0:1], None)
    y = _dft_stage6(f6, conv(y1, None, 1), vx, 2, y1, None, bias_t[1:2], gate.reshape(b, nh, n2 * c))
    return y.reshape(b, L, c)


def _ssd_body(xbc_ref, dtr_ref, dtb_ref, a_ref, e_ref, init_ref, y_ref, fin_ref, st_ref, *, reverse, nc, d):
    ci = pl.program_id(1)

    @pl.when(ci == 0)
    def _():
        st_ref[...] = init_ref[0]

    q = SSM_CHUNK
    gw = SSM_WIDTH // SSM_GROUPS
    hpg = SSM_HEADS // SSM_GROUPS
    xbc = xbc_ref[0]
    xs = xbc[:, :SSM_WIDTH]
    raw = dtr_ref[0] + dtb_ref[...]
    dt = jnp.maximum(raw, 0.0) + jnp.log1p(jnp.exp(-jnp.abs(raw)))
    da = dt * a_ref[...]
    li = lax.broadcasted_iota(jnp.int32, (q, q), 0)
    si = lax.broadcasted_iota(jnp.int32, (q, q), 1)
    mask = (li <= si) if reverse else (li >= si)
    acs = _dot(mask.astype(F32), da, HI)
    acs_t = acs.T
    e = e_ref[...]
    dtx = _dot(dt, e, HI)
    acsx = _dot(acs, e, HI)
    last = 0 if reverse else q - 1
    totx = acsx[last:last + 1, :]
    xd = xs * dtx
    xde = (xd * jnp.exp(totx - acsx)).astype(BF16)
    xdb = xd.astype(BF16)
    eacs = jnp.exp(acsx)
    etot = jnp.exp(totx)
    for g in range(SSM_GROUPS):
        bg = xbc[:, SSM_WIDTH + g * SSM_STATE:SSM_WIDTH + (g + 1) * SSM_STATE]
        cg = xbc[:, SSM_WIDTH + (SSM_GROUPS + g) * SSM_STATE:SSM_WIDTH + (SSM_GROUPS + g + 1) * SSM_STATE]
        cgb = cg.astype(BF16)
        gmat = _dot_t(cgb, bg.astype(BF16))
        st = st_ref[g]
        y_off = _dot(cgb, st.astype(BF16)) * eacs[:, g * gw:(g + 1) * gw]
        for j in range(hpg):
            h = g * hpg + j
            col = d * SSM_HEADS + h
            seg = acs[:, col:col + 1] - acs_t[col:col + 1, :]
            lm = jnp.where(mask, jnp.exp(seg), 0.0)
            lo = h * SSM_HEADDIM
            yd = _dot((gmat * lm).astype(BF16), xdb[:, lo:lo + SSM_HEADDIM])
            y_ref[0, :, lo:lo + SSM_HEADDIM] = yd + y_off[:, j * SSM_HEADDIM:(j + 1) * SSM_HEADDIM]
        st_ref[g] = st * etot[:, g * gw:(g + 1) * gw] + _dot(bg.T.astype(BF16), xde[:, g * gw:(g + 1) * gw])

    @pl.when(ci == nc - 1)
    def _():
        fin_ref[0] = st_ref[...]


def _ssd_scan(xbc, dt_raw, dt_bias_row, a_row, expand, init, *, d):
    b, L, _ = xbc.shape
    nc = L // SSM_CHUNK
    reverse = d == 1
    chunk = (lambda bi, c: (bi, nc - 1 - c, 0)) if reverse else (lambda bi, c: (bi, c, 0))
    full2 = lambda bi, c: (0, 0)
    st_shape = (SSM_GROUPS, SSM_STATE, SSM_WIDTH // SSM_GROUPS)
    st_spec = pl.BlockSpec((1,) + st_shape, lambda bi, c: (bi, 0, 0, 0))
    return pl.pallas_call(
        functools.partial(_ssd_body, reverse=reverse, nc=nc, d=d),
        grid=(b, nc),
        in_specs=[pl.BlockSpec((1, SSM_CHUNK, SSM_CONV_DIM), chunk),
                  pl.BlockSpec((1, SSM_CHUNK, LANES), chunk),
                  pl.BlockSpec((1, LANES), full2), pl.BlockSpec((1, LANES), full2),
                  pl.BlockSpec((LANES, SSM_WIDTH), full2), st_spec],
        out_specs=[pl.BlockSpec((1, SSM_CHUNK, SSM_WIDTH), chunk), st_spec],
        out_shape=[jax.ShapeDtypeStruct((b, L, SSM_WIDTH), F32),
                   jax.ShapeDtypeStruct((b,) + st_shape, F32)],
        scratch_shapes=[pltpu.VMEM(st_shape, F32)],
        compiler_params=_cp("parallel", "arbitrary"),
        name="ssd_scan",
    )(xbc, dt_raw, dt_bias_row, a_row, expand, init)


def _ssd_expand(d):
    e = np.zeros((LANES, SSM_WIDTH), np.float32)
    for h in range(SSM_HEADS):
        e[d * SSM_HEADS + h, h * SSM_HEADDIM:(h + 1) * SSM_HEADDIM] = 1.0
    return e


def _rope_tables(L):
    rows = L // GRID_W
    row = jnp.broadcast_to(jnp.arange(rows)[:, None], (rows, GRID_W)).reshape(L)
    col = jnp.broadcast_to(jnp.arange(GRID_W)[None, :], (rows, GRID_W)).reshape(L)
    nf = ATT_HEADDIM // 4
    inv = ROPE_BASE ** (-jnp.arange(nf, dtype=F32) / nf)
    ar = row.astype(F32)[:, None] * inv
    ac = col.astype(F32)[:, None] * inv
    cos = jnp.concatenate([jnp.cos(ar), jnp.cos(ar), jnp.cos(ac), jnp.cos(ac)], -1)
    sin = jnp.concatenate([-jnp.sin(ar), jnp.sin(ar), -jnp.sin(ac), jnp.sin(ac)], -1)
    return jnp.tile(cos, (1, ATT_HEADS)), jnp.tile(sin, (1, ATT_HEADS))


def _rope(x, cos, sin):
    w = x.shape[-1]
    quarter = ATT_HEADDIM // 4
    lane = lax.broadcasted_iota(jnp.int32, x.shape, x.ndim - 1)
    partner = jnp.where((lane // quarter) % 2 == 0, pltpu.roll(x, w - quarter, x.ndim - 1),
                        pltpu.roll(x, quarter, x.ndim - 1))
    return x * cos + partner * sin


def _rope_body(q_ref, kv_ref, cos_ref, sin_ref, qo_ref, ko_ref):
    cos = cos_ref[...]
    sin = sin_ref[...]
    qo_ref[0] = _rope(q_ref[0], cos, sin)
    ko_ref[0] = _rope(kv_ref[0], cos[:, :ATT_KV], sin[:, :ATT_KV])


def _apply_rope(u_q, u_kv, cos, sin):
    b, L, _ = u_q.shape
    tl = 512
    row = lambda i, bi: (bi, i, 0)
    tab = lambda i, bi: (i, 0)
    return pl.pallas_call(
        _rope_body,
        grid=(L // tl, b),
        in_specs=[pl.BlockSpec((1, tl, ATT_WIDTH), row), pl.BlockSpec((1, tl, ATT_KV), row),
                  pl.BlockSpec((tl, ATT_WIDTH), tab), pl.BlockSpec((tl, ATT_WIDTH), tab)],
        out_specs=[pl.BlockSpec((1, tl, ATT_WIDTH), row), pl.BlockSpec((1, tl, ATT_KV), row)],
        out_shape=[jax.ShapeDtypeStruct((b, L, ATT_WIDTH), F32), jax.ShapeDtypeStruct((b, L, ATT_KV), F32)],
        compiler_params=_cp("parallel", "arbitrary"),
        name="rope",
    )(u_q, u_kv, cos, sin)


def _wattn_body(sink_ref, q_ref, kp_ref, kc_ref, kn_ref, vp_ref, vc_ref, vn_ref, kx_ref, vx_ref, z_ref, o_ref, *, nb):
    i = pl.program_id(1)
    w = WINDOW
    scale = ATT_HEADDIM ** -0.5
    row = lax.broadcasted_iota(jnp.int32, (w, w), 0)
    col = lax.broadcasted_iota(jnp.int32, (w, w), 1)
    mask_p = jnp.logical_and(col >= row, i > 0)
    mask_n = jnp.logical_and(col <= row, i < nb - 1)
    q = q_ref[0]
    z = z_ref[0]
    for g in range(ATT_KV_HEADS):
        ks = slice(g * ATT_HEADDIM, (g + 1) * ATT_HEADDIM)
        kp, kc, kn, kx = (r[0, :, ks].astype(BF16) for r in (kp_ref, kc_ref, kn_ref, kx_ref))
        vp, vc, vn, vx = (r[0, :, ks].astype(BF16) for r in (vp_ref, vc_ref, vn_ref, vx_ref))
        for j in range(ATT_GROUP):
            h = g * ATT_GROUP + j
            hs = slice(h * ATT_HEADDIM, (h + 1) * ATT_HEADDIM)
            qh = q[:, hs].astype(BF16)
            sp = jnp.where(mask_p, _dot_t(qh, kp) * scale, NEG)
            sc = _dot_t(qh, kc) * scale
            sn = jnp.where(mask_n, _dot_t(qh, kn) * scale, NEG)
            sx = _dot_t(qh, kx) * scale
            sink = sink_ref[h]
            m = jnp.maximum(jnp.maximum(jnp.max(sp, -1, keepdims=True), jnp.max(sc, -1, keepdims=True)),
                            jnp.maximum(jnp.max(sn, -1, keepdims=True), jnp.max(sx, -1, keepdims=True)))
            m = jnp.maximum(m, sink)
            pp, pc, pn, px = jnp.exp(sp - m), jnp.exp(sc - m), jnp.exp(sn - m), jnp.exp(sx - m)
            den = (jnp.sum(pp, -1, keepdims=True) + jnp.sum(pc, -1, keepdims=True)
                   + jnp.sum(pn, -1, keepdims=True) + jnp.sum(px, -1, keepdims=True) + jnp.exp(sink - m))
            o = (_dot(pp.astype(BF16), vp) + _dot(pc.astype(BF16), vc)
                 + _dot(pn.astype(BF16), vn) + _dot(px.astype(BF16), vx))
            o_ref[0, :, hs] = o / den * _silu(z[:, hs])


def _window_attention(q_rot, k_rot, u_kv, uc_kv, sinks, z_a):
    b, L, _ = q_rot.shape
    lc = uc_kv.shape[1]
    nb = L // WINDOW
    hd2 = ATT_KV
    cur = lambda bi, i: (bi, i, 0)
    prv = lambda bi, i: (bi, jnp.maximum(i - 1, 0), 0)
    nxt = lambda bi, i: (bi, jnp.minimum(i + 1, nb - 1), 0)
    vcur = lambda bi, i: (bi, i, 1)
    vprv = lambda bi, i: (bi, jnp.maximum(i - 1, 0), 1)
    vnxt = lambda bi, i: (bi, jnp.minimum(i + 1, nb - 1), 1)
    kblk = lambda f: pl.BlockSpec((1, WINDOW, hd2), f)
    return pl.pallas_call(
        functools.partial(_wattn_body, nb=nb),
        grid=(b, nb),
        in_specs=[pl.BlockSpec(memory_space=pltpu.SMEM),
                  pl.BlockSpec((1, WINDOW, ATT_WIDTH), cur),
                  kblk(prv), kblk(cur), kblk(nxt), kblk(vprv), kblk(vcur), kblk(vnxt),
                  pl.BlockSpec((1, lc, hd2), lambda bi, i: (bi, 0, 0)),
                  pl.BlockSpec((1, lc, hd2), lambda bi, i: (bi, 0, 1)),
                  pl.BlockSpec((1, WINDOW, ATT_WIDTH), cur)],
        out_specs=pl.BlockSpec((1, WINDOW, ATT_WIDTH), cur),
        out_shape=jax.ShapeDtypeStruct((b, L, ATT_WIDTH), F32),
        compiler_params=_cp("parallel", "arbitrary"),
        name="window_attention",
    )(sinks, q_rot, k_rot, k_rot, k_rot, u_kv, u_kv, u_kv, uc_kv, uc_kv, z_a)


def _cattn_body(sink_ref, q_ref, k_ref, v_ref, z_ref, o_ref):
    scale = ATT_HEADDIM ** -0.5
    q = q_ref[0]
    z = z_ref[0]
    for g in range(ATT_KV_HEADS):
        ks = slice(g * ATT_HEADDIM, (g + 1) * ATT_HEADDIM)
        k = k_ref[0, :, ks].astype(BF16)
        v = v_ref[0, :, ks].astype(BF16)
        for j in range(ATT_GROUP):
            h = g * ATT_GROUP + j
            hs = slice(h * ATT_HEADDIM, (h + 1) * ATT_HEADDIM)
            s = _dot_t(q[:, hs].astype(BF16), k) * scale
            sink = sink_ref[h]
            m = jnp.maximum(jnp.max(s, -1, keepdims=True), sink)
            p = jnp.exp(s - m)
            den = jnp.sum(p, -1, keepdims=True) + jnp.exp(sink - m)
            o_ref[0, :, hs] = _dot(p.astype(BF16), v) / den * _silu(z[:, hs])


def _ctx_attention(uc_q, uc_kv, sinks, z_ac):
    b, lc, _ = uc_q.shape
    blk = lambda bi: (bi, 0, 0)
    return pl.pallas_call(
        _cattn_body,
        grid=(b,),
        in_specs=[pl.BlockSpec(memory_space=pltpu.SMEM),
                  pl.BlockSpec((1, lc, ATT_WIDTH), blk),
                  pl.BlockSpec((1, lc, ATT_KV), lambda bi: (bi, 0, 0)),
                  pl.BlockSpec((1, lc, ATT_KV), lambda bi: (bi, 0, 1)),
                  pl.BlockSpec((1, lc, ATT_WIDTH), blk)],
        out_specs=pl.BlockSpec((1, lc, ATT_WIDTH), blk),
        out_shape=jax.ShapeDtypeStruct((b, lc, ATT_WIDTH), F32),
        compiler_params=_cp("parallel"),
        name="ctx_attention",
    )(sinks, uc_q, uc_kv, uc_kv, z_ac)


def _out_body(h_ref, g_ref, yhy_ref, yf_ref, yb_ref, xs_ref, zs_ref, yat_ref, dsk_ref, nw_ref, w_ref,
              lg_ref, lb_ref, o_ref):
    gw = SSM_WIDTH // SSM_GROUPS
    ys = (yf_ref[0] + yb_ref[0] + xs_ref[0] * dsk_ref[...]) * _silu(zs_ref[0])
    acc = _dot(yhy_ref[0].astype(BF16), w_ref[0:HY_WIDTH, :])
    for g in range(SSM_GROUPS):
        seg = ys[:, g * gw:(g + 1) * gw]
        seg = seg * lax.rsqrt(jnp.mean(seg * seg, -1, keepdims=True) + RMS_EPS) * nw_ref[:, g * gw:(g + 1) * gw]
        lo = HY_WIDTH + g * gw
        acc = acc + _dot(seg.astype(BF16), w_ref[lo:lo + gw, :])
    acc = acc + _dot(yat_ref[0].astype(BF16), w_ref[HY_WIDTH + SSM_WIDTH:, :])
    r = DEEPNORM_ALPHA * h_ref[0] + g_ref[0] * acc
    mu = jnp.mean(r, -1, keepdims=True)
    rc = r - mu
    var = jnp.mean(rc * rc, -1, keepdims=True)
    o_ref[0] = rc * lax.rsqrt(var + LN_EPS) * lg_ref[...] + lb_ref[...]


def _out_projection(h, gate_mod, y_hy, y_f, y_b, xbc, z_s, y_at, d_skip, norm_w, w_out, ln_g, ln_b):
    b, L, d = h.shape
    tm = 256
    row = lambda bi, i: (bi, i, 0)
    vec = lambda bi, i: (bi, 0, 0)
    full = lambda bi, i: (0, 0)
    w512 = pl.BlockSpec((1, tm, SSM_WIDTH), row)
    return pl.pallas_call(
        _out_body,
        grid=(b, L // tm),
        in_specs=[pl.BlockSpec((1, tm, d), row), pl.BlockSpec((1, 1, d), vec),
                  w512, w512, w512, w512, w512, w512,
                  pl.BlockSpec((1, SSM_WIDTH), full), pl.BlockSpec((1, SSM_WIDTH), full),
                  pl.BlockSpec(w_out.shape, full), pl.BlockSpec((1, d), full), pl.BlockSpec((1, d), full)],
        out_specs=pl.BlockSpec((1, tm, d), row),
        out_shape=jax.ShapeDtypeStruct((b, L, d), F32),
        compiler_params=_cp("parallel", "arbitrary"),
        name="out_projection",
    )(h, gate_mod, y_hy, y_f, y_b, xbc, z_s, y_at, d_skip, norm_w, w_out.astype(BF16),
      ln_g.reshape(1, d), ln_b.reshape(1, d))


def _sequence_front(h, shift, scale, w_packed, p):
    u_hy3, u_hyg, u_xbc, u_zs, u_q, u_kv, u_za, u_dt = _in_projection(h, shift, scale, w_packed)
    vxx = _dwconv(u_hy3, p["hy_conv_w"], p["hy_conv_b"], act=False, split=HY_WIDTH)
    xbc = _dwconv(u_xbc, p["ssm_conv_w"], p["ssm_conv_b"], act=True, split=SSM_CONV_DIM)[0]
    return dict(vxx=vxx, hy_gate=u_hyg, xbc=xbc, z_s=u_zs, q=u_q, kv=u_kv, z_a=u_za, dt=u_dt)


def kernel(x, c, ctx, c_ctx, w_mod, b_mod, w_in, hy_conv_w, hy_conv_b, hy_f_w1, hy_f_b1, hy_f_w2, hy_f_b2,
           hy_f_w3, hy_f_b3, hy_f_freq, hy_f_wout, hy_bias, ssm_conv_w, ssm_conv_b, ssm_dt_bias, ssm_a_log,
           ssm_d, ssm_norm_w, attn_sinks, w_out, ln_g, ln_b):
    b, L, d = x.shape
    lc = ctx.shape[1]
    cos, sin = _rope_tables(L)
    cc = jnp.concatenate([c, c_ctx[None], jnp.zeros((16 - b - 1, d), F32)], 0)
    expand = [jnp.asarray(_ssd_expand(0)), jnp.asarray(_ssd_expand(1))]
    zero_state = jnp.zeros((b, SSM_GROUPS, SSM_STATE, SSM_WIDTH // SSM_GROUPS), F32)
    h_lat, h_ctx = x, ctx
    for i in range(DEPTH):
        ctx_needed = i < DEPTH - 1
        p = dict(hy_conv_w=hy_conv_w[i], hy_conv_b=hy_conv_b[i], ssm_conv_w=ssm_conv_w[i], ssm_conv_b=ssm_conv_b[i])
        mod = _modulation(cc, w_mod[i], b_mod[i])
        sh, sc, g = (mod[:b, None, j * d:(j + 1) * d] for j in range(3))
        sh_c, sc_c, g_c = (jnp.broadcast_to(mod[b:b + 1, None, j * d:(j + 1) * d], (b, 1, d)) for j in range(3))
        w_packed = _pack_w_in(w_in[i])
        lat = _sequence_front(h_lat, sh, sc, w_packed, p)
        cx = _sequence_front(h_ctx, sh_c, sc_c, w_packed, p)

        dt_bias_row = jnp.pad(ssm_dt_bias[i].reshape(1, -1), ((0, 0), (0, LANES - 2 * SSM_HEADS)))
        a_row = jnp.pad(-jnp.exp(ssm_a_log[i]).reshape(1, -1), ((0, 0), (0, LANES - 2 * SSM_HEADS)))
        ys_c, ys = [], []
        for dr in range(2):
            y_c, s_c = _ssd_scan(cx["xbc"], cx["dt"], dt_bias_row, a_row, expand[dr], zero_state, d=dr)
            y_l, _ = _ssd_scan(lat["xbc"], lat["dt"], dt_bias_row, a_row, expand[dr], s_c, d=dr)
            ys_c.append(y_c)
            ys.append(y_l)

        filt = (hy_f_w1[i], hy_f_b1[i], hy_f_w2[i], hy_f_b2[i], hy_f_w3[i], hy_f_b3[i], hy_f_freq[i], hy_f_wout[i])
        y_hy = _hyena(lat["vxx"], lat["hy_gate"], _hyena_spectrum(L, _hyena_filter_taps(L, *filt)), hy_bias[i], L)

        q_rot, k_rot = _apply_rope(lat["q"], lat["kv"], cos, sin)
        y_at = _window_attention(q_rot, k_rot, lat["kv"], cx["kv"], attn_sinks[i], lat["z_a"])

        d_skip = jnp.repeat(ssm_d[i], SSM_HEADDIM).reshape(1, SSM_WIDTH)
        norm_w = ssm_norm_w[i].reshape(1, SSM_WIDTH)
        new_lat = _out_projection(h_lat, g, y_hy, ys[0], ys[1], lat["xbc"], lat["z_s"], y_at, d_skip, norm_w,
                                  w_out[i], ln_g[i], ln_b[i])
        if ctx_needed:
            y_hy_c = _hyena(cx["vxx"], cx["hy_gate"], _hyena_spectrum(lc, _hyena_filter_taps(lc, *filt)),
                            hy_bias[i], lc)
            y_at_c = _ctx_attention(cx["q"], cx["kv"], attn_sinks[i], cx["z_a"])
            h_ctx = _out_projection(h_ctx, g_c, y_hy_c, ys_c[0], ys_c[1], cx["xbc"], cx["z_s"], y_at_c, d_skip,
                                    norm_w, w_out[i], ln_g[i], ln_b[i])
        h_lat = new_lat
    return h_lat
```

```python
import functools
import math

import numpy as np
import jax
import jax.numpy as jnp
from jax import lax
from jax.experimental import pallas as pl
from jax.experimental.pallas import tpu as pltpu

F32 = jnp.float32
BF16 = jnp.bfloat16
HI = lax.Precision.HIGHEST

D_MODEL = 1024
DEPTH = 2
GRID_W = 64
HY_WIDTH = 512
HY_BANDS = 16
HY_EMB = 1 + 2 * HY_BANDS
HY_FILTER_HIDDEN = 64
HY_DECAY_TARGET = 1e-2
HY_FAST_PCT = 0.3
HY_SLOW_PCT = 1.5
SSM_WIDTH = 512
SSM_HEADS = 8
SSM_HEADDIM = 64
SSM_GROUPS = 2
SSM_STATE = 128
SSM_CHUNK = 128
SSM_CONV_DIM = SSM_WIDTH + 2 * SSM_GROUPS * SSM_STATE
ATT_WIDTH = 512
ATT_HEADS = 8
ATT_KV_HEADS = 2
ATT_HEADDIM = 64
ATT_GROUP = ATT_HEADS // ATT_KV_HEADS
ATT_KV = ATT_KV_HEADS * ATT_HEADDIM
WINDOW = 128
ROPE_BASE = 10000.0
HY_IN = 4 * HY_WIDTH
SSM_IN = SSM_CONV_DIM + SSM_WIDTH + 2 * SSM_HEADS
DEEPNORM_ALPHA = (2 * DEPTH) ** 0.25
LN_EPS = 1e-6
RMS_EPS = 1e-5

LANES = 128
VMEM_LIMIT = 56 * 1024 * 1024
NEG = -1e30


def _cp(*sem):
    return pltpu.CompilerParams(dimension_semantics=sem, vmem_limit_bytes=VMEM_LIMIT)


def _silu(x):
    return x / (1.0 + jnp.exp(-x))


def _dot(a, b, precision=None):
    return jnp.dot(a, b, preferred_element_type=F32, precision=precision)


def _dot_t(a, b):
    return lax.dot_general(a, b, (((1,), (1,)), ((), ())), preferred_element_type=F32)


def _mod_body(c_ref, w_ref, b_ref, o_ref):
    o_ref[...] = _dot(_silu(c_ref[...]), w_ref[...], HI) + b_ref[...]


def _modulation(cc, w, b):
    rows, d = cc.shape
    n = w.shape[1]
    tn = 1024
    return pl.pallas_call(
        _mod_body,
        grid=(n // tn,),
        in_specs=[pl.BlockSpec((rows, d), lambda j: (0, 0)),
                  pl.BlockSpec((d, tn), lambda j: (0, j)),
                  pl.BlockSpec((1, tn), lambda j: (0, j))],
        out_specs=pl.BlockSpec((rows, tn), lambda j: (0, j)),
        out_shape=jax.ShapeDtypeStruct((rows, n), F32),
        compiler_params=_cp("arbitrary"),
        name="modulation",
    )(cc, w, b.reshape(1, n))


IN_SEGS = (3 * HY_WIDTH, HY_WIDTH, SSM_CONV_DIM, SSM_WIDTH, ATT_WIDTH, 2 * ATT_KV, ATT_WIDTH, LANES)
IN_CHUNK = 512


def _pack_w_in(w):
    o_ss = HY_IN
    o_at = HY_IN + SSM_IN
    dt = w[:, o_ss + SSM_CONV_DIM + SSM_WIDTH:o_at]
    parts = [w[:, :HY_IN], w[:, o_ss:o_ss + SSM_CONV_DIM + SSM_WIDTH], w[:, o_at:],
             dt, jnp.zeros((w.shape[0], LANES - dt.shape[1]), w.dtype)]
    return jnp.concatenate(parts, axis=1).astype(BF16)


def _inproj_body(h_ref, sh_ref, sc_ref, w_ref, *o_refs):
    x = h_ref[0]
    mu = jnp.mean(x, -1, keepdims=True)
    xc = x - mu
    var = jnp.mean(xc * xc, -1, keepdims=True)
    xm = (xc * lax.rsqrt(var + LN_EPS) * (1.0 + sc_ref[0]) + sh_ref[0]).astype(BF16)
    off = 0
    for o_ref, n in zip(o_refs, IN_SEGS):
        for j in range(0, n, IN_CHUNK):
            w = min(IN_CHUNK, n - j)
            o_ref[0, :, j:j + w] = _dot(xm, w_ref[:, off + j:off + j + w])
        off += n


def _in_projection(h, shift, scale, w_packed):
    b, L, d = h.shape
    tm = 256
    n_all = w_packed.shape[1]
    row = lambda bi, i: (bi, i, 0)
    vec = lambda bi, i: (bi, 0, 0)
    return pl.pallas_call(
        _inproj_body,
        grid=(b, L // tm),
        in_specs=[pl.BlockSpec((1, tm, d), row), pl.BlockSpec((1, 1, d), vec), pl.BlockSpec((1, 1, d), vec),
                  pl.BlockSpec((d, n_all), lambda bi, i: (0, 0))],
        out_specs=[pl.BlockSpec((1, tm, n), row) for n in IN_SEGS],
        out_shape=[jax.ShapeDtypeStruct((b, L, n), F32) for n in IN_SEGS],
        compiler_params=_cp("parallel", "arbitrary"),
        name="in_projection",
    )(h, shift, scale, w_packed)


def _dwconv_body(u_ref, w_ref, b_ref, o_ref, *, act):
    x = u_ref[0]
    L = x.shape[0]
    row = lax.broadcasted_iota(jnp.int32, x.shape, 0)
    prev = jnp.where(row == 0, 0.0, pltpu.roll(x, 1, 0))
    nxt = jnp.where(row == L - 1, 0.0, pltpu.roll(x, L - 1, 0))
    y = prev * w_ref[0:1, :] + x * w_ref[1:2, :] + nxt * w_ref[2:3, :] + b_ref[...]
    if act:
        y = _silu(y)
    o_ref[0, 0] = y


def _dwconv(u, w, bias, *, act, split):
    b, L, c = u.shape
    tc = 256
    per = split // tc
    return pl.pallas_call(
        functools.partial(_dwconv_body, act=act),
        grid=(b, c // tc),
        in_specs=[pl.BlockSpec((1, L, tc), lambda bi, j: (bi, 0, j)),
                  pl.BlockSpec((3, tc), lambda bi, j: (0, j)),
                  pl.BlockSpec((1, tc), lambda bi, j: (0, j))],
        out_specs=pl.BlockSpec((1, 1, L, tc), lambda bi, j: (j // per, bi, 0, j % per)),
        out_shape=jax.ShapeDtypeStruct((c // split, b, L, split), F32),
        compiler_params=_cp("parallel", "arbitrary"),
        name="dwconv",
    )(u, w, bias.reshape(1, c))


def _filter_features(L):
    t = jnp.linspace(0.0, 1.0, L, dtype=F32)[:, None]
    w = 2.0 * math.pi * jnp.arange(L, dtype=F32)[:, None] / L
    f = jnp.linspace(1e-4, HY_BANDS - 1, HY_BANDS, dtype=F32)[None]
    z = jnp.concatenate([t, jnp.cos(f * w), -jnp.sin(f * w)], -1)
    return jnp.pad(z, ((0, 0), (0, LANES - HY_EMB)))


def _pad_to(a, rows, cols):
    return jnp.pad(a, ((0, rows - a.shape[0]), (0, cols - a.shape[1])))


def _filter_body(z_ref, w1_ref, w2_ref, w3_ref, b_ref, fr_ref, wo_ref, ad_ref, o_ref):
    z = z_ref[...]
    fr = fr_ref[...]
    h = jnp.sin(fr * (_dot(z, w1_ref[...], HI) + b_ref[0:1, :]))
    h = jnp.sin(fr * (_dot(h, w2_ref[...], HI) + b_ref[1:2, :]))
    h = jnp.sin(fr * (_dot(h, w3_ref[...], HI) + b_ref[2:3, :]))
    win = jnp.exp(-z[:, 0:1] * ad_ref[...])
    for j in range(4):
        lo = j * HY_WIDTH
        o_ref[:, lo:lo + HY_WIDTH] = _dot(h, wo_ref[:, lo:lo + HY_WIDTH], HI) * win


def _hyena_filter_taps(L, w1, b1, w2, b2, w3, b3, freq, w_out):
    z = _filter_features(L)
    hp = LANES
    bias = jnp.stack([jnp.pad(b, (0, hp - b.shape[0])) for b in (b1, b2, b3)])
    bias = jnp.pad(bias, ((0, 5), (0, 0)))
    fr = jnp.pad(freq, (0, hp - freq.shape[0])).reshape(1, hp)
    max_decay = math.log(HY_DECAY_TARGET) / HY_FAST_PCT
    min_decay = math.log(HY_DECAY_TARGET) / HY_SLOW_PCT
    absd = jnp.abs(jnp.linspace(min_decay, max_decay, HY_WIDTH, dtype=F32)).reshape(1, HY_WIDTH)
    tl = 256
    n = 4 * HY_WIDTH
    full = lambda i: (0, 0)
    return pl.pallas_call(
        _filter_body,
        grid=(L // tl,),
        in_specs=[pl.BlockSpec((tl, hp), lambda i: (i, 0)),
                  pl.BlockSpec((hp, hp), full), pl.BlockSpec((hp, hp), full), pl.BlockSpec((hp, hp), full),
                  pl.BlockSpec((8, hp), full), pl.BlockSpec((1, hp), full),
                  pl.BlockSpec((hp, n), full), pl.BlockSpec((1, HY_WIDTH), full)],
        out_specs=pl.BlockSpec((tl, n), lambda i: (i, 0)),
        out_shape=jax.ShapeDtypeStruct((L, n), F32),
        compiler_params=_cp("arbitrary"),
        name="hyena_filter",
    )(z, _pad_to(w1, hp, hp), _pad_to(w2, hp, hp), _pad_to(w3, hp, hp), bias, fr, _pad_to(w_out, hp, n), absd)


def _hy_cfg(L):
    n2 = 128 if L >= 2048 else 16
    n1 = 2 * L // n2
    k1n = n1 // 2 + 1
    jp = -(-2 * k1n // 16) * 16
    pitch = n2 + 8
    return dict(L=L, n2=n2, n1=n1, nh=n1 // 2, k1n=k1n, jp=jp, pitch=pitch)


@functools.lru_cache(maxsize=None)
def _dft_tables(L):
    cfg = _hy_cfg(L)
    n, n1, n2, nh, k1n, jp = 2 * L, cfg["n1"], cfg["n2"], cfg["nh"], cfg["k1n"], cfg["jp"]
    a_n1 = np.arange(n1)
    a_k1 = np.arange(k1n)
    th = 2 * np.pi * np.outer(a_k1, a_n1) / n1
    f1 = np.zeros((jp, n1))
    f1[0:2 * k1n:2] = np.cos(th)
    f1[1:2 * k1n:2] = -np.sin(th)
    a_n2 = np.arange(n2)
    m1 = np.zeros((k1n, 2 * n2, 2 * n2))
    for k in range(k1n):
        f = np.exp(-2j * np.pi * (np.outer(a_n2, a_n2) / n2 + a_n2[None, :] * k / n))
        m1[k] = np.block([[f.real, -f.imag], [f.imag, f.real]])
    ck = np.full(k1n, 2.0)
    ck[0] = 1.0
    ck[-1] = 1.0
    th6 = 2 * np.pi * np.outer(np.arange(nh), a_k1) / n1
    f6 = np.zeros((nh, jp))
    f6[:, 0:2 * k1n:2] = ck * np.cos(th6) / n
    f6[:, 1:2 * k1n:2] = -ck * np.sin(th6) / n
    as32 = lambda a: np.asarray(a, np.float32)
    return dict(f1=as32(f1), m1=as32(m1), m2=as32(np.transpose(m1, (0, 2, 1))), f6=as32(f6))


def _first_stage(src_ref, a_scr, f1, rows, cfg, precision):
    n2, jp, pitch = cfg["n2"], cfg["jp"], cfg["pitch"]

    def step(i, carry):
        xs = jnp.concatenate([src_ref[pl.ds(2 * i, rows, stride=n2), :],
                              src_ref[pl.ds(2 * i + 1, rows, stride=n2), :]], 1)
        if precision is None:
            xs = xs.astype(BF16)
        r = _dot(f1, xs, precision)
        a_scr[pl.ds(2 * i, jp, stride=pitch), :] = r[:, :LANES]
        a_scr[pl.ds(2 * i + 1, jp, stride=pitch), :] = r[:, LANES:]
        return carry

    lax.fori_loop(0, n2 // 2, step, 0, unroll=min(16, n2 // 2))


def _k1_rows(a_scr, k, cfg):
    n2, pitch = cfg["n2"], cfg["pitch"]
    base = pl.multiple_of(2 * k * pitch, 8)
    return base, jnp.concatenate([a_scr[pl.ds(base, n2), :], a_scr[pl.ds(base + pitch, n2), :]], 0)


def _spectrum_body(k_ref, f1_ref, m1_ref, o_ref, a_scr, *, cfg):
    _first_stage(k_ref, a_scr, f1_ref[...], cfg["n1"], cfg, HI)

    def mid(k, carry):
        _, a = _k1_rows(a_scr, k, cfg)
        o_ref[0, k] = _dot(m1_ref[k], a, HI).astype(BF16)
        return carry

    lax.fori_loop(0, cfg["k1n"], mid, 0)


def _hyena_spectrum(L, taps):
    c = HY_WIDTH
    cfg = _hy_cfg(L)
    tb = _dft_tables(L)
    fwd = jnp.concatenate([taps[:, 0:c], taps[:, 2 * c:3 * c]], 1)
    bwd = jnp.concatenate([taps[:, c:2 * c], taps[:, 3 * c:4 * c]], 1)
    k_full = jnp.concatenate([fwd.at[0].add(bwd[0]), jnp.zeros_like(fwd[:1]), bwd[1:][::-1]], 0)
    n2, k1n = cfg["n2"], cfg["k1n"]
    nct = c // LANES
    return pl.pallas_call(
        functools.partial(_spectrum_body, cfg=cfg),
        grid=(2, nct),
        in_specs=[pl.BlockSpec((2 * L, LANES), lambda cv, j: (0, cv * nct + j)),
                  pl.BlockSpec(tb["f1"].shape, lambda cv, j: (0, 0)),
                  pl.BlockSpec(tb["m1"].shape, lambda cv, j: (0, 0, 0))],
        out_specs=pl.BlockSpec((1, k1n, 2 * n2, LANES), lambda cv, j: (cv, 0, 0, j)),
        out_shape=jax.ShapeDtypeStruct((2, k1n, 2 * n2, c), BF16),
        scratch_shapes=[pltpu.VMEM((cfg["jp"] * cfg["pitch"], LANES), F32)],
        compiler_params=_cp("arbitrary", "arbitrary"),
        name="hyena_spectrum",
    )(k_full, jnp.asarray(tb["f1"]), jnp.asarray(tb["m1"]))


def _short_conv(u_ref, w_ref, b_ref, which):
    x = u_ref[0]
    L = x.shape[0]
    row = lax.broadcasted_iota(jnp.int32, x.shape, 0)
    prev = jnp.where(row == 0, 0.0, pltpu.roll(x, 1, 0))
    nxt = jnp.where(row == L - 1, 0.0, pltpu.roll(x, L - 1, 0))
    sel = slice(which, which + 1)
    return prev * w_ref[0, sel, :] + x * w_ref[1, sel, :] + nxt * w_ref[2, sel, :] + b_ref[sel, :]


def _long_conv(src_scr, a_scr, y_scr, w_scr, f1_ref, f6_ref, m1_ref, m2_ref, kf_ref, conv, cfg):
    n2, nh, jp, pitch = cfg["n2"], cfg["nh"], cfg["jp"], cfg["pitch"]
    _first_stage(src_scr, a_scr, f1_ref[...], nh, cfg, None)

    def forward(k, carry):
        _, a = _k1_rows(a_scr, k, cfg)
        x = _dot(m1_ref[k], a.astype(BF16))
        xr, xi = x[:n2], x[n2:]
        kk = kf_ref[conv, k].astype(F32)
        kr, ki = kk[:n2], kk[n2:]
        w_scr[k] = jnp.concatenate([xr * kr - xi * ki, xr * ki + xi * kr], 0).astype(BF16)
        return carry

    def inverse(k, carry):
        base = pl.multiple_of(2 * k * pitch, 8)
        b = _dot(m2_ref[k], w_scr[k])
        a_scr[pl.ds(base, n2), :] = b[:n2]
        a_scr[pl.ds(base + pitch, n2), :] = b[n2:]
        return carry

    lax.fori_loop(0, cfg["k1n"], forward, 0, unroll=3)
    lax.fori_loop(0, cfg["k1n"], inverse, 0, unroll=3)
    f6 = f6_ref[...]

    def last(i, carry):
        bs = jnp.concatenate([a_scr[pl.ds(2 * i, jp, stride=pitch), :],
                              a_scr[pl.ds(2 * i + 1, jp, stride=pitch), :]], 1).astype(BF16)
        y = _dot(f6, bs)
        y_scr[pl.ds(2 * i, nh, stride=n2), :] = y[:, :LANES]
        y_scr[pl.ds(2 * i + 1, nh, stride=n2), :] = y[:, LANES:]
        return carry

    lax.fori_loop(0, n2 // 2, last, 0, unroll=min(16, n2 // 2))


def _hyena_body(v_ref, x1_ref, x2_ref, g_ref, cw_ref, cb_ref, hb_ref, f1_ref, f6_ref, m1_ref, m2_ref, kf_ref,
                o_ref, s_scr, a_scr, y_scr, w_scr, *, cfg):
    tabs = (w_scr, f1_ref, f6_ref, m1_ref, m2_ref, kf_ref)
    s_scr[...] = _short_conv(v_ref, cw_ref, cb_ref, 0)
    _long_conv(s_scr, a_scr, y_scr, *tabs, 0, cfg)
    s_scr[...] = _short_conv(x1_ref, cw_ref, cb_ref, 1) * (y_scr[...] + s_scr[...] * hb_ref[0:1, :])
    _long_conv(s_scr, a_scr, y_scr, *tabs, 1, cfg)
    y = _short_conv(x2_ref, cw_ref, cb_ref, 2) * (y_scr[...] + s_scr[...] * hb_ref[1:2, :])
    o_ref[0] = y * _silu(g_ref[0])


def _hyena(u_hy3, u_gate, conv_w, conv_b, kf, hy_bias):
    b, L, _ = u_gate.shape
    c = HY_WIDTH
    cfg = _hy_cfg(L)
    tb = _dft_tables(L)
    nct = c // LANES
    col = lambda off: pl.BlockSpec((1, L, LANES), lambda j, bi: (bi, 0, off * nct + j))
    full = lambda a: pl.BlockSpec(a.shape, lambda j, bi: (0,) * a.ndim)
    f1 = jnp.asarray(tb["f1"][:, :cfg["nh"]]).astype(BF16)
    f6 = jnp.asarray(tb["f6"]).astype(BF16)
    m1 = jnp.asarray(tb["m1"]).astype(BF16)
    m2 = jnp.asarray(tb["m2"]).astype(BF16)
    return pl.pallas_call(
        functools.partial(_hyena_body, cfg=cfg),
        grid=(nct, b),
        in_specs=[col(0), col(1), col(2), col(0),
                  pl.BlockSpec((3, 3, LANES), lambda j, bi: (0, 0, j)),
                  pl.BlockSpec((3, LANES), lambda j, bi: (0, j)),
                  pl.BlockSpec((2, LANES), lambda j, bi: (0, j)),
                  full(f1), full(f6), full(m1), full(m2),
                  pl.BlockSpec((2, cfg["k1n"], 2 * cfg["n2"], LANES), lambda j, bi: (0, 0, 0, j))],
        out_specs=col(0),
        out_shape=jax.ShapeDtypeStruct((b, L, c), F32),
        scratch_shapes=[pltpu.VMEM((L, LANES), F32),
                        pltpu.VMEM((cfg["jp"] * cfg["pitch"], LANES), F32),
                        pltpu.VMEM((L, LANES), F32),
                        pltpu.VMEM((cfg["k1n"], 2 * cfg["n2"], LANES), BF16)],
        compiler_params=_cp("arbitrary", "arbitrary"),
        name="hyena",
    )(u_hy3, u_hy3, u_hy3, u_gate, conv_w.reshape(3, 3, c), conv_b.reshape(3, c), hy_bias, f1, f6, m1, m2, kf)


def _ssd_body(xbc_ref, dtr_ref, dtb_ref, a_ref, e_ref, init_ref, y_ref, fin_ref, st_ref, *, reverse, nc, d):
    ci = pl.program_id(1)

    @pl.when(ci == 0)
    def _():
        st_ref[...] = init_ref[0]

    q = SSM_CHUNK
    gw = SSM_WIDTH // SSM_GROUPS
    hpg = SSM_HEADS // SSM_GROUPS
    xbc = xbc_ref[0]
    xs = xbc[:, :SSM_WIDTH]
    raw = dtr_ref[0] + dtb_ref[...]
    dt = jnp.maximum(raw, 0.0) + jnp.log1p(jnp.exp(-jnp.abs(raw)))
    da = dt * a_ref[...]
    li = lax.broadcasted_iota(jnp.int32, (q, q), 0)
    si = lax.broadcasted_iota(jnp.int32, (q, q), 1)
    mask = (li <= si) if reverse else (li >= si)
    acs = _dot(mask.astype(F32), da, HI)
    acs_t = acs.T
    e = e_ref[...]
    dtx = _dot(dt, e, HI)
    acsx = _dot(acs, e, HI)
    last = 0 if reverse else q - 1
    totx = acsx[last:last + 1, :]
    xd = xs * dtx
    xde = (xd * jnp.exp(totx - acsx)).astype(BF16)
    xdb = xd.astype(BF16)
    eacs = jnp.exp(acsx)
    etot = jnp.exp(totx)
    for g in range(SSM_GROUPS):
        bg = xbc[:, SSM_WIDTH + g * SSM_STATE:SSM_WIDTH + (g + 1) * SSM_STATE]
        cg = xbc[:, SSM_WIDTH + (SSM_GROUPS + g) * SSM_STATE:SSM_WIDTH + (SSM_GROUPS + g + 1) * SSM_STATE]
        cgb = cg.astype(BF16)
        gmat = _dot_t(cgb, bg.astype(BF16))
        st = st_ref[g]
        y_off = _dot(cgb, st.astype(BF16)) * eacs[:, g * gw:(g + 1) * gw]
        for j in range(hpg):
            h = g * hpg + j
            col = d * SSM_HEADS + h
            seg = acs[:, col:col + 1] - acs_t[col:col + 1, :]
            lm = jnp.where(mask, jnp.exp(seg), 0.0)
            lo = h * SSM_HEADDIM
            yd = _dot((gmat * lm).astype(BF16), xdb[:, lo:lo + SSM_HEADDIM])
            y_ref[0, :, lo:lo + SSM_HEADDIM] = yd + y_off[:, j * SSM_HEADDIM:(j + 1) * SSM_HEADDIM]
        st_ref[g] = st * etot[:, g * gw:(g + 1) * gw] + _dot(bg.T.astype(BF16), xde[:, g * gw:(g + 1) * gw])

    @pl.when(ci == nc - 1)
    def _():
        fin_ref[0] = st_ref[...]


def _ssd_scan(xbc, dt_raw, dt_bias_row, a_row, expand, init, *, d):
    b, L, _ = xbc.shape
    nc = L // SSM_CHUNK
    reverse = d == 1
    chunk = (lambda bi, c: (bi, nc - 1 - c, 0)) if reverse else (lambda bi, c: (bi, c, 0))
    full2 = lambda bi, c: (0, 0)
    st_shape = (SSM_GROUPS, SSM_STATE, SSM_WIDTH // SSM_GROUPS)
    st_spec = pl.BlockSpec((1,) + st_shape, lambda bi, c: (bi, 0, 0, 0))
    return pl.pallas_call(
        functools.partial(_ssd_body, reverse=reverse, nc=nc, d=d),
        grid=(b, nc),
        in_specs=[pl.BlockSpec((1, SSM_CHUNK, SSM_CONV_DIM), chunk),
                  pl.BlockSpec((1, SSM_CHUNK, LANES), chunk),
                  pl.BlockSpec((1, LANES), full2), pl.BlockSpec((1, LANES), full2),
                  pl.BlockSpec((LANES, SSM_WIDTH), full2), st_spec],
        out_specs=[pl.BlockSpec((1, SSM_CHUNK, SSM_WIDTH), chunk), st_spec],
        out_shape=[jax.ShapeDtypeStruct((b, L, SSM_WIDTH), F32),
                   jax.ShapeDtypeStruct((b,) + st_shape, F32)],
        scratch_shapes=[pltpu.VMEM(st_shape, F32)],
        compiler_params=_cp("parallel", "arbitrary"),
        name="ssd_scan",
    )(xbc, dt_raw, dt_bias_row, a_row, expand, init)


def _ssd_expand(d):
    e = np.zeros((LANES, SSM_WIDTH), np.float32)
    for h in range(SSM_HEADS):
        e[d * SSM_HEADS + h, h * SSM_HEADDIM:(h + 1) * SSM_HEADDIM] = 1.0
    return e


def _rope_tables(L):
    rows = L // GRID_W
    row = jnp.broadcast_to(jnp.arange(rows)[:, None], (rows, GRID_W)).reshape(L)
    col = jnp.broadcast_to(jnp.arange(GRID_W)[None, :], (rows, GRID_W)).reshape(L)
    nf = ATT_HEADDIM // 4
    inv = ROPE_BASE ** (-jnp.arange(nf, dtype=F32) / nf)
    ar = row.astype(F32)[:, None] * inv
    ac = col.astype(F32)[:, None] * inv
    cos = jnp.concatenate([jnp.cos(ar), jnp.cos(ar), jnp.cos(ac), jnp.cos(ac)], -1)
    sin = jnp.concatenate([-jnp.sin(ar), jnp.sin(ar), -jnp.sin(ac), jnp.sin(ac)], -1)
    return jnp.tile(cos, (1, ATT_HEADS)), jnp.tile(sin, (1, ATT_HEADS))


def _rope(x, cos, sin):
    w = x.shape[-1]
    quarter = ATT_HEADDIM // 4
    lane = lax.broadcasted_iota(jnp.int32, x.shape, x.ndim - 1)
    partner = jnp.where((lane // quarter) % 2 == 0, pltpu.roll(x, w - quarter, x.ndim - 1),
                        pltpu.roll(x, quarter, x.ndim - 1))
    return x * cos + partner * sin


def _rope_body(q_ref, kv_ref, cos_ref, sin_ref, qo_ref, ko_ref):
    cos = cos_ref[...]
    sin = sin_ref[...]
    qo_ref[0] = _rope(q_ref[0], cos, sin)
    ko_ref[0] = _rope(kv_ref[0], cos[:, :ATT_KV], sin[:, :ATT_KV])


def _apply_rope(u_q, u_kv, cos, sin):
    b, L, _ = u_q.shape
    tl = 512
    row = lambda i, bi: (bi, i, 0)
    tab = lambda i, bi: (i, 0)
    return pl.pallas_call(
        _rope_body,
        grid=(L // tl, b),
        in_specs=[pl.BlockSpec((1, tl, ATT_WIDTH), row), pl.BlockSpec((1, tl, ATT_KV), row),
                  pl.BlockSpec((tl, ATT_WIDTH), tab), pl.BlockSpec((tl, ATT_WIDTH), tab)],
        out_specs=[pl.BlockSpec((1, tl, ATT_WIDTH), row), pl.BlockSpec((1, tl, ATT_KV), row)],
        out_shape=[jax.ShapeDtypeStruct((b, L, ATT_WIDTH), F32), jax.ShapeDtypeStruct((b, L, ATT_KV), F32)],
        compiler_params=_cp("parallel", "arbitrary"),
        name="rope",
    )(u_q, u_kv, cos, sin)


def _wattn_body(sink_ref, q_ref, kp_ref, kc_ref, kn_ref, vp_ref, vc_ref, vn_ref, kx_ref, vx_ref, z_ref, o_ref, *, nb):
    i = pl.program_id(1)
    w = WINDOW
    scale = ATT_HEADDIM ** -0.5
    row = lax.broadcasted_iota(jnp.int32, (w, w), 0)
    col = lax.broadcasted_iota(jnp.int32, (w, w), 1)
    mask_p = jnp.logical_and(col >= row, i > 0)
    mask_n = jnp.logical_and(col <= row, i < nb - 1)
    q = q_ref[0]
    z = z_ref[0]
    for g in range(ATT_KV_HEADS):
        ks = slice(g * ATT_HEADDIM, (g + 1) * ATT_HEADDIM)
        kp, kc, kn, kx = (r[0, :, ks].astype(BF16) for r in (kp_ref, kc_ref, kn_ref, kx_ref))
        vp, vc, vn, vx = (r[0, :, ks].astype(BF16) for r in (vp_ref, vc_ref, vn_ref, vx_ref))
        for j in range(ATT_GROUP):
            h = g * ATT_GROUP + j
            hs = slice(h * ATT_HEADDIM, (h + 1) * ATT_HEADDIM)
            qh = q[:, hs].astype(BF16)
            sp = jnp.where(mask_p, _dot_t(qh, kp) * scale, NEG)
            sc = _dot_t(qh, kc) * scale
            sn = jnp.where(mask_n, _dot_t(qh, kn) * scale, NEG)
            sx = _dot_t(qh, kx) * scale
            sink = sink_ref[h]
            m = jnp.maximum(jnp.maximum(jnp.max(sp, -1, keepdims=True), jnp.max(sc, -1, keepdims=True)),
                            jnp.maximum(jnp.max(sn, -1, keepdims=True), jnp.max(sx, -1, keepdims=True)))
            m = jnp.maximum(m, sink)
            pp, pc, pn, px = jnp.exp(sp - m), jnp.exp(sc - m), jnp.exp(sn - m), jnp.exp(sx - m)
            den = (jnp.sum(pp, -1, keepdims=True) + jnp.sum(pc, -1, keepdims=True)
                   + jnp.sum(pn, -1, keepdims=True) + jnp.sum(px, -1, keepdims=True) + jnp.exp(sink - m))
            o = (_dot(pp.astype(BF16), vp) + _dot(pc.astype(BF16), vc)
                 + _dot(pn.astype(BF16), vn) + _dot(px.astype(BF16), vx))
            o_ref[0, :, hs] = o / den * _silu(z[:, hs])


def _window_attention(q_rot, k_rot, u_kv, uc_kv, sinks, z_a):
    b, L, _ = q_rot.shape
    lc = uc_kv.shape[1]
    nb = L // WINDOW
    hd2 = ATT_KV
    cur = lambda bi, i: (bi, i, 0)
    prv = lambda bi, i: (bi, jnp.maximum(i - 1, 0), 0)
    nxt = lambda bi, i: (bi, jnp.minimum(i + 1, nb - 1), 0)
    vcur = lambda bi, i: (bi, i, 1)
    vprv = lambda bi, i: (bi, jnp.maximum(i - 1, 0), 1)
    vnxt = lambda bi, i: (bi, jnp.minimum(i + 1, nb - 1), 1)
    kblk = lambda f: pl.BlockSpec((1, WINDOW, hd2), f)
    return pl.pallas_call(
        functools.partial(_wattn_body, nb=nb),
        grid=(b, nb),
        in_specs=[pl.BlockSpec(memory_space=pltpu.SMEM),
                  pl.BlockSpec((1, WINDOW, ATT_WIDTH), cur),
                  kblk(prv), kblk(cur), kblk(nxt), kblk(vprv), kblk(vcur), kblk(vnxt),
                  pl.BlockSpec((1, lc, hd2), lambda bi, i: (bi, 0, 0)),
                  pl.BlockSpec((1, lc, hd2), lambda bi, i: (bi, 0, 1)),
                  pl.BlockSpec((1, WINDOW, ATT_WIDTH), cur)],
        out_specs=pl.BlockSpec((1, WINDOW, ATT_WIDTH), cur),
        out_shape=jax.ShapeDtypeStruct((b, L, ATT_WIDTH), F32),
        compiler_params=_cp("parallel", "arbitrary"),
        name="window_attention",
    )(sinks, q_rot, k_rot, k_rot, k_rot, u_kv, u_kv, u_kv, uc_kv, uc_kv, z_a)


def _cattn_body(sink_ref, q_ref, k_ref, v_ref, z_ref, o_ref):
    scale = ATT_HEADDIM ** -0.5
    q = q_ref[0]
    z = z_ref[0]
    for g in range(ATT_KV_HEADS):
        ks = slice(g * ATT_HEADDIM, (g + 1) * ATT_HEADDIM)
        k = k_ref[0, :, ks].astype(BF16)
        v = v_ref[0, :, ks].astype(BF16)
        for j in range(ATT_GROUP):
            h = g * ATT_GROUP + j
            hs = slice(h * ATT_HEADDIM, (h + 1) * ATT_HEADDIM)
            s = _dot_t(q[:, hs].astype(BF16), k) * scale
            sink = sink_ref[h]
            m = jnp.maximum(jnp.max(s, -1, keepdims=True), sink)
            p = jnp.exp(s - m)
            den = jnp.sum(p, -1, keepdims=True) + jnp.exp(sink - m)
            o_ref[0, :, hs] = _dot(p.astype(BF16), v) / den * _silu(z[:, hs])


def _ctx_attention(uc_q, uc_kv, sinks, z_ac):
    b, lc, _ = uc_q.shape
    blk = lambda bi: (bi, 0, 0)
    return pl.pallas_call(
        _cattn_body,
        grid=(b,),
        in_specs=[pl.BlockSpec(memory_space=pltpu.SMEM),
                  pl.BlockSpec((1, lc, ATT_WIDTH), blk),
                  pl.BlockSpec((1, lc, ATT_KV), lambda bi: (bi, 0, 0)),
                  pl.BlockSpec((1, lc, ATT_KV), lambda bi: (bi, 0, 1)),
                  pl.BlockSpec((1, lc, ATT_WIDTH), blk)],
        out_specs=pl.BlockSpec((1, lc, ATT_WIDTH), blk),
        out_shape=jax.ShapeDtypeStruct((b, lc, ATT_WIDTH), F32),
        compiler_params=_cp("parallel"),
        name="ctx_attention",
    )(sinks, uc_q, uc_kv, uc_kv, z_ac)


def _out_body(h_ref, g_ref, yhy_ref, yf_ref, yb_ref, xs_ref, zs_ref, yat_ref, dsk_ref, nw_ref, w_ref,
              lg_ref, lb_ref, o_ref):
    gw = SSM_WIDTH // SSM_GROUPS
    ys = (yf_ref[0] + yb_ref[0] + xs_ref[0] * dsk_ref[...]) * _silu(zs_ref[0])
    acc = _dot(yhy_ref[0].astype(BF16), w_ref[0:HY_WIDTH, :])
    for g in range(SSM_GROUPS):
        seg = ys[:, g * gw:(g + 1) * gw]
        seg = seg * lax.rsqrt(jnp.mean(seg * seg, -1, keepdims=True) + RMS_EPS) * nw_ref[:, g * gw:(g + 1) * gw]
        lo = HY_WIDTH + g * gw
        acc = acc + _dot(seg.astype(BF16), w_ref[lo:lo + gw, :])
    acc = acc + _dot(yat_ref[0].astype(BF16), w_ref[HY_WIDTH + SSM_WIDTH:, :])
    r = DEEPNORM_ALPHA * h_ref[0] + g_ref[0] * acc
    mu = jnp.mean(r, -1, keepdims=True)
    rc = r - mu
    var = jnp.mean(rc * rc, -1, keepdims=True)
    o_ref[0] = rc * lax.rsqrt(var + LN_EPS) * lg_ref[...] + lb_ref[...]


def _out_projection(h, gate_mod, y_hy, y_f, y_b, xbc, z_s, y_at, d_skip, norm_w, w_out, ln_g, ln_b):
    b, L, d = h.shape
    tm = 256
    row = lambda bi, i: (bi, i, 0)
    vec = lambda bi, i: (bi, 0, 0)
    full = lambda bi, i: (0, 0)
    w512 = pl.BlockSpec((1, tm, SSM_WIDTH), row)
    return pl.pallas_call(
        _out_body,
        grid=(b, L // tm),
        in_specs=[pl.BlockSpec((1, tm, d), row), pl.BlockSpec((1, 1, d), vec),
                  w512, w512, w512, w512, w512, w512,
                  pl.BlockSpec((1, SSM_WIDTH), full), pl.BlockSpec((1, SSM_WIDTH), full),
                  pl.BlockSpec(w_out.shape, full), pl.BlockSpec((1, d), full), pl.BlockSpec((1, d), full)],
        out_specs=pl.BlockSpec((1, tm, d), row),
        out_shape=jax.ShapeDtypeStruct((b, L, d), F32),
        compiler_params=_cp("parallel", "arbitrary"),
        name="out_projection",
    )(h, gate_mod, y_hy, y_f, y_b, xbc, z_s, y_at, d_skip, norm_w, w_out.astype(BF16),
      ln_g.reshape(1, d), ln_b.reshape(1, d))


def _sequence_front(h, shift, scale, w_packed, ssm_conv_w, ssm_conv_b):
    u_hy3, u_hyg, u_xbc, u_zs, u_q, u_kv, u_za, u_dt = _in_projection(h, shift, scale, w_packed)
    xbc = _dwconv(u_xbc, ssm_conv_w, ssm_conv_b, act=True, split=SSM_CONV_DIM)[0]
    return dict(hy3=u_hy3, hy_gate=u_hyg, xbc=xbc, z_s=u_zs, q=u_q, kv=u_kv, z_a=u_za, dt=u_dt)


def kernel(x, c, ctx, c_ctx, w_mod, b_mod, w_in, hy_conv_w, hy_conv_b, hy_f_w1, hy_f_b1, hy_f_w2, hy_f_b2,
           hy_f_w3, hy_f_b3, hy_f_freq, hy_f_wout, hy_bias, ssm_conv_w, ssm_conv_b, ssm_dt_bias, ssm_a_log,
           ssm_d, ssm_norm_w, attn_sinks, w_out, ln_g, ln_b):
    b, L, d = x.shape
    lc = ctx.shape[1]
    cos, sin = _rope_tables(L)
    cc = jnp.concatenate([c, c_ctx[None], jnp.zeros((16 - b - 1, d), F32)], 0)
    expand = [jnp.asarray(_ssd_expand(0)), jnp.asarray(_ssd_expand(1))]
    zero_state = jnp.zeros((b, SSM_GROUPS, SSM_STATE, SSM_WIDTH // SSM_GROUPS), F32)
    h_lat, h_ctx = x, ctx
    for i in range(DEPTH):
        ctx_needed = i < DEPTH - 1
        mod = _modulation(cc, w_mod[i], b_mod[i])
        sh, sc, g = (mod[:b, None, j * d:(j + 1) * d] for j in range(3))
        sh_c, sc_c, g_c = (jnp.broadcast_to(mod[b:b + 1, None, j * d:(j + 1) * d], (b, 1, d)) for j in range(3))
        w_packed = _pack_w_in(w_in[i])
        lat = _sequence_front(h_lat, sh, sc, w_packed, ssm_conv_w[i], ssm_conv_b[i])
        cx = _sequence_front(h_ctx, sh_c, sc_c, w_packed, ssm_conv_w[i], ssm_conv_b[i])

        dt_bias_row = jnp.pad(ssm_dt_bias[i].reshape(1, -1), ((0, 0), (0, LANES - 2 * SSM_HEADS)))
        a_row = jnp.pad(-jnp.exp(ssm_a_log[i]).reshape(1, -1), ((0, 0), (0, LANES - 2 * SSM_HEADS)))
        ys_c, ys = [], []
        for dr in range(2):
            y_c, s_c = _ssd_scan(cx["xbc"], cx["dt"], dt_bias_row, a_row, expand[dr], zero_state, d=dr)
            y_l, _ = _ssd_scan(lat["xbc"], lat["dt"], dt_bias_row, a_row, expand[dr], s_c, d=dr)
            ys_c.append(y_c)
            ys.append(y_l)

        filt = (hy_f_w1[i], hy_f_b1[i], hy_f_w2[i], hy_f_b2[i], hy_f_w3[i], hy_f_b3[i], hy_f_freq[i], hy_f_wout[i])
        kf = _hyena_spectrum(L, _hyena_filter_taps(L, *filt))
        y_hy = _hyena(lat["hy3"], lat["hy_gate"], hy_conv_w[i], hy_conv_b[i], kf, hy_bias[i])

        q_rot, k_rot = _apply_rope(lat["q"], lat["kv"], cos, sin)
        y_at = _window_attention(q_rot, k_rot, lat["kv"], cx["kv"], attn_sinks[i], lat["z_a"])

        d_skip = jnp.repeat(ssm_d[i], SSM_HEADDIM).reshape(1, SSM_WIDTH)
        norm_w = ssm_norm_w[i].reshape(1, SSM_WIDTH)
        new_lat = _out_projection(h_lat, g, y_hy, ys[0], ys[1], lat["xbc"], lat["z_s"], y_at, d_skip, norm_w,
                                  w_out[i], ln_g[i], ln_b[i])
        if ctx_needed:
            kf_c = _hyena_spectrum(lc, _hyena_filter_taps(lc, *filt))
            y_hy_c = _hyena(cx["hy3"], cx["hy_gate"], hy_conv_w[i], hy_conv_b[i], kf_c, hy_bias[i])
            y_at_c = _ctx_attention(cx["q"], cx["kv"], attn_sinks[i], cx["z_a"])
            h_ctx = _out_projection(h_ctx, g_c, y_hy_c, ys_c[0], ys_c[1], cx["xbc"], cx["z_s"], y_at_c, d_skip,
                                    norm_w, w_out[i], ln_g[i], ln_b[i])
        h_lat = new_lat
    return h_lat
```

```python
import functools
import math

import numpy as np
import jax
import jax.numpy as jnp
from jax import lax
from jax.experimental import pallas as pl
from jax.experimental.pallas import tpu as pltpu

F32 = jnp.float32
BF16 = jnp.bfloat16
HI = lax.Precision.HIGHEST

D_MODEL = 1024
DEPTH = 2
GRID_W = 64
HY_WIDTH = 512
HY_BANDS = 16
HY_EMB = 1 + 2 * HY_BANDS
HY_FILTER_HIDDEN = 64
HY_DECAY_TARGET = 1e-2
HY_FAST_PCT = 0.3
HY_SLOW_PCT = 1.5
SSM_WIDTH = 512
SSM_HEADS = 8
SSM_HEADDIM = 64
SSM_GROUPS = 2
SSM_STATE = 128
SSM_CHUNK = 128
SSM_CONV_DIM = SSM_WIDTH + 2 * SSM_GROUPS * SSM_STATE
ATT_WIDTH = 512
ATT_HEADS = 8
ATT_KV_HEADS = 2
ATT_HEADDIM = 64
ATT_GROUP = ATT_HEADS // ATT_KV_HEADS
ATT_KV = ATT_KV_HEADS * ATT_HEADDIM
WINDOW = 128
ROPE_BASE = 10000.0
HY_IN = 4 * HY_WIDTH
SSM_IN = SSM_CONV_DIM + SSM_WIDTH + 2 * SSM_HEADS
DEEPNORM_ALPHA = (2 * DEPTH) ** 0.25
LN_EPS = 1e-6
RMS_EPS = 1e-5

LANES = 128
VMEM_LIMIT = 56 * 1024 * 1024
NEG = -1e30
LOG2E = math.log2(math.e)
FILTER_ROWS = 256


def _cp(*sem):
    return pltpu.CompilerParams(dimension_semantics=sem, vmem_limit_bytes=VMEM_LIMIT)


def _silu(x):
    return x / (1.0 + jnp.exp(-x))


def _dot(a, b, precision=None):
    return jnp.dot(a, b, preferred_element_type=F32, precision=precision)


def _dot_t(a, b):
    return lax.dot_general(a, b, (((1,), (1,)), ((), ())), preferred_element_type=F32)


def _mod_body(c_ref, w_ref, b_ref, o_ref):
    o_ref[...] = _dot(_silu(c_ref[...]), w_ref[...], HI) + b_ref[...]


def _modulation(cc, w, b):
    rows, d = cc.shape
    n = w.shape[1]
    tn = 1024
    return pl.pallas_call(
        _mod_body,
        grid=(n // tn,),
        in_specs=[pl.BlockSpec((rows, d), lambda j: (0, 0)),
                  pl.BlockSpec((d, tn), lambda j: (0, j)),
                  pl.BlockSpec((1, tn), lambda j: (0, j))],
        out_specs=pl.BlockSpec((rows, tn), lambda j: (0, j)),
        out_shape=jax.ShapeDtypeStruct((rows, n), F32),
        compiler_params=_cp("arbitrary"),
        name="modulation",
    )(cc, w, b.reshape(1, n))


IN_SEGS = (3 * HY_WIDTH, HY_WIDTH, SSM_CONV_DIM, SSM_WIDTH, ATT_WIDTH, 2 * ATT_KV, ATT_WIDTH, LANES)
IN_CHUNK = 512


def _pack_w_in(w):
    o_ss = HY_IN
    o_at = HY_IN + SSM_IN
    dt = w[:, o_ss + SSM_CONV_DIM + SSM_WIDTH:o_at]
    parts = [w[:, :HY_IN], w[:, o_ss:o_ss + SSM_CONV_DIM + SSM_WIDTH], w[:, o_at:],
             dt, jnp.zeros((w.shape[0], LANES - dt.shape[1]), w.dtype)]
    return jnp.concatenate(parts, axis=1).astype(BF16)


def _inproj_body(h_ref, sh_ref, sc_ref, w_ref, *o_refs):
    x = h_ref[0]
    mu = jnp.mean(x, -1, keepdims=True)
    xc = x - mu
    var = jnp.mean(xc * xc, -1, keepdims=True)
    xm = (xc * lax.rsqrt(var + LN_EPS) * (1.0 + sc_ref[0]) + sh_ref[0]).astype(BF16)
    off = 0
    for o_ref, n in zip(o_refs, IN_SEGS):
        for j in range(0, n, IN_CHUNK):
            w = min(IN_CHUNK, n - j)
            o_ref[0, :, j:j + w] = _dot(xm, w_ref[:, off + j:off + j + w])
        off += n


def _in_projection(h, shift, scale, w_packed):
    b, L, d = h.shape
    tm = 256
    n_all = w_packed.shape[1]
    row = lambda bi, i: (bi, i, 0)
    vec = lambda bi, i: (bi, 0, 0)
    return pl.pallas_call(
        _inproj_body,
        grid=(b, L // tm),
        in_specs=[pl.BlockSpec((1, tm, d), row), pl.BlockSpec((1, 1, d), vec), pl.BlockSpec((1, 1, d), vec),
                  pl.BlockSpec((d, n_all), lambda bi, i: (0, 0))],
        out_specs=[pl.BlockSpec((1, tm, n), row) for n in IN_SEGS],
        out_shape=[jax.ShapeDtypeStruct((b, L, n), F32) for n in IN_SEGS],
        compiler_params=_cp("parallel", "arbitrary"),
        name="in_projection",
    )(h, shift, scale, w_packed)


def _dwconv_body(u_ref, w_ref, b_ref, o_ref, *, act):
    x = u_ref[0]
    L = x.shape[0]
    row = lax.broadcasted_iota(jnp.int32, x.shape, 0)
    prev = jnp.where(row == 0, 0.0, pltpu.roll(x, 1, 0))
    nxt = jnp.where(row == L - 1, 0.0, pltpu.roll(x, L - 1, 0))
    y = prev * w_ref[0:1, :] + x * w_ref[1:2, :] + nxt * w_ref[2:3, :] + b_ref[...]
    if act:
        y = _silu(y)
    o_ref[0, 0] = y


def _dwconv(u, w, bias, *, act, split):
    b, L, c = u.shape
    tc = 256
    per = split // tc
    return pl.pallas_call(
        functools.partial(_dwconv_body, act=act),
        grid=(b, c // tc),
        in_specs=[pl.BlockSpec((1, L, tc), lambda bi, j: (bi, 0, j)),
                  pl.BlockSpec((3, tc), lambda bi, j: (0, j)),
                  pl.BlockSpec((1, tc), lambda bi, j: (0, j))],
        out_specs=pl.BlockSpec((1, 1, L, tc), lambda bi, j: (j // per, bi, 0, j % per)),
        out_shape=jax.ShapeDtypeStruct((c // split, b, L, split), F32),
        compiler_params=_cp("parallel", "arbitrary"),
        name="dwconv",
    )(u, w, bias.reshape(1, c))


def _filter_features(L):
    t = jnp.linspace(0.0, 1.0, L, dtype=F32)[:, None]
    w = 2.0 * math.pi * jnp.arange(L, dtype=F32)[:, None] / L
    f = jnp.linspace(1e-4, HY_BANDS - 1, HY_BANDS, dtype=F32)[None]
    z = jnp.concatenate([t, jnp.cos(f * w), -jnp.sin(f * w)], -1)
    return jnp.pad(z, ((0, 0), (0, LANES - HY_EMB)))


def _pad_to(a, rows, cols):
    return jnp.pad(a, ((0, rows - a.shape[0]), (0, cols - a.shape[1])))


def _filter_body(z_ref, w1_ref, w2_ref, w3_ref, b_ref, fr_ref, wo_ref, ad_ref, o_ref, *, nblk):
    i = pl.program_id(0)

    @pl.when(i < nblk)
    def _():
        z = z_ref[...]
        fr = fr_ref[...]
        h = jnp.sin(fr * (_dot(z, w1_ref[...], HI) + b_ref[0:1, :]))
        h = jnp.sin(fr * (_dot(h, w2_ref[...], HI) + b_ref[1:2, :]))
        h = jnp.sin(fr * (_dot(h, w3_ref[...], HI) + b_ref[2:3, :]))
        win = jnp.exp(-z[:, 0:1] * ad_ref[...])
        for j in range(4):
            lo = j * HY_WIDTH
            o_ref[:, lo:lo + HY_WIDTH] = _dot(h, wo_ref[:, lo:lo + HY_WIDTH], HI) * win

    @pl.when(i == nblk)
    def _():
        o_ref[...] = jnp.zeros_like(o_ref)


def _hyena_filter_taps(L, w1, b1, w2, b2, w3, b3, freq, w_out):
    z = _filter_features(L)
    hp = LANES
    bias = jnp.stack([jnp.pad(b, (0, hp - b.shape[0])) for b in (b1, b2, b3)])
    bias = jnp.pad(bias, ((0, 5), (0, 0)))
    fr = jnp.pad(freq, (0, hp - freq.shape[0])).reshape(1, hp)
    max_decay = math.log(HY_DECAY_TARGET) / HY_FAST_PCT
    min_decay = math.log(HY_DECAY_TARGET) / HY_SLOW_PCT
    absd = jnp.abs(jnp.linspace(min_decay, max_decay, HY_WIDTH, dtype=F32)).reshape(1, HY_WIDTH)
    tl = FILTER_ROWS
    nblk = L // tl
    n = 4 * HY_WIDTH
    full = lambda i: (0, 0)
    return pl.pallas_call(
        functools.partial(_filter_body, nblk=nblk),
        grid=(nblk + 1,),
        in_specs=[pl.BlockSpec((tl, hp), lambda i: (jnp.minimum(i, nblk - 1), 0)),
                  pl.BlockSpec((hp, hp), full), pl.BlockSpec((hp, hp), full), pl.BlockSpec((hp, hp), full),
                  pl.BlockSpec((8, hp), full), pl.BlockSpec((1, hp), full),
                  pl.BlockSpec((hp, n), full), pl.BlockSpec((1, HY_WIDTH), full)],
        out_specs=pl.BlockSpec((tl, n), lambda i: (i, 0)),
        out_shape=jax.ShapeDtypeStruct((L + tl, n), F32),
        compiler_params=_cp("arbitrary"),
        name="hyena_filter",
    )(z, _pad_to(w1, hp, hp), _pad_to(w2, hp, hp), _pad_to(w3, hp, hp), bias, fr, _pad_to(w_out, hp, n), absd)


def _hy_cfg(L):
    n2 = 128 if L >= 2048 else 16
    n1 = 2 * L // n2
    k1n = n1 // 2 + 1
    jp = -(-2 * k1n // 16) * 16
    pitch = n2 + 8
    return dict(L=L, n2=n2, n1=n1, nh=n1 // 2, k1n=k1n, jp=jp, pitch=pitch)


@functools.lru_cache(maxsize=None)
def _dft_tables(L):
    cfg = _hy_cfg(L)
    n, n1, n2, nh, k1n, jp = 2 * L, cfg["n1"], cfg["n2"], cfg["nh"], cfg["k1n"], cfg["jp"]
    a_n1 = np.arange(n1)
    a_k1 = np.arange(k1n)
    th = 2 * np.pi * np.outer(a_k1, a_n1) / n1
    f1 = np.zeros((jp, n1))
    f1[0:2 * k1n:2] = np.cos(th)
    f1[1:2 * k1n:2] = -np.sin(th)
    a_n2 = np.arange(n2)
    m1 = np.zeros((k1n, 2 * n2, 2 * n2))
    for k in range(k1n):
        f = np.exp(-2j * np.pi * (np.outer(a_n2, a_n2) / n2 + a_n2[None, :] * k / n))
        m1[k] = np.block([[f.real, -f.imag], [f.imag, f.real]])
    ck = np.full(k1n, 2.0)
    ck[0] = 1.0
    ck[-1] = 1.0
    th6 = 2 * np.pi * np.outer(np.arange(nh), a_k1) / n1
    f6 = np.zeros((nh, jp))
    f6[:, 0:2 * k1n:2] = ck * np.cos(th6) / n
    f6[:, 1:2 * k1n:2] = -ck * np.sin(th6) / n
    as32 = lambda a: np.asarray(a, np.float32)
    return dict(f1=as32(f1), m1=as32(m1), m2=as32(np.transpose(m1, (0, 2, 1))), f6=as32(f6))


def _first_stage(src_ref, a_scr, f1, rows, cfg, precision):
    n2, jp, pitch = cfg["n2"], cfg["jp"], cfg["pitch"]

    def step(i, carry):
        xs = jnp.concatenate([src_ref[pl.ds(2 * i, rows, stride=n2), :],
                              src_ref[pl.ds(2 * i + 1, rows, stride=n2), :]], 1)
        if precision is None:
            xs = xs.astype(BF16)
        r = _dot(f1, xs, precision)
        a_scr[pl.ds(2 * i, jp, stride=pitch), :] = r[:, :LANES]
        a_scr[pl.ds(2 * i + 1, jp, stride=pitch), :] = r[:, LANES:]
        return carry

    lax.fori_loop(0, n2 // 2, step, 0, unroll=min(16, n2 // 2))


def _k1_rows(a_scr, k, cfg):
    n2, pitch = cfg["n2"], cfg["pitch"]
    base = pl.multiple_of(2 * k * pitch, 8)
    return base, jnp.concatenate([a_scr[pl.ds(base, n2), :], a_scr[pl.ds(base + pitch, n2), :]], 0)


def _spectrum_body(kf_ref, kb_ref, f1_ref, m1_ref, o_ref, a_scr, *, cfg):
    n2, nh, jp, pitch = cfg["n2"], cfg["nh"], cfg["jp"], cfg["pitch"]
    f1 = f1_ref[...]

    def step(i, carry):
        cols = []
        for t in range(2):
            n = 2 * i + t
            cols.append(jnp.concatenate([kf_ref[pl.ds(n, nh, stride=n2), :],
                                         kb_ref[pl.ds(n2 - n, nh, stride=n2), :]], 0))
        r = _dot(f1, jnp.concatenate(cols, 1), HI)
        a_scr[pl.ds(2 * i, jp, stride=pitch), :] = r[:, :LANES]
        a_scr[pl.ds(2 * i + 1, jp, stride=pitch), :] = r[:, LANES:]
        return carry

    lax.fori_loop(0, n2 // 2, step, 0, unroll=min(16, n2 // 2))
    lag0 = pl.ds(0, jp, stride=pitch)
    a_scr[lag0, :] = a_scr[lag0, :] + f1[:, 0:1] * kb_ref[0:1, :]

    def mid(k, carry):
        _, a = _k1_rows(a_scr, k, cfg)
        o_ref[0, k] = _dot(m1_ref[k], a, HI).astype(BF16)
        return carry

    lax.fori_loop(0, cfg["k1n"], mid, 0)


def _hyena_spectrum(L, taps):
    c = HY_WIDTH
    cfg = _hy_cfg(L)
    tb = _dft_tables(L)
    n1, n2, nh, k1n = cfg["n1"], cfg["n2"], cfg["nh"], cfg["k1n"]
    f1 = np.concatenate([tb["f1"][:, :nh], tb["f1"][:, n1 - 1:nh - 1:-1]], 1)
    nct = c // LANES
    rows = taps.shape[0]
    return pl.pallas_call(
        functools.partial(_spectrum_body, cfg=cfg),
        grid=(2, nct),
        in_specs=[pl.BlockSpec((rows, LANES), lambda cv, j: (0, 2 * cv * nct + j)),
                  pl.BlockSpec((rows, LANES), lambda cv, j: (0, (2 * cv + 1) * nct + j)),
                  pl.BlockSpec(f1.shape, lambda cv, j: (0, 0)),
                  pl.BlockSpec(tb["m1"].shape, lambda cv, j: (0, 0, 0))],
        out_specs=pl.BlockSpec((1, k1n, 2 * n2, LANES), lambda cv, j: (cv, 0, 0, j)),
        out_shape=jax.ShapeDtypeStruct((2, k1n, 2 * n2, c), BF16),
        scratch_shapes=[pltpu.VMEM((cfg["jp"] * cfg["pitch"], LANES), F32)],
        compiler_params=_cp("arbitrary", "arbitrary"),
        name="hyena_spectrum",
    )(taps, taps, jnp.asarray(f1), jnp.asarray(tb["m1"]))


def _short_conv(u_ref, w_ref, b_ref, which):
    x = u_ref[0]
    L = x.shape[0]
    row = lax.broadcasted_iota(jnp.int32, x.shape, 0)
    prev = jnp.where(row == 0, 0.0, pltpu.roll(x, 1, 0))
    nxt = jnp.where(row == L - 1, 0.0, pltpu.roll(x, L - 1, 0))
    sel = slice(which, which + 1)
    return prev * w_ref[0, sel, :] + x * w_ref[1, sel, :] + nxt * w_ref[2, sel, :] + b_ref[sel, :]


def _long_conv(src_scr, a_scr, y_scr, w_scr, f1_ref, f6_ref, m1_ref, m2_ref, kf_ref, conv, cfg):
    n2, nh, jp, pitch = cfg["n2"], cfg["nh"], cfg["jp"], cfg["pitch"]
    _first_stage(src_scr, a_scr, f1_ref[...], nh, cfg, None)

    def forward(k, carry):
        _, a = _k1_rows(a_scr, k, cfg)
        x = _dot(m1_ref[k], a.astype(BF16))
        xr, xi = x[:n2], x[n2:]
        kk = kf_ref[conv, k].astype(F32)
        kr, ki = kk[:n2], kk[n2:]
        w_scr[k] = jnp.concatenate([xr * kr - xi * ki, xr * ki + xi * kr], 0).astype(BF16)
        return carry

    def inverse(k, carry):
        base = pl.multiple_of(2 * k * pitch, 8)
        b = _dot(m2_ref[k], w_scr[k])
        a_scr[pl.ds(base, n2), :] = b[:n2]
        a_scr[pl.ds(base + pitch, n2), :] = b[n2:]
        return carry

    lax.fori_loop(0, cfg["k1n"], forward, 0, unroll=3)
    lax.fori_loop(0, cfg["k1n"], inverse, 0, unroll=3)
    f6 = f6_ref[...]

    def last(i, carry):
        bs = jnp.concatenate([a_scr[pl.ds(2 * i, jp, stride=pitch), :],
                              a_scr[pl.ds(2 * i + 1, jp, stride=pitch), :]], 1).astype(BF16)
        y = _dot(f6, bs)
        y_scr[pl.ds(2 * i, nh, stride=n2), :] = y[:, :LANES]
        y_scr[pl.ds(2 * i + 1, nh, stride=n2), :] = y[:, LANES:]
        return carry

    lax.fori_loop(0, n2 // 2, last, 0, unroll=min(16, n2 // 2))


def _hyena_body(v_ref, x1_ref, x2_ref, g_ref, cw_ref, cb_ref, hb_ref, f1_ref, f6_ref, m1_ref, m2_ref, kf_ref,
                o_ref, s_scr, a_scr, y_scr, w_scr, *, cfg):
    tabs = (w_scr, f1_ref, f6_ref, m1_ref, m2_ref, kf_ref)
    s_scr[...] = _short_conv(v_ref, cw_ref, cb_ref, 0)
    _long_conv(s_scr, a_scr, y_scr, *tabs, 0, cfg)
    s_scr[...] = _short_conv(x1_ref, cw_ref, cb_ref, 1) * (y_scr[...] + s_scr[...] * hb_ref[0:1, :])
    _long_conv(s_scr, a_scr, y_scr, *tabs, 1, cfg)
    y = _short_conv(x2_ref, cw_ref, cb_ref, 2) * (y_scr[...] + s_scr[...] * hb_ref[1:2, :])
    o_ref[0] = y * _silu(g_ref[0])


def _hyena(u_hy3, u_gate, conv_w, conv_b, kf, hy_bias):
    b, L, _ = u_gate.shape
    c = HY_WIDTH
    cfg = _hy_cfg(L)
    tb = _dft_tables(L)
    nct = c // LANES
    col = lambda off: pl.BlockSpec((1, L, LANES), lambda j, bi: (bi, 0, off * nct + j))
    full = lambda a: pl.BlockSpec(a.shape, lambda j, bi: (0,) * a.ndim)
    f1 = jnp.asarray(tb["f1"][:, :cfg["nh"]]).astype(BF16)
    f6 = jnp.asarray(tb["f6"]).astype(BF16)
    m1 = jnp.asarray(tb["m1"]).astype(BF16)
    m2 = jnp.asarray(tb["m2"]).astype(BF16)
    return pl.pallas_call(
        functools.partial(_hyena_body, cfg=cfg),
        grid=(nct, b),
        in_specs=[col(0), col(1), col(2), col(0),
                  pl.BlockSpec((3, 3, LANES), lambda j, bi: (0, 0, j)),
                  pl.BlockSpec((3, LANES), lambda j, bi: (0, j)),
                  pl.BlockSpec((2, LANES), lambda j, bi: (0, j)),
                  full(f1), full(f6), full(m1), full(m2),
                  pl.BlockSpec((2, cfg["k1n"], 2 * cfg["n2"], LANES), lambda j, bi: (0, 0, 0, j))],
        out_specs=col(0),
        out_shape=jax.ShapeDtypeStruct((b, L, c), F32),
        scratch_shapes=[pltpu.VMEM((L, LANES), F32),
                        pltpu.VMEM((cfg["jp"] * cfg["pitch"], LANES), F32),
                        pltpu.VMEM((L, LANES), F32),
                        pltpu.VMEM((cfg["k1n"], 2 * cfg["n2"], LANES), BF16)],
        compiler_params=_cp("arbitrary", "arbitrary"),
        name="hyena",
    )(u_hy3, u_hy3, u_hy3, u_gate, conv_w.reshape(3, 3, c), conv_b.reshape(3, c), hy_bias, f1, f6, m1, m2, kf)


def _ssd_body(xbc_ref, dtr_ref, dtb_ref, a_ref, e_ref, init_ref, y_ref, fin_ref, st_ref, *, reverse, nc, d):
    ci = pl.program_id(1)

    @pl.when(ci == 0)
    def _():
        st_ref[...] = init_ref[0]

    q = SSM_CHUNK
    gw = SSM_WIDTH // SSM_GROUPS
    hpg = SSM_HEADS // SSM_GROUPS
    xbc = xbc_ref[0]
    xs = xbc[:, :SSM_WIDTH]
    raw = dtr_ref[0] + dtb_ref[...]
    dt = jnp.maximum(raw, 0.0) + jnp.log1p(jnp.exp(-jnp.abs(raw)))
    da = dt * a_ref[...]
    li = lax.broadcasted_iota(jnp.int32, (q, q), 0)
    si = lax.broadcasted_iota(jnp.int32, (q, q), 1)
    mask = (li <= si) if reverse else (li >= si)
    acs = _dot(mask.astype(F32), da, HI)
    acs_t = acs.T
    e = e_ref[...]
    dtx = _dot(dt, e, HI)
    acsx = _dot(acs, e, HI)
    last = 0 if reverse else q - 1
    totx = acsx[last:last + 1, :]
    xd = xs * dtx
    xde = (xd * jnp.exp(totx - acsx)).astype(BF16)
    xdb = xd.astype(BF16)
    eacs = jnp.exp(acsx)
    etot = jnp.exp(totx)
    for g in range(SSM_GROUPS):
        bg = xbc[:, SSM_WIDTH + g * SSM_STATE:SSM_WIDTH + (g + 1) * SSM_STATE]
        cg = xbc[:, SSM_WIDTH + (SSM_GROUPS + g) * SSM_STATE:SSM_WIDTH + (SSM_GROUPS + g + 1) * SSM_STATE]
        cgb = cg.astype(BF16)
        gmat = _dot_t(cgb, bg.astype(BF16))
        st = st_ref[g]
        y_off = _dot(cgb, st.astype(BF16)) * eacs[:, g * gw:(g + 1) * gw]
        for j in range(hpg):
            h = g * hpg + j
            col = d * SSM_HEADS + h
            seg = acs[:, col:col + 1] - acs_t[col:col + 1, :]
            lm = jnp.where(mask, jnp.exp(seg), 0.0)
            lo = h * SSM_HEADDIM
            yd = _dot((gmat * lm).astype(BF16), xdb[:, lo:lo + SSM_HEADDIM])
            y_ref[0, :, lo:lo + SSM_HEADDIM] = yd + y_off[:, j * SSM_HEADDIM:(j + 1) * SSM_HEADDIM]
        st_ref[g] = st * etot[:, g * gw:(g + 1) * gw] + _dot(bg.T.astype(BF16), xde[:, g * gw:(g + 1) * gw])

    @pl.when(ci == nc - 1)
    def _():
        fin_ref[0] = st_ref[...]


def _ssd_scan(xbc, dt_raw, dt_bias_row, a_row, expand, init, *, d):
    b, L, _ = xbc.shape
    nc = L // SSM_CHUNK
    reverse = d == 1
    chunk = (lambda bi, c: (bi, nc - 1 - c, 0)) if reverse else (lambda bi, c: (bi, c, 0))
    full2 = lambda bi, c: (0, 0)
    st_shape = (SSM_GROUPS, SSM_STATE, SSM_WIDTH // SSM_GROUPS)
    st_spec = pl.BlockSpec((1,) + st_shape, lambda bi, c: (bi, 0, 0, 0))
    return pl.pallas_call(
        functools.partial(_ssd_body, reverse=reverse, nc=nc, d=d),
        grid=(b, nc),
        in_specs=[pl.BlockSpec((1, SSM_CHUNK, SSM_CONV_DIM), chunk),
                  pl.BlockSpec((1, SSM_CHUNK, LANES), chunk),
                  pl.BlockSpec((1, LANES), full2), pl.BlockSpec((1, LANES), full2),
                  pl.BlockSpec((LANES, SSM_WIDTH), full2), st_spec],
        out_specs=[pl.BlockSpec((1, SSM_CHUNK, SSM_WIDTH), chunk), st_spec],
        out_shape=[jax.ShapeDtypeStruct((b, L, SSM_WIDTH), F32),
                   jax.ShapeDtypeStruct((b,) + st_shape, F32)],
        scratch_shapes=[pltpu.VMEM(st_shape, F32)],
        compiler_params=_cp("parallel", "arbitrary"),
        name="ssd_scan",
    )(xbc, dt_raw, dt_bias_row, a_row, expand, init)


def _ssd_expand(d):
    e = np.zeros((LANES, SSM_WIDTH), np.float32)
    for h in range(SSM_HEADS):
        e[d * SSM_HEADS + h, h * SSM_HEADDIM:(h + 1) * SSM_HEADDIM] = 1.0
    return e


def _rope_tables(L):
    rows = L // GRID_W
    row = jnp.broadcast_to(jnp.arange(rows)[:, None], (rows, GRID_W)).reshape(L)
    col = jnp.broadcast_to(jnp.arange(GRID_W)[None, :], (rows, GRID_W)).reshape(L)
    nf = ATT_HEADDIM // 4
    inv = ROPE_BASE ** (-jnp.arange(nf, dtype=F32) / nf)
    ar = row.astype(F32)[:, None] * inv
    ac = col.astype(F32)[:, None] * inv
    cos = jnp.concatenate([jnp.cos(ar), jnp.cos(ar), jnp.cos(ac), jnp.cos(ac)], -1)
    sin = jnp.concatenate([-jnp.sin(ar), jnp.sin(ar), -jnp.sin(ac), jnp.sin(ac)], -1)
    return jnp.tile(cos, (1, ATT_HEADS)), jnp.tile(sin, (1, ATT_HEADS))


def _rope(x, cos, sin):
    w = x.shape[-1]
    quarter = ATT_HEADDIM // 4
    lane = lax.broadcasted_iota(jnp.int32, x.shape, x.ndim - 1)
    partner = jnp.where((lane // quarter) % 2 == 0, pltpu.roll(x, w - quarter, x.ndim - 1),
                        pltpu.roll(x, quarter, x.ndim - 1))
    return x * cos + partner * sin


def _rope_body(q_ref, kv_ref, cos_ref, sin_ref, qo_ref, ko_ref):
    cos = cos_ref[...]
    sin = sin_ref[...]
    qo_ref[0] = _rope(q_ref[0], cos, sin)
    ko_ref[0] = _rope(kv_ref[0], cos[:, :ATT_KV], sin[:, :ATT_KV])


def _apply_rope(u_q, u_kv, cos, sin):
    b, L, _ = u_q.shape
    tl = 512
    row = lambda i, bi: (bi, i, 0)
    tab = lambda i, bi: (i, 0)
    return pl.pallas_call(
        _rope_body,
        grid=(L // tl, b),
        in_specs=[pl.BlockSpec((1, tl, ATT_WIDTH), row), pl.BlockSpec((1, tl, ATT_KV), row),
                  pl.BlockSpec((tl, ATT_WIDTH), tab), pl.BlockSpec((tl, ATT_WIDTH), tab)],
        out_specs=[pl.BlockSpec((1, tl, ATT_WIDTH), row), pl.BlockSpec((1, tl, ATT_KV), row)],
        out_shape=[jax.ShapeDtypeStruct((b, L, ATT_WIDTH), F32), jax.ShapeDtypeStruct((b, L, ATT_KV), F32)],
        compiler_params=_cp("parallel", "arbitrary"),
        name="rope",
    )(u_q, u_kv, cos, sin)


def _wattn_body(sink_ref, bias_ref, q_ref, kp_ref, kc_ref, kn_ref, vp_ref, vc_ref, vn_ref, kx_ref, vx_ref, z_ref,
                o_ref):
    hd = ATT_HEADDIM
    low_half = lax.broadcasted_iota(jnp.int32, (WINDOW, 2 * hd), 1) < hd
    q = q_ref[0] * (hd ** -0.5 * LOG2E)
    bias = bias_ref[0]
    z = z_ref[0]
    k_all, v_ext = [], []
    for g in range(ATT_KV_HEADS):
        ks = slice(g * hd, (g + 1) * hd)
        k_all.append(jnp.concatenate([r[0, :, ks] for r in (kp_ref, kc_ref, kn_ref, kx_ref)], 0).astype(BF16))
        v_all = jnp.concatenate([r[0, :, ks] for r in (vp_ref, vc_ref, vn_ref, vx_ref)], 0).astype(BF16)
        ones = jnp.ones_like(v_all)
        v_ext.append((jnp.concatenate([v_all, ones], 1), jnp.concatenate([ones, v_all], 1)))
    heads = range(ATT_HEADS)
    sinks = [sink_ref[h] * LOG2E for h in heads]
    scores = [_dot_t(q[:, h * hd:(h + 1) * hd].astype(BF16), k_all[h // ATT_GROUP]) + bias for h in heads]
    maxes = [jnp.maximum(jnp.max(s, -1, keepdims=True), sk) for s, sk in zip(scores, sinks)]
    probs = [jnp.exp2(s - m).astype(BF16) for s, m in zip(scores, maxes)]
    exts = [_dot(p, v_ext[h // ATT_GROUP][h % 2]) for h, p in zip(heads, probs)]
    outs = [e / (pltpu.roll(e, hd, 1) + jnp.exp2(sk - m)) for e, sk, m in zip(exts, sinks, maxes)]
    for pair in range(ATT_HEADS // 2):
        cs = slice(2 * pair * hd, (2 * pair + 2) * hd)
        o_ref[0, :, cs] = jnp.where(low_half, outs[2 * pair], outs[2 * pair + 1]) * _silu(z[:, cs])


def _window_attention(q_rot, k_rot, u_kv, uc_kv, sinks, z_a):
    b, L, _ = q_rot.shape
    lc = uc_kv.shape[1]
    nb = L // WINDOW
    hd2 = ATT_KV
    cur = lambda bi, i: (bi, i, 0)
    prv = lambda bi, i: (bi, jnp.maximum(i - 1, 0), 0)
    nxt = lambda bi, i: (bi, jnp.minimum(i + 1, nb - 1), 0)
    vcur = lambda bi, i: (bi, i, 1)
    vprv = lambda bi, i: (bi, jnp.maximum(i - 1, 0), 1)
    vnxt = lambda bi, i: (bi, jnp.minimum(i + 1, nb - 1), 1)
    kblk = lambda f: pl.BlockSpec((1, WINDOW, hd2), f)
    nk = 3 * WINDOW + lc
    row = np.arange(WINDOW)[:, None]
    col = np.arange(nk)[None, :]
    in_prev = (col < WINDOW) & (col >= row)
    in_next = (col >= 2 * WINDOW) & (col < 3 * WINDOW) & (col - 2 * WINDOW <= row)
    always = ((col >= WINDOW) & (col < 2 * WINDOW)) | (col >= 3 * WINDOW)
    kinds = [always | in_next, always | in_prev | in_next, always | in_prev]
    if nb == 1:
        kinds = [always] * 3
    bias = jnp.asarray(np.where(np.stack(kinds), 0.0, NEG).astype(np.float32))
    kind = lambda bi, i: (jnp.where(i == 0, 0, jnp.where(i == nb - 1, 2, 1)), 0, 0)
    return pl.pallas_call(
        _wattn_body,
        grid=(b, nb),
        in_specs=[pl.BlockSpec(memory_space=pltpu.SMEM),
                  pl.BlockSpec((1, WINDOW, nk), kind),
                  pl.BlockSpec((1, WINDOW, ATT_WIDTH), cur),
                  kblk(prv), kblk(cur), kblk(nxt), kblk(vprv), kblk(vcur), kblk(vnxt),
                  pl.BlockSpec((1, lc, hd2), lambda bi, i: (bi, 0, 0)),
                  pl.BlockSpec((1, lc, hd2), lambda bi, i: (bi, 0, 1)),
                  pl.BlockSpec((1, WINDOW, ATT_WIDTH), cur)],
        out_specs=pl.BlockSpec((1, WINDOW, ATT_WIDTH), cur),
        out_shape=jax.ShapeDtypeStruct((b, L, ATT_WIDTH), F32),
        compiler_params=_cp("parallel", "arbitrary"),
        name="window_attention",
    )(sinks, bias, q_rot, k_rot, k_rot, k_rot, u_kv, u_kv, u_kv, uc_kv, uc_kv, z_a)


def _cattn_body(sink_ref, q_ref, k_ref, v_ref, z_ref, o_ref):
    scale = ATT_HEADDIM ** -0.5
    q = q_ref[0]
    z = z_ref[0]
    for g in range(ATT_KV_HEADS):
        ks = slice(g * ATT_HEADDIM, (g + 1) * ATT_HEADDIM)
        k = k_ref[0, :, ks].astype(BF16)
        v = v_ref[0, :, ks].astype(BF16)
        for j in range(ATT_GROUP):
            h = g * ATT_GROUP + j
            hs = slice(h * ATT_HEADDIM, (h + 1) * ATT_HEADDIM)
            s = _dot_t(q[:, hs].astype(BF16), k) * scale
            sink = sink_ref[h]
            m = jnp.maximum(jnp.max(s, -1, keepdims=True), sink)
            p = jnp.exp(s - m)
            den = jnp.sum(p, -1, keepdims=True) + jnp.exp(sink - m)
            o_ref[0, :, hs] = _dot(p.astype(BF16), v) / den * _silu(z[:, hs])


def _ctx_attention(uc_q, uc_kv, sinks, z_ac):
    b, lc, _ = uc_q.shape
    blk = lambda bi: (bi, 0, 0)
    return pl.pallas_call(
        _cattn_body,
        grid=(b,),
        in_specs=[pl.BlockSpec(memory_space=pltpu.SMEM),
                  pl.BlockSpec((1, lc, ATT_WIDTH), blk),
                  pl.BlockSpec((1, lc, ATT_KV), lambda bi: (bi, 0, 0)),
                  pl.BlockSpec((1, lc, ATT_KV), lambda bi: (bi, 0, 1)),
                  pl.BlockSpec((1, lc, ATT_WIDTH), blk)],
        out_specs=pl.BlockSpec((1, lc, ATT_WIDTH), blk),
        out_shape=jax.ShapeDtypeStruct((b, lc, ATT_WIDTH), F32),
        compiler_params=_cp("parallel"),
        name="ctx_attention",
    )(sinks, uc_q, uc_kv, uc_kv, z_ac)


def _out_body(h_ref, g_ref, yhy_ref, yf_ref, yb_ref, xs_ref, zs_ref, yat_ref, dsk_ref, nw_ref, w_ref,
              lg_ref, lb_ref, o_ref):
    gw = SSM_WIDTH // SSM_GROUPS
    ys = (yf_ref[0] + yb_ref[0] + xs_ref[0] * dsk_ref[...]) * _silu(zs_ref[0])
    acc = _dot(yhy_ref[0].astype(BF16), w_ref[0:HY_WIDTH, :])
    for g in range(SSM_GROUPS):
        seg = ys[:, g * gw:(g + 1) * gw]
        seg = seg * lax.rsqrt(jnp.mean(seg * seg, -1, keepdims=True) + RMS_EPS) * nw_ref[:, g * gw:(g + 1) * gw]
        lo = HY_WIDTH + g * gw
        acc = acc + _dot(seg.astype(BF16), w_ref[lo:lo + gw, :])
    acc = acc + _dot(yat_ref[0].astype(BF16), w_ref[HY_WIDTH + SSM_WIDTH:, :])
    r = DEEPNORM_ALPHA * h_ref[0] + g_ref[0] * acc
    mu = jnp.mean(r, -1, keepdims=True)
    rc = r - mu
    var = jnp.mean(rc * rc, -1, keepdims=True)
    o_ref[0] = rc * lax.rsqrt(var + LN_EPS) * lg_ref[...] + lb_ref[...]


def _out_projection(h, gate_mod, y_hy, y_f, y_b, xbc, z_s, y_at, d_skip, norm_w, w_out, ln_g, ln_b):
    b, L, d = h.shape
    tm = 256
    row = lambda bi, i: (bi, i, 0)
    vec = lambda bi, i: (bi, 0, 0)
    full = lambda bi, i: (0, 0)
    w512 = pl.BlockSpec((1, tm, SSM_WIDTH), row)
    return pl.pallas_call(
        _out_body,
        grid=(b, L // tm),
        in_specs=[pl.BlockSpec((1, tm, d), row), pl.BlockSpec((1, 1, d), vec),
                  w512, w512, w512, w512, w512, w512,
                  pl.BlockSpec((1, SSM_WIDTH), full), pl.BlockSpec((1, SSM_WIDTH), full),
                  pl.BlockSpec(w_out.shape, full), pl.BlockSpec((1, d), full), pl.BlockSpec((1, d), full)],
        out_specs=pl.BlockSpec((1, tm, d), row),
        out_shape=jax.ShapeDtypeStruct((b, L, d), F32),
        compiler_params=_cp("parallel", "arbitrary"),
        name="out_projection",
    )(h, gate_mod, y_hy, y_f, y_b, xbc, z_s, y_at, d_skip, norm_w, w_out.astype(BF16),
      ln_g.reshape(1, d), ln_b.reshape(1, d))


def _sequence_front(h, shift, scale, w_packed, ssm_conv_w, ssm_conv_b):
    u_hy3, u_hyg, u_xbc, u_zs, u_q, u_kv, u_za, u_dt = _in_projection(h, shift, scale, w_packed)
    xbc = _dwconv(u_xbc, ssm_conv_w, ssm_conv_b, act=True, split=SSM_CONV_DIM)[0]
    return dict(hy3=u_hy3, hy_gate=u_hyg, xbc=xbc, z_s=u_zs, q=u_q, kv=u_kv, z_a=u_za, dt=u_dt)


def kernel(x, c, ctx, c_ctx, w_mod, b_mod, w_in, hy_conv_w, hy_conv_b, hy_f_w1, hy_f_b1, hy_f_w2, hy_f_b2,
           hy_f_w3, hy_f_b3, hy_f_freq, hy_f_wout, hy_bias, ssm_conv_w, ssm_conv_b, ssm_dt_bias, ssm_a_log,
           ssm_d, ssm_norm_w, attn_sinks, w_out, ln_g, ln_b):
    b, L, d = x.shape
    lc = ctx.shape[1]
    cos, sin = _rope_tables(L)
    cc = jnp.concatenate([c, c_ctx[None], jnp.zeros((16 - b - 1, d), F32)], 0)
    expand = [jnp.asarray(_ssd_expand(0)), jnp.asarray(_ssd_expand(1))]
    zero_state = jnp.zeros((b, SSM_GROUPS, SSM_STATE, SSM_WIDTH // SSM_GROUPS), F32)
    h_lat, h_ctx = x, ctx
    for i in range(DEPTH):
        ctx_needed = i < DEPTH - 1
        mod = _modulation(cc, w_mod[i], b_mod[i])
        sh, sc, g = (mod[:b, None, j * d:(j + 1) * d] for j in range(3))
        sh_c, sc_c, g_c = (jnp.broadcast_to(mod[b:b + 1, None, j * d:(j + 1) * d], (b, 1, d)) for j in range(3))
        w_packed = _pack_w_in(w_in[i])
        lat = _sequence_front(h_lat, sh, sc, w_packed, ssm_conv_w[i], ssm_conv_b[i])
        cx = _sequence_front(h_ctx, sh_c, sc_c, w_packed, ssm_conv_w[i], ssm_conv_b[i])

        dt_bias_row = jnp.pad(ssm_dt_bias[i].reshape(1, -1), ((0, 0), (0, LANES - 2 * SSM_HEADS)))
        a_row = jnp.pad(-jnp.exp(ssm_a_log[i]).reshape(1, -1), ((0, 0), (0, LANES - 2 * SSM_HEADS)))
        ys_c, ys = [], []
        for dr in range(2):
            y_c, s_c = _ssd_scan(cx["xbc"], cx["dt"], dt_bias_row, a_row, expand[dr], zero_state, d=dr)
            y_l, _ = _ssd_scan(lat["xbc"], lat["dt"], dt_bias_row, a_row, expand[dr], s_c, d=dr)
            ys_c.append(y_c)
            ys.append(y_l)

        filt = (hy_f_w1[i], hy_f_b1[i], hy_f_w2[i], hy_f_b2[i], hy_f_w3[i], hy_f_b3[i], hy_f_freq[i], hy_f_wout[i])
        kf = _hyena_spectrum(L, _hyena_filter_taps(L, *filt))
        y_hy = _hyena(lat["hy3"], lat["hy_gate"], hy_conv_w[i], hy_conv_b[i], kf, hy_bias[i])

        q_rot, k_rot = _apply_rope(lat["q"], lat["kv"], cos, sin)
        y_at = _window_attention(q_rot, k_rot, lat["kv"], cx["kv"], attn_sinks[i], lat["z_a"])

        d_skip = jnp.repeat(ssm_d[i], SSM_HEADDIM).reshape(1, SSM_WIDTH)
        norm_w = ssm_norm_w[i].reshape(1, SSM_WIDTH)
        new_lat = _out_projection(h_lat, g, y_hy, ys[0], ys[1], lat["xbc"], lat["z_s"], y_at, d_skip, norm_w,
                                  w_out[i], ln_g[i], ln_b[i])
        if ctx_needed:
            kf_c = _hyena_spectrum(lc, _hyena_filter_taps(lc, *filt))
            y_hy_c = _hyena(cx["hy3"], cx["hy_gate"], hy_conv_w[i], hy_conv_b[i], kf_c, hy_bias[i])
            y_at_c = _ctx_attention(cx["q"], cx["kv"], attn_sinks[i], cx["z_a"])
            h_ctx = _out_projection(h_ctx, g_c, y_hy_c, ys_c[0], ys_c[1], cx["xbc"], cx["z_s"], y_at_c, d_skip,
                                    norm_w, w_out[i], ln_g[i], ln_b[i])
        h_lat = new_lat
    return h_lat
```

```python
import functools
import math

import numpy as np
import jax
import jax.numpy as jnp
from jax import lax
from jax.experimental import pallas as pl
from jax.experimental.pallas import tpu as pltpu

F32 = jnp.float32
BF16 = jnp.bfloat16
HI = lax.Precision.HIGHEST

D_MODEL = 1024
DEPTH = 2
GRID_W = 64
HY_WIDTH = 512
HY_BANDS = 16
HY_EMB = 1 + 2 * HY_BANDS
HY_FILTER_HIDDEN = 64
HY_DECAY_TARGET = 1e-2
HY_FAST_PCT = 0.3
HY_SLOW_PCT = 1.5
SSM_WIDTH = 512
SSM_HEADS = 8
SSM_HEADDIM = 64
SSM_GROUPS = 2
SSM_STATE = 128
SSM_CHUNK = 128
SSM_CONV_DIM = SSM_WIDTH + 2 * SSM_GROUPS * SSM_STATE
ATT_WIDTH = 512
ATT_HEADS = 8
ATT_KV_HEADS = 2
ATT_HEADDIM = 64
ATT_GROUP = ATT_HEADS // ATT_KV_HEADS
ATT_KV = ATT_KV_HEADS * ATT_HEADDIM
WINDOW = 128
ROPE_BASE = 10000.0
HY_IN = 4 * HY_WIDTH
SSM_IN = SSM_CONV_DIM + SSM_WIDTH + 2 * SSM_HEADS
DEEPNORM_ALPHA = (2 * DEPTH) ** 0.25
LN_EPS = 1e-6
RMS_EPS = 1e-5

LANES = 128
VMEM_LIMIT = 56 * 1024 * 1024
NEG = -1e30
BIG = 1e30
LOG2E = math.log2(math.e)
FILTER_ROWS = 256


def _cp(*sem):
    return pltpu.CompilerParams(dimension_semantics=sem, vmem_limit_bytes=VMEM_LIMIT)


def _silu(x):
    return x / (1.0 + jnp.exp(-x))


def _dot(a, b, precision=None):
    return jnp.dot(a, b, preferred_element_type=F32, precision=precision)


def _dot_t(a, b):
    return lax.dot_general(a, b, (((1,), (1,)), ((), ())), preferred_element_type=F32)


def _mod_body(c_ref, w_ref, b_ref, o_ref):
    o_ref[...] = _dot(_silu(c_ref[...]), w_ref[...], HI) + b_ref[...]


def _modulation(cc, w, b):
    rows, d = cc.shape
    n = w.shape[1]
    tn = 1024
    return pl.pallas_call(
        _mod_body,
        grid=(n // tn,),
        in_specs=[pl.BlockSpec((rows, d), lambda j: (0, 0)),
                  pl.BlockSpec((d, tn), lambda j: (0, j)),
                  pl.BlockSpec((1, tn), lambda j: (0, j))],
        out_specs=pl.BlockSpec((rows, tn), lambda j: (0, j)),
        out_shape=jax.ShapeDtypeStruct((rows, n), F32),
        compiler_params=_cp("arbitrary"),
        name="modulation",
    )(cc, w, b.reshape(1, n))


IN_SEGS = (3 * HY_WIDTH, HY_WIDTH, SSM_CONV_DIM, SSM_WIDTH, ATT_WIDTH, 2 * ATT_KV, ATT_WIDTH, LANES)
IN_CHUNK = 512


def _pack_w_in(w):
    o_ss = HY_IN
    o_at = HY_IN + SSM_IN
    dt = w[:, o_ss + SSM_CONV_DIM + SSM_WIDTH:o_at]
    parts = [w[:, :HY_IN], w[:, o_ss:o_ss + SSM_CONV_DIM + SSM_WIDTH], w[:, o_at:],
             dt, jnp.zeros((w.shape[0], LANES - dt.shape[1]), w.dtype)]
    return jnp.concatenate(parts, axis=1).astype(BF16)


def _inproj_body(h_ref, sh_ref, sc_ref, w_ref, *o_refs):
    x = h_ref[0]
    mu = jnp.mean(x, -1, keepdims=True)
    xc = x - mu
    var = jnp.mean(xc * xc, -1, keepdims=True)
    xm = (xc * lax.rsqrt(var + LN_EPS) * (1.0 + sc_ref[0]) + sh_ref[0]).astype(BF16)
    off = 0
    for o_ref, n in zip(o_refs, IN_SEGS):
        for j in range(0, n, IN_CHUNK):
            w = min(IN_CHUNK, n - j)
            o_ref[0, :, j:j + w] = _dot(xm, w_ref[:, off + j:off + j + w])
        off += n


def _in_projection(h, shift, scale, w_packed):
    b, L, d = h.shape
    tm = 256
    n_all = w_packed.shape[1]
    row = lambda bi, i: (bi, i, 0)
    vec = lambda bi, i: (bi, 0, 0)
    return pl.pallas_call(
        _inproj_body,
        grid=(b, L // tm),
        in_specs=[pl.BlockSpec((1, tm, d), row), pl.BlockSpec((1, 1, d), vec), pl.BlockSpec((1, 1, d), vec),
                  pl.BlockSpec((d, n_all), lambda bi, i: (0, 0))],
        out_specs=[pl.BlockSpec((1, tm, n), row) for n in IN_SEGS],
        out_shape=[jax.ShapeDtypeStruct((b, L, n), F32) for n in IN_SEGS],
        compiler_params=_cp("parallel", "arbitrary"),
        name="in_projection",
    )(h, shift, scale, w_packed)


def _dwconv_body(u_ref, w_ref, b_ref, o_ref, *, act):
    x = u_ref[0]
    L = x.shape[0]
    row = lax.broadcasted_iota(jnp.int32, x.shape, 0)
    prev = jnp.where(row == 0, 0.0, pltpu.roll(x, 1, 0))
    nxt = jnp.where(row == L - 1, 0.0, pltpu.roll(x, L - 1, 0))
    y = prev * w_ref[0:1, :] + x * w_ref[1:2, :] + nxt * w_ref[2:3, :] + b_ref[...]
    if act:
        y = _silu(y)
    o_ref[0, 0] = y


def _dwconv(u, w, bias, *, act, split):
    b, L, c = u.shape
    tc = 256
    per = split // tc
    return pl.pallas_call(
        functools.partial(_dwconv_body, act=act),
        grid=(b, c // tc),
        in_specs=[pl.BlockSpec((1, L, tc), lambda bi, j: (bi, 0, j)),
                  pl.BlockSpec((3, tc), lambda bi, j: (0, j)),
                  pl.BlockSpec((1, tc), lambda bi, j: (0, j))],
        out_specs=pl.BlockSpec((1, 1, L, tc), lambda bi, j: (j // per, bi, 0, j % per)),
        out_shape=jax.ShapeDtypeStruct((c // split, b, L, split), F32),
        compiler_params=_cp("parallel", "arbitrary"),
        name="dwconv",
    )(u, w, bias.reshape(1, c))


def _filter_features(L):
    t = jnp.linspace(0.0, 1.0, L, dtype=F32)[:, None]
    w = 2.0 * math.pi * jnp.arange(L, dtype=F32)[:, None] / L
    f = jnp.linspace(1e-4, HY_BANDS - 1, HY_BANDS, dtype=F32)[None]
    z = jnp.concatenate([t, jnp.cos(f * w), -jnp.sin(f * w)], -1)
    return jnp.pad(z, ((0, 0), (0, LANES - HY_EMB)))


def _pad_to(a, rows, cols):
    return jnp.pad(a, ((0, rows - a.shape[0]), (0, cols - a.shape[1])))


def _filter_body(z_ref, w1_ref, w2_ref, w3_ref, b_ref, fr_ref, wo_ref, ad_ref, o_ref, *, nblk):
    i = pl.program_id(0)

    @pl.when(i < nblk)
    def _():
        z = z_ref[...]
        fr = fr_ref[...]
        h = jnp.sin(fr * (_dot(z, w1_ref[...], HI) + b_ref[0:1, :]))
        h = jnp.sin(fr * (_dot(h, w2_ref[...], HI) + b_ref[1:2, :]))
        h = jnp.sin(fr * (_dot(h, w3_ref[...], HI) + b_ref[2:3, :]))
        win = jnp.exp(-z[:, 0:1] * ad_ref[...])
        for j in range(4):
            lo = j * HY_WIDTH
            o_ref[:, lo:lo + HY_WIDTH] = _dot(h, wo_ref[:, lo:lo + HY_WIDTH], HI) * win

    @pl.when(i == nblk)
    def _():
        o_ref[...] = jnp.zeros_like(o_ref)


def _hyena_filter_taps(L, w1, b1, w2, b2, w3, b3, freq, w_out):
    z = _filter_features(L)
    hp = LANES
    bias = jnp.stack([jnp.pad(b, (0, hp - b.shape[0])) for b in (b1, b2, b3)])
    bias = jnp.pad(bias, ((0, 5), (0, 0)))
    fr = jnp.pad(freq, (0, hp - freq.shape[0])).reshape(1, hp)
    max_decay = math.log(HY_DECAY_TARGET) / HY_FAST_PCT
    min_decay = math.log(HY_DECAY_TARGET) / HY_SLOW_PCT
    absd = jnp.abs(jnp.linspace(min_decay, max_decay, HY_WIDTH, dtype=F32)).reshape(1, HY_WIDTH)
    tl = FILTER_ROWS
    nblk = L // tl
    n = 4 * HY_WIDTH
    full = lambda i: (0, 0)
    return pl.pallas_call(
        functools.partial(_filter_body, nblk=nblk),
        grid=(nblk + 1,),
        in_specs=[pl.BlockSpec((tl, hp), lambda i: (jnp.minimum(i, nblk - 1), 0)),
                  pl.BlockSpec((hp, hp), full), pl.BlockSpec((hp, hp), full), pl.BlockSpec((hp, hp), full),
                  pl.BlockSpec((8, hp), full), pl.BlockSpec((1, hp), full),
                  pl.BlockSpec((hp, n), full), pl.BlockSpec((1, HY_WIDTH), full)],
        out_specs=pl.BlockSpec((tl, n), lambda i: (i, 0)),
        out_shape=jax.ShapeDtypeStruct((L + tl, n), F32),
        compiler_params=_cp("arbitrary"),
        name="hyena_filter",
    )(z, _pad_to(w1, hp, hp), _pad_to(w2, hp, hp), _pad_to(w3, hp, hp), bias, fr, _pad_to(w_out, hp, n), absd)


def _hy_cfg(L):
    n2 = 128 if L >= 2048 else 16
    n1 = 2 * L // n2
    k1n = n1 // 2 + 1
    jp = -(-2 * k1n // 16) * 16
    pitch = n2 + 8
    return dict(L=L, n2=n2, n1=n1, nh=n1 // 2, k1n=k1n, jp=jp, pitch=pitch)


@functools.lru_cache(maxsize=None)
def _dft_tables(L):
    cfg = _hy_cfg(L)
    n, n1, n2, nh, k1n, jp = 2 * L, cfg["n1"], cfg["n2"], cfg["nh"], cfg["k1n"], cfg["jp"]
    a_n1 = np.arange(n1)
    a_k1 = np.arange(k1n)
    th = 2 * np.pi * np.outer(a_k1, a_n1) / n1
    f1 = np.zeros((jp, n1))
    f1[0:2 * k1n:2] = np.cos(th)
    f1[1:2 * k1n:2] = -np.sin(th)
    a_n2 = np.arange(n2)
    m1 = np.zeros((k1n, 2 * n2, 2 * n2))
    for k in range(k1n):
        f = np.exp(-2j * np.pi * (np.outer(a_n2, a_n2) / n2 + a_n2[None, :] * k / n))
        m1[k] = np.block([[f.real, -f.imag], [f.imag, f.real]])
    ck = np.full(k1n, 2.0)
    ck[0] = 1.0
    ck[-1] = 1.0
    th6 = 2 * np.pi * np.outer(np.arange(nh), a_k1) / n1
    f6 = np.zeros((nh, jp))
    f6[:, 0:2 * k1n:2] = ck * np.cos(th6) / n
    f6[:, 1:2 * k1n:2] = -ck * np.sin(th6) / n
    as32 = lambda a: np.asarray(a, np.float32)
    return dict(f1=as32(f1), m1=as32(m1), m2=as32(np.transpose(m1, (0, 2, 1))), f6=as32(f6))


def _first_stage(src_ref, a_scr, f1, rows, cfg, precision):
    n2, jp, pitch = cfg["n2"], cfg["jp"], cfg["pitch"]

    def step(i, carry):
        xs = jnp.concatenate([src_ref[pl.ds(2 * i, rows, stride=n2), :],
                              src_ref[pl.ds(2 * i + 1, rows, stride=n2), :]], 1)
        if precision is None:
            xs = xs.astype(BF16)
        r = _dot(f1, xs, precision)
        a_scr[pl.ds(2 * i, jp, stride=pitch), :] = r[:, :LANES]
        a_scr[pl.ds(2 * i + 1, jp, stride=pitch), :] = r[:, LANES:]
        return carry

    lax.fori_loop(0, n2 // 2, step, 0, unroll=min(16, n2 // 2))


def _k1_rows(a_scr, k, cfg):
    n2, pitch = cfg["n2"], cfg["pitch"]
    base = pl.multiple_of(2 * k * pitch, 8)
    return base, jnp.concatenate([a_scr[pl.ds(base, n2), :], a_scr[pl.ds(base + pitch, n2), :]], 0)


def _spectrum_body(kf_ref, kb_ref, f1_ref, m1_ref, o_ref, a_scr, *, cfg):
    n2, nh, jp, pitch = cfg["n2"], cfg["nh"], cfg["jp"], cfg["pitch"]
    f1 = f1_ref[...]

    def step(i, carry):
        cols = []
        for t in range(2):
            n = 2 * i + t
            cols.append(jnp.concatenate([kf_ref[pl.ds(n, nh, stride=n2), :],
                                         kb_ref[pl.ds(n2 - n, nh, stride=n2), :]], 0))
        r = _dot(f1, jnp.concatenate(cols, 1), HI)
        a_scr[pl.ds(2 * i, jp, stride=pitch), :] = r[:, :LANES]
        a_scr[pl.ds(2 * i + 1, jp, stride=pitch), :] = r[:, LANES:]
        return carry

    lax.fori_loop(0, n2 // 2, step, 0, unroll=min(16, n2 // 2))
    lag0 = pl.ds(0, jp, stride=pitch)
    a_scr[lag0, :] = a_scr[lag0, :] + f1[:, 0:1] * kb_ref[0:1, :]

    def mid(k, carry):
        _, a = _k1_rows(a_scr, k, cfg)
        o_ref[0, k] = _dot(m1_ref[k], a, HI).astype(BF16)
        return carry

    lax.fori_loop(0, cfg["k1n"], mid, 0)


def _hyena_spectrum(L, taps):
    c = HY_WIDTH
    cfg = _hy_cfg(L)
    tb = _dft_tables(L)
    n1, n2, nh, k1n = cfg["n1"], cfg["n2"], cfg["nh"], cfg["k1n"]
    f1 = np.concatenate([tb["f1"][:, :nh], tb["f1"][:, n1 - 1:nh - 1:-1]], 1)
    nct = c // LANES
    rows = taps.shape[0]
    return pl.pallas_call(
        functools.partial(_spectrum_body, cfg=cfg),
        grid=(2, nct),
        in_specs=[pl.BlockSpec((rows, LANES), lambda cv, j: (0, 2 * cv * nct + j)),
                  pl.BlockSpec((rows, LANES), lambda cv, j: (0, (2 * cv + 1) * nct + j)),
                  pl.BlockSpec(f1.shape, lambda cv, j: (0, 0)),
                  pl.BlockSpec(tb["m1"].shape, lambda cv, j: (0, 0, 0))],
        out_specs=pl.BlockSpec((1, k1n, 2 * n2, LANES), lambda cv, j: (cv, 0, 0, j)),
        out_shape=jax.ShapeDtypeStruct((2, k1n, 2 * n2, c), BF16),
        scratch_shapes=[pltpu.VMEM((cfg["jp"] * cfg["pitch"], LANES), F32)],
        compiler_params=_cp("arbitrary", "arbitrary"),
        name="hyena_spectrum",
    )(taps, taps, jnp.asarray(f1), jnp.asarray(tb["m1"]))


def _short_conv(u_ref, w_ref, b_ref, which):
    x = u_ref[0]
    L = x.shape[0]
    row = lax.broadcasted_iota(jnp.int32, x.shape, 0)
    prev = jnp.where(row == 0, 0.0, pltpu.roll(x, 1, 0))
    nxt = jnp.where(row == L - 1, 0.0, pltpu.roll(x, L - 1, 0))
    sel = slice(which, which + 1)
    return prev * w_ref[0, sel, :] + x * w_ref[1, sel, :] + nxt * w_ref[2, sel, :] + b_ref[sel, :]


def _long_conv(src_scr, a_scr, y_scr, w_scr, f1_ref, f6_ref, m1_ref, m2_ref, kf_ref, conv, cfg):
    n2, nh, jp, pitch = cfg["n2"], cfg["nh"], cfg["jp"], cfg["pitch"]
    _first_stage(src_scr, a_scr, f1_ref[...], nh, cfg, None)

    def forward(k, carry):
        _, a = _k1_rows(a_scr, k, cfg)
        x = _dot(m1_ref[k], a.astype(BF16))
        xr, xi = x[:n2], x[n2:]
        kk = kf_ref[conv, k].astype(F32)
        kr, ki = kk[:n2], kk[n2:]
        w_scr[k] = jnp.concatenate([xr * kr - xi * ki, xr * ki + xi * kr], 0).astype(BF16)
        return carry

    def inverse(k, carry):
        base = pl.multiple_of(2 * k * pitch, 8)
        b = _dot(m2_ref[k], w_scr[k])
        a_scr[pl.ds(base, n2), :] = b[:n2]
        a_scr[pl.ds(base + pitch, n2), :] = b[n2:]
        return carry

    lax.fori_loop(0, cfg["k1n"], forward, 0, unroll=3)
    lax.fori_loop(0, cfg["k1n"], inverse, 0, unroll=3)
    f6 = f6_ref[...]

    def last(i, carry):
        bs = jnp.concatenate([a_scr[pl.ds(2 * i, jp, stride=pitch), :],
                              a_scr[pl.ds(2 * i + 1, jp, stride=pitch), :]], 1).astype(BF16)
        y = _dot(f6, bs)
        y_scr[pl.ds(2 * i, nh, stride=n2), :] = y[:, :LANES]
        y_scr[pl.ds(2 * i + 1, nh, stride=n2), :] = y[:, LANES:]
        return carry

    lax.fori_loop(0, n2 // 2, last, 0, unroll=min(16, n2 // 2))


def _hyena_body(v_ref, x1_ref, x2_ref, g_ref, cw_ref, cb_ref, hb_ref, f1_ref, f6_ref, m1_ref, m2_ref, kf_ref,
                o_ref, s_scr, a_scr, y_scr, w_scr, *, cfg):
    tabs = (w_scr, f1_ref, f6_ref, m1_ref, m2_ref, kf_ref)
    s_scr[...] = _short_conv(v_ref, cw_ref, cb_ref, 0)
    _long_conv(s_scr, a_scr, y_scr, *tabs, 0, cfg)
    s_scr[...] = _short_conv(x1_ref, cw_ref, cb_ref, 1) * (y_scr[...] + s_scr[...] * hb_ref[0:1, :])
    _long_conv(s_scr, a_scr, y_scr, *tabs, 1, cfg)
    y = _short_conv(x2_ref, cw_ref, cb_ref, 2) * (y_scr[...] + s_scr[...] * hb_ref[1:2, :])
    o_ref[0] = y * _silu(g_ref[0])


def _hyena(u_hy3, u_gate, conv_w, conv_b, kf, hy_bias):
    b, L, _ = u_gate.shape
    c = HY_WIDTH
    cfg = _hy_cfg(L)
    tb = _dft_tables(L)
    nct = c // LANES
    col = lambda off: pl.BlockSpec((1, L, LANES), lambda j, bi: (bi, 0, off * nct + j))
    full = lambda a: pl.BlockSpec(a.shape, lambda j, bi: (0,) * a.ndim)
    f1 = jnp.asarray(tb["f1"][:, :cfg["nh"]]).astype(BF16)
    f6 = jnp.asarray(tb["f6"]).astype(BF16)
    m1 = jnp.asarray(tb["m1"]).astype(BF16)
    m2 = jnp.asarray(tb["m2"]).astype(BF16)
    return pl.pallas_call(
        functools.partial(_hyena_body, cfg=cfg),
        grid=(nct, b),
        in_specs=[col(0), col(1), col(2), col(0),
                  pl.BlockSpec((3, 3, LANES), lambda j, bi: (0, 0, j)),
                  pl.BlockSpec((3, LANES), lambda j, bi: (0, j)),
                  pl.BlockSpec((2, LANES), lambda j, bi: (0, j)),
                  full(f1), full(f6), full(m1), full(m2),
                  pl.BlockSpec((2, cfg["k1n"], 2 * cfg["n2"], LANES), lambda j, bi: (0, 0, 0, j))],
        out_specs=col(0),
        out_shape=jax.ShapeDtypeStruct((b, L, c), F32),
        scratch_shapes=[pltpu.VMEM((L, LANES), F32),
                        pltpu.VMEM((cfg["jp"] * cfg["pitch"], LANES), F32),
                        pltpu.VMEM((L, LANES), F32),
                        pltpu.VMEM((cfg["k1n"], 2 * cfg["n2"], LANES), BF16)],
        compiler_params=_cp("arbitrary", "arbitrary"),
        name="hyena",
    )(u_hy3, u_hy3, u_hy3, u_gate, conv_w.reshape(3, 3, c), conv_b.reshape(3, c), hy_bias, f1, f6, m1, m2, kf)


SSD_CPS = 2
SPLIT_STRIDE = 2 * SSM_HEADS


def _pack3(x):
    hi = x.astype(BF16).astype(F32)
    r1 = x - hi
    mid = r1.astype(BF16).astype(F32)
    lo = (r1 - mid).astype(BF16).astype(F32)
    return (hi + pltpu.roll(mid, SPLIT_STRIDE, 1) + pltpu.roll(lo, 2 * SPLIT_STRIDE, 1)).astype(BF16)


def _unpack3(x3, used):
    return jnp.where(used, x3 + pltpu.roll(x3, LANES - SPLIT_STRIDE, 1) + pltpu.roll(x3, LANES - 2 * SPLIT_STRIDE, 1), 0.0)


@functools.lru_cache(maxsize=None)
def _ssd_spread_tables():
    col = np.zeros((LANES, 2 * SSM_HEADS * LANES), np.float32)
    head = np.zeros((LANES, 2 * SSM_WIDTH), np.float32)
    for c in range(2 * SSM_HEADS):
        d, h = divmod(c, SSM_HEADS)
        for piece in range(3):
            col[c + piece * SPLIT_STRIDE, c * LANES:(c + 1) * LANES] = 1.0
            lo = d * SSM_WIDTH + h * SSM_HEADDIM
            head[c + piece * SPLIT_STRIDE, lo:lo + SSM_HEADDIM] = 1.0
    return col, head


def _ssd_chunk_body(xbc_ref, dtr_ref, dtb_ref, a_ref, ecol_ref, ehead_ref, yd_ref, cs_ref, ex_ref, et_ref):
    q = SSM_CHUNK
    hpg = SSM_HEADS // SSM_GROUPS
    gw = SSM_WIDTH // SSM_GROUPS
    li = lax.broadcasted_iota(jnp.int32, (q, q), 0)
    si = lax.broadcasted_iota(jnp.int32, (q, q), 1)
    below = li > si
    diag = li == si
    fwd_lane = si < SSM_HEADS
    used = si < 2 * SSM_HEADS
    tril = (li >= si).astype(BF16)
    triu = (li <= si).astype(BF16)
    chunks = range(SSD_CPS)
    rows = [slice(c * q, (c + 1) * q) for c in chunks]
    ehead = ehead_ref[...]

    raws = [dtr_ref[0, r, :] + dtb_ref[...] for r in rows]
    dts = [jnp.where(used, jnp.maximum(x, 0.0) + jnp.log1p(jnp.exp(-jnp.abs(x))), 0.0) for x in raws]
    da3 = [_pack3(dt * a_ref[...]) for dt in dts]
    acs = [jnp.where(fwd_lane, _unpack3(_dot(tril, x), used), _unpack3(_dot(triu, x), used)) for x in da3]
    tots = [jnp.where(fwd_lane[0:1], a[q - 1:q, :], a[0:1, :]) for a in acs]
    ws = [dt * jnp.exp(t - a) for dt, t, a in zip(dts, tots, acs)]
    acs3 = [_pack3(a) for a in acs]
    colb = [_dot(x, ecol_ref[...]) for x in acs3]
    wx = [_dot(_pack3(w), ehead) for w in ws]
    for c in chunks:
        ex_ref[0, rows[c], :] = jnp.exp(_dot(acs3[c], ehead)).astype(BF16)
        et_ref[0, c] = jnp.exp(_dot(_pack3(jnp.broadcast_to(tots[c], (8, LANES))), ehead))
    rowt = [(a - jnp.where(dt > 0.0, jnp.log(dt), -BIG)).T for a, dt in zip(acs, dts)]
    dsum = [(dt + pltpu.roll(dt, LANES - SSM_HEADS, 1)).T for dt in dts]
    xbc = [xbc_ref[0, r, :] for r in rows]
    xsb = [x[:, :SSM_WIDTH].astype(BF16) for x in xbc]
    bgs = [[x[:, SSM_WIDTH + g * SSM_STATE:SSM_WIDTH + (g + 1) * SSM_STATE] for g in range(SSM_GROUPS)] for x in xbc]
    cgs = [[x[:, SSM_WIDTH + (SSM_GROUPS + g) * SSM_STATE:SSM_WIDTH + (SSM_GROUPS + g + 1) * SSM_STATE]
            for g in range(SSM_GROUPS)] for x in xbc]
    gmat = [[_dot_t(cgs[c][g].astype(BF16), bgs[c][g].astype(BF16)) for g in range(SSM_GROUPS)] for c in chunks]
    mats = []
    for c in chunks:
        for h in range(SSM_HEADS):
            hb = SSM_HEADS + h
            arg = jnp.where(below, colb[c][:, h * q:(h + 1) * q] - rowt[c][h:h + 1, :],
                            colb[c][:, hb * q:(hb + 1) * q] - rowt[c][hb:hb + 1, :])
            dec = jnp.where(diag, dsum[c][h:h + 1, :], jnp.exp(arg))
            mats.append((gmat[c][h // hpg] * dec).astype(BF16))
    for c in chunks:
        for h in range(SSM_HEADS):
            lo = h * SSM_HEADDIM
            yd_ref[0, rows[c], lo:lo + SSM_HEADDIM] = _dot(mats[c * SSM_HEADS + h], xsb[c][:, lo:lo + SSM_HEADDIM])
    for c in chunks:
        for g in range(SSM_GROUPS):
            bgt = bgs[c][g].T.astype(BF16)
            xg = xbc[c][:, g * gw:(g + 1) * gw]
            for d in range(2):
                lo = d * SSM_WIDTH + g * gw
                cs_ref[0, c, d, g] = _dot(bgt, (xg * wx[c][:, lo:lo + gw]).astype(BF16))


def _ssd_chunks(xbc, dt_raw, dt_bias_row, a_row):
    b, L, _ = xbc.shape
    nc = L // SSM_CHUNK
    rows = SSD_CPS * SSM_CHUNK
    gw = SSM_WIDTH // SSM_GROUPS
    blk = lambda bi, i: (bi, i, 0)
    full2 = lambda bi, i: (0, 0)
    ecol, ehead = (jnp.asarray(t).astype(BF16) for t in _ssd_spread_tables())
    return pl.pallas_call(
        _ssd_chunk_body,
        grid=(b, nc // SSD_CPS),
        in_specs=[pl.BlockSpec((1, rows, SSM_CONV_DIM), blk), pl.BlockSpec((1, rows, LANES), blk),
                  pl.BlockSpec((1, LANES), full2), pl.BlockSpec((1, LANES), full2),
                  pl.BlockSpec(ecol.shape, full2), pl.BlockSpec(ehead.shape, full2)],
        out_specs=[pl.BlockSpec((1, rows, SSM_WIDTH), blk),
                   pl.BlockSpec((1, SSD_CPS, 2, SSM_GROUPS, SSM_STATE, gw), lambda bi, i: (bi, i, 0, 0, 0, 0)),
                   pl.BlockSpec((1, rows, 2 * SSM_WIDTH), blk),
                   pl.BlockSpec((1, SSD_CPS, 8, 2 * SSM_WIDTH), lambda bi, i: (bi, i, 0, 0))],
        out_shape=[jax.ShapeDtypeStruct((b, L, SSM_WIDTH), F32),
                   jax.ShapeDtypeStruct((b, nc, 2, SSM_GROUPS, SSM_STATE, gw), F32),
                   jax.ShapeDtypeStruct((b, L, 2 * SSM_WIDTH), BF16),
                   jax.ShapeDtypeStruct((b, nc, 8, 2 * SSM_WIDTH), F32)],
        compiler_params=_cp("parallel", "arbitrary"),
        name="ssd_chunks",
    )(xbc, dt_raw, dt_bias_row, a_row, ecol, ehead)


def _ssd_state_body(cf_ref, cb_ref, xf_ref, xb_ref, ef_ref, eb_ref, sf_ref, sb_ref, init_ref, yf_ref, yb_ref, fin_ref,
                    st_ref, *, nc):
    ci = pl.program_id(1)

    @pl.when(ci == 0)
    def _():
        st_ref[...] = init_ref[0]

    gw = SSM_WIDTH // SSM_GROUPS
    dirs = ((cf_ref, xf_ref, ef_ref, sf_ref, yf_ref), (cb_ref, xb_ref, eb_ref, sb_ref, yb_ref))
    for d, (c_ref, x_ref, e_ref, s_ref, y_ref) in enumerate(dirs):
        cmat = c_ref[0].astype(BF16)
        for g in range(SSM_GROUPS):
            cols = slice(g * gw, (g + 1) * gw)
            st = st_ref[d, g]
            y_ref[0, :, cols] = (_dot(cmat[:, g * SSM_STATE:(g + 1) * SSM_STATE], st.astype(BF16))
                                 * x_ref[0, :, cols].astype(F32))
            st_ref[d, g] = st * e_ref[0, 0, 0:1, cols] + s_ref[0, 0, 0, g]

    @pl.when(ci == nc - 1)
    def _():
        fin_ref[0] = st_ref[...]


def _ssd_states(xbc, ex, et, cs, init):
    b, L, _ = xbc.shape
    nc = L // SSM_CHUNK
    gw = SSM_WIDTH // SSM_GROUPS
    c_col = SSM_CONV_DIM // (SSM_GROUPS * SSM_STATE) - 1
    st_shape = (2, SSM_GROUPS, SSM_STATE, gw)
    st_spec = pl.BlockSpec((1,) + st_shape, lambda bi, c: (bi, 0, 0, 0, 0))
    cs_blk = (1, 1, 1, SSM_GROUPS, SSM_STATE, gw)
    return pl.pallas_call(
        functools.partial(_ssd_state_body, nc=nc),
        grid=(b, nc),
        in_specs=[pl.BlockSpec((1, SSM_CHUNK, SSM_GROUPS * SSM_STATE), lambda bi, c: (bi, c, c_col)),
                  pl.BlockSpec((1, SSM_CHUNK, SSM_GROUPS * SSM_STATE), lambda bi, c: (bi, nc - 1 - c, c_col)),
                  pl.BlockSpec((1, SSM_CHUNK, SSM_WIDTH), lambda bi, c: (bi, c, 0)),
                  pl.BlockSpec((1, SSM_CHUNK, SSM_WIDTH), lambda bi, c: (bi, nc - 1 - c, 1)),
                  pl.BlockSpec((1, 1, 8, SSM_WIDTH), lambda bi, c: (bi, c, 0, 0)),
                  pl.BlockSpec((1, 1, 8, SSM_WIDTH), lambda bi, c: (bi, nc - 1 - c, 0, 1)),
                  pl.BlockSpec(cs_blk, lambda bi, c: (bi, c, 0, 0, 0, 0)),
                  pl.BlockSpec(cs_blk, lambda bi, c: (bi, nc - 1 - c, 1, 0, 0, 0)),
                  st_spec],
        out_specs=[pl.BlockSpec((1, SSM_CHUNK, SSM_WIDTH), lambda bi, c: (bi, c, 0)),
                   pl.BlockSpec((1, SSM_CHUNK, SSM_WIDTH), lambda bi, c: (bi, nc - 1 - c, 0)),
                   st_spec],
        out_shape=[jax.ShapeDtypeStruct((b, L, SSM_WIDTH), F32), jax.ShapeDtypeStruct((b, L, SSM_WIDTH), F32),
                   jax.ShapeDtypeStruct((b,) + st_shape, F32)],
        scratch_shapes=[pltpu.VMEM(st_shape, F32)],
        compiler_params=_cp("parallel", "arbitrary"),
        name="ssd_states",
    )(xbc, xbc, ex, ex, et, et, cs, cs, init)


def _ssd(xbc, dt_raw, dt_bias_row, a_row, init):
    y_diag, cs, ex, et = _ssd_chunks(xbc, dt_raw, dt_bias_row, a_row)
    y_f, y_b, fin = _ssd_states(xbc, ex, et, cs, init)
    return (y_diag, y_f, y_b), fin


def _ssd_body(xbc_ref, dtr_ref, dtb_ref, a_ref, e_ref, init_ref, y_ref, fin_ref, st_ref, *, reverse, nc, d):
    ci = pl.program_id(1)

    @pl.when(ci == 0)
    def _():
        st_ref[...] = init_ref[0]

    q = SSM_CHUNK
    gw = SSM_WIDTH // SSM_GROUPS
    hpg = SSM_HEADS // SSM_GROUPS
    xbc = xbc_ref[0]
    xs = xbc[:, :SSM_WIDTH]
    raw = dtr_ref[0] + dtb_ref[...]
    dt = jnp.maximum(raw, 0.0) + jnp.log1p(jnp.exp(-jnp.abs(raw)))
    da = dt * a_ref[...]
    li = lax.broadcasted_iota(jnp.int32, (q, q), 0)
    si = lax.broadcasted_iota(jnp.int32, (q, q), 1)
    mask = (li <= si) if reverse else (li >= si)
    acs = _dot(mask.astype(F32), da, HI)
    acs_t = acs.T
    e = e_ref[...]
    dtx = _dot(dt, e, HI)
    acsx = _dot(acs, e, HI)
    last = 0 if reverse else q - 1
    totx = acsx[last:last + 1, :]
    xd = xs * dtx
    xde = (xd * jnp.exp(totx - acsx)).astype(BF16)
    xdb = xd.astype(BF16)
    eacs = jnp.exp(acsx)
    etot = jnp.exp(totx)
    for g in range(SSM_GROUPS):
        bg = xbc[:, SSM_WIDTH + g * SSM_STATE:SSM_WIDTH + (g + 1) * SSM_STATE]
        cg = xbc[:, SSM_WIDTH + (SSM_GROUPS + g) * SSM_STATE:SSM_WIDTH + (SSM_GROUPS + g + 1) * SSM_STATE]
        cgb = cg.astype(BF16)
        gmat = _dot_t(cgb, bg.astype(BF16))
        st = st_ref[g]
        y_off = _dot(cgb, st.astype(BF16)) * eacs[:, g * gw:(g + 1) * gw]
        for j in range(hpg):
            h = g * hpg + j
            col = d * SSM_HEADS + h
            seg = acs[:, col:col + 1] - acs_t[col:col + 1, :]
            lm = jnp.where(mask, jnp.exp(seg), 0.0)
            lo = h * SSM_HEADDIM
            yd = _dot((gmat * lm).astype(BF16), xdb[:, lo:lo + SSM_HEADDIM])
            y_ref[0, :, lo:lo + SSM_HEADDIM] = yd + y_off[:, j * SSM_HEADDIM:(j + 1) * SSM_HEADDIM]
        st_ref[g] = st * etot[:, g * gw:(g + 1) * gw] + _dot(bg.T.astype(BF16), xde[:, g * gw:(g + 1) * gw])

    @pl.when(ci == nc - 1)
    def _():
        fin_ref[0] = st_ref[...]


def _ssd_scan(xbc, dt_raw, dt_bias_row, a_row, expand, init, *, d):
    b, L, _ = xbc.shape
    nc = L // SSM_CHUNK
    reverse = d == 1
    chunk = (lambda bi, c: (bi, nc - 1 - c, 0)) if reverse else (lambda bi, c: (bi, c, 0))
    full2 = lambda bi, c: (0, 0)
    st_shape = (SSM_GROUPS, SSM_STATE, SSM_WIDTH // SSM_GROUPS)
    st_spec = pl.BlockSpec((1,) + st_shape, lambda bi, c: (bi, 0, 0, 0))
    return pl.pallas_call(
        functools.partial(_ssd_body, reverse=reverse, nc=nc, d=d),
        grid=(b, nc),
        in_specs=[pl.BlockSpec((1, SSM_CHUNK, SSM_CONV_DIM), chunk),
                  pl.BlockSpec((1, SSM_CHUNK, LANES), chunk),
                  pl.BlockSpec((1, LANES), full2), pl.BlockSpec((1, LANES), full2),
                  pl.BlockSpec((LANES, SSM_WIDTH), full2), st_spec],
        out_specs=[pl.BlockSpec((1, SSM_CHUNK, SSM_WIDTH), chunk), st_spec],
        out_shape=[jax.ShapeDtypeStruct((b, L, SSM_WIDTH), F32),
                   jax.ShapeDtypeStruct((b,) + st_shape, F32)],
        scratch_shapes=[pltpu.VMEM(st_shape, F32)],
        compiler_params=_cp("parallel", "arbitrary"),
        name="ssd_scan",
    )(xbc, dt_raw, dt_bias_row, a_row, expand, init)


def _ssd_expand(d):
    e = np.zeros((LANES, SSM_WIDTH), np.float32)
    for h in range(SSM_HEADS):
        e[d * SSM_HEADS + h, h * SSM_HEADDIM:(h + 1) * SSM_HEADDIM] = 1.0
    return e


def _rope_tables(L):
    rows = L // GRID_W
    row = jnp.broadcast_to(jnp.arange(rows)[:, None], (rows, GRID_W)).reshape(L)
    col = jnp.broadcast_to(jnp.arange(GRID_W)[None, :], (rows, GRID_W)).reshape(L)
    nf = ATT_HEADDIM // 4
    inv = ROPE_BASE ** (-jnp.arange(nf, dtype=F32) / nf)
    ar = row.astype(F32)[:, None] * inv
    ac = col.astype(F32)[:, None] * inv
    cos = jnp.concatenate([jnp.cos(ar), jnp.cos(ar), jnp.cos(ac), jnp.cos(ac)], -1)
    sin = jnp.concatenate([-jnp.sin(ar), jnp.sin(ar), -jnp.sin(ac), jnp.sin(ac)], -1)
    return jnp.tile(cos, (1, ATT_HEADS)), jnp.tile(sin, (1, ATT_HEADS))


def _rope(x, cos, sin):
    w = x.shape[-1]
    quarter = ATT_HEADDIM // 4
    lane = lax.broadcasted_iota(jnp.int32, x.shape, x.ndim - 1)
    partner = jnp.where((lane // quarter) % 2 == 0, pltpu.roll(x, w - quarter, x.ndim - 1),
                        pltpu.roll(x, quarter, x.ndim - 1))
    return x * cos + partner * sin


def _rope_body(q_ref, kv_ref, cos_ref, sin_ref, qo_ref, ko_ref):
    cos = cos_ref[...]
    sin = sin_ref[...]
    qo_ref[0] = _rope(q_ref[0], cos, sin)
    ko_ref[0] = _rope(kv_ref[0], cos[:, :ATT_KV], sin[:, :ATT_KV])


def _apply_rope(u_q, u_kv, cos, sin):
    b, L, _ = u_q.shape
    tl = 512
    row = lambda i, bi: (bi, i, 0)
    tab = lambda i, bi: (i, 0)
    return pl.pallas_call(
        _rope_body,
        grid=(L // tl, b),
        in_specs=[pl.BlockSpec((1, tl, ATT_WIDTH), row), pl.BlockSpec((1, tl, ATT_KV), row),
                  pl.BlockSpec((tl, ATT_WIDTH), tab), pl.BlockSpec((tl, ATT_WIDTH), tab)],
        out_specs=[pl.BlockSpec((1, tl, ATT_WIDTH), row), pl.BlockSpec((1, tl, ATT_KV), row)],
        out_shape=[jax.ShapeDtypeStruct((b, L, ATT_WIDTH), F32), jax.ShapeDtypeStruct((b, L, ATT_KV), F32)],
        compiler_params=_cp("parallel", "arbitrary"),
        name="rope",
    )(u_q, u_kv, cos, sin)


def _wattn_body(sink_ref, bias_ref, q_ref, kp_ref, kc_ref, kn_ref, vp_ref, vc_ref, vn_ref, kx_ref, vx_ref, z_ref,
                o_ref):
    hd = ATT_HEADDIM
    low_half = lax.broadcasted_iota(jnp.int32, (WINDOW, 2 * hd), 1) < hd
    q = q_ref[0] * (hd ** -0.5 * LOG2E)
    bias = bias_ref[0]
    z = z_ref[0]
    k_all, v_ext = [], []
    for g in range(ATT_KV_HEADS):
        ks = slice(g * hd, (g + 1) * hd)
        k_all.append(jnp.concatenate([r[0, :, ks] for r in (kp_ref, kc_ref, kn_ref, kx_ref)], 0).astype(BF16))
        v_all = jnp.concatenate([r[0, :, ks] for r in (vp_ref, vc_ref, vn_ref, vx_ref)], 0).astype(BF16)
        ones = jnp.ones_like(v_all)
        v_ext.append((jnp.concatenate([v_all, ones], 1), jnp.concatenate([ones, v_all], 1)))
    heads = range(ATT_HEADS)
    sinks = [sink_ref[h] * LOG2E for h in heads]
    scores = [_dot_t(q[:, h * hd:(h + 1) * hd].astype(BF16), k_all[h // ATT_GROUP]) + bias for h in heads]
    maxes = [jnp.maximum(jnp.max(s, -1, keepdims=True), sk) for s, sk in zip(scores, sinks)]
    probs = [jnp.exp2(s - m).astype(BF16) for s, m in zip(scores, maxes)]
    exts = [_dot(p, v_ext[h // ATT_GROUP][h % 2]) for h, p in zip(heads, probs)]
    outs = [e / (pltpu.roll(e, hd, 1) + jnp.exp2(sk - m)) for e, sk, m in zip(exts, sinks, maxes)]
    for pair in range(ATT_HEADS // 2):
        cs = slice(2 * pair * hd, (2 * pair + 2) * hd)
        o_ref[0, :, cs] = jnp.where(low_half, outs[2 * pair], outs[2 * pair + 1]) * _silu(z[:, cs])


def _window_attention(q_rot, k_rot, u_kv, uc_kv, sinks, z_a):
    b, L, _ = q_rot.shape
    lc = uc_kv.shape[1]
    nb = L // WINDOW
    hd2 = ATT_KV
    cur = lambda bi, i: (bi, i, 0)
    prv = lambda bi, i: (bi, jnp.maximum(i - 1, 0), 0)
    nxt = lambda bi, i: (bi, jnp.minimum(i + 1, nb - 1), 0)
    vcur = lambda bi, i: (bi, i, 1)
    vprv = lambda bi, i: (bi, jnp.maximum(i - 1, 0), 1)
    vnxt = lambda bi, i: (bi, jnp.minimum(i + 1, nb - 1), 1)
    kblk = lambda f: pl.BlockSpec((1, WINDOW, hd2), f)
    nk = 3 * WINDOW + lc
    row = np.arange(WINDOW)[:, None]
    col = np.arange(nk)[None, :]
    in_prev = (col < WINDOW) & (col >= row)
    in_next = (col >= 2 * WINDOW) & (col < 3 * WINDOW) & (col - 2 * WINDOW <= row)
    always = ((col >= WINDOW) & (col < 2 * WINDOW)) | (col >= 3 * WINDOW)
    kinds = [always | in_next, always | in_prev | in_next, always | in_prev]
    if nb == 1:
        kinds = [always] * 3
    bias = jnp.asarray(np.where(np.stack(kinds), 0.0, NEG).astype(np.float32))
    kind = lambda bi, i: (jnp.where(i == 0, 0, jnp.where(i == nb - 1, 2, 1)), 0, 0)
    return pl.pallas_call(
        _wattn_body,
        grid=(b, nb),
        in_specs=[pl.BlockSpec(memory_space=pltpu.SMEM),
                  pl.BlockSpec((1, WINDOW, nk), kind),
                  pl.BlockSpec((1, WINDOW, ATT_WIDTH), cur),
                  kblk(prv), kblk(cur), kblk(nxt), kblk(vprv), kblk(vcur), kblk(vnxt),
                  pl.BlockSpec((1, lc, hd2), lambda bi, i: (bi, 0, 0)),
                  pl.BlockSpec((1, lc, hd2), lambda bi, i: (bi, 0, 1)),
                  pl.BlockSpec((1, WINDOW, ATT_WIDTH), cur)],
        out_specs=pl.BlockSpec((1, WINDOW, ATT_WIDTH), cur),
        out_shape=jax.ShapeDtypeStruct((b, L, ATT_WIDTH), F32),
        compiler_params=_cp("parallel", "arbitrary"),
        name="window_attention",
    )(sinks, bias, q_rot, k_rot, k_rot, k_rot, u_kv, u_kv, u_kv, uc_kv, uc_kv, z_a)


def _cattn_body(sink_ref, q_ref, k_ref, v_ref, z_ref, o_ref):
    scale = ATT_HEADDIM ** -0.5
    q = q_ref[0]
    z = z_ref[0]
    for g in range(ATT_KV_HEADS):
        ks = slice(g * ATT_HEADDIM, (g + 1) * ATT_HEADDIM)
        k = k_ref[0, :, ks].astype(BF16)
        v = v_ref[0, :, ks].astype(BF16)
        for j in range(ATT_GROUP):
            h = g * ATT_GROUP + j
            hs = slice(h * ATT_HEADDIM, (h + 1) * ATT_HEADDIM)
            s = _dot_t(q[:, hs].astype(BF16), k) * scale
            sink = sink_ref[h]
            m = jnp.maximum(jnp.max(s, -1, keepdims=True), sink)
            p = jnp.exp(s - m)
            den = jnp.sum(p, -1, keepdims=True) + jnp.exp(sink - m)
            o_ref[0, :, hs] = _dot(p.astype(BF16), v) / den * _silu(z[:, hs])


def _ctx_attention(uc_q, uc_kv, sinks, z_ac):
    b, lc, _ = uc_q.shape
    blk = lambda bi: (bi, 0, 0)
    return pl.pallas_call(
        _cattn_body,
        grid=(b,),
        in_specs=[pl.BlockSpec(memory_space=pltpu.SMEM),
                  pl.BlockSpec((1, lc, ATT_WIDTH), blk),
                  pl.BlockSpec((1, lc, ATT_KV), lambda bi: (bi, 0, 0)),
                  pl.BlockSpec((1, lc, ATT_KV), lambda bi: (bi, 0, 1)),
                  pl.BlockSpec((1, lc, ATT_WIDTH), blk)],
        out_specs=pl.BlockSpec((1, lc, ATT_WIDTH), blk),
        out_shape=jax.ShapeDtypeStruct((b, lc, ATT_WIDTH), F32),
        compiler_params=_cp("parallel"),
        name="ctx_attention",
    )(sinks, uc_q, uc_kv, uc_kv, z_ac)


def _out_body(h_ref, g_ref, yhy_ref, yd_ref, yf_ref, yb_ref, xs_ref, zs_ref, yat_ref, dsk_ref, nw_ref, w_ref,
              lg_ref, lb_ref, o_ref):
    gw = SSM_WIDTH // SSM_GROUPS
    ys = (yd_ref[0] + yf_ref[0] + yb_ref[0] + xs_ref[0] * dsk_ref[...]) * _silu(zs_ref[0])
    acc = _dot(yhy_ref[0].astype(BF16), w_ref[0:HY_WIDTH, :])
    for g in range(SSM_GROUPS):
        seg = ys[:, g * gw:(g + 1) * gw]
        seg = seg * lax.rsqrt(jnp.mean(seg * seg, -1, keepdims=True) + RMS_EPS) * nw_ref[:, g * gw:(g + 1) * gw]
        lo = HY_WIDTH + g * gw
        acc = acc + _dot(seg.astype(BF16), w_ref[lo:lo + gw, :])
    acc = acc + _dot(yat_ref[0].astype(BF16), w_ref[HY_WIDTH + SSM_WIDTH:, :])
    r = DEEPNORM_ALPHA * h_ref[0] + g_ref[0] * acc
    mu = jnp.mean(r, -1, keepdims=True)
    rc = r - mu
    var = jnp.mean(rc * rc, -1, keepdims=True)
    o_ref[0] = rc * lax.rsqrt(var + LN_EPS) * lg_ref[...] + lb_ref[...]


def _out_projection(h, gate_mod, y_hy, y_ssd, xbc, z_s, y_at, d_skip, norm_w, w_out, ln_g, ln_b):
    b, L, d = h.shape
    tm = 256
    row = lambda bi, i: (bi, i, 0)
    vec = lambda bi, i: (bi, 0, 0)
    full = lambda bi, i: (0, 0)
    w512 = pl.BlockSpec((1, tm, SSM_WIDTH), row)
    return pl.pallas_call(
        _out_body,
        grid=(b, L // tm),
        in_specs=[pl.BlockSpec((1, tm, d), row), pl.BlockSpec((1, 1, d), vec),
                  w512, w512, w512, w512, w512, w512, w512,
                  pl.BlockSpec((1, SSM_WIDTH), full), pl.BlockSpec((1, SSM_WIDTH), full),
                  pl.BlockSpec(w_out.shape, full), pl.BlockSpec((1, d), full), pl.BlockSpec((1, d), full)],
        out_specs=pl.BlockSpec((1, tm, d), row),
        out_shape=jax.ShapeDtypeStruct((b, L, d), F32),
        compiler_params=_cp("parallel", "arbitrary"),
        name="out_projection",
    )(h, gate_mod, y_hy, *y_ssd, xbc, z_s, y_at, d_skip, norm_w, w_out.astype(BF16),
      ln_g.reshape(1, d), ln_b.reshape(1, d))


def _sequence_front(h, shift, scale, w_packed, ssm_conv_w, ssm_conv_b):
    u_hy3, u_hyg, u_xbc, u_zs, u_q, u_kv, u_za, u_dt = _in_projection(h, shift, scale, w_packed)
    xbc = _dwconv(u_xbc, ssm_conv_w, ssm_conv_b, act=True, split=SSM_CONV_DIM)[0]
    return dict(hy3=u_hy3, hy_gate=u_hyg, xbc=xbc, z_s=u_zs, q=u_q, kv=u_kv, z_a=u_za, dt=u_dt)


def kernel(x, c, ctx, c_ctx, w_mod, b_mod, w_in, hy_conv_w, hy_conv_b, hy_f_w1, hy_f_b1, hy_f_w2, hy_f_b2,
           hy_f_w3, hy_f_b3, hy_f_freq, hy_f_wout, hy_bias, ssm_conv_w, ssm_conv_b, ssm_dt_bias, ssm_a_log,
           ssm_d, ssm_norm_w, attn_sinks, w_out, ln_g, ln_b):
    b, L, d = x.shape
    lc = ctx.shape[1]
    cos, sin = _rope_tables(L)
    cc = jnp.concatenate([c, c_ctx[None], jnp.zeros((16 - b - 1, d), F32)], 0)
    zero_state = jnp.zeros((b, 2, SSM_GROUPS, SSM_STATE, SSM_WIDTH // SSM_GROUPS), F32)
    h_lat, h_ctx = x, ctx
    for i in range(DEPTH):
        ctx_needed = i < DEPTH - 1
        mod = _modulation(cc, w_mod[i], b_mod[i])
        sh, sc, g = (mod[:b, None, j * d:(j + 1) * d] for j in range(3))
        sh_c, sc_c, g_c = (jnp.broadcast_to(mod[b:b + 1, None, j * d:(j + 1) * d], (b, 1, d)) for j in range(3))
        w_packed = _pack_w_in(w_in[i])
        lat = _sequence_front(h_lat, sh, sc, w_packed, ssm_conv_w[i], ssm_conv_b[i])
        cx = _sequence_front(h_ctx, sh_c, sc_c, w_packed, ssm_conv_w[i], ssm_conv_b[i])

        dt_bias_row = jnp.pad(ssm_dt_bias[i].reshape(1, -1), ((0, 0), (0, LANES - 2 * SSM_HEADS)))
        a_row = jnp.pad(-jnp.exp(ssm_a_log[i]).reshape(1, -1), ((0, 0), (0, LANES - 2 * SSM_HEADS)))
        ys_c, s_c = _ssd(cx["xbc"], cx["dt"], dt_bias_row, a_row, zero_state)
        ys, _ = _ssd(lat["xbc"], lat["dt"], dt_bias_row, a_row, s_c)

        filt = (hy_f_w1[i], hy_f_b1[i], hy_f_w2[i], hy_f_b2[i], hy_f_w3[i], hy_f_b3[i], hy_f_freq[i], hy_f_wout[i])
        kf = _hyena_spectrum(L, _hyena_filter_taps(L, *filt))
        y_hy = _hyena(lat["hy3"], lat["hy_gate"], hy_conv_w[i], hy_conv_b[i], kf, hy_bias[i])

        q_rot, k_rot = _apply_rope(lat["q"], lat["kv"], cos, sin)
        y_at = _window_attention(q_rot, k_rot, lat["kv"], cx["kv"], attn_sinks[i], lat["z_a"])

        d_skip = jnp.repeat(ssm_d[i], SSM_HEADDIM).reshape(1, SSM_WIDTH)
        norm_w = ssm_norm_w[i].reshape(1, SSM_WIDTH)
        new_lat = _out_projection(h_lat, g, y_hy, ys, lat["xbc"], lat["z_s"], y_at, d_skip, norm_w,
                                  w_out[i], ln_g[i], ln_b[i])
        if ctx_needed:
            kf_c = _hyena_spectrum(lc, _hyena_filter_taps(lc, *filt))
            y_hy_c = _hyena(cx["hy3"], cx["hy_gate"], hy_conv_w[i], hy_conv_b[i], kf_c, hy_bias[i])
            y_at_c = _ctx_attention(cx["q"], cx["kv"], attn_sinks[i], cx["z_a"])
            h_ctx = _out_projection(h_ctx, g_c, y_hy_c, ys_c, cx["xbc"], cx["z_s"], y_at_c, d_skip,
                                    norm_w, w_out[i], ln_g[i], ln_b[i])
        h_lat = new_lat
    return h_lat
```

```python
import functools
import math

import numpy as np
import jax
import jax.numpy as jnp
from jax import lax
from jax.experimental import pallas as pl
from jax.experimental.pallas import tpu as pltpu

F32 = jnp.float32
BF16 = jnp.bfloat16
HI = lax.Precision.HIGHEST

D_MODEL = 1024
DEPTH = 2
GRID_W = 64
HY_WIDTH = 512
HY_BANDS = 16
HY_EMB = 1 + 2 * HY_BANDS
HY_FILTER_HIDDEN = 64
HY_DECAY_TARGET = 1e-2
HY_FAST_PCT = 0.3
HY_SLOW_PCT = 1.5
SSM_WIDTH = 512
SSM_HEADS = 8
SSM_HEADDIM = 64
SSM_GROUPS = 2
SSM_STATE = 128
SSM_CHUNK = 128
SSM_CONV_DIM = SSM_WIDTH + 2 * SSM_GROUPS * SSM_STATE
ATT_WIDTH = 512
ATT_HEADS = 8
ATT_KV_HEADS = 2
ATT_HEADDIM = 64
ATT_GROUP = ATT_HEADS // ATT_KV_HEADS
ATT_KV = ATT_KV_HEADS * ATT_HEADDIM
WINDOW = 128
ROPE_BASE = 10000.0
HY_IN = 4 * HY_WIDTH
SSM_IN = SSM_CONV_DIM + SSM_WIDTH + 2 * SSM_HEADS
DEEPNORM_ALPHA = (2 * DEPTH) ** 0.25
LN_EPS = 1e-6
RMS_EPS = 1e-5

LANES = 128
VMEM_LIMIT = 56 * 1024 * 1024
NEG = -1e30
BIG = 1e30
LOG2E = math.log2(math.e)
FILTER_ROWS = 256


def _cp(*sem):
    return pltpu.CompilerParams(dimension_semantics=sem, vmem_limit_bytes=VMEM_LIMIT)


def _silu(x):
    return x / (1.0 + jnp.exp(-x))


def _dot(a, b, precision=None):
    return jnp.dot(a, b, preferred_element_type=F32, precision=precision)


def _dot_t(a, b):
    return lax.dot_general(a, b, (((1,), (1,)), ((), ())), preferred_element_type=F32)


def _mod_body(c_ref, w_ref, b_ref, o_ref):
    o_ref[...] = _dot(_silu(c_ref[...]), w_ref[...], HI) + b_ref[...]


def _modulation(cc, w, b):
    rows, d = cc.shape
    n = w.shape[1]
    tn = 1024
    return pl.pallas_call(
        _mod_body,
        grid=(n // tn,),
        in_specs=[pl.BlockSpec((rows, d), lambda j: (0, 0)),
                  pl.BlockSpec((d, tn), lambda j: (0, j)),
                  pl.BlockSpec((1, tn), lambda j: (0, j))],
        out_specs=pl.BlockSpec((rows, tn), lambda j: (0, j)),
        out_shape=jax.ShapeDtypeStruct((rows, n), F32),
        compiler_params=_cp("arbitrary"),
        name="modulation",
    )(cc, w, b.reshape(1, n))


IN_SEGS = (3 * HY_WIDTH, HY_WIDTH, SSM_CONV_DIM, SSM_WIDTH, ATT_WIDTH, 2 * ATT_KV, ATT_WIDTH, LANES)
IN_DTYPES = (BF16,) * (len(IN_SEGS) - 1) + (F32,)
IN_CHUNK = 512


def _pack_w_in(w):
    o_ss = HY_IN
    o_at = HY_IN + SSM_IN
    dt = w[:, o_ss + SSM_CONV_DIM + SSM_WIDTH:o_at]
    parts = [w[:, :HY_IN], w[:, o_ss:o_ss + SSM_CONV_DIM + SSM_WIDTH], w[:, o_at:],
             dt, jnp.zeros((w.shape[0], LANES - dt.shape[1]), w.dtype)]
    return jnp.concatenate(parts, axis=1).astype(BF16)


def _inproj_body(h_ref, sh_ref, sc_ref, w_ref, *o_refs):
    x = h_ref[0]
    mu = jnp.mean(x, -1, keepdims=True)
    xc = x - mu
    var = jnp.mean(xc * xc, -1, keepdims=True)
    xm = (xc * lax.rsqrt(var + LN_EPS) * (1.0 + sc_ref[0]) + sh_ref[0]).astype(BF16)
    off = 0
    for o_ref, n in zip(o_refs, IN_SEGS):
        for j in range(0, n, IN_CHUNK):
            w = min(IN_CHUNK, n - j)
            o_ref[0, :, j:j + w] = _dot(xm, w_ref[:, off + j:off + j + w]).astype(o_ref.dtype)
        off += n


def _in_projection(h, shift, scale, w_packed):
    b, L, d = h.shape
    tm = 256
    n_all = w_packed.shape[1]
    row = lambda bi, i: (bi, i, 0)
    vec = lambda bi, i: (bi, 0, 0)
    return pl.pallas_call(
        _inproj_body,
        grid=(b, L // tm),
        in_specs=[pl.BlockSpec((1, tm, d), row), pl.BlockSpec((1, 1, d), vec), pl.BlockSpec((1, 1, d), vec),
                  pl.BlockSpec((d, n_all), lambda bi, i: (0, 0))],
        out_specs=[pl.BlockSpec((1, tm, n), row) for n in IN_SEGS],
        out_shape=[jax.ShapeDtypeStruct((b, L, n), dt) for n, dt in zip(IN_SEGS, IN_DTYPES)],
        compiler_params=_cp("parallel", "arbitrary"),
        name="in_projection",
    )(h, shift, scale, w_packed)


def _dwconv_body(u_ref, w_ref, b_ref, o_ref, *, act):
    x = u_ref[0].astype(F32)
    L = x.shape[0]
    row = lax.broadcasted_iota(jnp.int32, x.shape, 0)
    prev = jnp.where(row == 0, 0.0, pltpu.roll(x, 1, 0))
    nxt = jnp.where(row == L - 1, 0.0, pltpu.roll(x, L - 1, 0))
    y = prev * w_ref[0:1, :] + x * w_ref[1:2, :] + nxt * w_ref[2:3, :] + b_ref[...]
    if act:
        y = _silu(y)
    o_ref[0, 0] = y.astype(o_ref.dtype)


def _dwconv(u, w, bias, *, act, split):
    b, L, c = u.shape
    tc = 256
    per = split // tc
    return pl.pallas_call(
        functools.partial(_dwconv_body, act=act),
        grid=(b, c // tc),
        in_specs=[pl.BlockSpec((1, L, tc), lambda bi, j: (bi, 0, j)),
                  pl.BlockSpec((3, tc), lambda bi, j: (0, j)),
                  pl.BlockSpec((1, tc), lambda bi, j: (0, j))],
        out_specs=pl.BlockSpec((1, 1, L, tc), lambda bi, j: (j // per, bi, 0, j % per)),
        out_shape=jax.ShapeDtypeStruct((c // split, b, L, split), BF16),
        compiler_params=_cp("parallel", "arbitrary"),
        name="dwconv",
    )(u, w, bias.reshape(1, c))


def _filter_features(L):
    t = jnp.linspace(0.0, 1.0, L, dtype=F32)[:, None]
    w = 2.0 * math.pi * jnp.arange(L, dtype=F32)[:, None] / L
    f = jnp.linspace(1e-4, HY_BANDS - 1, HY_BANDS, dtype=F32)[None]
    z = jnp.concatenate([t, jnp.cos(f * w), -jnp.sin(f * w)], -1)
    return jnp.pad(z, ((0, 0), (0, LANES - HY_EMB)))


def _pad_to(a, rows, cols):
    return jnp.pad(a, ((0, rows - a.shape[0]), (0, cols - a.shape[1])))


def _filter_body(z_ref, w1_ref, w2_ref, w3_ref, b_ref, fr_ref, wo_ref, ad_ref, o_ref, *, nblk):
    i = pl.program_id(0)

    @pl.when(i < nblk)
    def _():
        z = z_ref[...]
        fr = fr_ref[...]
        h = jnp.sin(fr * (_dot(z, w1_ref[...], HI) + b_ref[0:1, :]))
        h = jnp.sin(fr * (_dot(h, w2_ref[...], HI) + b_ref[1:2, :]))
        h = jnp.sin(fr * (_dot(h, w3_ref[...], HI) + b_ref[2:3, :]))
        win = jnp.exp(-z[:, 0:1] * ad_ref[...])
        h_hi = h.astype(BF16)
        h_lo = (h - h_hi.astype(F32)).astype(BF16)
        for j in range(4):
            cols = slice(j * HY_WIDTH, (j + 1) * HY_WIDTH)
            taps = _dot(h_hi, wo_ref[0, :, cols]) + _dot(h_lo, wo_ref[0, :, cols]) + _dot(h_hi, wo_ref[1, :, cols])
            o_ref[:, cols] = taps * win

    @pl.when(i == nblk)
    def _():
        o_ref[...] = jnp.zeros_like(o_ref)


def _hyena_filter_taps(L, w1, b1, w2, b2, w3, b3, freq, w_out):
    z = _filter_features(L)
    hp = LANES
    bias = jnp.stack([jnp.pad(b, (0, hp - b.shape[0])) for b in (b1, b2, b3)])
    bias = jnp.pad(bias, ((0, 5), (0, 0)))
    fr = jnp.pad(freq, (0, hp - freq.shape[0])).reshape(1, hp)
    max_decay = math.log(HY_DECAY_TARGET) / HY_FAST_PCT
    min_decay = math.log(HY_DECAY_TARGET) / HY_SLOW_PCT
    absd = jnp.abs(jnp.linspace(min_decay, max_decay, HY_WIDTH, dtype=F32)).reshape(1, HY_WIDTH)
    tl = FILTER_ROWS
    nblk = L // tl
    n = 4 * HY_WIDTH
    full = lambda i: (0, 0)
    return pl.pallas_call(
        functools.partial(_filter_body, nblk=nblk),
        grid=(nblk + 1,),
        in_specs=[pl.BlockSpec((tl, hp), lambda i: (jnp.minimum(i, nblk - 1), 0)),
                  pl.BlockSpec((hp, hp), full), pl.BlockSpec((hp, hp), full), pl.BlockSpec((hp, hp), full),
                  pl.BlockSpec((8, hp), full), pl.BlockSpec((1, hp), full),
                  pl.BlockSpec((2, hp, n), lambda i: (0, 0, 0)), pl.BlockSpec((1, HY_WIDTH), full)],
        out_specs=pl.BlockSpec((tl, n), lambda i: (i, 0)),
        out_shape=jax.ShapeDtypeStruct((L + tl, n), F32),
        compiler_params=_cp("arbitrary"),
        name="hyena_filter",
    )(z, _pad_to(w1, hp, hp), _pad_to(w2, hp, hp), _pad_to(w3, hp, hp), bias, fr, _split2(_pad_to(w_out, hp, n)),
      absd)


def _hy_cfg(L):
    n2 = 128 if L >= 2048 else 16
    n1 = 2 * L // n2
    k1n = n1 // 2 + 1
    jp = -(-2 * k1n // 16) * 16
    pitch = n2 + 8
    return dict(L=L, n2=n2, n1=n1, nh=n1 // 2, k1n=k1n, jp=jp, pitch=pitch)


@functools.lru_cache(maxsize=None)
def _dft_tables(L):
    cfg = _hy_cfg(L)
    n, n1, n2, nh, k1n, jp = 2 * L, cfg["n1"], cfg["n2"], cfg["nh"], cfg["k1n"], cfg["jp"]
    a_n1 = np.arange(n1)
    a_k1 = np.arange(k1n)
    th = 2 * np.pi * np.outer(a_k1, a_n1) / n1
    f1 = np.zeros((jp, n1))
    f1[0:2 * k1n:2] = np.cos(th)
    f1[1:2 * k1n:2] = -np.sin(th)
    a_n2 = np.arange(n2)
    m1 = np.zeros((k1n, 2 * n2, 2 * n2))
    for k in range(k1n):
        f = np.exp(-2j * np.pi * (np.outer(a_n2, a_n2) / n2 + a_n2[None, :] * k / n))
        m1[k] = np.block([[f.real, -f.imag], [f.imag, f.real]])
    ck = np.full(k1n, 2.0)
    ck[0] = 1.0
    ck[-1] = 1.0
    th6 = 2 * np.pi * np.outer(np.arange(nh), a_k1) / n1
    f6 = np.zeros((nh, jp))
    f6[:, 0:2 * k1n:2] = ck * np.cos(th6) / n
    f6[:, 1:2 * k1n:2] = -ck * np.sin(th6) / n
    as32 = lambda a: np.asarray(a, np.float32)
    return dict(f1=as32(f1), m1=as32(m1), m2=as32(np.transpose(m1, (0, 2, 1))), f6=as32(f6))


def _first_stage(src_ref, a_scr, f1, rows, cfg, precision):
    n2, jp, pitch = cfg["n2"], cfg["jp"], cfg["pitch"]

    def step(i, carry):
        xs = jnp.concatenate([src_ref[pl.ds(2 * i, rows, stride=n2), :],
                              src_ref[pl.ds(2 * i + 1, rows, stride=n2), :]], 1)
        if precision is None:
            xs = xs.astype(BF16)
        r = _dot(f1, xs, precision)
        a_scr[pl.ds(2 * i, jp, stride=pitch), :] = r[:, :LANES]
        a_scr[pl.ds(2 * i + 1, jp, stride=pitch), :] = r[:, LANES:]
        return carry

    lax.fori_loop(0, n2 // 2, step, 0, unroll=min(16, n2 // 2))


def _k1_rows(a_scr, k, cfg):
    n2, pitch = cfg["n2"], cfg["pitch"]
    base = pl.multiple_of(2 * k * pitch, 8)
    return base, jnp.concatenate([a_scr[pl.ds(base, n2), :], a_scr[pl.ds(base + pitch, n2), :]], 0)


def _split2(table):
    t = jnp.asarray(table)
    hi = t.astype(BF16)
    return jnp.stack([hi, (t - hi.astype(F32)).astype(BF16)])


def _spectrum_body(kf_ref, kb_ref, f1_ref, m1_ref, o_ref, a_scr, *, cfg):
    n2, nh, jp, pitch = cfg["n2"], cfg["nh"], cfg["jp"], cfg["pitch"]

    def dot3(m_hi, m_lo, x):
        x_hi = x.astype(BF16)
        x_lo = (x - x_hi.astype(F32)).astype(BF16)
        return _dot(m_hi, x_hi) + _dot(m_lo, x_hi) + _dot(m_hi, x_lo)

    def step(i, carry):
        cols = []
        for t in range(2):
            n = 2 * i + t
            cols.append(jnp.concatenate([kf_ref[pl.ds(n, nh, stride=n2), :],
                                         kb_ref[pl.ds(n2 - n, nh, stride=n2), :]], 0))
        r = dot3(f1_ref[0], f1_ref[1], jnp.concatenate(cols, 1))
        a_scr[pl.ds(2 * i, jp, stride=pitch), :] = r[:, :LANES]
        a_scr[pl.ds(2 * i + 1, jp, stride=pitch), :] = r[:, LANES:]
        return carry

    lax.fori_loop(0, n2 // 2, step, 0, unroll=min(16, n2 // 2))
    lag0 = pl.ds(0, jp, stride=pitch)
    a_scr[lag0, :] = a_scr[lag0, :] + f1_ref[0, :, 0:1].astype(F32) * kb_ref[0:1, :]

    def mid(k, carry):
        _, a = _k1_rows(a_scr, k, cfg)
        o_ref[0, k] = dot3(m1_ref[0, k], m1_ref[1, k], a).astype(BF16)
        return carry

    lax.fori_loop(0, cfg["k1n"], mid, 0)


def _hyena_spectrum(L, taps):
    c = HY_WIDTH
    cfg = _hy_cfg(L)
    tb = _dft_tables(L)
    n1, n2, nh, k1n = cfg["n1"], cfg["n2"], cfg["nh"], cfg["k1n"]
    f1 = _split2(np.concatenate([tb["f1"][:, :nh], tb["f1"][:, n1 - 1:nh - 1:-1]], 1))
    m1 = _split2(tb["m1"])
    nct = c // LANES
    rows = taps.shape[0]
    return pl.pallas_call(
        functools.partial(_spectrum_body, cfg=cfg),
        grid=(2, nct),
        in_specs=[pl.BlockSpec((rows, LANES), lambda cv, j: (0, 2 * cv * nct + j)),
                  pl.BlockSpec((rows, LANES), lambda cv, j: (0, (2 * cv + 1) * nct + j)),
                  pl.BlockSpec(f1.shape, lambda cv, j: (0, 0, 0)),
                  pl.BlockSpec(m1.shape, lambda cv, j: (0, 0, 0, 0))],
        out_specs=pl.BlockSpec((1, k1n, 2 * n2, LANES), lambda cv, j: (cv, 0, 0, j)),
        out_shape=jax.ShapeDtypeStruct((2, k1n, 2 * n2, c), BF16),
        scratch_shapes=[pltpu.VMEM((cfg["jp"] * cfg["pitch"], LANES), F32)],
        compiler_params=_cp("arbitrary", "arbitrary"),
        name="hyena_spectrum",
    )(taps, taps, f1, m1)


def _short_conv(u_ref, w_ref, b_ref, which):
    x = u_ref[0].astype(F32)
    L = x.shape[0]
    row = lax.broadcasted_iota(jnp.int32, x.shape, 0)
    prev = jnp.where(row == 0, 0.0, pltpu.roll(x, 1, 0))
    nxt = jnp.where(row == L - 1, 0.0, pltpu.roll(x, L - 1, 0))
    sel = slice(which, which + 1)
    return prev * w_ref[0, sel, :] + x * w_ref[1, sel, :] + nxt * w_ref[2, sel, :] + b_ref[sel, :]


def _long_conv(src_scr, a_scr, y_scr, w_scr, f1_ref, f6_ref, m1_ref, m2_ref, kf_ref, conv, cfg):
    n2, nh, jp, pitch = cfg["n2"], cfg["nh"], cfg["jp"], cfg["pitch"]
    _first_stage(src_scr, a_scr, f1_ref[...], nh, cfg, None)

    def forward(k, carry):
        _, a = _k1_rows(a_scr, k, cfg)
        x = _dot(m1_ref[k], a.astype(BF16))
        xr, xi = x[:n2], x[n2:]
        kk = kf_ref[conv, k].astype(F32)
        kr, ki = kk[:n2], kk[n2:]
        w_scr[k] = jnp.concatenate([xr * kr - xi * ki, xr * ki + xi * kr], 0).astype(BF16)
        return carry

    def inverse(k, carry):
        base = pl.multiple_of(2 * k * pitch, 8)
        b = _dot(m2_ref[k], w_scr[k])
        a_scr[pl.ds(base, n2), :] = b[:n2]
        a_scr[pl.ds(base + pitch, n2), :] = b[n2:]
        return carry

    lax.fori_loop(0, cfg["k1n"], forward, 0, unroll=3)
    lax.fori_loop(0, cfg["k1n"], inverse, 0, unroll=3)
    f6 = f6_ref[...]

    def last(i, carry):
        bs = jnp.concatenate([a_scr[pl.ds(2 * i, jp, stride=pitch), :],
                              a_scr[pl.ds(2 * i + 1, jp, stride=pitch), :]], 1).astype(BF16)
        y = _dot(f6, bs)
        y_scr[pl.ds(2 * i, nh, stride=n2), :] = y[:, :LANES]
        y_scr[pl.ds(2 * i + 1, nh, stride=n2), :] = y[:, LANES:]
        return carry

    lax.fori_loop(0, n2 // 2, last, 0, unroll=min(16, n2 // 2))


def _hyena_body(v_ref, x1_ref, x2_ref, g_ref, cw_ref, cb_ref, hb_ref, f1_ref, f6_ref, m1_ref, m2_ref, kf_ref,
                o_ref, s_scr, a_scr, y_scr, w_scr, *, cfg):
    tabs = (w_scr, f1_ref, f6_ref, m1_ref, m2_ref, kf_ref)
    s_scr[...] = _short_conv(v_ref, cw_ref, cb_ref, 0)
    _long_conv(s_scr, a_scr, y_scr, *tabs, 0, cfg)
    s_scr[...] = _short_conv(x1_ref, cw_ref, cb_ref, 1) * (y_scr[...] + s_scr[...] * hb_ref[0:1, :])
    _long_conv(s_scr, a_scr, y_scr, *tabs, 1, cfg)
    y = _short_conv(x2_ref, cw_ref, cb_ref, 2) * (y_scr[...] + s_scr[...] * hb_ref[1:2, :])
    o_ref[0] = (y * _silu(g_ref[0].astype(F32))).astype(o_ref.dtype)


def _hyena(u_hy3, u_gate, conv_w, conv_b, kf, hy_bias):
    b, L, _ = u_gate.shape
    c = HY_WIDTH
    cfg = _hy_cfg(L)
    tb = _dft_tables(L)
    nct = c // LANES
    col = lambda off: pl.BlockSpec((1, L, LANES), lambda j, bi: (bi, 0, off * nct + j))
    full = lambda a: pl.BlockSpec(a.shape, lambda j, bi: (0,) * a.ndim)
    f1 = jnp.asarray(tb["f1"][:, :cfg["nh"]]).astype(BF16)
    f6 = jnp.asarray(tb["f6"]).astype(BF16)
    m1 = jnp.asarray(tb["m1"]).astype(BF16)
    m2 = jnp.asarray(tb["m2"]).astype(BF16)
    return pl.pallas_call(
        functools.partial(_hyena_body, cfg=cfg),
        grid=(nct, b),
        in_specs=[col(0), col(1), col(2), col(0),
                  pl.BlockSpec((3, 3, LANES), lambda j, bi: (0, 0, j)),
                  pl.BlockSpec((3, LANES), lambda j, bi: (0, j)),
                  pl.BlockSpec((2, LANES), lambda j, bi: (0, j)),
                  full(f1), full(f6), full(m1), full(m2),
                  pl.BlockSpec((2, cfg["k1n"], 2 * cfg["n2"], LANES), lambda j, bi: (0, 0, 0, j))],
        out_specs=col(0),
        out_shape=jax.ShapeDtypeStruct((b, L, c), BF16),
        scratch_shapes=[pltpu.VMEM((L, LANES), F32),
                        pltpu.VMEM((cfg["jp"] * cfg["pitch"], LANES), F32),
                        pltpu.VMEM((L, LANES), F32),
                        pltpu.VMEM((cfg["k1n"], 2 * cfg["n2"], LANES), BF16)],
        compiler_params=_cp("arbitrary", "arbitrary"),
        name="hyena",
    )(u_hy3, u_hy3, u_hy3, u_gate, conv_w.reshape(3, 3, c), conv_b.reshape(3, c), hy_bias, f1, f6, m1, m2, kf)


SSD_CPS = 2
SPLIT_STRIDE = 2 * SSM_HEADS


def _pack3(x):
    hi = x.astype(BF16).astype(F32)
    r1 = x - hi
    mid = r1.astype(BF16).astype(F32)
    lo = (r1 - mid).astype(BF16).astype(F32)
    return (hi + pltpu.roll(mid, SPLIT_STRIDE, 1) + pltpu.roll(lo, 2 * SPLIT_STRIDE, 1)).astype(BF16)


def _unpack3(x3, used):
    return jnp.where(used, x3 + pltpu.roll(x3, LANES - SPLIT_STRIDE, 1) + pltpu.roll(x3, LANES - 2 * SPLIT_STRIDE, 1), 0.0)


@functools.lru_cache(maxsize=None)
def _ssd_spread_tables():
    col = np.zeros((LANES, 2 * SSM_HEADS * LANES), np.float32)
    head = np.zeros((LANES, 2 * SSM_WIDTH), np.float32)
    for c in range(2 * SSM_HEADS):
        d, h = divmod(c, SSM_HEADS)
        for piece in range(3):
            col[c + piece * SPLIT_STRIDE, c * LANES:(c + 1) * LANES] = 1.0
            lo = d * SSM_WIDTH + h * SSM_HEADDIM
            head[c + piece * SPLIT_STRIDE, lo:lo + SSM_HEADDIM] = 1.0
    return col, head


def _ssd_chunk_body(xbc_ref, dtr_ref, dtb_ref, a_ref, ecol_ref, ehead_ref, yd_ref, cs_ref, ex_ref, et_ref):
    q = SSM_CHUNK
    hpg = SSM_HEADS // SSM_GROUPS
    gw = SSM_WIDTH // SSM_GROUPS
    li = lax.broadcasted_iota(jnp.int32, (q, q), 0)
    si = lax.broadcasted_iota(jnp.int32, (q, q), 1)
    below = li > si
    diag = li == si
    fwd_lane = si < SSM_HEADS
    used = si < 2 * SSM_HEADS
    tril = (li >= si).astype(BF16)
    triu = (li <= si).astype(BF16)
    chunks = range(SSD_CPS)
    rows = [slice(c * q, (c + 1) * q) for c in chunks]
    ehead = ehead_ref[...]

    raws = [dtr_ref[0, r, :] + dtb_ref[...] for r in rows]
    dts = [jnp.where(used, jnp.maximum(x, 0.0) + jnp.log1p(jnp.exp(-jnp.abs(x))), 0.0) for x in raws]
    da3 = [_pack3(dt * a_ref[...]) for dt in dts]
    acs = [jnp.where(fwd_lane, _unpack3(_dot(tril, x), used), _unpack3(_dot(triu, x), used)) for x in da3]
    tots = [jnp.where(fwd_lane[0:1], a[q - 1:q, :], a[0:1, :]) for a in acs]
    ws = [dt * jnp.exp(t - a) for dt, t, a in zip(dts, tots, acs)]
    acs3 = [_pack3(a) for a in acs]
    colb = [_dot(x, ecol_ref[...]) for x in acs3]
    wx = [_dot(_pack3(w), ehead) for w in ws]
    for c in chunks:
        ex_ref[0, rows[c], :] = jnp.exp(_dot(acs3[c], ehead)).astype(BF16)
        et_ref[0, c] = jnp.exp(_dot(_pack3(jnp.broadcast_to(tots[c], (8, LANES))), ehead))
    rowt = [(a - jnp.where(dt > 0.0, jnp.log(dt), -BIG)).T for a, dt in zip(acs, dts)]
    dsum = [(dt + pltpu.roll(dt, LANES - SSM_HEADS, 1)).T for dt in dts]
    xbc = [xbc_ref[0, r, :] for r in rows]
    xsb = [x[:, :SSM_WIDTH].astype(BF16) for x in xbc]
    bgs = [[x[:, SSM_WIDTH + g * SSM_STATE:SSM_WIDTH + (g + 1) * SSM_STATE] for g in range(SSM_GROUPS)] for x in xbc]
    cgs = [[x[:, SSM_WIDTH + (SSM_GROUPS + g) * SSM_STATE:SSM_WIDTH + (SSM_GROUPS + g + 1) * SSM_STATE]
            for g in range(SSM_GROUPS)] for x in xbc]
    gmat = [[_dot_t(cgs[c][g].astype(BF16), bgs[c][g].astype(BF16)) for g in range(SSM_GROUPS)] for c in chunks]
    mats = []
    for c in chunks:
        for h in range(SSM_HEADS):
            hb = SSM_HEADS + h
            arg = jnp.where(below, colb[c][:, h * q:(h + 1) * q] - rowt[c][h:h + 1, :],
                            colb[c][:, hb * q:(hb + 1) * q] - rowt[c][hb:hb + 1, :])
            dec = jnp.where(diag, dsum[c][h:h + 1, :], jnp.exp(arg))
            mats.append((gmat[c][h // hpg] * dec).astype(BF16))
    for c in chunks:
        for h in range(SSM_HEADS):
            lo = h * SSM_HEADDIM
            yd_ref[0, rows[c], lo:lo + SSM_HEADDIM] = _dot(mats[c * SSM_HEADS + h],
                                                             xsb[c][:, lo:lo + SSM_HEADDIM]).astype(yd_ref.dtype)
    for c in chunks:
        for g in range(SSM_GROUPS):
            bgt = bgs[c][g].astype(F32).T.astype(BF16)
            xg = xbc[c][:, g * gw:(g + 1) * gw].astype(F32)
            for d in range(2):
                lo = d * SSM_WIDTH + g * gw
                cs_ref[0, c, d, g] = _dot(bgt, (xg * wx[c][:, lo:lo + gw]).astype(BF16))


def _ssd_chunks(xbc, dt_raw, dt_bias_row, a_row):
    b, L, _ = xbc.shape
    nc = L // SSM_CHUNK
    rows = SSD_CPS * SSM_CHUNK
    gw = SSM_WIDTH // SSM_GROUPS
    blk = lambda bi, i: (bi, i, 0)
    full2 = lambda bi, i: (0, 0)
    ecol, ehead = (jnp.asarray(t).astype(BF16) for t in _ssd_spread_tables())
    return pl.pallas_call(
        _ssd_chunk_body,
        grid=(b, nc // SSD_CPS),
        in_specs=[pl.BlockSpec((1, rows, SSM_CONV_DIM), blk), pl.BlockSpec((1, rows, LANES), blk),
                  pl.BlockSpec((1, LANES), full2), pl.BlockSpec((1, LANES), full2),
                  pl.BlockSpec(ecol.shape, full2), pl.BlockSpec(ehead.shape, full2)],
        out_specs=[pl.BlockSpec((1, rows, SSM_WIDTH), blk),
                   pl.BlockSpec((1, SSD_CPS, 2, SSM_GROUPS, SSM_STATE, gw), lambda bi, i: (bi, i, 0, 0, 0, 0)),
                   pl.BlockSpec((1, rows, 2 * SSM_WIDTH), blk),
                   pl.BlockSpec((1, SSD_CPS, 8, 2 * SSM_WIDTH), lambda bi, i: (bi, i, 0, 0))],
        out_shape=[jax.ShapeDtypeStruct((b, L, SSM_WIDTH), BF16),
                   jax.ShapeDtypeStruct((b, nc, 2, SSM_GROUPS, SSM_STATE, gw), F32),
                   jax.ShapeDtypeStruct((b, L, 2 * SSM_WIDTH), BF16),
                   jax.ShapeDtypeStruct((b, nc, 8, 2 * SSM_WIDTH), F32)],
        compiler_params=_cp("parallel", "arbitrary"),
        name="ssd_chunks",
    )(xbc, dt_raw, dt_bias_row, a_row, ecol, ehead)


def _ssd_state_body(cf_ref, cb_ref, xf_ref, xb_ref, ef_ref, eb_ref, sf_ref, sb_ref, init_ref, yf_ref, yb_ref, fin_ref,
                    st_ref, *, nc):
    ci = pl.program_id(1)

    @pl.when(ci == 0)
    def _():
        st_ref[...] = init_ref[0]

    gw = SSM_WIDTH // SSM_GROUPS
    dirs = ((cf_ref, xf_ref, ef_ref, sf_ref, yf_ref), (cb_ref, xb_ref, eb_ref, sb_ref, yb_ref))
    for d, (c_ref, x_ref, e_ref, s_ref, y_ref) in enumerate(dirs):
        cmat = c_ref[0].astype(BF16)
        for g in range(SSM_GROUPS):
            cols = slice(g * gw, (g + 1) * gw)
            st = st_ref[d, g]
            y_ref[0, :, cols] = (_dot(cmat[:, g * SSM_STATE:(g + 1) * SSM_STATE], st.astype(BF16))
                                 * x_ref[0, :, cols].astype(F32)).astype(y_ref.dtype)
            st_ref[d, g] = st * e_ref[0, 0, 0:1, cols] + s_ref[0, 0, 0, g]

    @pl.when(ci == nc - 1)
    def _():
        fin_ref[0] = st_ref[...]


def _ssd_states(xbc, ex, et, cs, init):
    b, L, _ = xbc.shape
    nc = L // SSM_CHUNK
    gw = SSM_WIDTH // SSM_GROUPS
    c_col = SSM_CONV_DIM // (SSM_GROUPS * SSM_STATE) - 1
    st_shape = (2, SSM_GROUPS, SSM_STATE, gw)
    st_spec = pl.BlockSpec((1,) + st_shape, lambda bi, c: (bi, 0, 0, 0, 0))
    cs_blk = (1, 1, 1, SSM_GROUPS, SSM_STATE, gw)
    return pl.pallas_call(
        functools.partial(_ssd_state_body, nc=nc),
        grid=(b, nc),
        in_specs=[pl.BlockSpec((1, SSM_CHUNK, SSM_GROUPS * SSM_STATE), lambda bi, c: (bi, c, c_col)),
                  pl.BlockSpec((1, SSM_CHUNK, SSM_GROUPS * SSM_STATE), lambda bi, c: (bi, nc - 1 - c, c_col)),
                  pl.BlockSpec((1, SSM_CHUNK, SSM_WIDTH), lambda bi, c: (bi, c, 0)),
                  pl.BlockSpec((1, SSM_CHUNK, SSM_WIDTH), lambda bi, c: (bi, nc - 1 - c, 1)),
                  pl.BlockSpec((1, 1, 8, SSM_WIDTH), lambda bi, c: (bi, c, 0, 0)),
                  pl.BlockSpec((1, 1, 8, SSM_WIDTH), lambda bi, c: (bi, nc - 1 - c, 0, 1)),
                  pl.BlockSpec(cs_blk, lambda bi, c: (bi, c, 0, 0, 0, 0)),
                  pl.BlockSpec(cs_blk, lambda bi, c: (bi, nc - 1 - c, 1, 0, 0, 0)),
                  st_spec],
        out_specs=[pl.BlockSpec((1, SSM_CHUNK, SSM_WIDTH), lambda bi, c: (bi, c, 0)),
                   pl.BlockSpec((1, SSM_CHUNK, SSM_WIDTH), lambda bi, c: (bi, nc - 1 - c, 0)),
                   st_spec],
        out_shape=[jax.ShapeDtypeStruct((b, L, SSM_WIDTH), BF16), jax.ShapeDtypeStruct((b, L, SSM_WIDTH), BF16),
                   jax.ShapeDtypeStruct((b,) + st_shape, F32)],
        scratch_shapes=[pltpu.VMEM(st_shape, F32)],
        compiler_params=_cp("parallel", "arbitrary"),
        name="ssd_states",
    )(xbc, xbc, ex, ex, et, et, cs, cs, init)


def _ssd(xbc, dt_raw, dt_bias_row, a_row, init):
    y_diag, cs, ex, et = _ssd_chunks(xbc, dt_raw, dt_bias_row, a_row)
    y_f, y_b, fin = _ssd_states(xbc, ex, et, cs, init)
    return (y_diag, y_f, y_b), fin


def _ssd_body(xbc_ref, dtr_ref, dtb_ref, a_ref, e_ref, init_ref, y_ref, fin_ref, st_ref, *, reverse, nc, d):
    ci = pl.program_id(1)

    @pl.when(ci == 0)
    def _():
        st_ref[...] = init_ref[0]

    q = SSM_CHUNK
    gw = SSM_WIDTH // SSM_GROUPS
    hpg = SSM_HEADS // SSM_GROUPS
    xbc = xbc_ref[0]
    xs = xbc[:, :SSM_WIDTH]
    raw = dtr_ref[0] + dtb_ref[...]
    dt = jnp.maximum(raw, 0.0) + jnp.log1p(jnp.exp(-jnp.abs(raw)))
    da = dt * a_ref[...]
    li = lax.broadcasted_iota(jnp.int32, (q, q), 0)
    si = lax.broadcasted_iota(jnp.int32, (q, q), 1)
    mask = (li <= si) if reverse else (li >= si)
    acs = _dot(mask.astype(F32), da, HI)
    acs_t = acs.T
    e = e_ref[...]
    dtx = _dot(dt, e, HI)
    acsx = _dot(acs, e, HI)
    last = 0 if reverse else q - 1
    totx = acsx[last:last + 1, :]
    xd = xs * dtx
    xde = (xd * jnp.exp(totx - acsx)).astype(BF16)
    xdb = xd.astype(BF16)
    eacs = jnp.exp(acsx)
    etot = jnp.exp(totx)
    for g in range(SSM_GROUPS):
        bg = xbc[:, SSM_WIDTH + g * SSM_STATE:SSM_WIDTH + (g + 1) * SSM_STATE]
        cg = xbc[:, SSM_WIDTH + (SSM_GROUPS + g) * SSM_STATE:SSM_WIDTH + (SSM_GROUPS + g + 1) * SSM_STATE]
        cgb = cg.astype(BF16)
        gmat = _dot_t(cgb, bg.astype(BF16))
        st = st_ref[g]
        y_off = _dot(cgb, st.astype(BF16)) * eacs[:, g * gw:(g + 1) * gw]
        for j in range(hpg):
            h = g * hpg + j
            col = d * SSM_HEADS + h
            seg = acs[:, col:col + 1] - acs_t[col:col + 1, :]
            lm = jnp.where(mask, jnp.exp(seg), 0.0)
            lo = h * SSM_HEADDIM
            yd = _dot((gmat * lm).astype(BF16), xdb[:, lo:lo + SSM_HEADDIM])
            y_ref[0, :, lo:lo + SSM_HEADDIM] = yd + y_off[:, j * SSM_HEADDIM:(j + 1) * SSM_HEADDIM]
        st_ref[g] = st * etot[:, g * gw:(g + 1) * gw] + _dot(bg.T.astype(BF16), xde[:, g * gw:(g + 1) * gw])

    @pl.when(ci == nc - 1)
    def _():
        fin_ref[0] = st_ref[...]


def _ssd_scan(xbc, dt_raw, dt_bias_row, a_row, expand, init, *, d):
    b, L, _ = xbc.shape
    nc = L // SSM_CHUNK
    reverse = d == 1
    chunk = (lambda bi, c: (bi, nc - 1 - c, 0)) if reverse else (lambda bi, c: (bi, c, 0))
    full2 = lambda bi, c: (0, 0)
    st_shape = (SSM_GROUPS, SSM_STATE, SSM_WIDTH // SSM_GROUPS)
    st_spec = pl.BlockSpec((1,) + st_shape, lambda bi, c: (bi, 0, 0, 0))
    return pl.pallas_call(
        functools.partial(_ssd_body, reverse=reverse, nc=nc, d=d),
        grid=(b, nc),
        in_specs=[pl.BlockSpec((1, SSM_CHUNK, SSM_CONV_DIM), chunk),
                  pl.BlockSpec((1, SSM_CHUNK, LANES), chunk),
                  pl.BlockSpec((1, LANES), full2), pl.BlockSpec((1, LANES), full2),
                  pl.BlockSpec((LANES, SSM_WIDTH), full2), st_spec],
        out_specs=[pl.BlockSpec((1, SSM_CHUNK, SSM_WIDTH), chunk), st_spec],
        out_shape=[jax.ShapeDtypeStruct((b, L, SSM_WIDTH), F32),
                   jax.ShapeDtypeStruct((b,) + st_shape, F32)],
        scratch_shapes=[pltpu.VMEM(st_shape, F32)],
        compiler_params=_cp("parallel", "arbitrary"),
        name="ssd_scan",
    )(xbc, dt_raw, dt_bias_row, a_row, expand, init)


def _ssd_expand(d):
    e = np.zeros((LANES, SSM_WIDTH), np.float32)
    for h in range(SSM_HEADS):
        e[d * SSM_HEADS + h, h * SSM_HEADDIM:(h + 1) * SSM_HEADDIM] = 1.0
    return e


def _rope_tables(L):
    rows = L // GRID_W
    row = jnp.broadcast_to(jnp.arange(rows)[:, None], (rows, GRID_W)).reshape(L)
    col = jnp.broadcast_to(jnp.arange(GRID_W)[None, :], (rows, GRID_W)).reshape(L)
    nf = ATT_HEADDIM // 4
    inv = ROPE_BASE ** (-jnp.arange(nf, dtype=F32) / nf)
    ar = row.astype(F32)[:, None] * inv
    ac = col.astype(F32)[:, None] * inv
    cos = jnp.concatenate([jnp.cos(ar), jnp.cos(ar), jnp.cos(ac), jnp.cos(ac)], -1)
    sin = jnp.concatenate([-jnp.sin(ar), jnp.sin(ar), -jnp.sin(ac), jnp.sin(ac)], -1)
    return jnp.tile(cos, (1, ATT_HEADS)), jnp.tile(sin, (1, ATT_HEADS))


def _rope(x, cos, sin):
    w = x.shape[-1]
    quarter = ATT_HEADDIM // 4
    lane = lax.broadcasted_iota(jnp.int32, x.shape, x.ndim - 1)
    partner = jnp.where((lane // quarter) % 2 == 0, pltpu.roll(x, w - quarter, x.ndim - 1),
                        pltpu.roll(x, quarter, x.ndim - 1))
    return x * cos + partner * sin


def _rope_body(q_ref, kv_ref, cos_ref, sin_ref, qo_ref, ko_ref):
    cos = cos_ref[...]
    sin = sin_ref[...]
    qo_ref[0] = (_rope(q_ref[0].astype(F32), cos, sin) * (ATT_HEADDIM ** -0.5 * LOG2E)).astype(qo_ref.dtype)
    ko_ref[0] = _rope(kv_ref[0].astype(F32), cos[:, :ATT_KV], sin[:, :ATT_KV]).astype(ko_ref.dtype)


def _apply_rope(u_q, u_kv, cos, sin):
    b, L, _ = u_q.shape
    tl = 512
    row = lambda i, bi: (bi, i, 0)
    tab = lambda i, bi: (i, 0)
    return pl.pallas_call(
        _rope_body,
        grid=(L // tl, b),
        in_specs=[pl.BlockSpec((1, tl, ATT_WIDTH), row), pl.BlockSpec((1, tl, ATT_KV), row),
                  pl.BlockSpec((tl, ATT_WIDTH), tab), pl.BlockSpec((tl, ATT_WIDTH), tab)],
        out_specs=[pl.BlockSpec((1, tl, ATT_WIDTH), row), pl.BlockSpec((1, tl, ATT_KV), row)],
        out_shape=[jax.ShapeDtypeStruct((b, L, ATT_WIDTH), BF16), jax.ShapeDtypeStruct((b, L, ATT_KV), BF16)],
        compiler_params=_cp("parallel", "arbitrary"),
        name="rope",
    )(u_q, u_kv, cos, sin)


def _wattn_body(sink_ref, bias_ref, q_ref, kp_ref, kc_ref, kn_ref, vp_ref, vc_ref, vn_ref, kx_ref, vx_ref, z_ref,
                o_ref):
    hd = ATT_HEADDIM
    low_half = lax.broadcasted_iota(jnp.int32, (WINDOW, 2 * hd), 1) < hd
    q = q_ref[0]
    bias = bias_ref[0]
    z = z_ref[0].astype(F32)
    k_all, v_ext = [], []
    for g in range(ATT_KV_HEADS):
        ks = slice(g * hd, (g + 1) * hd)
        k_all.append(jnp.concatenate([r[0, :, ks] for r in (kp_ref, kc_ref, kn_ref, kx_ref)], 0).astype(BF16))
        v_all = jnp.concatenate([r[0, :, ks] for r in (vp_ref, vc_ref, vn_ref, vx_ref)], 0).astype(BF16)
        ones = jnp.ones_like(v_all)
        v_ext.append((jnp.concatenate([v_all, ones], 1), jnp.concatenate([ones, v_all], 1)))
    heads = range(ATT_HEADS)
    sinks = [sink_ref[h] * LOG2E for h in heads]
    scores = [_dot_t(q[:, h * hd:(h + 1) * hd].astype(BF16), k_all[h // ATT_GROUP]) + bias for h in heads]
    maxes = [jnp.maximum(jnp.max(s, -1, keepdims=True), sk) for s, sk in zip(scores, sinks)]
    probs = [jnp.exp2(s - m).astype(BF16) for s, m in zip(scores, maxes)]
    exts = [_dot(p, v_ext[h // ATT_GROUP][h % 2]) for h, p in zip(heads, probs)]
    outs = [e / (pltpu.roll(e, hd, 1) + jnp.exp2(sk - m)) for e, sk, m in zip(exts, sinks, maxes)]
    for pair in range(ATT_HEADS // 2):
        cs = slice(2 * pair * hd, (2 * pair + 2) * hd)
        o_ref[0, :, cs] = (jnp.where(low_half, outs[2 * pair], outs[2 * pair + 1]) * _silu(z[:, cs])).astype(o_ref.dtype)


def _window_attention(q_rot, k_rot, u_kv, uc_kv, sinks, z_a):
    b, L, _ = q_rot.shape
    lc = uc_kv.shape[1]
    nb = L // WINDOW
    hd2 = ATT_KV
    cur = lambda bi, i: (bi, i, 0)
    prv = lambda bi, i: (bi, jnp.maximum(i - 1, 0), 0)
    nxt = lambda bi, i: (bi, jnp.minimum(i + 1, nb - 1), 0)
    vcur = lambda bi, i: (bi, i, 1)
    vprv = lambda bi, i: (bi, jnp.maximum(i - 1, 0), 1)
    vnxt = lambda bi, i: (bi, jnp.minimum(i + 1, nb - 1), 1)
    kblk = lambda f: pl.BlockSpec((1, WINDOW, hd2), f)
    nk = 3 * WINDOW + lc
    row = np.arange(WINDOW)[:, None]
    col = np.arange(nk)[None, :]
    in_prev = (col < WINDOW) & (col >= row)
    in_next = (col >= 2 * WINDOW) & (col < 3 * WINDOW) & (col - 2 * WINDOW <= row)
    always = ((col >= WINDOW) & (col < 2 * WINDOW)) | (col >= 3 * WINDOW)
    kinds = [always | in_next, always | in_prev | in_next, always | in_prev]
    if nb == 1:
        kinds = [always] * 3
    bias = jnp.asarray(np.where(np.stack(kinds), 0.0, NEG).astype(np.float32))
    kind = lambda bi, i: (jnp.where(i == 0, 0, jnp.where(i == nb - 1, 2, 1)), 0, 0)
    return pl.pallas_call(
        _wattn_body,
        grid=(b, nb),
        in_specs=[pl.BlockSpec(memory_space=pltpu.SMEM),
                  pl.BlockSpec((1, WINDOW, nk), kind),
                  pl.BlockSpec((1, WINDOW, ATT_WIDTH), cur),
                  kblk(prv), kblk(cur), kblk(nxt), kblk(vprv), kblk(vcur), kblk(vnxt),
                  pl.BlockSpec((1, lc, hd2), lambda bi, i: (bi, 0, 0)),
                  pl.BlockSpec((1, lc, hd2), lambda bi, i: (bi, 0, 1)),
                  pl.BlockSpec((1, WINDOW, ATT_WIDTH), cur)],
        out_specs=pl.BlockSpec((1, WINDOW, ATT_WIDTH), cur),
        out_shape=jax.ShapeDtypeStruct((b, L, ATT_WIDTH), BF16),
        compiler_params=_cp("parallel", "arbitrary"),
        name="window_attention",
    )(sinks, bias, q_rot, k_rot, k_rot, k_rot, u_kv, u_kv, u_kv, uc_kv, uc_kv, z_a)


def _cattn_body(sink_ref, q_ref, k_ref, v_ref, z_ref, o_ref):
    scale = ATT_HEADDIM ** -0.5
    q = q_ref[0]
    z = z_ref[0].astype(F32)
    for g in range(ATT_KV_HEADS):
        ks = slice(g * ATT_HEADDIM, (g + 1) * ATT_HEADDIM)
        k = k_ref[0, :, ks].astype(BF16)
        v = v_ref[0, :, ks].astype(BF16)
        for j in range(ATT_GROUP):
            h = g * ATT_GROUP + j
            hs = slice(h * ATT_HEADDIM, (h + 1) * ATT_HEADDIM)
            s = _dot_t(q[:, hs].astype(BF16), k) * scale
            sink = sink_ref[h]
            m = jnp.maximum(jnp.max(s, -1, keepdims=True), sink)
            p = jnp.exp(s - m)
            den = jnp.sum(p, -1, keepdims=True) + jnp.exp(sink - m)
            o_ref[0, :, hs] = (_dot(p.astype(BF16), v) / den * _silu(z[:, hs])).astype(o_ref.dtype)


def _ctx_attention(uc_q, uc_kv, sinks, z_ac):
    b, lc, _ = uc_q.shape
    blk = lambda bi: (bi, 0, 0)
    return pl.pallas_call(
        _cattn_body,
        grid=(b,),
        in_specs=[pl.BlockSpec(memory_space=pltpu.SMEM),
                  pl.BlockSpec((1, lc, ATT_WIDTH), blk),
                  pl.BlockSpec((1, lc, ATT_KV), lambda bi: (bi, 0, 0)),
                  pl.BlockSpec((1, lc, ATT_KV), lambda bi: (bi, 0, 1)),
                  pl.BlockSpec((1, lc, ATT_WIDTH), blk)],
        out_specs=pl.BlockSpec((1, lc, ATT_WIDTH), blk),
        out_shape=jax.ShapeDtypeStruct((b, lc, ATT_WIDTH), BF16),
        compiler_params=_cp("parallel"),
        name="ctx_attention",
    )(sinks, uc_q, uc_kv, uc_kv, z_ac)


def _out_body(h_ref, g_ref, yhy_ref, yd_ref, yf_ref, yb_ref, xs_ref, zs_ref, yat_ref, dsk_ref, nw_ref, w_ref,
              lg_ref, lb_ref, o_ref):
    gw = SSM_WIDTH // SSM_GROUPS
    y_scan = yd_ref[0].astype(F32) + yf_ref[0].astype(F32) + yb_ref[0].astype(F32)
    ys = (y_scan + xs_ref[0].astype(F32) * dsk_ref[...]) * _silu(zs_ref[0].astype(F32))
    acc = _dot(yhy_ref[0].astype(BF16), w_ref[0:HY_WIDTH, :])
    for g in range(SSM_GROUPS):
        seg = ys[:, g * gw:(g + 1) * gw]
        seg = seg * lax.rsqrt(jnp.mean(seg * seg, -1, keepdims=True) + RMS_EPS) * nw_ref[:, g * gw:(g + 1) * gw]
        lo = HY_WIDTH + g * gw
        acc = acc + _dot(seg.astype(BF16), w_ref[lo:lo + gw, :])
    acc = acc + _dot(yat_ref[0].astype(BF16), w_ref[HY_WIDTH + SSM_WIDTH:, :])
    r = DEEPNORM_ALPHA * h_ref[0] + g_ref[0] * acc
    mu = jnp.mean(r, -1, keepdims=True)
    rc = r - mu
    var = jnp.mean(rc * rc, -1, keepdims=True)
    o_ref[0] = rc * lax.rsqrt(var + LN_EPS) * lg_ref[...] + lb_ref[...]


def _out_projection(h, gate_mod, y_hy, y_ssd, xbc, z_s, y_at, d_skip, norm_w, w_out, ln_g, ln_b):
    b, L, d = h.shape
    tm = 256
    row = lambda bi, i: (bi, i, 0)
    vec = lambda bi, i: (bi, 0, 0)
    full = lambda bi, i: (0, 0)
    w512 = pl.BlockSpec((1, tm, SSM_WIDTH), row)
    return pl.pallas_call(
        _out_body,
        grid=(b, L // tm),
        in_specs=[pl.BlockSpec((1, tm, d), row), pl.BlockSpec((1, 1, d), vec),
                  w512, w512, w512, w512, w512, w512, w512,
                  pl.BlockSpec((1, SSM_WIDTH), full), pl.BlockSpec((1, SSM_WIDTH), full),
                  pl.BlockSpec(w_out.shape, full), pl.BlockSpec((1, d), full), pl.BlockSpec((1, d), full)],
        out_specs=pl.BlockSpec((1, tm, d), row),
        out_shape=jax.ShapeDtypeStruct((b, L, d), F32),
        compiler_params=_cp("parallel", "arbitrary"),
        name="out_projection",
    )(h, gate_mod, y_hy, *y_ssd, xbc, z_s, y_at, d_skip, norm_w, w_out.astype(BF16),
      ln_g.reshape(1, d), ln_b.reshape(1, d))


def _sequence_front(h, shift, scale, w_packed, ssm_conv_w, ssm_conv_b):
    u_hy3, u_hyg, u_xbc, u_zs, u_q, u_kv, u_za, u_dt = _in_projection(h, shift, scale, w_packed)
    xbc = _dwconv(u_xbc, ssm_conv_w, ssm_conv_b, act=True, split=SSM_CONV_DIM)[0]
    return dict(hy3=u_hy3, hy_gate=u_hyg, xbc=xbc, z_s=u_zs, q=u_q, kv=u_kv, z_a=u_za, dt=u_dt)


def kernel(x, c, ctx, c_ctx, w_mod, b_mod, w_in, hy_conv_w, hy_conv_b, hy_f_w1, hy_f_b1, hy_f_w2, hy_f_b2,
           hy_f_w3, hy_f_b3, hy_f_freq, hy_f_wout, hy_bias, ssm_conv_w, ssm_conv_b, ssm_dt_bias, ssm_a_log,
           ssm_d, ssm_norm_w, attn_sinks, w_out, ln_g, ln_b):
    b, L, d = x.shape
    lc = ctx.shape[1]
    cos, sin = _rope_tables(L)
    cc = jnp.concatenate([c, c_ctx[None], jnp.zeros((16 - b - 1, d), F32)], 0)
    zero_state = jnp.zeros((b, 2, SSM_GROUPS, SSM_STATE, SSM_WIDTH // SSM_GROUPS), F32)
    h_lat, h_ctx = x, ctx
    for i in range(DEPTH):
        ctx_needed = i < DEPTH - 1
        mod = _modulation(cc, w_mod[i], b_mod[i])
        sh, sc, g = (mod[:b, None, j * d:(j + 1) * d] for j in range(3))
        sh_c, sc_c, g_c = (jnp.broadcast_to(mod[b:b + 1, None, j * d:(j + 1) * d], (b, 1, d)) for j in range(3))
        w_packed = _pack_w_in(w_in[i])
        lat = _sequence_front(h_lat, sh, sc, w_packed, ssm_conv_w[i], ssm_conv_b[i])
        cx = _sequence_front(h_ctx, sh_c, sc_c, w_packed, ssm_conv_w[i], ssm_conv_b[i])

        dt_bias_row = jnp.pad(ssm_dt_bias[i].reshape(1, -1), ((0, 0), (0, LANES - 2 * SSM_HEADS)))
        a_row = jnp.pad(-jnp.exp(ssm_a_log[i]).reshape(1, -1), ((0, 0), (0, LANES - 2 * SSM_HEADS)))
        ys_c, s_c = _ssd(cx["xbc"], cx["dt"], dt_bias_row, a_row, zero_state)
        ys, _ = _ssd(lat["xbc"], lat["dt"], dt_bias_row, a_row, s_c)

        filt = (hy_f_w1[i], hy_f_b1[i], hy_f_w2[i], hy_f_b2[i], hy_f_w3[i], hy_f_b3[i], hy_f_freq[i], hy_f_wout[i])
        kf = _hyena_spectrum(L, _hyena_filter_taps(L, *filt))
        y_hy = _hyena(lat["hy3"], lat["hy_gate"], hy_conv_w[i], hy_conv_b[i], kf, hy_bias[i])

        q_rot, k_rot = _apply_rope(lat["q"], lat["kv"], cos, sin)
        y_at = _window_attention(q_rot, k_rot, lat["kv"], cx["kv"], attn_sinks[i], lat["z_a"])

        d_skip = jnp.repeat(ssm_d[i], SSM_HEADDIM).reshape(1, SSM_WIDTH)
        norm_w = ssm_norm_w[i].reshape(1, SSM_WIDTH)
        new_lat = _out_projection(h_lat, g, y_hy, ys, lat["xbc"], lat["z_s"], y_at, d_skip, norm_w,
                                  w_out[i], ln_g[i], ln_b[i])
        if ctx_needed:
            kf_c = _hyena_spectrum(lc, _hyena_filter_taps(lc, *filt))
            y_hy_c = _hyena(cx["hy3"], cx["hy_gate"], hy_conv_w[i], hy_conv_b[i], kf_c, hy_bias[i])
            y_at_c = _ctx_attention(cx["q"], cx["kv"], attn_sinks[i], cx["z_a"])
            h_ctx = _out_projection(h_ctx, g_c, y_hy_c, ys_c, cx["xbc"], cx["z_s"], y_at_c, d_skip,
                                    norm_w, w_out[i], ln_g[i], ln_b[i])
        h_lat = new_lat
    return h_lat
```

```python
import functools
import math

import numpy as np
import jax
import jax.numpy as jnp
from jax import lax
from jax.experimental import pallas as pl
from jax.experimental.pallas import tpu as pltpu

F32 = jnp.float32
BF16 = jnp.bfloat16
HI = lax.Precision.HIGHEST

D_MODEL = 1024
DEPTH = 2
GRID_W = 64
HY_WIDTH = 512
HY_BANDS = 16
HY_EMB = 1 + 2 * HY_BANDS
HY_FILTER_HIDDEN = 64
HY_DECAY_TARGET = 1e-2
HY_FAST_PCT = 0.3
HY_SLOW_PCT = 1.5
SSM_WIDTH = 512
SSM_HEADS = 8
SSM_HEADDIM = 64
SSM_GROUPS = 2
SSM_STATE = 128
SSM_CHUNK = 128
SSM_CONV_DIM = SSM_WIDTH + 2 * SSM_GROUPS * SSM_STATE
ATT_WIDTH = 512
ATT_HEADS = 8
ATT_KV_HEADS = 2
ATT_HEADDIM = 64
ATT_GROUP = ATT_HEADS // ATT_KV_HEADS
ATT_KV = ATT_KV_HEADS * ATT_HEADDIM
WINDOW = 128
ROPE_BASE = 10000.0
HY_IN = 4 * HY_WIDTH
SSM_IN = SSM_CONV_DIM + SSM_WIDTH + 2 * SSM_HEADS
DEEPNORM_ALPHA = (2 * DEPTH) ** 0.25
LN_EPS = 1e-6
RMS_EPS = 1e-5

LANES = 128
VMEM_LIMIT = 56 * 1024 * 1024
NEG = -1e30
BIG = 1e30
LOG2E = math.log2(math.e)
FILTER_ROWS = 256


def _cp(*sem):
    return pltpu.CompilerParams(dimension_semantics=sem, vmem_limit_bytes=VMEM_LIMIT)


def _silu(x):
    return x / (1.0 + jnp.exp(-x))


def _dot(a, b, precision=None):
    return jnp.dot(a, b, preferred_element_type=F32, precision=precision)


def _dot_t(a, b):
    return lax.dot_general(a, b, (((1,), (1,)), ((), ())), preferred_element_type=F32)


def _mod_body(c_ref, w_ref, b_ref, o_ref):
    o_ref[...] = _dot(_silu(c_ref[...]), w_ref[...], HI) + b_ref[...]


def _modulation(cc, w, b):
    rows, d = cc.shape
    n = w.shape[1]
    tn = 1024
    return pl.pallas_call(
        _mod_body,
        grid=(n // tn,),
        in_specs=[pl.BlockSpec((rows, d), lambda j: (0, 0)),
                  pl.BlockSpec((d, tn), lambda j: (0, j)),
                  pl.BlockSpec((1, tn), lambda j: (0, j))],
        out_specs=pl.BlockSpec((rows, tn), lambda j: (0, j)),
        out_shape=jax.ShapeDtypeStruct((rows, n), F32),
        compiler_params=_cp("arbitrary"),
        name="modulation",
    )(cc, w, b.reshape(1, n))


IN_SEGS = (3 * HY_WIDTH, HY_WIDTH, SSM_CONV_DIM, SSM_WIDTH, ATT_WIDTH, 2 * ATT_KV, ATT_WIDTH, LANES)
IN_DTYPES = (BF16,) * (len(IN_SEGS) - 1) + (F32,)
IN_CHUNK = 512


def _pack_w_in(w):
    o_ss = HY_IN
    o_at = HY_IN + SSM_IN
    dt = w[:, o_ss + SSM_CONV_DIM + SSM_WIDTH:o_at]
    parts = [w[:, :HY_IN], w[:, o_ss:o_ss + SSM_CONV_DIM + SSM_WIDTH], w[:, o_at:],
             dt, jnp.zeros((w.shape[0], LANES - dt.shape[1]), w.dtype)]
    return jnp.concatenate(parts, axis=1).astype(BF16)


def _inproj_body(h_ref, sh_ref, sc_ref, w_ref, *o_refs):
    x = h_ref[0]
    mu = jnp.mean(x, -1, keepdims=True)
    xc = x - mu
    var = jnp.mean(xc * xc, -1, keepdims=True)
    xm = (xc * lax.rsqrt(var + LN_EPS) * (1.0 + sc_ref[0]) + sh_ref[0]).astype(BF16)
    off = 0
    for o_ref, n in zip(o_refs, IN_SEGS):
        for j in range(0, n, IN_CHUNK):
            w = min(IN_CHUNK, n - j)
            o_ref[0, :, j:j + w] = _dot(xm, w_ref[:, off + j:off + j + w]).astype(o_ref.dtype)
        off += n


def _in_projection(h, shift, scale, w_packed):
    b, L, d = h.shape
    tm = 256
    n_all = w_packed.shape[1]
    row = lambda bi, i: (bi, i, 0)
    vec = lambda bi, i: (bi, 0, 0)
    return pl.pallas_call(
        _inproj_body,
        grid=(b, L // tm),
        in_specs=[pl.BlockSpec((1, tm, d), row), pl.BlockSpec((1, 1, d), vec), pl.BlockSpec((1, 1, d), vec),
                  pl.BlockSpec((d, n_all), lambda bi, i: (0, 0))],
        out_specs=[pl.BlockSpec((1, tm, n), row) for n in IN_SEGS],
        out_shape=[jax.ShapeDtypeStruct((b, L, n), dt) for n, dt in zip(IN_SEGS, IN_DTYPES)],
        compiler_params=_cp("parallel", "arbitrary"),
        name="in_projection",
    )(h, shift, scale, w_packed)


def _dwconv_body(u_ref, w_ref, b_ref, o_ref, *, act):
    x = u_ref[0].astype(F32)
    L = x.shape[0]
    row = lax.broadcasted_iota(jnp.int32, x.shape, 0)
    prev = jnp.where(row == 0, 0.0, pltpu.roll(x, 1, 0))
    nxt = jnp.where(row == L - 1, 0.0, pltpu.roll(x, L - 1, 0))
    y = prev * w_ref[0:1, :] + x * w_ref[1:2, :] + nxt * w_ref[2:3, :] + b_ref[...]
    if act:
        y = _silu(y)
    o_ref[0, 0] = y.astype(o_ref.dtype)


def _dwconv(u, w, bias, *, act, split):
    b, L, c = u.shape
    tc = 256
    per = split // tc
    return pl.pallas_call(
        functools.partial(_dwconv_body, act=act),
        grid=(b, c // tc),
        in_specs=[pl.BlockSpec((1, L, tc), lambda bi, j: (bi, 0, j)),
                  pl.BlockSpec((3, tc), lambda bi, j: (0, j)),
                  pl.BlockSpec((1, tc), lambda bi, j: (0, j))],
        out_specs=pl.BlockSpec((1, 1, L, tc), lambda bi, j: (j // per, bi, 0, j % per)),
        out_shape=jax.ShapeDtypeStruct((c // split, b, L, split), BF16),
        compiler_params=_cp("parallel", "arbitrary"),
        name="dwconv",
    )(u, w, bias.reshape(1, c))


def _filter_features(L):
    t = jnp.linspace(0.0, 1.0, L, dtype=F32)[:, None]
    w = 2.0 * math.pi * jnp.arange(L, dtype=F32)[:, None] / L
    f = jnp.linspace(1e-4, HY_BANDS - 1, HY_BANDS, dtype=F32)[None]
    z = jnp.concatenate([t, jnp.cos(f * w), -jnp.sin(f * w)], -1)
    return jnp.pad(z, ((0, 0), (0, LANES - HY_EMB)))


def _pad_to(a, rows, cols):
    return jnp.pad(a, ((0, rows - a.shape[0]), (0, cols - a.shape[1])))


def _filter_body(z_ref, w1_ref, w2_ref, w3_ref, b_ref, fr_ref, wo_ref, ad_ref, o_ref, *, nblk):
    i = pl.program_id(0)

    @pl.when(i < nblk)
    def _():
        z = z_ref[...]
        fr = fr_ref[...]
        h = jnp.sin(fr * (_dot(z, w1_ref[...], HI) + b_ref[0:1, :]))
        h = jnp.sin(fr * (_dot(h, w2_ref[...], HI) + b_ref[1:2, :]))
        h = jnp.sin(fr * (_dot(h, w3_ref[...], HI) + b_ref[2:3, :]))
        win = jnp.exp(-z[:, 0:1] * ad_ref[...])
        h_hi = h.astype(BF16)
        h_lo = (h - h_hi.astype(F32)).astype(BF16)
        for j in range(4):
            cols = slice(j * HY_WIDTH, (j + 1) * HY_WIDTH)
            taps = _dot(h_hi, wo_ref[0, :, cols]) + _dot(h_lo, wo_ref[0, :, cols]) + _dot(h_hi, wo_ref[1, :, cols])
            o_ref[:, cols] = taps * win

    @pl.when(i == nblk)
    def _():
        o_ref[...] = jnp.zeros_like(o_ref)


def _hyena_filter_taps(L, w1, b1, w2, b2, w3, b3, freq, w_out):
    z = _filter_features(L)
    hp = LANES
    bias = jnp.stack([jnp.pad(b, (0, hp - b.shape[0])) for b in (b1, b2, b3)])
    bias = jnp.pad(bias, ((0, 5), (0, 0)))
    fr = jnp.pad(freq, (0, hp - freq.shape[0])).reshape(1, hp)
    max_decay = math.log(HY_DECAY_TARGET) / HY_FAST_PCT
    min_decay = math.log(HY_DECAY_TARGET) / HY_SLOW_PCT
    absd = jnp.abs(jnp.linspace(min_decay, max_decay, HY_WIDTH, dtype=F32)).reshape(1, HY_WIDTH)
    tl = FILTER_ROWS
    nblk = L // tl
    n = 4 * HY_WIDTH
    full = lambda i: (0, 0)
    return pl.pallas_call(
        functools.partial(_filter_body, nblk=nblk),
        grid=(nblk + 1,),
        in_specs=[pl.BlockSpec((tl, hp), lambda i: (jnp.minimum(i, nblk - 1), 0)),
                  pl.BlockSpec((hp, hp), full), pl.BlockSpec((hp, hp), full), pl.BlockSpec((hp, hp), full),
                  pl.BlockSpec((8, hp), full), pl.BlockSpec((1, hp), full),
                  pl.BlockSpec((2, hp, n), lambda i: (0, 0, 0)), pl.BlockSpec((1, HY_WIDTH), full)],
        out_specs=pl.BlockSpec((tl, n), lambda i: (i, 0)),
        out_shape=jax.ShapeDtypeStruct((L + tl, n), F32),
        compiler_params=_cp("arbitrary"),
        name="hyena_filter",
    )(z, _pad_to(w1, hp, hp), _pad_to(w2, hp, hp), _pad_to(w3, hp, hp), bias, fr, _split2(_pad_to(w_out, hp, n)),
      absd)


def _hy_cfg(L):
    n2 = 128 if L >= 2048 else 16
    n1 = 2 * L // n2
    k1n = n1 // 2 + 1
    jp = -(-2 * k1n // 16) * 16
    pitch = n2 + 8
    return dict(L=L, n2=n2, n1=n1, nh=n1 // 2, k1n=k1n, jp=jp, pitch=pitch)


@functools.lru_cache(maxsize=None)
def _dft_tables(L):
    cfg = _hy_cfg(L)
    n, n1, n2, nh, k1n, jp = 2 * L, cfg["n1"], cfg["n2"], cfg["nh"], cfg["k1n"], cfg["jp"]
    a_n1 = np.arange(n1)
    a_k1 = np.arange(k1n)
    th = 2 * np.pi * np.outer(a_k1, a_n1) / n1
    f1 = np.zeros((jp, n1))
    f1[0:2 * k1n:2] = np.cos(th)
    f1[1:2 * k1n:2] = -np.sin(th)
    a_n2 = np.arange(n2)
    m1 = np.zeros((k1n, 2 * n2, 2 * n2))
    for k in range(k1n):
        f = np.exp(-2j * np.pi * (np.outer(a_n2, a_n2) / n2 + a_n2[None, :] * k / n))
        m1[k] = np.block([[f.real, -f.imag], [f.imag, f.real]])
    ck = np.full(k1n, 2.0)
    ck[0] = 1.0
    ck[-1] = 1.0
    th6 = 2 * np.pi * np.outer(np.arange(nh), a_k1) / n1
    f6 = np.zeros((nh, jp))
    f6[:, 0:2 * k1n:2] = ck * np.cos(th6) / n
    f6[:, 1:2 * k1n:2] = -ck * np.sin(th6) / n
    as32 = lambda a: np.asarray(a, np.float32)
    return dict(f1=as32(f1), m1=as32(m1), f6=as32(f6))


def _first_stage(src_ref, a_scr, f1, cfg):
    n2, nh, jp, pitch = cfg["n2"], cfg["nh"], cfg["jp"], cfg["pitch"]
    bt = src_ref.shape[0]

    def step(i, carry):
        xs = jnp.concatenate([src_ref[t, pl.ds(2 * i + u, nh, stride=n2), :] for t in range(bt) for u in range(2)], 1)
        r = _dot(f1, xs.astype(BF16))
        for t in range(bt):
            for u in range(2):
                lo = (2 * t + u) * LANES
                a_scr[t, pl.ds(2 * i + u, jp, stride=pitch), :] = r[:, lo:lo + LANES]
        return carry

    lax.fori_loop(0, n2 // 2, step, 0, unroll=min(16 // bt, n2 // 2))


def _k1_rows(a_scr, k, cfg):
    n2, pitch = cfg["n2"], cfg["pitch"]
    base = pl.multiple_of(2 * k * pitch, 8)
    parts = [jnp.concatenate([a_scr[t, pl.ds(base, n2), :], a_scr[t, pl.ds(base + pitch, n2), :]], 0)
             for t in range(a_scr.shape[0])]
    return base, jnp.concatenate(parts, 1)


def _split2(table):
    t = jnp.asarray(table)
    hi = t.astype(BF16)
    return jnp.stack([hi, (t - hi.astype(F32)).astype(BF16)])


def _spectrum_body(kf_ref, kb_ref, f1_ref, m1_ref, o_ref, a_scr, *, cfg):
    n2, nh, jp, pitch = cfg["n2"], cfg["nh"], cfg["jp"], cfg["pitch"]

    def dot3(m_hi, m_lo, x):
        x_hi = x.astype(BF16)
        x_lo = (x - x_hi.astype(F32)).astype(BF16)
        return _dot(m_hi, x_hi) + _dot(m_lo, x_hi) + _dot(m_hi, x_lo)

    def step(i, carry):
        cols = []
        for t in range(2):
            n = 2 * i + t
            cols.append(jnp.concatenate([kf_ref[pl.ds(n, nh, stride=n2), :],
                                         kb_ref[pl.ds(n2 - n, nh, stride=n2), :]], 0))
        r = dot3(f1_ref[0], f1_ref[1], jnp.concatenate(cols, 1))
        a_scr[0, pl.ds(2 * i, jp, stride=pitch), :] = r[:, :LANES]
        a_scr[0, pl.ds(2 * i + 1, jp, stride=pitch), :] = r[:, LANES:]
        return carry

    lax.fori_loop(0, n2 // 2, step, 0, unroll=min(16, n2 // 2))
    lag0 = pl.ds(0, jp, stride=pitch)
    a_scr[0, lag0, :] = a_scr[0, lag0, :] + f1_ref[0, :, 0:1].astype(F32) * kb_ref[0:1, :]

    def mid(k, carry):
        _, a = _k1_rows(a_scr, k, cfg)
        o_ref[0, k] = dot3(m1_ref[0, k], m1_ref[1, k], a).astype(BF16)
        return carry

    lax.fori_loop(0, cfg["k1n"], mid, 0)


def _hyena_spectrum(L, taps):
    c = HY_WIDTH
    cfg = _hy_cfg(L)
    tb = _dft_tables(L)
    n1, n2, nh, k1n = cfg["n1"], cfg["n2"], cfg["nh"], cfg["k1n"]
    f1 = _split2(np.concatenate([tb["f1"][:, :nh], tb["f1"][:, n1 - 1:nh - 1:-1]], 1))
    m1 = _split2(tb["m1"])
    nct = c // LANES
    rows = taps.shape[0]
    return pl.pallas_call(
        functools.partial(_spectrum_body, cfg=cfg),
        grid=(2, nct),
        in_specs=[pl.BlockSpec((rows, LANES), lambda cv, j: (0, 2 * cv * nct + j)),
                  pl.BlockSpec((rows, LANES), lambda cv, j: (0, (2 * cv + 1) * nct + j)),
                  pl.BlockSpec(f1.shape, lambda cv, j: (0, 0, 0)),
                  pl.BlockSpec(m1.shape, lambda cv, j: (0, 0, 0, 0))],
        out_specs=pl.BlockSpec((1, k1n, 2 * n2, LANES), lambda cv, j: (cv, 0, 0, j)),
        out_shape=jax.ShapeDtypeStruct((2, k1n, 2 * n2, c), BF16),
        scratch_shapes=[pltpu.VMEM((1, cfg["jp"] * cfg["pitch"], LANES), F32)],
        compiler_params=_cp("arbitrary", "arbitrary"),
        name="hyena_spectrum",
    )(taps, taps, f1, m1)


def _short_conv(u_ref, w_ref, b_ref, which):
    x = u_ref[...].astype(F32)
    L = x.shape[0]
    row = lax.broadcasted_iota(jnp.int32, x.shape, 0)
    prev = jnp.where(row == 0, 0.0, pltpu.roll(x, 1, 0))
    nxt = jnp.where(row == L - 1, 0.0, pltpu.roll(x, L - 1, 0))
    sel = slice(which, which + 1)
    return prev * w_ref[0, sel, :] + x * w_ref[1, sel, :] + nxt * w_ref[2, sel, :] + b_ref[sel, :]


def _long_conv(src_scr, a_scr, y_scr, w_scr, f1_ref, f6_ref, m1_ref, kf_ref, conv, cfg):
    n2, nh, jp, pitch = cfg["n2"], cfg["nh"], cfg["jp"], cfg["pitch"]
    bt = src_scr.shape[0]
    _first_stage(src_scr, a_scr, f1_ref[...], cfg)

    def forward(k, carry):
        _, a = _k1_rows(a_scr, k, cfg)
        x = _dot(m1_ref[k], a.astype(BF16))
        kk = kf_ref[conv, k].astype(F32)
        kr, ki = kk[:n2], kk[n2:]
        cols = []
        for t in range(bt):
            xr, xi = x[:n2, t * LANES:(t + 1) * LANES], x[n2:, t * LANES:(t + 1) * LANES]
            cols.append(jnp.concatenate([xr * kr - xi * ki, xr * ki + xi * kr], 0))
        w_scr[k] = jnp.concatenate(cols, 1).astype(BF16)
        return carry

    def inverse(k, carry):
        base = pl.multiple_of(2 * k * pitch, 8)
        b = lax.dot_general(m1_ref[k], w_scr[k], (((0,), (0,)), ((), ())), preferred_element_type=F32)
        for t in range(bt):
            a_scr[t, pl.ds(base, n2), :] = b[:n2, t * LANES:(t + 1) * LANES]
            a_scr[t, pl.ds(base + pitch, n2), :] = b[n2:, t * LANES:(t + 1) * LANES]
        return carry

    lax.fori_loop(0, cfg["k1n"], forward, 0, unroll=3)
    lax.fori_loop(0, cfg["k1n"], inverse, 0, unroll=3)
    f6 = f6_ref[...]

    def last(i, carry):
        bs = jnp.concatenate([a_scr[t, pl.ds(2 * i + u, jp, stride=pitch), :] for t in range(bt) for u in range(2)], 1)
        y = _dot(f6, bs.astype(BF16))
        for t in range(bt):
            for u in range(2):
                lo = (2 * t + u) * LANES
                y_scr[t, pl.ds(2 * i + u, nh, stride=n2), :] = y[:, lo:lo + LANES]
        return carry

    lax.fori_loop(0, n2 // 2, last, 0, unroll=min(16 // bt, n2 // 2))


def _hyena_body(v_ref, x1_ref, x2_ref, g_ref, cw_ref, cb_ref, hb_ref, f1_ref, f6_ref, m1_ref, kf_ref,
                o_ref, s_scr, a_scr, y_scr, w_scr, *, cfg):
    tabs = (w_scr, f1_ref, f6_ref, m1_ref, kf_ref)
    slots = range(s_scr.shape[0])
    for t in slots:
        s_scr[t] = _short_conv(v_ref.at[t], cw_ref, cb_ref, 0)
    _long_conv(s_scr, a_scr, y_scr, *tabs, 0, cfg)
    for t in slots:
        s_scr[t] = _short_conv(x1_ref.at[t], cw_ref, cb_ref, 1) * (y_scr[t] + s_scr[t] * hb_ref[0:1, :])
    _long_conv(s_scr, a_scr, y_scr, *tabs, 1, cfg)
    for t in slots:
        y = _short_conv(x2_ref.at[t], cw_ref, cb_ref, 2) * (y_scr[t] + s_scr[t] * hb_ref[1:2, :])
        o_ref[t] = (y * _silu(g_ref[t].astype(F32))).astype(o_ref.dtype)


def _hyena(u_hy3, u_gate, conv_w, conv_b, kf, hy_bias):
    b, L, _ = u_gate.shape
    c = HY_WIDTH
    cfg = _hy_cfg(L)
    tb = _dft_tables(L)
    nct = c // LANES
    bt = 2 if b % 2 == 0 else 1
    col = lambda off: pl.BlockSpec((bt, L, LANES), lambda j, bi: (bi, 0, off * nct + j))
    full = lambda a: pl.BlockSpec(a.shape, lambda j, bi: (0,) * a.ndim)
    f1 = jnp.asarray(tb["f1"][:, :cfg["nh"]]).astype(BF16)
    f6 = jnp.asarray(tb["f6"]).astype(BF16)
    m1 = jnp.asarray(tb["m1"]).astype(BF16)
    return pl.pallas_call(
        functools.partial(_hyena_body, cfg=cfg),
        grid=(nct, b // bt),
        in_specs=[col(0), col(1), col(2), col(0),
                  pl.BlockSpec((3, 3, LANES), lambda j, bi: (0, 0, j)),
                  pl.BlockSpec((3, LANES), lambda j, bi: (0, j)),
                  pl.BlockSpec((2, LANES), lambda j, bi: (0, j)),
                  full(f1), full(f6), full(m1),
                  pl.BlockSpec((2, cfg["k1n"], 2 * cfg["n2"], LANES), lambda j, bi: (0, 0, 0, j),
                               pipeline_mode=pl.Buffered(1))],
        out_specs=col(0),
        out_shape=jax.ShapeDtypeStruct((b, L, c), BF16),
        scratch_shapes=[pltpu.VMEM((bt, L, LANES), F32),
                        pltpu.VMEM((bt, cfg["jp"] * cfg["pitch"], LANES), F32),
                        pltpu.VMEM((bt, L, LANES), F32),
                        pltpu.VMEM((cfg["k1n"], 2 * cfg["n2"], bt * LANES), BF16)],
        compiler_params=_cp("arbitrary", "arbitrary"),
        name="hyena",
    )(u_hy3, u_hy3, u_hy3, u_gate, conv_w.reshape(3, 3, c), conv_b.reshape(3, c), hy_bias, f1, f6, m1, kf)


SSD_CPS = 2
SPLIT_STRIDE = 2 * SSM_HEADS


def _pack3(x):
    hi = x.astype(BF16).astype(F32)
    r1 = x - hi
    mid = r1.astype(BF16).astype(F32)
    lo = (r1 - mid).astype(BF16).astype(F32)
    return (hi + pltpu.roll(mid, SPLIT_STRIDE, 1) + pltpu.roll(lo, 2 * SPLIT_STRIDE, 1)).astype(BF16)


def _unpack3(x3, used):
    return jnp.where(used, x3 + pltpu.roll(x3, LANES - SPLIT_STRIDE, 1) + pltpu.roll(x3, LANES - 2 * SPLIT_STRIDE, 1), 0.0)


@functools.lru_cache(maxsize=None)
def _ssd_spread_tables():
    col = np.zeros((LANES, 2 * SSM_HEADS * LANES), np.float32)
    head = np.zeros((LANES, 2 * SSM_WIDTH), np.float32)
    for c in range(2 * SSM_HEADS):
        d, h = divmod(c, SSM_HEADS)
        for piece in range(3):
            col[c + piece * SPLIT_STRIDE, c * LANES:(c + 1) * LANES] = 1.0
            lo = d * SSM_WIDTH + h * SSM_HEADDIM
            head[c + piece * SPLIT_STRIDE, lo:lo + SSM_HEADDIM] = 1.0
    return col, head


def _ssd_chunk_body(xbc_ref, dtr_ref, dtb_ref, a_ref, ecol_ref, ehead_ref, yd_ref, cs_ref, ex_ref, et_ref):
    q = SSM_CHUNK
    hpg = SSM_HEADS // SSM_GROUPS
    gw = SSM_WIDTH // SSM_GROUPS
    li = lax.broadcasted_iota(jnp.int32, (q, q), 0)
    si = lax.broadcasted_iota(jnp.int32, (q, q), 1)
    below = li > si
    diag = li == si
    fwd_lane = si < SSM_HEADS
    used = si < 2 * SSM_HEADS
    tril = (li >= si).astype(BF16)
    triu = (li <= si).astype(BF16)
    chunks = range(SSD_CPS)
    rows = [slice(c * q, (c + 1) * q) for c in chunks]
    ehead = ehead_ref[...]

    raws = [dtr_ref[0, r, :] + dtb_ref[...] for r in rows]
    dts = [jnp.where(used, jnp.maximum(x, 0.0) + jnp.log1p(jnp.exp(-jnp.abs(x))), 0.0) for x in raws]
    da3 = [_pack3(dt * a_ref[...]) for dt in dts]
    acs = [jnp.where(fwd_lane, _unpack3(_dot(tril, x), used), _unpack3(_dot(triu, x), used)) for x in da3]
    tots = [jnp.where(fwd_lane[0:1], a[q - 1:q, :], a[0:1, :]) for a in acs]
    ws = [dt * jnp.exp(t - a) for dt, t, a in zip(dts, tots, acs)]
    acs3 = [_pack3(a) for a in acs]
    colb = [_dot(x, ecol_ref[...]) for x in acs3]
    wx = [_dot(_pack3(w), ehead) for w in ws]
    for c in chunks:
        ex_ref[0, rows[c], :] = jnp.exp(_dot(acs3[c], ehead)).astype(BF16)
        et_ref[0, c] = jnp.exp(_dot(_pack3(jnp.broadcast_to(tots[c], (8, LANES))), ehead))
    rowt = [(a - jnp.where(dt > 0.0, jnp.log(dt), -BIG)).T for a, dt in zip(acs, dts)]
    dsum = [(dt + pltpu.roll(dt, LANES - SSM_HEADS, 1)).T for dt in dts]
    xbc = [xbc_ref[0, r, :] for r in rows]
    xsb = [x[:, :SSM_WIDTH].astype(BF16) for x in xbc]
    bgs = [[x[:, SSM_WIDTH + g * SSM_STATE:SSM_WIDTH + (g + 1) * SSM_STATE] for g in range(SSM_GROUPS)] for x in xbc]
    cgs = [[x[:, SSM_WIDTH + (SSM_GROUPS + g) * SSM_STATE:SSM_WIDTH + (SSM_GROUPS + g + 1) * SSM_STATE]
            for g in range(SSM_GROUPS)] for x in xbc]
    gmat = [[_dot_t(cgs[c][g].astype(BF16), bgs[c][g].astype(BF16)) for g in range(SSM_GROUPS)] for c in chunks]
    mats = []
    for c in chunks:
        for h in range(SSM_HEADS):
            hb = SSM_HEADS + h
            arg = jnp.where(below, colb[c][:, h * q:(h + 1) * q] - rowt[c][h:h + 1, :],
                            colb[c][:, hb * q:(hb + 1) * q] - rowt[c][hb:hb + 1, :])
            dec = jnp.where(diag, dsum[c][h:h + 1, :], jnp.exp(arg))
            mats.append((gmat[c][h // hpg] * dec).astype(BF16))
    for c in chunks:
        for h in range(SSM_HEADS):
            lo = h * SSM_HEADDIM
            yd_ref[0, rows[c], lo:lo + SSM_HEADDIM] = _dot(mats[c * SSM_HEADS + h],
                                                             xsb[c][:, lo:lo + SSM_HEADDIM]).astype(yd_ref.dtype)
    for c in chunks:
        for g in range(SSM_GROUPS):
            bgt = bgs[c][g].astype(F32).T.astype(BF16)
            xg = xbc[c][:, g * gw:(g + 1) * gw].astype(F32)
            for d in range(2):
                lo = d * SSM_WIDTH + g * gw
                cs_ref[0, c, d, g] = _dot(bgt, (xg * wx[c][:, lo:lo + gw]).astype(BF16))


def _ssd_chunks(xbc, dt_raw, dt_bias_row, a_row):
    b, L, _ = xbc.shape
    nc = L // SSM_CHUNK
    rows = SSD_CPS * SSM_CHUNK
    gw = SSM_WIDTH // SSM_GROUPS
    blk = lambda bi, i: (bi, i, 0)
    full2 = lambda bi, i: (0, 0)
    ecol, ehead = (jnp.asarray(t).astype(BF16) for t in _ssd_spread_tables())
    return pl.pallas_call(
        _ssd_chunk_body,
        grid=(b, nc // SSD_CPS),
        in_specs=[pl.BlockSpec((1, rows, SSM_CONV_DIM), blk), pl.BlockSpec((1, rows, LANES), blk),
                  pl.BlockSpec((1, LANES), full2), pl.BlockSpec((1, LANES), full2),
                  pl.BlockSpec(ecol.shape, full2), pl.BlockSpec(ehead.shape, full2)],
        out_specs=[pl.BlockSpec((1, rows, SSM_WIDTH), blk),
                   pl.BlockSpec((1, SSD_CPS, 2, SSM_GROUPS, SSM_STATE, gw), lambda bi, i: (bi, i, 0, 0, 0, 0)),
                   pl.BlockSpec((1, rows, 2 * SSM_WIDTH), blk),
                   pl.BlockSpec((1, SSD_CPS, 8, 2 * SSM_WIDTH), lambda bi, i: (bi, i, 0, 0))],
        out_shape=[jax.ShapeDtypeStruct((b, L, SSM_WIDTH), BF16),
                   jax.ShapeDtypeStruct((b, nc, 2, SSM_GROUPS, SSM_STATE, gw), F32),
                   jax.ShapeDtypeStruct((b, L, 2 * SSM_WIDTH), BF16),
                   jax.ShapeDtypeStruct((b, nc, 8, 2 * SSM_WIDTH), F32)],
        compiler_params=_cp("parallel", "arbitrary"),
        name="ssd_chunks",
    )(xbc, dt_raw, dt_bias_row, a_row, ecol, ehead)


def _ssd_state_body(cf_ref, cb_ref, xf_ref, xb_ref, ef_ref, eb_ref, sf_ref, sb_ref, init_ref, yf_ref, yb_ref, fin_ref,
                    st_ref, *, nc):
    ci = pl.program_id(1)

    @pl.when(ci == 0)
    def _():
        st_ref[...] = init_ref[0]

    gw = SSM_WIDTH // SSM_GROUPS
    dirs = ((cf_ref, xf_ref, ef_ref, sf_ref, yf_ref), (cb_ref, xb_ref, eb_ref, sb_ref, yb_ref))
    for d, (c_ref, x_ref, e_ref, s_ref, y_ref) in enumerate(dirs):
        cmat = c_ref[0].astype(BF16)
        for g in range(SSM_GROUPS):
            cols = slice(g * gw, (g + 1) * gw)
            st = st_ref[d, g]
            y_ref[0, :, cols] = (_dot(cmat[:, g * SSM_STATE:(g + 1) * SSM_STATE], st.astype(BF16))
                                 * x_ref[0, :, cols].astype(F32)).astype(y_ref.dtype)
            st_ref[d, g] = st * e_ref[0, 0, 0:1, cols] + s_ref[0, 0, 0, g]

    @pl.when(ci == nc - 1)
    def _():
        fin_ref[0] = st_ref[...]


def _ssd_states(xbc, ex, et, cs, init):
    b, L, _ = xbc.shape
    nc = L // SSM_CHUNK
    gw = SSM_WIDTH // SSM_GROUPS
    c_col = SSM_CONV_DIM // (SSM_GROUPS * SSM_STATE) - 1
    st_shape = (2, SSM_GROUPS, SSM_STATE, gw)
    st_spec = pl.BlockSpec((1,) + st_shape, lambda bi, c: (bi, 0, 0, 0, 0))
    cs_blk = (1, 1, 1, SSM_GROUPS, SSM_STATE, gw)
    return pl.pallas_call(
        functools.partial(_ssd_state_body, nc=nc),
        grid=(b, nc),
        in_specs=[pl.BlockSpec((1, SSM_CHUNK, SSM_GROUPS * SSM_STATE), lambda bi, c: (bi, c, c_col)),
                  pl.BlockSpec((1, SSM_CHUNK, SSM_GROUPS * SSM_STATE), lambda bi, c: (bi, nc - 1 - c, c_col)),
                  pl.BlockSpec((1, SSM_CHUNK, SSM_WIDTH), lambda bi, c: (bi, c, 0)),
                  pl.BlockSpec((1, SSM_CHUNK, SSM_WIDTH), lambda bi, c: (bi, nc - 1 - c, 1)),
                  pl.BlockSpec((1, 1, 8, SSM_WIDTH), lambda bi, c: (bi, c, 0, 0)),
                  pl.BlockSpec((1, 1, 8, SSM_WIDTH), lambda bi, c: (bi, nc - 1 - c, 0, 1)),
                  pl.BlockSpec(cs_blk, lambda bi, c: (bi, c, 0, 0, 0, 0)),
                  pl.BlockSpec(cs_blk, lambda bi, c: (bi, nc - 1 - c, 1, 0, 0, 0)),
                  st_spec],
        out_specs=[pl.BlockSpec((1, SSM_CHUNK, SSM_WIDTH), lambda bi, c: (bi, c, 0)),
                   pl.BlockSpec((1, SSM_CHUNK, SSM_WIDTH), lambda bi, c: (bi, nc - 1 - c, 0)),
                   st_spec],
        out_shape=[jax.ShapeDtypeStruct((b, L, SSM_WIDTH), BF16), jax.ShapeDtypeStruct((b, L, SSM_WIDTH), BF16),
                   jax.ShapeDtypeStruct((b,) + st_shape, F32)],
        scratch_shapes=[pltpu.VMEM(st_shape, F32)],
        compiler_params=_cp("parallel", "arbitrary"),
        name="ssd_states",
    )(xbc, xbc, ex, ex, et, et, cs, cs, init)


def _ssd(xbc, dt_raw, dt_bias_row, a_row, init):
    y_diag, cs, ex, et = _ssd_chunks(xbc, dt_raw, dt_bias_row, a_row)
    y_f, y_b, fin = _ssd_states(xbc, ex, et, cs, init)
    return (y_diag, y_f, y_b), fin


def _ssd_body(xbc_ref, dtr_ref, dtb_ref, a_ref, e_ref, init_ref, y_ref, fin_ref, st_ref, *, reverse, nc, d):
    ci = pl.program_id(1)

    @pl.when(ci == 0)
    def _():
        st_ref[...] = init_ref[0]

    q = SSM_CHUNK
    gw = SSM_WIDTH // SSM_GROUPS
    hpg = SSM_HEADS // SSM_GROUPS
    xbc = xbc_ref[0]
    xs = xbc[:, :SSM_WIDTH]
    raw = dtr_ref[0] + dtb_ref[...]
    dt = jnp.maximum(raw, 0.0) + jnp.log1p(jnp.exp(-jnp.abs(raw)))
    da = dt * a_ref[...]
    li = lax.broadcasted_iota(jnp.int32, (q, q), 0)
    si = lax.broadcasted_iota(jnp.int32, (q, q), 1)
    mask = (li <= si) if reverse else (li >= si)
    acs = _dot(mask.astype(F32), da, HI)
    acs_t = acs.T
    e = e_ref[...]
    dtx = _dot(dt, e, HI)
    acsx = _dot(acs, e, HI)
    last = 0 if reverse else q - 1
    totx = acsx[last:last + 1, :]
    xd = xs * dtx
    xde = (xd * jnp.exp(totx - acsx)).astype(BF16)
    xdb = xd.astype(BF16)
    eacs = jnp.exp(acsx)
    etot = jnp.exp(totx)
    for g in range(SSM_GROUPS):
        bg = xbc[:, SSM_WIDTH + g * SSM_STATE:SSM_WIDTH + (g + 1) * SSM_STATE]
        cg = xbc[:, SSM_WIDTH + (SSM_GROUPS + g) * SSM_STATE:SSM_WIDTH + (SSM_GROUPS + g + 1) * SSM_STATE]
        cgb = cg.astype(BF16)
        gmat = _dot_t(cgb, bg.astype(BF16))
        st = st_ref[g]
        y_off = _dot(cgb, st.astype(BF16)) * eacs[:, g * gw:(g + 1) * gw]
        for j in range(hpg):
            h = g * hpg + j
            col = d * SSM_HEADS + h
            seg = acs[:, col:col + 1] - acs_t[col:col + 1, :]
            lm = jnp.where(mask, jnp.exp(seg), 0.0)
            lo = h * SSM_HEADDIM
            yd = _dot((gmat * lm).astype(BF16), xdb[:, lo:lo + SSM_HEADDIM])
            y_ref[0, :, lo:lo + SSM_HEADDIM] = yd + y_off[:, j * SSM_HEADDIM:(j + 1) * SSM_HEADDIM]
        st_ref[g] = st * etot[:, g * gw:(g + 1) * gw] + _dot(bg.T.astype(BF16), xde[:, g * gw:(g + 1) * gw])

    @pl.when(ci == nc - 1)
    def _():
        fin_ref[0] = st_ref[...]


def _ssd_scan(xbc, dt_raw, dt_bias_row, a_row, expand, init, *, d):
    b, L, _ = xbc.shape
    nc = L // SSM_CHUNK
    reverse = d == 1
    chunk = (lambda bi, c: (bi, nc - 1 - c, 0)) if reverse else (lambda bi, c: (bi, c, 0))
    full2 = lambda bi, c: (0, 0)
    st_shape = (SSM_GROUPS, SSM_STATE, SSM_WIDTH // SSM_GROUPS)
    st_spec = pl.BlockSpec((1,) + st_shape, lambda bi, c: (bi, 0, 0, 0))
    return pl.pallas_call(
        functools.partial(_ssd_body, reverse=reverse, nc=nc, d=d),
        grid=(b, nc),
        in_specs=[pl.BlockSpec((1, SSM_CHUNK, SSM_CONV_DIM), chunk),
                  pl.BlockSpec((1, SSM_CHUNK, LANES), chunk),
                  pl.BlockSpec((1, LANES), full2), pl.BlockSpec((1, LANES), full2),
                  pl.BlockSpec((LANES, SSM_WIDTH), full2), st_spec],
        out_specs=[pl.BlockSpec((1, SSM_CHUNK, SSM_WIDTH), chunk), st_spec],
        out_shape=[jax.ShapeDtypeStruct((b, L, SSM_WIDTH), F32),
                   jax.ShapeDtypeStruct((b,) + st_shape, F32)],
        scratch_shapes=[pltpu.VMEM(st_shape, F32)],
        compiler_params=_cp("parallel", "arbitrary"),
        name="ssd_scan",
    )(xbc, dt_raw, dt_bias_row, a_row, expand, init)


def _ssd_expand(d):
    e = np.zeros((LANES, SSM_WIDTH), np.float32)
    for h in range(SSM_HEADS):
        e[d * SSM_HEADS + h, h * SSM_HEADDIM:(h + 1) * SSM_HEADDIM] = 1.0
    return e


def _rope_tables(L):
    rows = L // GRID_W
    row = jnp.broadcast_to(jnp.arange(rows)[:, None], (rows, GRID_W)).reshape(L)
    col = jnp.broadcast_to(jnp.arange(GRID_W)[None, :], (rows, GRID_W)).reshape(L)
    nf = ATT_HEADDIM // 4
    inv = ROPE_BASE ** (-jnp.arange(nf, dtype=F32) / nf)
    ar = row.astype(F32)[:, None] * inv
    ac = col.astype(F32)[:, None] * inv
    cos = jnp.concatenate([jnp.cos(ar), jnp.cos(ar), jnp.cos(ac), jnp.cos(ac)], -1)
    sin = jnp.concatenate([-jnp.sin(ar), jnp.sin(ar), -jnp.sin(ac), jnp.sin(ac)], -1)
    return jnp.tile(cos, (1, ATT_HEADS)), jnp.tile(sin, (1, ATT_HEADS))


def _rope(x, cos, sin):
    w = x.shape[-1]
    quarter = ATT_HEADDIM // 4
    lane = lax.broadcasted_iota(jnp.int32, x.shape, x.ndim - 1)
    partner = jnp.where((lane // quarter) % 2 == 0, pltpu.roll(x, w - quarter, x.ndim - 1),
                        pltpu.roll(x, quarter, x.ndim - 1))
    return x * cos + partner * sin


def _rope_body(q_ref, kv_ref, cos_ref, sin_ref, qo_ref, ko_ref):
    cos = cos_ref[...]
    sin = sin_ref[...]
    qo_ref[0] = (_rope(q_ref[0].astype(F32), cos, sin) * (ATT_HEADDIM ** -0.5 * LOG2E)).astype(qo_ref.dtype)
    ko_ref[0] = _rope(kv_ref[0].astype(F32), cos[:, :ATT_KV], sin[:, :ATT_KV]).astype(ko_ref.dtype)


def _apply_rope(u_q, u_kv, cos, sin):
    b, L, _ = u_q.shape
    tl = 512
    row = lambda i, bi: (bi, i, 0)
    tab = lambda i, bi: (i, 0)
    return pl.pallas_call(
        _rope_body,
        grid=(L // tl, b),
        in_specs=[pl.BlockSpec((1, tl, ATT_WIDTH), row), pl.BlockSpec((1, tl, ATT_KV), row),
                  pl.BlockSpec((tl, ATT_WIDTH), tab), pl.BlockSpec((tl, ATT_WIDTH), tab)],
        out_specs=[pl.BlockSpec((1, tl, ATT_WIDTH), row), pl.BlockSpec((1, tl, ATT_KV), row)],
        out_shape=[jax.ShapeDtypeStruct((b, L, ATT_WIDTH), BF16), jax.ShapeDtypeStruct((b, L, ATT_KV), BF16)],
        compiler_params=_cp("parallel", "arbitrary"),
        name="rope",
    )(u_q, u_kv, cos, sin)


def _wattn_body(sink_ref, bias_ref, q_ref, kp_ref, kc_ref, kn_ref, vp_ref, vc_ref, vn_ref, kx_ref, vx_ref, z_ref,
                o_ref):
    hd = ATT_HEADDIM
    low_half = lax.broadcasted_iota(jnp.int32, (WINDOW, 2 * hd), 1) < hd
    q = q_ref[0]
    bias = bias_ref[0]
    z = z_ref[0].astype(F32)
    k_all, v_ext = [], []
    for g in range(ATT_KV_HEADS):
        ks = slice(g * hd, (g + 1) * hd)
        k_all.append(jnp.concatenate([r[0, :, ks] for r in (kp_ref, kc_ref, kn_ref, kx_ref)], 0).astype(BF16))
        v_all = jnp.concatenate([r[0, :, ks] for r in (vp_ref, vc_ref, vn_ref, vx_ref)], 0).astype(BF16)
        ones = jnp.ones_like(v_all)
        v_ext.append((jnp.concatenate([v_all, ones], 1), jnp.concatenate([ones, v_all], 1)))
    heads = range(ATT_HEADS)
    sinks = [sink_ref[h] * LOG2E for h in heads]
    scores = [_dot_t(q[:, h * hd:(h + 1) * hd].astype(BF16), k_all[h // ATT_GROUP]) + bias for h in heads]
    maxes = [jnp.maximum(jnp.max(s, -1, keepdims=True), sk) for s, sk in zip(scores, sinks)]
    probs = [jnp.exp2(s - m).astype(BF16) for s, m in zip(scores, maxes)]
    exts = [_dot(p, v_ext[h // ATT_GROUP][h % 2]) for h, p in zip(heads, probs)]
    outs = [e / (pltpu.roll(e, hd, 1) + jnp.exp2(sk - m)) for e, sk, m in zip(exts, sinks, maxes)]
    for pair in range(ATT_HEADS // 2):
        cs = slice(2 * pair * hd, (2 * pair + 2) * hd)
        o_ref[0, :, cs] = (jnp.where(low_half, outs[2 * pair], outs[2 * pair + 1]) * _silu(z[:, cs])).astype(o_ref.dtype)


def _window_attention(q_rot, k_rot, u_kv, uc_kv, sinks, z_a):
    b, L, _ = q_rot.shape
    lc = uc_kv.shape[1]
    nb = L // WINDOW
    hd2 = ATT_KV
    cur = lambda bi, i: (bi, i, 0)
    prv = lambda bi, i: (bi, jnp.maximum(i - 1, 0), 0)
    nxt = lambda bi, i: (bi, jnp.minimum(i + 1, nb - 1), 0)
    vcur = lambda bi, i: (bi, i, 1)
    vprv = lambda bi, i: (bi, jnp.maximum(i - 1, 0), 1)
    vnxt = lambda bi, i: (bi, jnp.minimum(i + 1, nb - 1), 1)
    kblk = lambda f: pl.BlockSpec((1, WINDOW, hd2), f)
    nk = 3 * WINDOW + lc
    row = np.arange(WINDOW)[:, None]
    col = np.arange(nk)[None, :]
    in_prev = (col < WINDOW) & (col >= row)
    in_next = (col >= 2 * WINDOW) & (col < 3 * WINDOW) & (col - 2 * WINDOW <= row)
    always = ((col >= WINDOW) & (col < 2 * WINDOW)) | (col >= 3 * WINDOW)
    kinds = [always | in_next, always | in_prev | in_next, always | in_prev]
    if nb == 1:
        kinds = [always] * 3
    bias = jnp.asarray(np.where(np.stack(kinds), 0.0, NEG).astype(np.float32))
    kind = lambda bi, i: (jnp.where(i == 0, 0, jnp.where(i == nb - 1, 2, 1)), 0, 0)
    return pl.pallas_call(
        _wattn_body,
        grid=(b, nb),
        in_specs=[pl.BlockSpec(memory_space=pltpu.SMEM),
                  pl.BlockSpec((1, WINDOW, nk), kind),
                  pl.BlockSpec((1, WINDOW, ATT_WIDTH), cur),
                  kblk(prv), kblk(cur), kblk(nxt), kblk(vprv), kblk(vcur), kblk(vnxt),
                  pl.BlockSpec((1, lc, hd2), lambda bi, i: (bi, 0, 0)),
                  pl.BlockSpec((1, lc, hd2), lambda bi, i: (bi, 0, 1)),
                  pl.BlockSpec((1, WINDOW, ATT_WIDTH), cur)],
        out_specs=pl.BlockSpec((1, WINDOW, ATT_WIDTH), cur),
        out_shape=jax.ShapeDtypeStruct((b, L, ATT_WIDTH), BF16),
        compiler_params=_cp("parallel", "arbitrary"),
        name="window_attention",
    )(sinks, bias, q_rot, k_rot, k_rot, k_rot, u_kv, u_kv, u_kv, uc_kv, uc_kv, z_a)


def _cattn_body(sink_ref, q_ref, k_ref, v_ref, z_ref, o_ref):
    scale = ATT_HEADDIM ** -0.5
    q = q_ref[0]
    z = z_ref[0].astype(F32)
    for g in range(ATT_KV_HEADS):
        ks = slice(g * ATT_HEADDIM, (g + 1) * ATT_HEADDIM)
        k = k_ref[0, :, ks].astype(BF16)
        v = v_ref[0, :, ks].astype(BF16)
        for j in range(ATT_GROUP):
            h = g * ATT_GROUP + j
            hs = slice(h * ATT_HEADDIM, (h + 1) * ATT_HEADDIM)
            s = _dot_t(q[:, hs].astype(BF16), k) * scale
            sink = sink_ref[h]
            m = jnp.maximum(jnp.max(s, -1, keepdims=True), sink)
            p = jnp.exp(s - m)
            den = jnp.sum(p, -1, keepdims=True) + jnp.exp(sink - m)
            o_ref[0, :, hs] = (_dot(p.astype(BF16), v) / den * _silu(z[:, hs])).astype(o_ref.dtype)


def _ctx_attention(uc_q, uc_kv, sinks, z_ac):
    b, lc, _ = uc_q.shape
    blk = lambda bi: (bi, 0, 0)
    return pl.pallas_call(
        _cattn_body,
        grid=(b,),
        in_specs=[pl.BlockSpec(memory_space=pltpu.SMEM),
                  pl.BlockSpec((1, lc, ATT_WIDTH), blk),
                  pl.BlockSpec((1, lc, ATT_KV), lambda bi: (bi, 0, 0)),
                  pl.BlockSpec((1, lc, ATT_KV), lambda bi: (bi, 0, 1)),
                  pl.BlockSpec((1, lc, ATT_WIDTH), blk)],
        out_specs=pl.BlockSpec((1, lc, ATT_WIDTH), blk),
        out_shape=jax.ShapeDtypeStruct((b, lc, ATT_WIDTH), BF16),
        compiler_params=_cp("parallel"),
        name="ctx_attention",
    )(sinks, uc_q, uc_kv, uc_kv, z_ac)


def _out_body(h_ref, g_ref, yhy_ref, yd_ref, yf_ref, yb_ref, xs_ref, zs_ref, yat_ref, dsk_ref, nw_ref, w_ref,
              lg_ref, lb_ref, o_ref):
    gw = SSM_WIDTH // SSM_GROUPS
    y_scan = yd_ref[0].astype(F32) + yf_ref[0].astype(F32) + yb_ref[0].astype(F32)
    ys = (y_scan + xs_ref[0].astype(F32) * dsk_ref[...]) * _silu(zs_ref[0].astype(F32))
    acc = _dot(yhy_ref[0].astype(BF16), w_ref[0:HY_WIDTH, :])
    for g in range(SSM_GROUPS):
        seg = ys[:, g * gw:(g + 1) * gw]
        seg = seg * lax.rsqrt(jnp.mean(seg * seg, -1, keepdims=True) + RMS_EPS) * nw_ref[:, g * gw:(g + 1) * gw]
        lo = HY_WIDTH + g * gw
        acc = acc + _dot(seg.astype(BF16), w_ref[lo:lo + gw, :])
    acc = acc + _dot(yat_ref[0].astype(BF16), w_ref[HY_WIDTH + SSM_WIDTH:, :])
    r = DEEPNORM_ALPHA * h_ref[0] + g_ref[0] * acc
    mu = jnp.mean(r, -1, keepdims=True)
    rc = r - mu
    var = jnp.mean(rc * rc, -1, keepdims=True)
    o_ref[0] = rc * lax.rsqrt(var + LN_EPS) * lg_ref[...] + lb_ref[...]


def _out_projection(h, gate_mod, y_hy, y_ssd, xbc, z_s, y_at, d_skip, norm_w, w_out, ln_g, ln_b):
    b, L, d = h.shape
    tm = 256
    row = lambda bi, i: (bi, i, 0)
    vec = lambda bi, i: (bi, 0, 0)
    full = lambda bi, i: (0, 0)
    w512 = pl.BlockSpec((1, tm, SSM_WIDTH), row)
    return pl.pallas_call(
        _out_body,
        grid=(b, L // tm),
        in_specs=[pl.BlockSpec((1, tm, d), row), pl.BlockSpec((1, 1, d), vec),
                  w512, w512, w512, w512, w512, w512, w512,
                  pl.BlockSpec((1, SSM_WIDTH), full), pl.BlockSpec((1, SSM_WIDTH), full),
                  pl.BlockSpec(w_out.shape, full), pl.BlockSpec((1, d), full), pl.BlockSpec((1, d), full)],
        out_specs=pl.BlockSpec((1, tm, d), row),
        out_shape=jax.ShapeDtypeStruct((b, L, d), F32),
        compiler_params=_cp("parallel", "arbitrary"),
        name="out_projection",
    )(h, gate_mod, y_hy, *y_ssd, xbc, z_s, y_at, d_skip, norm_w, w_out.astype(BF16),
      ln_g.reshape(1, d), ln_b.reshape(1, d))


def _sequence_front(h, shift, scale, w_packed, ssm_conv_w, ssm_conv_b):
    u_hy3, u_hyg, u_xbc, u_zs, u_q, u_kv, u_za, u_dt = _in_projection(h, shift, scale, w_packed)
    xbc = _dwconv(u_xbc, ssm_conv_w, ssm_conv_b, act=True, split=SSM_CONV_DIM)[0]
    return dict(hy3=u_hy3, hy_gate=u_hyg, xbc=xbc, z_s=u_zs, q=u_q, kv=u_kv, z_a=u_za, dt=u_dt)


def kernel(x, c, ctx, c_ctx, w_mod, b_mod, w_in, hy_conv_w, hy_conv_b, hy_f_w1, hy_f_b1, hy_f_w2, hy_f_b2,
           hy_f_w3, hy_f_b3, hy_f_freq, hy_f_wout, hy_bias, ssm_conv_w, ssm_conv_b, ssm_dt_bias, ssm_a_log,
           ssm_d, ssm_norm_w, attn_sinks, w_out, ln_g, ln_b):
    b, L, d = x.shape
    lc = ctx.shape[1]
    cos, sin = _rope_tables(L)
    cc = jnp.concatenate([c, c_ctx[None], jnp.zeros((16 - b - 1, d), F32)], 0)
    zero_state = jnp.zeros((b, 2, SSM_GROUPS, SSM_STATE, SSM_WIDTH // SSM_GROUPS), F32)
    h_lat, h_ctx = x, ctx
    for i in range(DEPTH):
        ctx_needed = i < DEPTH - 1
        mod = _modulation(cc, w_mod[i], b_mod[i])
        sh, sc, g = (mod[:b, None, j * d:(j + 1) * d] for j in range(3))
        sh_c, sc_c, g_c = (jnp.broadcast_to(mod[b:b + 1, None, j * d:(j + 1) * d], (b, 1, d)) for j in range(3))
        w_packed = _pack_w_in(w_in[i])
        lat = _sequence_front(h_lat, sh, sc, w_packed, ssm_conv_w[i], ssm_conv_b[i])
        cx = _sequence_front(h_ctx, sh_c, sc_c, w_packed, ssm_conv_w[i], ssm_conv_b[i])

        dt_bias_row = jnp.pad(ssm_dt_bias[i].reshape(1, -1), ((0, 0), (0, LANES - 2 * SSM_HEADS)))
        a_row = jnp.pad(-jnp.exp(ssm_a_log[i]).reshape(1, -1), ((0, 0), (0, LANES - 2 * SSM_HEADS)))
        ys_c, s_c = _ssd(cx["xbc"], cx["dt"], dt_bias_row, a_row, zero_state)
        ys, _ = _ssd(lat["xbc"], lat["dt"], dt_bias_row, a_row, s_c)

        filt = (hy_f_w1[i], hy_f_b1[i], hy_f_w2[i], hy_f_b2[i], hy_f_w3[i], hy_f_b3[i], hy_f_freq[i], hy_f_wout[i])
        kf = _hyena_spectrum(L, _hyena_filter_taps(L, *filt))
        y_hy = _hyena(lat["hy3"], lat["hy_gate"], hy_conv_w[i], hy_conv_b[i], kf, hy_bias[i])

        q_rot, k_rot = _apply_rope(lat["q"], lat["kv"], cos, sin)
        y_at = _window_attention(q_rot, k_rot, lat["kv"], cx["kv"], attn_sinks[i], lat["z_a"])

        d_skip = jnp.repeat(ssm_d[i], SSM_HEADDIM).reshape(1, SSM_WIDTH)
        norm_w = ssm_norm_w[i].reshape(1, SSM_WIDTH)
        new_lat = _out_projection(h_lat, g, y_hy, ys, lat["xbc"], lat["z_s"], y_at, d_skip, norm_w,
                                  w_out[i], ln_g[i], ln_b[i])
        if ctx_needed:
            kf_c = _hyena_spectrum(lc, _hyena_filter_taps(lc, *filt))
            y_hy_c = _hyena(cx["hy3"], cx["hy_gate"], hy_conv_w[i], hy_conv_b[i], kf_c, hy_bias[i])
            y_at_c = _ctx_attention(cx["q"], cx["kv"], attn_sinks[i], cx["z_a"])
            h_ctx = _out_projection(h_ctx, g_c, y_hy_c, ys_c, cx["xbc"], cx["z_s"], y_at_c, d_skip,
                                    norm_w, w_out[i], ln_g[i], ln_b[i])
        h_lat = new_lat
    return h_lat
```

```python
import functools
import math

import numpy as np
import jax
import jax.numpy as jnp
from jax import lax
from jax.experimental import pallas as pl
from jax.experimental.pallas import tpu as pltpu

F32 = jnp.float32
BF16 = jnp.bfloat16
HI = lax.Precision.HIGHEST

D_MODEL = 1024
DEPTH = 2
GRID_W = 64
HY_WIDTH = 512
HY_BANDS = 16
HY_EMB = 1 + 2 * HY_BANDS
HY_FILTER_HIDDEN = 64
HY_DECAY_TARGET = 1e-2
HY_FAST_PCT = 0.3
HY_SLOW_PCT = 1.5
SSM_WIDTH = 512
SSM_HEADS = 8
SSM_HEADDIM = 64
SSM_GROUPS = 2
SSM_STATE = 128
SSM_CHUNK = 128
SSM_CONV_DIM = SSM_WIDTH + 2 * SSM_GROUPS * SSM_STATE
ATT_WIDTH = 512
ATT_HEADS = 8
ATT_KV_HEADS = 2
ATT_HEADDIM = 64
ATT_GROUP = ATT_HEADS // ATT_KV_HEADS
ATT_KV = ATT_KV_HEADS * ATT_HEADDIM
WINDOW = 128
ROPE_BASE = 10000.0
HY_IN = 4 * HY_WIDTH
SSM_IN = SSM_CONV_DIM + SSM_WIDTH + 2 * SSM_HEADS
DEEPNORM_ALPHA = (2 * DEPTH) ** 0.25
LN_EPS = 1e-6
RMS_EPS = 1e-5

LANES = 128
VMEM_LIMIT = 56 * 1024 * 1024
NEG = -1e30
BIG = 1e30
LOG2E = math.log2(math.e)
FILTER_ROWS = 256


def _cp(*sem):
    return pltpu.CompilerParams(dimension_semantics=sem, vmem_limit_bytes=VMEM_LIMIT)


def _silu(x):
    return x / (1.0 + jnp.exp(-x))


def _dot(a, b, precision=None):
    return jnp.dot(a, b, preferred_element_type=F32, precision=precision)


def _dot_t(a, b):
    return lax.dot_general(a, b, (((1,), (1,)), ((), ())), preferred_element_type=F32)


def _mod_body(c_ref, w_ref, b_ref, o_ref):
    o_ref[...] = _dot(_silu(c_ref[...]), w_ref[...], HI) + b_ref[...]


def _modulation(cc, w, b):
    rows, d = cc.shape
    n = w.shape[1]
    tn = 1024
    return pl.pallas_call(
        _mod_body,
        grid=(n // tn,),
        in_specs=[pl.BlockSpec((rows, d), lambda j: (0, 0)),
                  pl.BlockSpec((d, tn), lambda j: (0, j)),
                  pl.BlockSpec((1, tn), lambda j: (0, j))],
        out_specs=pl.BlockSpec((rows, tn), lambda j: (0, j)),
        out_shape=jax.ShapeDtypeStruct((rows, n), F32),
        compiler_params=_cp("arbitrary"),
        name="modulation",
    )(cc, w, b.reshape(1, n))


IN_SEGS = (3 * HY_WIDTH, HY_WIDTH, SSM_CONV_DIM, SSM_WIDTH, ATT_WIDTH, 2 * ATT_KV, ATT_WIDTH, LANES)
SEG_Q, SEG_KV = 4, 5
IN_DTYPES = (BF16,) * (len(IN_SEGS) - 1) + (F32,)
IN_CHUNK = 512


def _pack_w_in(w):
    o_ss = HY_IN
    o_at = HY_IN + SSM_IN
    dt = w[:, o_ss + SSM_CONV_DIM + SSM_WIDTH:o_at]
    parts = [w[:, :HY_IN], w[:, o_ss:o_ss + SSM_CONV_DIM + SSM_WIDTH], w[:, o_at:],
             dt, jnp.zeros((w.shape[0], LANES - dt.shape[1]), w.dtype)]
    return jnp.concatenate(parts, axis=1).astype(BF16)


def _inproj_body(h_ref, sh_ref, sc_ref, w_ref, *rest, rope):
    tabs, o_refs = (rest[:2], rest[2:]) if rope else ((), rest)
    x = h_ref[0]
    mu = jnp.mean(x, -1, keepdims=True)
    xc = x - mu
    var = jnp.mean(xc * xc, -1, keepdims=True)
    xm = (xc * lax.rsqrt(var + LN_EPS) * (1.0 + sc_ref[0]) + sh_ref[0]).astype(BF16)
    off = 0
    for seg, (o_ref, n) in enumerate(zip(o_refs, IN_SEGS)):
        for j in range(0, n, IN_CHUNK):
            w = min(IN_CHUNK, n - j)
            r = _dot(xm, w_ref[:, off + j:off + j + w])
            if rope and seg == SEG_Q:
                r = _rope(r, tabs[0][...], tabs[1][...]) * (ATT_HEADDIM ** -0.5 * LOG2E)
            if rope and seg == SEG_KV:
                o_ref[0, :, :ATT_KV] = _rope(r[:, :ATT_KV], tabs[0][:, :ATT_KV], tabs[1][:, :ATT_KV]).astype(o_ref.dtype)
                o_ref[0, :, ATT_KV:] = r[:, ATT_KV:].astype(o_ref.dtype)
            else:
                o_ref[0, :, j:j + w] = r.astype(o_ref.dtype)
        off += n


def _in_projection(h, shift, scale, w_packed, rope_tables=None):
    b, L, d = h.shape
    tm = 256
    n_all = w_packed.shape[1]
    row = lambda bi, i: (bi, i, 0)
    vec = lambda bi, i: (bi, 0, 0)
    rope = rope_tables is not None
    tab_specs = [pl.BlockSpec((tm, ATT_WIDTH), lambda bi, i: (i, 0))] * 2 if rope else []
    return pl.pallas_call(
        functools.partial(_inproj_body, rope=rope),
        grid=(b, L // tm),
        in_specs=[pl.BlockSpec((1, tm, d), row), pl.BlockSpec((1, 1, d), vec), pl.BlockSpec((1, 1, d), vec),
                  pl.BlockSpec((d, n_all), lambda bi, i: (0, 0))] + tab_specs,
        out_specs=[pl.BlockSpec((1, tm, n), row) for n in IN_SEGS],
        out_shape=[jax.ShapeDtypeStruct((b, L, n), dt) for n, dt in zip(IN_SEGS, IN_DTYPES)],
        compiler_params=_cp("parallel", "arbitrary"),
        name="in_projection",
    )(h, shift, scale, w_packed, *(rope_tables or ()))


def _dwconv_body(u_ref, w_ref, b_ref, o_ref, *, act):
    x = u_ref[0].astype(F32)
    L = x.shape[0]
    row = lax.broadcasted_iota(jnp.int32, x.shape, 0)
    prev = jnp.where(row == 0, 0.0, pltpu.roll(x, 1, 0))
    nxt = jnp.where(row == L - 1, 0.0, pltpu.roll(x, L - 1, 0))
    y = prev * w_ref[0:1, :] + x * w_ref[1:2, :] + nxt * w_ref[2:3, :] + b_ref[...]
    if act:
        y = _silu(y)
    o_ref[0, 0] = y.astype(o_ref.dtype)


def _dwconv(u, w, bias, *, act, split):
    b, L, c = u.shape
    tc = 256
    per = split // tc
    return pl.pallas_call(
        functools.partial(_dwconv_body, act=act),
        grid=(b, c // tc),
        in_specs=[pl.BlockSpec((1, L, tc), lambda bi, j: (bi, 0, j)),
                  pl.BlockSpec((3, tc), lambda bi, j: (0, j)),
                  pl.BlockSpec((1, tc), lambda bi, j: (0, j))],
        out_specs=pl.BlockSpec((1, 1, L, tc), lambda bi, j: (j // per, bi, 0, j % per)),
        out_shape=jax.ShapeDtypeStruct((c // split, b, L, split), BF16),
        compiler_params=_cp("parallel", "arbitrary"),
        name="dwconv",
    )(u, w, bias.reshape(1, c))


def _filter_features(L):
    t = jnp.linspace(0.0, 1.0, L, dtype=F32)[:, None]
    w = 2.0 * math.pi * jnp.arange(L, dtype=F32)[:, None] / L
    f = jnp.linspace(1e-4, HY_BANDS - 1, HY_BANDS, dtype=F32)[None]
    z = jnp.concatenate([t, jnp.cos(f * w), -jnp.sin(f * w)], -1)
    return jnp.pad(z, ((0, 0), (0, LANES - HY_EMB)))


def _pad_to(a, rows, cols):
    return jnp.pad(a, ((0, rows - a.shape[0]), (0, cols - a.shape[1])))


def _filter_body(z_ref, w1_ref, w2_ref, w3_ref, b_ref, fr_ref, wo_ref, ad_ref, o_ref, *, nblk):
    i = pl.program_id(0)

    @pl.when(i < nblk)
    def _():
        z = z_ref[...]
        fr = fr_ref[...]
        h = jnp.sin(fr * (_dot(z, w1_ref[...], HI) + b_ref[0:1, :]))
        h = jnp.sin(fr * (_dot(h, w2_ref[...], HI) + b_ref[1:2, :]))
        h = jnp.sin(fr * (_dot(h, w3_ref[...], HI) + b_ref[2:3, :]))
        win = jnp.exp(-z[:, 0:1] * ad_ref[...])
        h_hi = h.astype(BF16)
        h_lo = (h - h_hi.astype(F32)).astype(BF16)
        for j in range(4):
            cols = slice(j * HY_WIDTH, (j + 1) * HY_WIDTH)
            taps = _dot(h_hi, wo_ref[0, :, cols]) + _dot(h_lo, wo_ref[0, :, cols]) + _dot(h_hi, wo_ref[1, :, cols])
            o_ref[:, cols] = taps * win

    @pl.when(i == nblk)
    def _():
        o_ref[...] = jnp.zeros_like(o_ref)


def _hyena_filter_taps(L, w1, b1, w2, b2, w3, b3, freq, w_out):
    z = _filter_features(L)
    hp = LANES
    bias = jnp.stack([jnp.pad(b, (0, hp - b.shape[0])) for b in (b1, b2, b3)])
    bias = jnp.pad(bias, ((0, 5), (0, 0)))
    fr = jnp.pad(freq, (0, hp - freq.shape[0])).reshape(1, hp)
    max_decay = math.log(HY_DECAY_TARGET) / HY_FAST_PCT
    min_decay = math.log(HY_DECAY_TARGET) / HY_SLOW_PCT
    absd = jnp.abs(jnp.linspace(min_decay, max_decay, HY_WIDTH, dtype=F32)).reshape(1, HY_WIDTH)
    tl = FILTER_ROWS
    nblk = L // tl
    n = 4 * HY_WIDTH
    full = lambda i: (0, 0)
    return pl.pallas_call(
        functools.partial(_filter_body, nblk=nblk),
        grid=(nblk + 1,),
        in_specs=[pl.BlockSpec((tl, hp), lambda i: (jnp.minimum(i, nblk - 1), 0)),
                  pl.BlockSpec((hp, hp), full), pl.BlockSpec((hp, hp), full), pl.BlockSpec((hp, hp), full),
                  pl.BlockSpec((8, hp), full), pl.BlockSpec((1, hp), full),
                  pl.BlockSpec((2, hp, n), lambda i: (0, 0, 0)), pl.BlockSpec((1, HY_WIDTH), full)],
        out_specs=pl.BlockSpec((tl, n), lambda i: (i, 0)),
        out_shape=jax.ShapeDtypeStruct((L + tl, n), F32),
        compiler_params=_cp("arbitrary"),
        name="hyena_filter",
    )(z, _pad_to(w1, hp, hp), _pad_to(w2, hp, hp), _pad_to(w3, hp, hp), bias, fr, _split2(_pad_to(w_out, hp, n)),
      absd)


def _hy_cfg(L):
    n2 = 128 if L >= 2048 else 16
    n1 = 2 * L // n2
    k1n = n1 // 2 + 1
    jp = -(-2 * k1n // 16) * 16
    pitch = n2 + 8
    return dict(L=L, n2=n2, n1=n1, nh=n1 // 2, k1n=k1n, jp=jp, pitch=pitch)


@functools.lru_cache(maxsize=None)
def _dft_tables(L):
    cfg = _hy_cfg(L)
    n, n1, n2, nh, k1n, jp = 2 * L, cfg["n1"], cfg["n2"], cfg["nh"], cfg["k1n"], cfg["jp"]
    a_n1 = np.arange(n1)
    a_k1 = np.arange(k1n)
    th = 2 * np.pi * np.outer(a_k1, a_n1) / n1
    f1 = np.zeros((jp, n1))
    f1[0:2 * k1n:2] = np.cos(th)
    f1[1:2 * k1n:2] = -np.sin(th)
    a_n2 = np.arange(n2)
    m1 = np.zeros((k1n, 2 * n2, 2 * n2))
    for k in range(k1n):
        f = np.exp(-2j * np.pi * (np.outer(a_n2, a_n2) / n2 + a_n2[None, :] * k / n))
        m1[k] = np.block([[f.real, -f.imag], [f.imag, f.real]])
    ck = np.full(k1n, 2.0)
    ck[0] = 1.0
    ck[-1] = 1.0
    th6 = 2 * np.pi * np.outer(np.arange(nh), a_k1) / n1
    f6 = np.zeros((nh, jp))
    f6[:, 0:2 * k1n:2] = ck * np.cos(th6) / n
    f6[:, 1:2 * k1n:2] = -ck * np.sin(th6) / n
    as32 = lambda a: np.asarray(a, np.float32)
    return dict(f1=as32(f1), m1=as32(m1), f6=as32(f6))


def _first_stage(src_ref, a_scr, f1, cfg):
    n2, nh, jp, pitch = cfg["n2"], cfg["nh"], cfg["jp"], cfg["pitch"]
    bt = src_ref.shape[0]

    def step(i, carry):
        xs = jnp.concatenate([src_ref[t, pl.ds(2 * i + u, nh, stride=n2), :] for t in range(bt) for u in range(2)], 1)
        r = _dot(f1, xs.astype(BF16))
        for t in range(bt):
            for u in range(2):
                lo = (2 * t + u) * LANES
                a_scr[t, pl.ds(2 * i + u, jp, stride=pitch), :] = r[:, lo:lo + LANES]
        return carry

    lax.fori_loop(0, n2 // 2, step, 0, unroll=min(16 // bt, n2 // 2))


def _k1_rows(a_scr, k, cfg):
    n2, pitch = cfg["n2"], cfg["pitch"]
    base = pl.multiple_of(2 * k * pitch, 8)
    parts = [jnp.concatenate([a_scr[t, pl.ds(base, n2), :], a_scr[t, pl.ds(base + pitch, n2), :]], 0)
             for t in range(a_scr.shape[0])]
    return base, jnp.concatenate(parts, 1)


def _split2(table):
    t = jnp.asarray(table)
    hi = t.astype(BF16)
    return jnp.stack([hi, (t - hi.astype(F32)).astype(BF16)])


def _spectrum_body(kf_ref, kb_ref, f1_ref, m1_ref, o_ref, a_scr, *, cfg):
    n2, nh, jp, pitch = cfg["n2"], cfg["nh"], cfg["jp"], cfg["pitch"]

    def dot3(m_hi, m_lo, x):
        x_hi = x.astype(BF16)
        x_lo = (x - x_hi.astype(F32)).astype(BF16)
        return _dot(m_hi, x_hi) + _dot(m_lo, x_hi) + _dot(m_hi, x_lo)

    def step(i, carry):
        cols = []
        for t in range(2):
            n = 2 * i + t
            cols.append(jnp.concatenate([kf_ref[pl.ds(n, nh, stride=n2), :],
                                         kb_ref[pl.ds(n2 - n, nh, stride=n2), :]], 0))
        r = dot3(f1_ref[0], f1_ref[1], jnp.concatenate(cols, 1))
        a_scr[0, pl.ds(2 * i, jp, stride=pitch), :] = r[:, :LANES]
        a_scr[0, pl.ds(2 * i + 1, jp, stride=pitch), :] = r[:, LANES:]
        return carry

    lax.fori_loop(0, n2 // 2, step, 0, unroll=min(16, n2 // 2))
    lag0 = pl.ds(0, jp, stride=pitch)
    a_scr[0, lag0, :] = a_scr[0, lag0, :] + f1_ref[0, :, 0:1].astype(F32) * kb_ref[0:1, :]

    def mid(k, carry):
        _, a = _k1_rows(a_scr, k, cfg)
        o_ref[0, k] = dot3(m1_ref[0, k], m1_ref[1, k], a).astype(BF16)
        return carry

    lax.fori_loop(0, cfg["k1n"], mid, 0)


def _hyena_spectrum(L, taps):
    c = HY_WIDTH
    cfg = _hy_cfg(L)
    tb = _dft_tables(L)
    n1, n2, nh, k1n = cfg["n1"], cfg["n2"], cfg["nh"], cfg["k1n"]
    f1 = _split2(np.concatenate([tb["f1"][:, :nh], tb["f1"][:, n1 - 1:nh - 1:-1]], 1))
    m1 = _split2(tb["m1"])
    nct = c // LANES
    rows = taps.shape[0]
    return pl.pallas_call(
        functools.partial(_spectrum_body, cfg=cfg),
        grid=(2, nct),
        in_specs=[pl.BlockSpec((rows, LANES), lambda cv, j: (0, 2 * cv * nct + j)),
                  pl.BlockSpec((rows, LANES), lambda cv, j: (0, (2 * cv + 1) * nct + j)),
                  pl.BlockSpec(f1.shape, lambda cv, j: (0, 0, 0)),
                  pl.BlockSpec(m1.shape, lambda cv, j: (0, 0, 0, 0))],
        out_specs=pl.BlockSpec((1, k1n, 2 * n2, LANES), lambda cv, j: (cv, 0, 0, j)),
        out_shape=jax.ShapeDtypeStruct((2, k1n, 2 * n2, c), BF16),
        scratch_shapes=[pltpu.VMEM((1, cfg["jp"] * cfg["pitch"], LANES), F32)],
        compiler_params=_cp("arbitrary", "arbitrary"),
        name="hyena_spectrum",
    )(taps, taps, f1, m1)


def _short_conv(u_ref, w_ref, b_ref, which):
    x = u_ref[...].astype(F32)
    L = x.shape[0]
    row = lax.broadcasted_iota(jnp.int32, x.shape, 0)
    prev = jnp.where(row == 0, 0.0, pltpu.roll(x, 1, 0))
    nxt = jnp.where(row == L - 1, 0.0, pltpu.roll(x, L - 1, 0))
    sel = slice(which, which + 1)
    return prev * w_ref[0, sel, :] + x * w_ref[1, sel, :] + nxt * w_ref[2, sel, :] + b_ref[sel, :]


def _long_conv(src_scr, a_scr, y_scr, w_scr, f1_ref, f6_ref, m1_ref, kf_ref, conv, cfg):
    n2, nh, jp, pitch = cfg["n2"], cfg["nh"], cfg["jp"], cfg["pitch"]
    bt = src_scr.shape[0]
    _first_stage(src_scr, a_scr, f1_ref[...], cfg)

    def forward(k, carry):
        _, a = _k1_rows(a_scr, k, cfg)
        x = _dot(m1_ref[k], a.astype(BF16))
        kk = kf_ref[conv, k].astype(F32)
        kr, ki = kk[:n2], kk[n2:]
        cols = []
        for t in range(bt):
            xr, xi = x[:n2, t * LANES:(t + 1) * LANES], x[n2:, t * LANES:(t + 1) * LANES]
            cols.append(jnp.concatenate([xr * kr - xi * ki, xr * ki + xi * kr], 0))
        w_scr[k] = jnp.concatenate(cols, 1).astype(BF16)
        return carry

    def inverse(k, carry):
        base = pl.multiple_of(2 * k * pitch, 8)
        b = lax.dot_general(m1_ref[k], w_scr[k], (((0,), (0,)), ((), ())), preferred_element_type=F32)
        for t in range(bt):
            a_scr[t, pl.ds(base, n2), :] = b[:n2, t * LANES:(t + 1) * LANES]
            a_scr[t, pl.ds(base + pitch, n2), :] = b[n2:, t * LANES:(t + 1) * LANES]
        return carry

    lax.fori_loop(0, cfg["k1n"], forward, 0, unroll=3)
    lax.fori_loop(0, cfg["k1n"], inverse, 0, unroll=3)
    f6 = f6_ref[...]

    def last(i, carry):
        bs = jnp.concatenate([a_scr[t, pl.ds(2 * i + u, jp, stride=pitch), :] for t in range(bt) for u in range(2)], 1)
        y = _dot(f6, bs.astype(BF16))
        for t in range(bt):
            for u in range(2):
                lo = (2 * t + u) * LANES
                y_scr[t, pl.ds(2 * i + u, nh, stride=n2), :] = y[:, lo:lo + LANES]
        return carry

    lax.fori_loop(0, n2 // 2, last, 0, unroll=min(16 // bt, n2 // 2))


def _hyena_body(v_ref, x1_ref, x2_ref, g_ref, cw_ref, cb_ref, hb_ref, f1_ref, f6_ref, m1_ref, kf_ref,
                o_ref, s_scr, a_scr, y_scr, w_scr, *, cfg):
    tabs = (w_scr, f1_ref, f6_ref, m1_ref, kf_ref)
    slots = range(s_scr.shape[0])
    for t in slots:
        s_scr[t] = _short_conv(v_ref.at[t], cw_ref, cb_ref, 0)
    _long_conv(s_scr, a_scr, y_scr, *tabs, 0, cfg)
    for t in slots:
        s_scr[t] = _short_conv(x1_ref.at[t], cw_ref, cb_ref, 1) * (y_scr[t] + s_scr[t] * hb_ref[0:1, :])
    _long_conv(s_scr, a_scr, y_scr, *tabs, 1, cfg)
    for t in slots:
        y = _short_conv(x2_ref.at[t], cw_ref, cb_ref, 2) * (y_scr[t] + s_scr[t] * hb_ref[1:2, :])
        o_ref[t] = (y * _silu(g_ref[t].astype(F32))).astype(o_ref.dtype)


def _hyena(u_hy3, u_gate, conv_w, conv_b, kf, hy_bias):
    b, L, _ = u_gate.shape
    c = HY_WIDTH
    cfg = _hy_cfg(L)
    tb = _dft_tables(L)
    nct = c // LANES
    bt = 2 if b % 2 == 0 else 1
    col = lambda off: pl.BlockSpec((bt, L, LANES), lambda j, bi: (bi, 0, off * nct + j))
    full = lambda a: pl.BlockSpec(a.shape, lambda j, bi: (0,) * a.ndim)
    f1 = jnp.asarray(tb["f1"][:, :cfg["nh"]]).astype(BF16)
    f6 = jnp.asarray(tb["f6"]).astype(BF16)
    m1 = jnp.asarray(tb["m1"]).astype(BF16)
    return pl.pallas_call(
        functools.partial(_hyena_body, cfg=cfg),
        grid=(nct, b // bt),
        in_specs=[col(0), col(1), col(2), col(0),
                  pl.BlockSpec((3, 3, LANES), lambda j, bi: (0, 0, j)),
                  pl.BlockSpec((3, LANES), lambda j, bi: (0, j)),
                  pl.BlockSpec((2, LANES), lambda j, bi: (0, j)),
                  full(f1), full(f6), full(m1),
                  pl.BlockSpec((2, cfg["k1n"], 2 * cfg["n2"], LANES), lambda j, bi: (0, 0, 0, j),
                               pipeline_mode=pl.Buffered(1))],
        out_specs=col(0),
        out_shape=jax.ShapeDtypeStruct((b, L, c), BF16),
        scratch_shapes=[pltpu.VMEM((bt, L, LANES), F32),
                        pltpu.VMEM((bt, cfg["jp"] * cfg["pitch"], LANES), F32),
                        pltpu.VMEM((bt, L, LANES), F32),
                        pltpu.VMEM((cfg["k1n"], 2 * cfg["n2"], bt * LANES), BF16)],
        compiler_params=_cp("arbitrary", "arbitrary"),
        name="hyena",
    )(u_hy3, u_hy3, u_hy3, u_gate, conv_w.reshape(3, 3, c), conv_b.reshape(3, c), hy_bias, f1, f6, m1, kf)


SSD_CPS = 2
SSD_SPS = 2
SPLIT_STRIDE = 2 * SSM_HEADS


def _pack3(x):
    hi = x.astype(BF16).astype(F32)
    r1 = x - hi
    mid = r1.astype(BF16).astype(F32)
    lo = (r1 - mid).astype(BF16).astype(F32)
    return (hi + pltpu.roll(mid, SPLIT_STRIDE, 1) + pltpu.roll(lo, 2 * SPLIT_STRIDE, 1)).astype(BF16)


def _unpack3(x3, used):
    return jnp.where(used, x3 + pltpu.roll(x3, LANES - SPLIT_STRIDE, 1) + pltpu.roll(x3, LANES - 2 * SPLIT_STRIDE, 1), 0.0)


@functools.lru_cache(maxsize=None)
def _ssd_spread_tables():
    col = np.zeros((LANES, 2 * SSM_HEADS * LANES), np.float32)
    head = np.zeros((LANES, 2 * SSM_WIDTH), np.float32)
    for c in range(2 * SSM_HEADS):
        d, h = divmod(c, SSM_HEADS)
        for piece in range(3):
            col[c + piece * SPLIT_STRIDE, c * LANES:(c + 1) * LANES] = 1.0
            lo = d * SSM_WIDTH + h * SSM_HEADDIM
            head[c + piece * SPLIT_STRIDE, lo:lo + SSM_HEADDIM] = 1.0
    return col, head


def _ssd_chunk_body(xbc_ref, dtr_ref, dtb_ref, a_ref, ecol_ref, ehead_ref, yd_ref, cs_ref, ex_ref, et_ref):
    q = SSM_CHUNK
    hpg = SSM_HEADS // SSM_GROUPS
    gw = SSM_WIDTH // SSM_GROUPS
    li = lax.broadcasted_iota(jnp.int32, (q, q), 0)
    si = lax.broadcasted_iota(jnp.int32, (q, q), 1)
    below = li > si
    diag = li == si
    fwd_lane = si < SSM_HEADS
    used = si < 2 * SSM_HEADS
    tril = (li >= si).astype(BF16)
    triu = (li <= si).astype(BF16)
    chunks = range(SSD_CPS)
    rows = [slice(c * q, (c + 1) * q) for c in chunks]
    ehead = ehead_ref[...]

    raws = [dtr_ref[0, r, :] + dtb_ref[...] for r in rows]
    dts = [jnp.where(used, jnp.maximum(x, 0.0) + jnp.log1p(jnp.exp(-jnp.abs(x))), 0.0) for x in raws]
    da3 = [_pack3(dt * a_ref[...]) for dt in dts]
    acs = [jnp.where(fwd_lane, _unpack3(_dot(tril, x), used), _unpack3(_dot(triu, x), used)) for x in da3]
    tots = [jnp.where(fwd_lane[0:1], a[q - 1:q, :], a[0:1, :]) for a in acs]
    ws = [dt * jnp.exp(t - a) for dt, t, a in zip(dts, tots, acs)]
    acs3 = [_pack3(a) for a in acs]
    colb = [_dot(x, ecol_ref[...]) for x in acs3]
    wx = [_dot(_pack3(w), ehead) for w in ws]
    for c in chunks:
        ex_ref[0, rows[c], :] = jnp.exp(_dot(acs3[c], ehead)).astype(BF16)
        et_ref[0, c] = jnp.exp(_dot(_pack3(jnp.broadcast_to(tots[c], (8, LANES))), ehead))
    rowt = [(a - jnp.where(dt > 0.0, jnp.log(dt), -BIG)).T for a, dt in zip(acs, dts)]
    dsum = [(dt + pltpu.roll(dt, LANES - SSM_HEADS, 1)).T for dt in dts]
    xbc = [xbc_ref[0, r, :] for r in rows]
    xsb = [x[:, :SSM_WIDTH].astype(BF16) for x in xbc]
    bgs = [[x[:, SSM_WIDTH + g * SSM_STATE:SSM_WIDTH + (g + 1) * SSM_STATE] for g in range(SSM_GROUPS)] for x in xbc]
    cgs = [[x[:, SSM_WIDTH + (SSM_GROUPS + g) * SSM_STATE:SSM_WIDTH + (SSM_GROUPS + g + 1) * SSM_STATE]
            for g in range(SSM_GROUPS)] for x in xbc]
    gmat = [[_dot_t(cgs[c][g].astype(BF16), bgs[c][g].astype(BF16)) for g in range(SSM_GROUPS)] for c in chunks]
    mats = []
    for c in chunks:
        for h in range(SSM_HEADS):
            hb = SSM_HEADS + h
            arg = jnp.where(below, colb[c][:, h * q:(h + 1) * q] - rowt[c][h:h + 1, :],
                            colb[c][:, hb * q:(hb + 1) * q] - rowt[c][hb:hb + 1, :])
            dec = jnp.where(diag, dsum[c][h:h + 1, :], jnp.exp(arg))
            mats.append((gmat[c][h // hpg] * dec).astype(BF16))
    for c in chunks:
        for h in range(SSM_HEADS):
            lo = h * SSM_HEADDIM
            yd_ref[0, rows[c], lo:lo + SSM_HEADDIM] = _dot(mats[c * SSM_HEADS + h],
                                                             xsb[c][:, lo:lo + SSM_HEADDIM]).astype(yd_ref.dtype)
    for c in chunks:
        for g in range(SSM_GROUPS):
            bgt = bgs[c][g].astype(F32).T.astype(BF16)
            xg = xbc[c][:, g * gw:(g + 1) * gw].astype(F32)
            for d in range(2):
                lo = d * SSM_WIDTH + g * gw
                cs_ref[0, c, d, g] = _dot(bgt, (xg * wx[c][:, lo:lo + gw]).astype(BF16))


def _ssd_chunks(xbc, dt_raw, dt_bias_row, a_row):
    b, L, _ = xbc.shape
    nc = L // SSM_CHUNK
    rows = SSD_CPS * SSM_CHUNK
    gw = SSM_WIDTH // SSM_GROUPS
    blk = lambda bi, i: (bi, i, 0)
    full2 = lambda bi, i: (0, 0)
    ecol, ehead = (jnp.asarray(t).astype(BF16) for t in _ssd_spread_tables())
    return pl.pallas_call(
        _ssd_chunk_body,
        grid=(b, nc // SSD_CPS),
        in_specs=[pl.BlockSpec((1, rows, SSM_CONV_DIM), blk), pl.BlockSpec((1, rows, LANES), blk),
                  pl.BlockSpec((1, LANES), full2), pl.BlockSpec((1, LANES), full2),
                  pl.BlockSpec(ecol.shape, full2), pl.BlockSpec(ehead.shape, full2)],
        out_specs=[pl.BlockSpec((1, rows, SSM_WIDTH), blk),
                   pl.BlockSpec((1, SSD_CPS, 2, SSM_GROUPS, SSM_STATE, gw), lambda bi, i: (bi, i, 0, 0, 0, 0)),
                   pl.BlockSpec((1, rows, 2 * SSM_WIDTH), blk),
                   pl.BlockSpec((1, SSD_CPS, 8, 2 * SSM_WIDTH), lambda bi, i: (bi, i, 0, 0))],
        out_shape=[jax.ShapeDtypeStruct((b, L, SSM_WIDTH), BF16),
                   jax.ShapeDtypeStruct((b, nc, 2, SSM_GROUPS, SSM_STATE, gw), F32),
                   jax.ShapeDtypeStruct((b, L, 2 * SSM_WIDTH), BF16),
                   jax.ShapeDtypeStruct((b, nc, 8, 2 * SSM_WIDTH), F32)],
        compiler_params=_cp("parallel", "arbitrary"),
        name="ssd_chunks",
    )(xbc, dt_raw, dt_bias_row, a_row, ecol, ehead)


def _ssd_state_body(cf_ref, cb_ref, xf_ref, xb_ref, ef_ref, eb_ref, sf_ref, sb_ref, init_ref, yf_ref, yb_ref, fin_ref,
                    st_ref, *, nsteps):
    ci = pl.program_id(1)

    @pl.when(ci == 0)
    def _():
        st_ref[...] = init_ref[0]

    q = SSM_CHUNK
    gw = SSM_WIDTH // SSM_GROUPS
    for s in range(SSD_SPS):
        dirs = ((cf_ref, xf_ref, ef_ref, sf_ref, yf_ref, s), (cb_ref, xb_ref, eb_ref, sb_ref, yb_ref, SSD_SPS - 1 - s))
        for d, (c_ref, x_ref, e_ref, s_ref, y_ref, j) in enumerate(dirs):
            rows = slice(j * q, (j + 1) * q)
            cmat = c_ref[0, rows, :].astype(BF16)
            for g in range(SSM_GROUPS):
                cols = slice(g * gw, (g + 1) * gw)
                st = st_ref[d, g]
                y_ref[0, rows, cols] = (_dot(cmat[:, g * SSM_STATE:(g + 1) * SSM_STATE], st.astype(BF16))
                                        * x_ref[0, rows, cols].astype(F32)).astype(y_ref.dtype)
                st_ref[d, g] = st * e_ref[0, j, 0:1, cols] + s_ref[0, j, 0, g]

    @pl.when(ci == nsteps - 1)
    def _():
        fin_ref[0] = st_ref[...]


def _ssd_states(xbc, ex, et, cs, init):
    b, L, _ = xbc.shape
    nc = L // SSM_CHUNK
    gw = SSM_WIDTH // SSM_GROUPS
    c_col = SSM_CONV_DIM // (SSM_GROUPS * SSM_STATE) - 1
    st_shape = (2, SSM_GROUPS, SSM_STATE, gw)
    st_spec = pl.BlockSpec((1,) + st_shape, lambda bi, c: (bi, 0, 0, 0, 0))
    nsteps = nc // SSD_SPS
    rows = SSD_SPS * SSM_CHUNK
    cs_blk = (1, SSD_SPS, 1, SSM_GROUPS, SSM_STATE, gw)
    fwd = lambda *tail: (lambda bi, c: (bi, c) + tail)
    bwd = lambda *tail: (lambda bi, c: (bi, nsteps - 1 - c) + tail)
    return pl.pallas_call(
        functools.partial(_ssd_state_body, nsteps=nsteps),
        grid=(b, nsteps),
        in_specs=[pl.BlockSpec((1, rows, SSM_GROUPS * SSM_STATE), fwd(c_col)),
                  pl.BlockSpec((1, rows, SSM_GROUPS * SSM_STATE), bwd(c_col)),
                  pl.BlockSpec((1, rows, SSM_WIDTH), fwd(0)),
                  pl.BlockSpec((1, rows, SSM_WIDTH), bwd(1)),
                  pl.BlockSpec((1, SSD_SPS, 8, SSM_WIDTH), fwd(0, 0)),
                  pl.BlockSpec((1, SSD_SPS, 8, SSM_WIDTH), bwd(0, 1)),
                  pl.BlockSpec(cs_blk, fwd(0, 0, 0, 0)),
                  pl.BlockSpec(cs_blk, bwd(1, 0, 0, 0)),
                  st_spec],
        out_specs=[pl.BlockSpec((1, rows, SSM_WIDTH), fwd(0)),
                   pl.BlockSpec((1, rows, SSM_WIDTH), bwd(0)),
                   st_spec],
        out_shape=[jax.ShapeDtypeStruct((b, L, SSM_WIDTH), BF16), jax.ShapeDtypeStruct((b, L, SSM_WIDTH), BF16),
                   jax.ShapeDtypeStruct((b,) + st_shape, F32)],
        scratch_shapes=[pltpu.VMEM(st_shape, F32)],
        compiler_params=_cp("parallel", "arbitrary"),
        name="ssd_states",
    )(xbc, xbc, ex, ex, et, et, cs, cs, init)


def _ssd(xbc, dt_raw, dt_bias_row, a_row, init):
    y_diag, cs, ex, et = _ssd_chunks(xbc, dt_raw, dt_bias_row, a_row)
    y_f, y_b, fin = _ssd_states(xbc, ex, et, cs, init)
    return (y_diag, y_f, y_b), fin


def _ssd_body(xbc_ref, dtr_ref, dtb_ref, a_ref, e_ref, init_ref, y_ref, fin_ref, st_ref, *, reverse, nc, d):
    ci = pl.program_id(1)

    @pl.when(ci == 0)
    def _():
        st_ref[...] = init_ref[0]

    q = SSM_CHUNK
    gw = SSM_WIDTH // SSM_GROUPS
    hpg = SSM_HEADS // SSM_GROUPS
    xbc = xbc_ref[0]
    xs = xbc[:, :SSM_WIDTH]
    raw = dtr_ref[0] + dtb_ref[...]
    dt = jnp.maximum(raw, 0.0) + jnp.log1p(jnp.exp(-jnp.abs(raw)))
    da = dt * a_ref[...]
    li = lax.broadcasted_iota(jnp.int32, (q, q), 0)
    si = lax.broadcasted_iota(jnp.int32, (q, q), 1)
    mask = (li <= si) if reverse else (li >= si)
    acs = _dot(mask.astype(F32), da, HI)
    acs_t = acs.T
    e = e_ref[...]
    dtx = _dot(dt, e, HI)
    acsx = _dot(acs, e, HI)
    last = 0 if reverse else q - 1
    totx = acsx[last:last + 1, :]
    xd = xs * dtx
    xde = (xd * jnp.exp(totx - acsx)).astype(BF16)
    xdb = xd.astype(BF16)
    eacs = jnp.exp(acsx)
    etot = jnp.exp(totx)
    for g in range(SSM_GROUPS):
        bg = xbc[:, SSM_WIDTH + g * SSM_STATE:SSM_WIDTH + (g + 1) * SSM_STATE]
        cg = xbc[:, SSM_WIDTH + (SSM_GROUPS + g) * SSM_STATE:SSM_WIDTH + (SSM_GROUPS + g + 1) * SSM_STATE]
        cgb = cg.astype(BF16)
        gmat = _dot_t(cgb, bg.astype(BF16))
        st = st_ref[g]
        y_off = _dot(cgb, st.astype(BF16)) * eacs[:, g * gw:(g + 1) * gw]
        for j in range(hpg):
            h = g * hpg + j
            col = d * SSM_HEADS + h
            seg = acs[:, col:col + 1] - acs_t[col:col + 1, :]
            lm = jnp.where(mask, jnp.exp(seg), 0.0)
            lo = h * SSM_HEADDIM
            yd = _dot((gmat * lm).astype(BF16), xdb[:, lo:lo + SSM_HEADDIM])
            y_ref[0, :, lo:lo + SSM_HEADDIM] = yd + y_off[:, j * SSM_HEADDIM:(j + 1) * SSM_HEADDIM]
        st_ref[g] = st * etot[:, g * gw:(g + 1) * gw] + _dot(bg.T.astype(BF16), xde[:, g * gw:(g + 1) * gw])

    @pl.when(ci == nc - 1)
    def _():
        fin_ref[0] = st_ref[...]


def _ssd_scan(xbc, dt_raw, dt_bias_row, a_row, expand, init, *, d):
    b, L, _ = xbc.shape
    nc = L // SSM_CHUNK
    reverse = d == 1
    chunk = (lambda bi, c: (bi, nc - 1 - c, 0)) if reverse else (lambda bi, c: (bi, c, 0))
    full2 = lambda bi, c: (0, 0)
    st_shape = (SSM_GROUPS, SSM_STATE, SSM_WIDTH // SSM_GROUPS)
    st_spec = pl.BlockSpec((1,) + st_shape, lambda bi, c: (bi, 0, 0, 0))
    return pl.pallas_call(
        functools.partial(_ssd_body, reverse=reverse, nc=nc, d=d),
        grid=(b, nc),
        in_specs=[pl.BlockSpec((1, SSM_CHUNK, SSM_CONV_DIM), chunk),
                  pl.BlockSpec((1, SSM_CHUNK, LANES), chunk),
                  pl.BlockSpec((1, LANES), full2), pl.BlockSpec((1, LANES), full2),
                  pl.BlockSpec((LANES, SSM_WIDTH), full2), st_spec],
        out_specs=[pl.BlockSpec((1, SSM_CHUNK, SSM_WIDTH), chunk), st_spec],
        out_shape=[jax.ShapeDtypeStruct((b, L, SSM_WIDTH), F32),
                   jax.ShapeDtypeStruct((b,) + st_shape, F32)],
        scratch_shapes=[pltpu.VMEM(st_shape, F32)],
        compiler_params=_cp("parallel", "arbitrary"),
        name="ssd_scan",
    )(xbc, dt_raw, dt_bias_row, a_row, expand, init)


def _ssd_expand(d):
    e = np.zeros((LANES, SSM_WIDTH), np.float32)
    for h in range(SSM_HEADS):
        e[d * SSM_HEADS + h, h * SSM_HEADDIM:(h + 1) * SSM_HEADDIM] = 1.0
    return e


def _rope_tables(L):
    rows = L // GRID_W
    row = jnp.broadcast_to(jnp.arange(rows)[:, None], (rows, GRID_W)).reshape(L)
    col = jnp.broadcast_to(jnp.arange(GRID_W)[None, :], (rows, GRID_W)).reshape(L)
    nf = ATT_HEADDIM // 4
    inv = ROPE_BASE ** (-jnp.arange(nf, dtype=F32) / nf)
    ar = row.astype(F32)[:, None] * inv
    ac = col.astype(F32)[:, None] * inv
    cos = jnp.concatenate([jnp.cos(ar), jnp.cos(ar), jnp.cos(ac), jnp.cos(ac)], -1)
    sin = jnp.concatenate([-jnp.sin(ar), jnp.sin(ar), -jnp.sin(ac), jnp.sin(ac)], -1)
    return jnp.tile(cos, (1, ATT_HEADS)), jnp.tile(sin, (1, ATT_HEADS))


def _rope(x, cos, sin):
    w = x.shape[-1]
    quarter = ATT_HEADDIM // 4
    lane = lax.broadcasted_iota(jnp.int32, x.shape, x.ndim - 1)
    partner = jnp.where((lane // quarter) % 2 == 0, pltpu.roll(x, w - quarter, x.ndim - 1),
                        pltpu.roll(x, quarter, x.ndim - 1))
    return x * cos + partner * sin


def _rope_body(q_ref, kv_ref, cos_ref, sin_ref, qo_ref, ko_ref):
    cos = cos_ref[...]
    sin = sin_ref[...]
    qo_ref[0] = (_rope(q_ref[0].astype(F32), cos, sin) * (ATT_HEADDIM ** -0.5 * LOG2E)).astype(qo_ref.dtype)
    ko_ref[0] = _rope(kv_ref[0].astype(F32), cos[:, :ATT_KV], sin[:, :ATT_KV]).astype(ko_ref.dtype)


def _apply_rope(u_q, u_kv, cos, sin):
    b, L, _ = u_q.shape
    tl = 512
    row = lambda i, bi: (bi, i, 0)
    tab = lambda i, bi: (i, 0)
    return pl.pallas_call(
        _rope_body,
        grid=(L // tl, b),
        in_specs=[pl.BlockSpec((1, tl, ATT_WIDTH), row), pl.BlockSpec((1, tl, ATT_KV), row),
                  pl.BlockSpec((tl, ATT_WIDTH), tab), pl.BlockSpec((tl, ATT_WIDTH), tab)],
        out_specs=[pl.BlockSpec((1, tl, ATT_WIDTH), row), pl.BlockSpec((1, tl, ATT_KV), row)],
        out_shape=[jax.ShapeDtypeStruct((b, L, ATT_WIDTH), BF16), jax.ShapeDtypeStruct((b, L, ATT_KV), BF16)],
        compiler_params=_cp("parallel", "arbitrary"),
        name="rope",
    )(u_q, u_kv, cos, sin)


def _wattn_body(sink_ref, bias_ref, q_ref, kp_ref, kc_ref, kn_ref, vp_ref, vc_ref, vn_ref, kx_ref, vx_ref, z_ref,
                o_ref):
    hd = ATT_HEADDIM
    low_half = lax.broadcasted_iota(jnp.int32, (WINDOW, 2 * hd), 1) < hd
    q = q_ref[0]
    bias = bias_ref[0]
    z = z_ref[0].astype(F32)
    k_all, v_ext = [], []
    for g in range(ATT_KV_HEADS):
        ks = slice(g * hd, (g + 1) * hd)
        k_all.append(jnp.concatenate([r[0, :, ks] for r in (kp_ref, kc_ref, kn_ref, kx_ref)], 0).astype(BF16))
        v_all = jnp.concatenate([r[0, :, ks] for r in (vp_ref, vc_ref, vn_ref, vx_ref)], 0).astype(BF16)
        ones = jnp.ones_like(v_all)
        v_ext.append((jnp.concatenate([v_all, ones], 1), jnp.concatenate([ones, v_all], 1)))
    heads = range(ATT_HEADS)
    sinks = [sink_ref[h] * LOG2E for h in heads]
    scores = [_dot_t(q[:, h * hd:(h + 1) * hd].astype(BF16), k_all[h // ATT_GROUP]) + bias for h in heads]
    maxes = [jnp.maximum(jnp.max(s, -1, keepdims=True), sk) for s, sk in zip(scores, sinks)]
    probs = [jnp.exp2(s - m).astype(BF16) for s, m in zip(scores, maxes)]
    exts = [_dot(p, v_ext[h // ATT_GROUP][h % 2]) for h, p in zip(heads, probs)]
    outs = [e / (pltpu.roll(e, hd, 1) + jnp.exp2(sk - m)) for e, sk, m in zip(exts, sinks, maxes)]
    for pair in range(ATT_HEADS // 2):
        cs = slice(2 * pair * hd, (2 * pair + 2) * hd)
        o_ref[0, :, cs] = (jnp.where(low_half, outs[2 * pair], outs[2 * pair + 1]) * _silu(z[:, cs])).astype(o_ref.dtype)


def _window_attention(q_rot, u_kv, uc_kv, sinks, z_a):
    b, L, _ = q_rot.shape
    lc = uc_kv.shape[1]
    nb = L // WINDOW
    hd2 = ATT_KV
    cur = lambda bi, i: (bi, i, 0)
    prv = lambda bi, i: (bi, jnp.maximum(i - 1, 0), 0)
    nxt = lambda bi, i: (bi, jnp.minimum(i + 1, nb - 1), 0)
    vcur = lambda bi, i: (bi, i, 1)
    vprv = lambda bi, i: (bi, jnp.maximum(i - 1, 0), 1)
    vnxt = lambda bi, i: (bi, jnp.minimum(i + 1, nb - 1), 1)
    kblk = lambda f: pl.BlockSpec((1, WINDOW, hd2), f)
    nk = 3 * WINDOW + lc
    row = np.arange(WINDOW)[:, None]
    col = np.arange(nk)[None, :]
    in_prev = (col < WINDOW) & (col >= row)
    in_next = (col >= 2 * WINDOW) & (col < 3 * WINDOW) & (col - 2 * WINDOW <= row)
    always = ((col >= WINDOW) & (col < 2 * WINDOW)) | (col >= 3 * WINDOW)
    kinds = [always | in_next, always | in_prev | in_next, always | in_prev]
    if nb == 1:
        kinds = [always] * 3
    bias = jnp.asarray(np.where(np.stack(kinds), 0.0, NEG).astype(np.float32))
    kind = lambda bi, i: (jnp.where(i == 0, 0, jnp.where(i == nb - 1, 2, 1)), 0, 0)
    return pl.pallas_call(
        _wattn_body,
        grid=(b, nb),
        in_specs=[pl.BlockSpec(memory_space=pltpu.SMEM),
                  pl.BlockSpec((1, WINDOW, nk), kind),
                  pl.BlockSpec((1, WINDOW, ATT_WIDTH), cur),
                  kblk(prv), kblk(cur), kblk(nxt), kblk(vprv), kblk(vcur), kblk(vnxt),
                  pl.BlockSpec((1, lc, hd2), lambda bi, i: (bi, 0, 0)),
                  pl.BlockSpec((1, lc, hd2), lambda bi, i: (bi, 0, 1)),
                  pl.BlockSpec((1, WINDOW, ATT_WIDTH), cur)],
        out_specs=pl.BlockSpec((1, WINDOW, ATT_WIDTH), cur),
        out_shape=jax.ShapeDtypeStruct((b, L, ATT_WIDTH), BF16),
        compiler_params=_cp("parallel", "arbitrary"),
        name="window_attention",
    )(sinks, bias, q_rot, u_kv, u_kv, u_kv, u_kv, u_kv, u_kv, uc_kv, uc_kv, z_a)


def _cattn_body(sink_ref, q_ref, k_ref, v_ref, z_ref, o_ref):
    scale = ATT_HEADDIM ** -0.5
    q = q_ref[0]
    z = z_ref[0].astype(F32)
    for g in range(ATT_KV_HEADS):
        ks = slice(g * ATT_HEADDIM, (g + 1) * ATT_HEADDIM)
        k = k_ref[0, :, ks].astype(BF16)
        v = v_ref[0, :, ks].astype(BF16)
        for j in range(ATT_GROUP):
            h = g * ATT_GROUP + j
            hs = slice(h * ATT_HEADDIM, (h + 1) * ATT_HEADDIM)
            s = _dot_t(q[:, hs].astype(BF16), k) * scale
            sink = sink_ref[h]
            m = jnp.maximum(jnp.max(s, -1, keepdims=True), sink)
            p = jnp.exp(s - m)
            den = jnp.sum(p, -1, keepdims=True) + jnp.exp(sink - m)
            o_ref[0, :, hs] = (_dot(p.astype(BF16), v) / den * _silu(z[:, hs])).astype(o_ref.dtype)


def _ctx_attention(uc_q, uc_kv, sinks, z_ac):
    b, lc, _ = uc_q.shape
    blk = lambda bi: (bi, 0, 0)
    return pl.pallas_call(
        _cattn_body,
        grid=(b,),
        in_specs=[pl.BlockSpec(memory_space=pltpu.SMEM),
                  pl.BlockSpec((1, lc, ATT_WIDTH), blk),
                  pl.BlockSpec((1, lc, ATT_KV), lambda bi: (bi, 0, 0)),
                  pl.BlockSpec((1, lc, ATT_KV), lambda bi: (bi, 0, 1)),
                  pl.BlockSpec((1, lc, ATT_WIDTH), blk)],
        out_specs=pl.BlockSpec((1, lc, ATT_WIDTH), blk),
        out_shape=jax.ShapeDtypeStruct((b, lc, ATT_WIDTH), BF16),
        compiler_params=_cp("parallel"),
        name="ctx_attention",
    )(sinks, uc_q, uc_kv, uc_kv, z_ac)


def _out_body(h_ref, g_ref, yhy_ref, yd_ref, yf_ref, yb_ref, xs_ref, zs_ref, yat_ref, dsk_ref, nw_ref, w_ref,
              lg_ref, lb_ref, o_ref):
    gw = SSM_WIDTH // SSM_GROUPS
    y_scan = yd_ref[0].astype(F32) + yf_ref[0].astype(F32) + yb_ref[0].astype(F32)
    ys = (y_scan + xs_ref[0].astype(F32) * dsk_ref[...]) * _silu(zs_ref[0].astype(F32))
    acc = _dot(yhy_ref[0].astype(BF16), w_ref[0:HY_WIDTH, :])
    for g in range(SSM_GROUPS):
        seg = ys[:, g * gw:(g + 1) * gw]
        seg = seg * lax.rsqrt(jnp.mean(seg * seg, -1, keepdims=True) + RMS_EPS) * nw_ref[:, g * gw:(g + 1) * gw]
        lo = HY_WIDTH + g * gw
        acc = acc + _dot(seg.astype(BF16), w_ref[lo:lo + gw, :])
    acc = acc + _dot(yat_ref[0].astype(BF16), w_ref[HY_WIDTH + SSM_WIDTH:, :])
    r = DEEPNORM_ALPHA * h_ref[0] + g_ref[0] * acc
    mu = jnp.mean(r, -1, keepdims=True)
    rc = r - mu
    var = jnp.mean(rc * rc, -1, keepdims=True)
    o_ref[0] = rc * lax.rsqrt(var + LN_EPS) * lg_ref[...] + lb_ref[...]


def _out_projection(h, gate_mod, y_hy, y_ssd, xbc, z_s, y_at, d_skip, norm_w, w_out, ln_g, ln_b):
    b, L, d = h.shape
    tm = 256
    row = lambda bi, i: (bi, i, 0)
    vec = lambda bi, i: (bi, 0, 0)
    full = lambda bi, i: (0, 0)
    w512 = pl.BlockSpec((1, tm, SSM_WIDTH), row)
    return pl.pallas_call(
        _out_body,
        grid=(b, L // tm),
        in_specs=[pl.BlockSpec((1, tm, d), row), pl.BlockSpec((1, 1, d), vec),
                  w512, w512, w512, w512, w512, w512, w512,
                  pl.BlockSpec((1, SSM_WIDTH), full), pl.BlockSpec((1, SSM_WIDTH), full),
                  pl.BlockSpec(w_out.shape, full), pl.BlockSpec((1, d), full), pl.BlockSpec((1, d), full)],
        out_specs=pl.BlockSpec((1, tm, d), row),
        out_shape=jax.ShapeDtypeStruct((b, L, d), F32),
        compiler_params=_cp("parallel", "arbitrary"),
        name="out_projection",
    )(h, gate_mod, y_hy, *y_ssd, xbc, z_s, y_at, d_skip, norm_w, w_out.astype(BF16),
      ln_g.reshape(1, d), ln_b.reshape(1, d))


def _sequence_front(h, shift, scale, w_packed, ssm_conv_w, ssm_conv_b, rope_tables=None):
    u_hy3, u_hyg, u_xbc, u_zs, u_q, u_kv, u_za, u_dt = _in_projection(h, shift, scale, w_packed, rope_tables)
    xbc = _dwconv(u_xbc, ssm_conv_w, ssm_conv_b, act=True, split=SSM_CONV_DIM)[0]
    return dict(hy3=u_hy3, hy_gate=u_hyg, xbc=xbc, z_s=u_zs, q=u_q, kv=u_kv, z_a=u_za, dt=u_dt)


def kernel(x, c, ctx, c_ctx, w_mod, b_mod, w_in, hy_conv_w, hy_conv_b, hy_f_w1, hy_f_b1, hy_f_w2, hy_f_b2,
           hy_f_w3, hy_f_b3, hy_f_freq, hy_f_wout, hy_bias, ssm_conv_w, ssm_conv_b, ssm_dt_bias, ssm_a_log,
           ssm_d, ssm_norm_w, attn_sinks, w_out, ln_g, ln_b):
    b, L, d = x.shape
    lc = ctx.shape[1]
    cos, sin = _rope_tables(L)
    cc = jnp.concatenate([c, c_ctx[None], jnp.zeros((16 - b - 1, d), F32)], 0)
    zero_state = jnp.zeros((b, 2, SSM_GROUPS, SSM_STATE, SSM_WIDTH // SSM_GROUPS), F32)
    h_lat, h_ctx = x, ctx
    for i in range(DEPTH):
        ctx_needed = i < DEPTH - 1
        mod = _modulation(cc, w_mod[i], b_mod[i])
        sh, sc, g = (mod[:b, None, j * d:(j + 1) * d] for j in range(3))
        sh_c, sc_c, g_c = (jnp.broadcast_to(mod[b:b + 1, None, j * d:(j + 1) * d], (b, 1, d)) for j in range(3))
        w_packed = _pack_w_in(w_in[i])
        lat = _sequence_front(h_lat, sh, sc, w_packed, ssm_conv_w[i], ssm_conv_b[i], (cos, sin))
        cx = _sequence_front(h_ctx, sh_c, sc_c, w_packed, ssm_conv_w[i], ssm_conv_b[i])

        dt_bias_row = jnp.pad(ssm_dt_bias[i].reshape(1, -1), ((0, 0), (0, LANES - 2 * SSM_HEADS)))
        a_row = jnp.pad(-jnp.exp(ssm_a_log[i]).reshape(1, -1), ((0, 0), (0, LANES - 2 * SSM_HEADS)))
        ys_c, s_c = _ssd(cx["xbc"], cx["dt"], dt_bias_row, a_row, zero_state)
        ys, _ = _ssd(lat["xbc"], lat["dt"], dt_bias_row, a_row, s_c)

        filt = (hy_f_w1[i], hy_f_b1[i], hy_f_w2[i], hy_f_b2[i], hy_f_w3[i], hy_f_b3[i], hy_f_freq[i], hy_f_wout[i])
        kf = _hyena_spectrum(L, _hyena_filter_taps(L, *filt))
        y_hy = _hyena(lat["hy3"], lat["hy_gate"], hy_conv_w[i], hy_conv_b[i], kf, hy_bias[i])

        y_at = _window_attention(lat["q"], lat["kv"], cx["kv"], attn_sinks[i], lat["z_a"])

        d_skip = jnp.repeat(ssm_d[i], SSM_HEADDIM).reshape(1, SSM_WIDTH)
        norm_w = ssm_norm_w[i].reshape(1, SSM_WIDTH)
        new_lat = _out_projection(h_lat, g, y_hy, ys, lat["xbc"], lat["z_s"], y_at, d_skip, norm_w,
                                  w_out[i], ln_g[i], ln_b[i])
        if ctx_needed:
            kf_c = _hyena_spectrum(lc, _hyena_filter_taps(lc, *filt))
            y_hy_c = _hyena(cx["hy3"], cx["hy_gate"], hy_conv_w[i], hy_conv_b[i], kf_c, hy_bias[i])
            y_at_c = _ctx_attention(cx["q"], cx["kv"], attn_sinks[i], cx["z_a"])
            h_ctx = _out_projection(h_ctx, g_c, y_hy_c, ys_c, cx["xbc"], cx["z_s"], y_at_c, d_skip,
                                    norm_w, w_out[i], ln_g[i], ln_b[i])
        h_lat = new_lat
    return h_lat
```

```python
import functools
import math

import numpy as np
import jax
import jax.numpy as jnp
from jax import lax
from jax.experimental import pallas as pl
from jax.experimental.pallas import tpu as pltpu

F32 = jnp.float32
BF16 = jnp.bfloat16
HI = lax.Precision.HIGHEST

D_MODEL = 1024
DEPTH = 2
GRID_W = 64
HY_WIDTH = 512
HY_BANDS = 16
HY_EMB = 1 + 2 * HY_BANDS
HY_FILTER_HIDDEN = 64
HY_DECAY_TARGET = 1e-2
HY_FAST_PCT = 0.3
HY_SLOW_PCT = 1.5
SSM_WIDTH = 512
SSM_HEADS = 8
SSM_HEADDIM = 64
SSM_GROUPS = 2
SSM_STATE = 128
SSM_CHUNK = 128
SSM_CONV_DIM = SSM_WIDTH + 2 * SSM_GROUPS * SSM_STATE
ATT_WIDTH = 512
ATT_HEADS = 8
ATT_KV_HEADS = 2
ATT_HEADDIM = 64
ATT_GROUP = ATT_HEADS // ATT_KV_HEADS
ATT_KV = ATT_KV_HEADS * ATT_HEADDIM
WINDOW = 128
ROPE_BASE = 10000.0
HY_IN = 4 * HY_WIDTH
SSM_IN = SSM_CONV_DIM + SSM_WIDTH + 2 * SSM_HEADS
DEEPNORM_ALPHA = (2 * DEPTH) ** 0.25
LN_EPS = 1e-6
RMS_EPS = 1e-5

LANES = 128
VMEM_LIMIT = 56 * 1024 * 1024
NEG = -1e30
BIG = 1e30
LOG2E = math.log2(math.e)
FILTER_ROWS = 256


def _cp(*sem):
    return pltpu.CompilerParams(dimension_semantics=sem, vmem_limit_bytes=VMEM_LIMIT)


def _silu(x):
    return x / (1.0 + jnp.exp(-x))


def _dot(a, b, precision=None):
    return jnp.dot(a, b, preferred_element_type=F32, precision=precision)


def _dot_t(a, b):
    return lax.dot_general(a, b, (((1,), (1,)), ((), ())), preferred_element_type=F32)


def _mod_body(c_ref, w_ref, b_ref, o_ref):
    o_ref[...] = _dot(_silu(c_ref[...]), w_ref[...], HI) + b_ref[...]


def _modulation(cc, w, b):
    rows, d = cc.shape
    n = w.shape[1]
    tn = 1024
    return pl.pallas_call(
        _mod_body,
        grid=(n // tn,),
        in_specs=[pl.BlockSpec((rows, d), lambda j: (0, 0)),
                  pl.BlockSpec((d, tn), lambda j: (0, j)),
                  pl.BlockSpec((1, tn), lambda j: (0, j))],
        out_specs=pl.BlockSpec((rows, tn), lambda j: (0, j)),
        out_shape=jax.ShapeDtypeStruct((rows, n), F32),
        compiler_params=_cp("arbitrary"),
        name="modulation",
    )(cc, w, b.reshape(1, n))


IN_SEGS = (3 * HY_WIDTH, HY_WIDTH, SSM_CONV_DIM, SSM_WIDTH, ATT_WIDTH, 2 * ATT_KV, ATT_WIDTH, LANES)
SEG_Q, SEG_KV = 4, 5
IN_DTYPES = (BF16,) * (len(IN_SEGS) - 1) + (F32,)
IN_CHUNK = 512


def _pack_w_in(w):
    o_ss = HY_IN
    o_at = HY_IN + SSM_IN
    dt = w[:, o_ss + SSM_CONV_DIM + SSM_WIDTH:o_at]
    parts = [w[:, :HY_IN], w[:, o_ss:o_ss + SSM_CONV_DIM + SSM_WIDTH], w[:, o_at:],
             dt, jnp.zeros((w.shape[0], LANES - dt.shape[1]), w.dtype)]
    return jnp.concatenate(parts, axis=1).astype(BF16)


def _inproj_body(h_ref, sh_ref, sc_ref, w_ref, *rest, rope):
    tabs, o_refs = (rest[:2], rest[2:]) if rope else ((), rest)
    x = h_ref[0]
    mu = jnp.mean(x, -1, keepdims=True)
    xc = x - mu
    var = jnp.mean(xc * xc, -1, keepdims=True)
    xm = (xc * lax.rsqrt(var + LN_EPS) * (1.0 + sc_ref[0]) + sh_ref[0]).astype(BF16)
    off = 0
    for seg, (o_ref, n) in enumerate(zip(o_refs, IN_SEGS)):
        for j in range(0, n, IN_CHUNK):
            w = min(IN_CHUNK, n - j)
            r = _dot(xm, w_ref[:, off + j:off + j + w])
            if rope and seg == SEG_Q:
                r = _rope(r, tabs[0][...], tabs[1][...]) * (ATT_HEADDIM ** -0.5 * LOG2E)
            if rope and seg == SEG_KV:
                o_ref[0, :, :ATT_KV] = _rope(r[:, :ATT_KV], tabs[0][:, :ATT_KV], tabs[1][:, :ATT_KV]).astype(o_ref.dtype)
                o_ref[0, :, ATT_KV:] = r[:, ATT_KV:].astype(o_ref.dtype)
            else:
                o_ref[0, :, j:j + w] = r.astype(o_ref.dtype)
        off += n


def _in_projection(h, shift, scale, w_packed, rope_tables=None):
    b, L, d = h.shape
    tm = 256
    n_all = w_packed.shape[1]
    row = lambda bi, i: (bi, i, 0)
    vec = lambda bi, i: (bi, 0, 0)
    rope = rope_tables is not None
    tab_specs = [pl.BlockSpec((tm, ATT_WIDTH), lambda bi, i: (i, 0))] * 2 if rope else []
    return pl.pallas_call(
        functools.partial(_inproj_body, rope=rope),
        grid=(b, L // tm),
        in_specs=[pl.BlockSpec((1, tm, d), row), pl.BlockSpec((1, 1, d), vec), pl.BlockSpec((1, 1, d), vec),
                  pl.BlockSpec((d, n_all), lambda bi, i: (0, 0))] + tab_specs,
        out_specs=[pl.BlockSpec((1, tm, n), row) for n in IN_SEGS],
        out_shape=[jax.ShapeDtypeStruct((b, L, n), dt) for n, dt in zip(IN_SEGS, IN_DTYPES)],
        compiler_params=_cp("parallel", "arbitrary"),
        name="in_projection",
    )(h, shift, scale, w_packed, *(rope_tables or ()))


def _dwconv_body(u_ref, w_ref, b_ref, o_ref, *, act):
    x = u_ref[0].astype(F32)
    L = x.shape[0]
    row = lax.broadcasted_iota(jnp.int32, x.shape, 0)
    prev = jnp.where(row == 0, 0.0, pltpu.roll(x, 1, 0))
    nxt = jnp.where(row == L - 1, 0.0, pltpu.roll(x, L - 1, 0))
    y = prev * w_ref[0:1, :] + x * w_ref[1:2, :] + nxt * w_ref[2:3, :] + b_ref[...]
    if act:
        y = _silu(y)
    o_ref[0, 0] = y.astype(o_ref.dtype)


def _dwconv(u, w, bias, *, act, split):
    b, L, c = u.shape
    tc = 256
    per = split // tc
    return pl.pallas_call(
        functools.partial(_dwconv_body, act=act),
        grid=(b, c // tc),
        in_specs=[pl.BlockSpec((1, L, tc), lambda bi, j: (bi, 0, j)),
                  pl.BlockSpec((3, tc), lambda bi, j: (0, j)),
                  pl.BlockSpec((1, tc), lambda bi, j: (0, j))],
        out_specs=pl.BlockSpec((1, 1, L, tc), lambda bi, j: (j // per, bi, 0, j % per)),
        out_shape=jax.ShapeDtypeStruct((c // split, b, L, split), BF16),
        compiler_params=_cp("parallel", "arbitrary"),
        name="dwconv",
    )(u, w, bias.reshape(1, c))


def _filter_features(L):
    t = jnp.linspace(0.0, 1.0, L, dtype=F32)[:, None]
    w = 2.0 * math.pi * jnp.arange(L, dtype=F32)[:, None] / L
    f = jnp.linspace(1e-4, HY_BANDS - 1, HY_BANDS, dtype=F32)[None]
    z = jnp.concatenate([t, jnp.cos(f * w), -jnp.sin(f * w)], -1)
    return jnp.pad(z, ((0, 0), (0, LANES - HY_EMB)))


def _pad_to(a, rows, cols):
    return jnp.pad(a, ((0, rows - a.shape[0]), (0, cols - a.shape[1])))


def _filter_body(z_ref, w1_ref, w2_ref, w3_ref, b_ref, fr_ref, wo_ref, ad_ref, o_ref, *, nblk):
    i = pl.program_id(0)

    @pl.when(i < nblk)
    def _():
        z = z_ref[...]
        fr = fr_ref[...]
        h = jnp.sin(fr * (_dot(z, w1_ref[...], HI) + b_ref[0:1, :]))
        h = jnp.sin(fr * (_dot(h, w2_ref[...], HI) + b_ref[1:2, :]))
        h = jnp.sin(fr * (_dot(h, w3_ref[...], HI) + b_ref[2:3, :]))
        win = jnp.exp(-z[:, 0:1] * ad_ref[...])
        h_hi = h.astype(BF16)
        h_lo = (h - h_hi.astype(F32)).astype(BF16)
        for j in range(4):
            cols = slice(j * HY_WIDTH, (j + 1) * HY_WIDTH)
            taps = _dot(h_hi, wo_ref[0, :, cols]) + _dot(h_lo, wo_ref[0, :, cols]) + _dot(h_hi, wo_ref[1, :, cols])
            o_ref[:, cols] = taps * win

    @pl.when(i == nblk)
    def _():
        o_ref[...] = jnp.zeros_like(o_ref)


def _hyena_filter_taps(L, w1, b1, w2, b2, w3, b3, freq, w_out):
    z = _filter_features(L)
    hp = LANES
    bias = jnp.stack([jnp.pad(b, (0, hp - b.shape[0])) for b in (b1, b2, b3)])
    bias = jnp.pad(bias, ((0, 5), (0, 0)))
    fr = jnp.pad(freq, (0, hp - freq.shape[0])).reshape(1, hp)
    max_decay = math.log(HY_DECAY_TARGET) / HY_FAST_PCT
    min_decay = math.log(HY_DECAY_TARGET) / HY_SLOW_PCT
    absd = jnp.abs(jnp.linspace(min_decay, max_decay, HY_WIDTH, dtype=F32)).reshape(1, HY_WIDTH)
    tl = FILTER_ROWS
    nblk = L // tl
    n = 4 * HY_WIDTH
    full = lambda i: (0, 0)
    return pl.pallas_call(
        functools.partial(_filter_body, nblk=nblk),
        grid=(nblk + 1,),
        in_specs=[pl.BlockSpec((tl, hp), lambda i: (jnp.minimum(i, nblk - 1), 0)),
                  pl.BlockSpec((hp, hp), full), pl.BlockSpec((hp, hp), full), pl.BlockSpec((hp, hp), full),
                  pl.BlockSpec((8, hp), full), pl.BlockSpec((1, hp), full),
                  pl.BlockSpec((2, hp, n), lambda i: (0, 0, 0)), pl.BlockSpec((1, HY_WIDTH), full)],
        out_specs=pl.BlockSpec((tl, n), lambda i: (i, 0)),
        out_shape=jax.ShapeDtypeStruct((L + tl, n), F32),
        compiler_params=_cp("arbitrary"),
        name="hyena_filter",
    )(z, _pad_to(w1, hp, hp), _pad_to(w2, hp, hp), _pad_to(w3, hp, hp), bias, fr, _split2(_pad_to(w_out, hp, n)),
      absd)


def _hy_cfg(L):
    n2 = 128 if L >= 2048 else 16
    n1 = 2 * L // n2
    k1n = n1 // 2 + 1
    jp = -(-2 * k1n // 16) * 16
    pitch = n2 + 8
    return dict(L=L, n2=n2, n1=n1, nh=n1 // 2, k1n=k1n, jp=jp, pitch=pitch)


@functools.lru_cache(maxsize=None)
def _dft_tables(L):
    cfg = _hy_cfg(L)
    n, n1, n2, nh, k1n, jp = 2 * L, cfg["n1"], cfg["n2"], cfg["nh"], cfg["k1n"], cfg["jp"]
    a_n1 = np.arange(n1)
    a_k1 = np.arange(k1n)
    th = 2 * np.pi * np.outer(a_k1, a_n1) / n1
    f1 = np.zeros((jp, n1))
    f1[0:2 * k1n:2] = np.cos(th)
    f1[1:2 * k1n:2] = -np.sin(th)
    a_n2 = np.arange(n2)
    m1 = np.zeros((k1n, 2 * n2, 2 * n2))
    for k in range(k1n):
        f = np.exp(-2j * np.pi * (np.outer(a_n2, a_n2) / n2 + a_n2[None, :] * k / n))
        m1[k] = np.block([[f.real, -f.imag], [f.imag, f.real]])
    ck = np.full(k1n, 2.0)
    ck[0] = 1.0
    ck[-1] = 1.0
    th6 = 2 * np.pi * np.outer(np.arange(nh), a_k1) / n1
    f6 = np.zeros((nh, jp))
    f6[:, 0:2 * k1n:2] = ck * np.cos(th6) / n
    f6[:, 1:2 * k1n:2] = -ck * np.sin(th6) / n
    as32 = lambda a: np.asarray(a, np.float32)
    return dict(f1=as32(f1), m1=as32(m1), f6=as32(f6))


def _first_stage(src_ref, a_scr, f1, cfg):
    n2, nh, jp, pitch = cfg["n2"], cfg["nh"], cfg["jp"], cfg["pitch"]
    bt = src_ref.shape[0]

    def step(i, carry):
        xs = jnp.concatenate([src_ref[t, pl.ds(2 * i + u, nh, stride=pitch), :] for t in range(bt) for u in range(2)], 1)
        r = _dot(f1, xs.astype(BF16))
        for t in range(bt):
            for u in range(2):
                lo = (2 * t + u) * LANES
                a_scr[t, pl.ds(2 * i + u, jp, stride=pitch), :] = r[:, lo:lo + LANES]
        return carry

    lax.fori_loop(0, n2 // 2, step, 0, unroll=min(16 // bt, n2 // 2))


def _k1_rows(a_scr, k, cfg):
    n2, pitch = cfg["n2"], cfg["pitch"]
    base = pl.multiple_of(2 * k * pitch, 8)
    parts = [jnp.concatenate([a_scr[t, pl.ds(base, n2), :], a_scr[t, pl.ds(base + pitch, n2), :]], 0)
             for t in range(a_scr.shape[0])]
    return base, jnp.concatenate(parts, 1)


def _split2(table):
    t = jnp.asarray(table)
    hi = t.astype(BF16)
    return jnp.stack([hi, (t - hi.astype(F32)).astype(BF16)])


def _spectrum_body(kf_ref, kb_ref, f1_ref, m1_ref, o_ref, a_scr, *, cfg):
    n2, nh, jp, pitch = cfg["n2"], cfg["nh"], cfg["jp"], cfg["pitch"]

    def dot3(m_hi, m_lo, x):
        x_hi = x.astype(BF16)
        x_lo = (x - x_hi.astype(F32)).astype(BF16)
        return _dot(m_hi, x_hi) + _dot(m_lo, x_hi) + _dot(m_hi, x_lo)

    def step(i, carry):
        cols = []
        for t in range(2):
            n = 2 * i + t
            cols.append(jnp.concatenate([kf_ref[pl.ds(n, nh, stride=n2), :],
                                         kb_ref[pl.ds(n2 - n, nh, stride=n2), :]], 0))
        r = dot3(f1_ref[0], f1_ref[1], jnp.concatenate(cols, 1))
        a_scr[0, pl.ds(2 * i, jp, stride=pitch), :] = r[:, :LANES]
        a_scr[0, pl.ds(2 * i + 1, jp, stride=pitch), :] = r[:, LANES:]
        return carry

    lax.fori_loop(0, n2 // 2, step, 0, unroll=min(16, n2 // 2))
    lag0 = pl.ds(0, jp, stride=pitch)
    a_scr[0, lag0, :] = a_scr[0, lag0, :] + f1_ref[0, :, 0:1].astype(F32) * kb_ref[0:1, :]

    def mid(k, carry):
        _, a = _k1_rows(a_scr, k, cfg)
        o_ref[0, k] = dot3(m1_ref[0, k], m1_ref[1, k], a).astype(BF16)
        return carry

    lax.fori_loop(0, cfg["k1n"], mid, 0, unroll=3)


def _hyena_spectrum(L, taps):
    c = HY_WIDTH
    cfg = _hy_cfg(L)
    tb = _dft_tables(L)
    n1, n2, nh, k1n = cfg["n1"], cfg["n2"], cfg["nh"], cfg["k1n"]
    f1 = _split2(np.concatenate([tb["f1"][:, :nh], tb["f1"][:, n1 - 1:nh - 1:-1]], 1))
    m1 = _split2(tb["m1"])
    nct = c // LANES
    rows = taps.shape[0]
    return pl.pallas_call(
        functools.partial(_spectrum_body, cfg=cfg),
        grid=(2, nct),
        in_specs=[pl.BlockSpec((rows, LANES), lambda cv, j: (0, 2 * cv * nct + j)),
                  pl.BlockSpec((rows, LANES), lambda cv, j: (0, (2 * cv + 1) * nct + j)),
                  pl.BlockSpec(f1.shape, lambda cv, j: (0, 0, 0)),
                  pl.BlockSpec(m1.shape, lambda cv, j: (0, 0, 0, 0))],
        out_specs=pl.BlockSpec((1, k1n, 2 * n2, LANES), lambda cv, j: (cv, 0, 0, j)),
        out_shape=jax.ShapeDtypeStruct((2, k1n, 2 * n2, c), BF16),
        scratch_shapes=[pltpu.VMEM((1, cfg["jp"] * cfg["pitch"], LANES), F32)],
        compiler_params=_cp("arbitrary", "arbitrary"),
        name="hyena_spectrum",
    )(taps, taps, f1, m1)


def _short_conv(u_ref, w_ref, b_ref, which):
    x = u_ref[...].astype(F32)
    L = x.shape[0]
    row = lax.broadcasted_iota(jnp.int32, x.shape, 0)
    prev = jnp.where(row == 0, 0.0, pltpu.roll(x, 1, 0))
    nxt = jnp.where(row == L - 1, 0.0, pltpu.roll(x, L - 1, 0))
    sel = slice(which, which + 1)
    return prev * w_ref[0, sel, :] + x * w_ref[1, sel, :] + nxt * w_ref[2, sel, :] + b_ref[sel, :]


def _long_conv(src_scr, a_scr, y_scr, w_scr, f1_ref, f6_ref, m1_ref, kf_ref, conv, cfg):
    n2, nh, jp, pitch = cfg["n2"], cfg["nh"], cfg["jp"], cfg["pitch"]
    bt = src_scr.shape[0]
    _first_stage(src_scr, a_scr, f1_ref[...], cfg)

    def forward(k, carry):
        _, a = _k1_rows(a_scr, k, cfg)
        x = _dot(m1_ref[k], a.astype(BF16))
        kk = kf_ref[conv, k].astype(F32)
        kr, ki = kk[:n2], kk[n2:]
        cols = []
        for t in range(bt):
            xr, xi = x[:n2, t * LANES:(t + 1) * LANES], x[n2:, t * LANES:(t + 1) * LANES]
            cols.append(jnp.concatenate([xr * kr - xi * ki, xr * ki + xi * kr], 0))
        w_scr[k] = jnp.concatenate(cols, 1).astype(BF16)
        return carry

    def inverse(k, carry):
        base = pl.multiple_of(2 * k * pitch, 8)
        b = lax.dot_general(m1_ref[k], w_scr[k], (((0,), (0,)), ((), ())), preferred_element_type=F32)
        for t in range(bt):
            a_scr[t, pl.ds(base, n2), :] = b[:n2, t * LANES:(t + 1) * LANES]
            a_scr[t, pl.ds(base + pitch, n2), :] = b[n2:, t * LANES:(t + 1) * LANES]
        return carry

    lax.fori_loop(0, cfg["k1n"], forward, 0, unroll=3)
    lax.fori_loop(0, cfg["k1n"], inverse, 0, unroll=3)
    f6 = f6_ref[...]

    def last(i, carry):
        bs = jnp.concatenate([a_scr[t, pl.ds(2 * i + u, jp, stride=pitch), :] for t in range(bt) for u in range(2)], 1)
        y = _dot(f6, bs.astype(BF16))
        for t in range(bt):
            for u in range(2):
                lo = (2 * t + u) * LANES
                y_scr[t, pl.ds(2 * i + u, nh, stride=pitch), :] = y[:, lo:lo + LANES]
        return carry

    lax.fori_loop(0, n2 // 2, last, 0, unroll=min(16 // bt, n2 // 2))


def _hyena_body(v_ref, x1_ref, x2_ref, g_ref, cw_ref, cb_ref, hb_ref, f1_ref, f6_ref, m1_ref, kf_ref,
                o_ref, s_scr, a_scr, y_scr, w_scr, *, cfg):
    tabs = (w_scr, f1_ref, f6_ref, m1_ref, kf_ref)
    n2, nh, pitch = cfg["n2"], cfg["nh"], cfg["pitch"]
    slots = range(s_scr.shape[0])

    def put(scr, t, val):
        for n1 in range(nh):
            scr[t, n1 * pitch:n1 * pitch + n2, :] = val[n1 * n2:(n1 + 1) * n2]

    def get(scr, t):
        return jnp.concatenate([scr[t, n1 * pitch:n1 * pitch + n2, :] for n1 in range(nh)], 0)

    for t in slots:
        put(s_scr, t, _short_conv(v_ref.at[t], cw_ref, cb_ref, 0))
    _long_conv(s_scr, a_scr, y_scr, *tabs, 0, cfg)
    for t in slots:
        put(s_scr, t, _short_conv(x1_ref.at[t], cw_ref, cb_ref, 1) * (get(y_scr, t) + get(s_scr, t) * hb_ref[0:1, :]))
    _long_conv(s_scr, a_scr, y_scr, *tabs, 1, cfg)
    for t in slots:
        y = _short_conv(x2_ref.at[t], cw_ref, cb_ref, 2) * (get(y_scr, t) + get(s_scr, t) * hb_ref[1:2, :])
        o_ref[t] = (y * _silu(g_ref[t].astype(F32))).astype(o_ref.dtype)


def _hyena(u_hy3, u_gate, conv_w, conv_b, kf, hy_bias):
    b, L, _ = u_gate.shape
    c = HY_WIDTH
    cfg = _hy_cfg(L)
    tb = _dft_tables(L)
    nct = c // LANES
    bt = 2 if b % 2 == 0 else 1
    col = lambda off: pl.BlockSpec((bt, L, LANES), lambda j, bi: (bi, 0, off * nct + j))
    full = lambda a: pl.BlockSpec(a.shape, lambda j, bi: (0,) * a.ndim)
    f1 = jnp.asarray(tb["f1"][:, :cfg["nh"]]).astype(BF16)
    f6 = jnp.asarray(tb["f6"]).astype(BF16)
    m1 = jnp.asarray(tb["m1"]).astype(BF16)
    return pl.pallas_call(
        functools.partial(_hyena_body, cfg=cfg),
        grid=(nct, b // bt),
        in_specs=[col(0), col(1), col(2), col(0),
                  pl.BlockSpec((3, 3, LANES), lambda j, bi: (0, 0, j)),
                  pl.BlockSpec((3, LANES), lambda j, bi: (0, j)),
                  pl.BlockSpec((2, LANES), lambda j, bi: (0, j)),
                  full(f1), full(f6), full(m1),
                  pl.BlockSpec((2, cfg["k1n"], 2 * cfg["n2"], LANES), lambda j, bi: (0, 0, 0, j),
                               pipeline_mode=pl.Buffered(1))],
        out_specs=col(0),
        out_shape=jax.ShapeDtypeStruct((b, L, c), BF16),
        scratch_shapes=[pltpu.VMEM((bt, cfg["nh"] * cfg["pitch"], LANES), F32),
                        pltpu.VMEM((bt, cfg["jp"] * cfg["pitch"], LANES), F32),
                        pltpu.VMEM((bt, cfg["nh"] * cfg["pitch"], LANES), F32),
                        pltpu.VMEM((cfg["k1n"], 2 * cfg["n2"], bt * LANES), BF16)],
        compiler_params=_cp("arbitrary", "arbitrary"),
        name="hyena",
    )(u_hy3, u_hy3, u_hy3, u_gate, conv_w.reshape(3, 3, c), conv_b.reshape(3, c), hy_bias, f1, f6, m1, kf)


SSD_CPS = 2
SSD_SPS = 2
SPLIT_STRIDE = 2 * SSM_HEADS


def _pack3(x):
    hi = x.astype(BF16).astype(F32)
    r1 = x - hi
    mid = r1.astype(BF16).astype(F32)
    lo = (r1 - mid).astype(BF16).astype(F32)
    return (hi + pltpu.roll(mid, SPLIT_STRIDE, 1) + pltpu.roll(lo, 2 * SPLIT_STRIDE, 1)).astype(BF16)


def _unpack3(x3, used):
    return jnp.where(used, x3 + pltpu.roll(x3, LANES - SPLIT_STRIDE, 1) + pltpu.roll(x3, LANES - 2 * SPLIT_STRIDE, 1), 0.0)


@functools.lru_cache(maxsize=None)
def _ssd_spread_tables():
    col = np.zeros((LANES, 2 * SSM_HEADS * LANES), np.float32)
    head = np.zeros((LANES, 2 * SSM_WIDTH), np.float32)
    for c in range(2 * SSM_HEADS):
        d, h = divmod(c, SSM_HEADS)
        for piece in range(3):
            col[c + piece * SPLIT_STRIDE, c * LANES:(c + 1) * LANES] = 1.0
            lo = d * SSM_WIDTH + h * SSM_HEADDIM
            head[c + piece * SPLIT_STRIDE, lo:lo + SSM_HEADDIM] = 1.0
    return col, head


def _ssd_chunk_body(xbc_ref, dtr_ref, dtb_ref, a_ref, ecol_ref, ehead_ref, yd_ref, cs_ref, ex_ref, et_ref):
    q = SSM_CHUNK
    hpg = SSM_HEADS // SSM_GROUPS
    gw = SSM_WIDTH // SSM_GROUPS
    li = lax.broadcasted_iota(jnp.int32, (q, q), 0)
    si = lax.broadcasted_iota(jnp.int32, (q, q), 1)
    below = li > si
    diag = li == si
    fwd_lane = si < SSM_HEADS
    used = si < 2 * SSM_HEADS
    tril = (li >= si).astype(BF16)
    triu = (li <= si).astype(BF16)
    chunks = range(SSD_CPS)
    rows = [slice(c * q, (c + 1) * q) for c in chunks]
    ehead = ehead_ref[...]

    raws = [dtr_ref[0, r, :] + dtb_ref[...] for r in rows]
    dts = [jnp.where(used, jnp.maximum(x, 0.0) + jnp.log1p(jnp.exp(-jnp.abs(x))), 0.0) for x in raws]
    da3 = [_pack3(dt * a_ref[...]) for dt in dts]
    acs = [jnp.where(fwd_lane, _unpack3(_dot(tril, x), used), _unpack3(_dot(triu, x), used)) for x in da3]
    tots = [jnp.where(fwd_lane[0:1], a[q - 1:q, :], a[0:1, :]) for a in acs]
    ws = [dt * jnp.exp(t - a) for dt, t, a in zip(dts, tots, acs)]
    acs3 = [_pack3(a) for a in acs]
    colb = [_dot(x, ecol_ref[...]) for x in acs3]
    wx = [_dot(_pack3(w), ehead) for w in ws]
    for c in chunks:
        ex_ref[0, rows[c], :] = jnp.exp(_dot(acs3[c], ehead)).astype(BF16)
        et_ref[0, c] = jnp.exp(_dot(_pack3(jnp.broadcast_to(tots[c], (8, LANES))), ehead))
    rowt = [(a - jnp.where(dt > 0.0, jnp.log(dt), -BIG)).T for a, dt in zip(acs, dts)]
    dsum = [(dt + pltpu.roll(dt, LANES - SSM_HEADS, 1)).T for dt in dts]
    xbc = [xbc_ref[0, r, :] for r in rows]
    xsb = [x[:, :SSM_WIDTH].astype(BF16) for x in xbc]
    bgs = [[x[:, SSM_WIDTH + g * SSM_STATE:SSM_WIDTH + (g + 1) * SSM_STATE] for g in range(SSM_GROUPS)] for x in xbc]
    cgs = [[x[:, SSM_WIDTH + (SSM_GROUPS + g) * SSM_STATE:SSM_WIDTH + (SSM_GROUPS + g + 1) * SSM_STATE]
            for g in range(SSM_GROUPS)] for x in xbc]
    gmat = [[_dot_t(cgs[c][g].astype(BF16), bgs[c][g].astype(BF16)) for g in range(SSM_GROUPS)] for c in chunks]
    mats = []
    for c in chunks:
        for h in range(SSM_HEADS):
            hb = SSM_HEADS + h
            arg = jnp.where(below, colb[c][:, h * q:(h + 1) * q] - rowt[c][h:h + 1, :],
                            colb[c][:, hb * q:(hb + 1) * q] - rowt[c][hb:hb + 1, :])
            dec = jnp.where(diag, dsum[c][h:h + 1, :], jnp.exp(arg))
            mats.append((gmat[c][h // hpg] * dec).astype(BF16))
    for c in chunks:
        for h in range(SSM_HEADS):
            lo = h * SSM_HEADDIM
            yd_ref[0, rows[c], lo:lo + SSM_HEADDIM] = _dot(mats[c * SSM_HEADS + h],
                                                             xsb[c][:, lo:lo + SSM_HEADDIM]).astype(yd_ref.dtype)
    for c in chunks:
        for g in range(SSM_GROUPS):
            bgt = bgs[c][g].astype(F32).T.astype(BF16)
            xg = xbc[c][:, g * gw:(g + 1) * gw].astype(F32)
            for d in range(2):
                lo = d * SSM_WIDTH + g * gw
                cs_ref[0, c, d, g] = _dot(bgt, (xg * wx[c][:, lo:lo + gw]).astype(BF16))


def _ssd_chunks(xbc, dt_raw, dt_bias_row, a_row):
    b, L, _ = xbc.shape
    nc = L // SSM_CHUNK
    rows = SSD_CPS * SSM_CHUNK
    gw = SSM_WIDTH // SSM_GROUPS
    blk = lambda bi, i: (bi, i, 0)
    full2 = lambda bi, i: (0, 0)
    ecol, ehead = (jnp.asarray(t).astype(BF16) for t in _ssd_spread_tables())
    return pl.pallas_call(
        _ssd_chunk_body,
        grid=(b, nc // SSD_CPS),
        in_specs=[pl.BlockSpec((1, rows, SSM_CONV_DIM), blk), pl.BlockSpec((1, rows, LANES), blk),
                  pl.BlockSpec((1, LANES), full2), pl.BlockSpec((1, LANES), full2),
                  pl.BlockSpec(ecol.shape, full2), pl.BlockSpec(ehead.shape, full2)],
        out_specs=[pl.BlockSpec((1, rows, SSM_WIDTH), blk),
                   pl.BlockSpec((1, SSD_CPS, 2, SSM_GROUPS, SSM_STATE, gw), lambda bi, i: (bi, i, 0, 0, 0, 0)),
                   pl.BlockSpec((1, rows, 2 * SSM_WIDTH), blk),
                   pl.BlockSpec((1, SSD_CPS, 8, 2 * SSM_WIDTH), lambda bi, i: (bi, i, 0, 0))],
        out_shape=[jax.ShapeDtypeStruct((b, L, SSM_WIDTH), BF16),
                   jax.ShapeDtypeStruct((b, nc, 2, SSM_GROUPS, SSM_STATE, gw), F32),
                   jax.ShapeDtypeStruct((b, L, 2 * SSM_WIDTH), BF16),
                   jax.ShapeDtypeStruct((b, nc, 8, 2 * SSM_WIDTH), F32)],
        compiler_params=_cp("parallel", "arbitrary"),
        name="ssd_chunks",
    )(xbc, dt_raw, dt_bias_row, a_row, ecol, ehead)


def _ssd_state_body(cf_ref, cb_ref, xf_ref, xb_ref, ef_ref, eb_ref, sf_ref, sb_ref, init_ref, yf_ref, yb_ref, fin_ref,
                    st_ref, *, nsteps):
    ci = pl.program_id(1)

    @pl.when(ci == 0)
    def _():
        st_ref[...] = init_ref[0]

    q = SSM_CHUNK
    gw = SSM_WIDTH // SSM_GROUPS
    for s in range(SSD_SPS):
        dirs = ((cf_ref, xf_ref, ef_ref, sf_ref, yf_ref, s), (cb_ref, xb_ref, eb_ref, sb_ref, yb_ref, SSD_SPS - 1 - s))
        for d, (c_ref, x_ref, e_ref, s_ref, y_ref, j) in enumerate(dirs):
            rows = slice(j * q, (j + 1) * q)
            cmat = c_ref[0, rows, :].astype(BF16)
            for g in range(SSM_GROUPS):
                cols = slice(g * gw, (g + 1) * gw)
                st = st_ref[d, g]
                y_ref[0, rows, cols] = (_dot(cmat[:, g * SSM_STATE:(g + 1) * SSM_STATE], st.astype(BF16))
                                        * x_ref[0, rows, cols].astype(F32)).astype(y_ref.dtype)
                st_ref[d, g] = st * e_ref[0, j, 0:1, cols] + s_ref[0, j, 0, g]

    @pl.when(ci == nsteps - 1)
    def _():
        fin_ref[0] = st_ref[...]


def _ssd_states(xbc, ex, et, cs, init):
    b, L, _ = xbc.shape
    nc = L // SSM_CHUNK
    gw = SSM_WIDTH // SSM_GROUPS
    c_col = SSM_CONV_DIM // (SSM_GROUPS * SSM_STATE) - 1
    st_shape = (2, SSM_GROUPS, SSM_STATE, gw)
    st_spec = pl.BlockSpec((1,) + st_shape, lambda bi, c: (bi, 0, 0, 0, 0))
    nsteps = nc // SSD_SPS
    rows = SSD_SPS * SSM_CHUNK
    cs_blk = (1, SSD_SPS, 1, SSM_GROUPS, SSM_STATE, gw)
    fwd = lambda *tail: (lambda bi, c: (bi, c) + tail)
    bwd = lambda *tail: (lambda bi, c: (bi, nsteps - 1 - c) + tail)
    return pl.pallas_call(
        functools.partial(_ssd_state_body, nsteps=nsteps),
        grid=(b, nsteps),
        in_specs=[pl.BlockSpec((1, rows, SSM_GROUPS * SSM_STATE), fwd(c_col)),
                  pl.BlockSpec((1, rows, SSM_GROUPS * SSM_STATE), bwd(c_col)),
                  pl.BlockSpec((1, rows, SSM_WIDTH), fwd(0)),
                  pl.BlockSpec((1, rows, SSM_WIDTH), bwd(1)),
                  pl.BlockSpec((1, SSD_SPS, 8, SSM_WIDTH), fwd(0, 0)),
                  pl.BlockSpec((1, SSD_SPS, 8, SSM_WIDTH), bwd(0, 1)),
                  pl.BlockSpec(cs_blk, fwd(0, 0, 0, 0)),
                  pl.BlockSpec(cs_blk, bwd(1, 0, 0, 0)),
                  st_spec],
        out_specs=[pl.BlockSpec((1, rows, SSM_WIDTH), fwd(0)),
                   pl.BlockSpec((1, rows, SSM_WIDTH), bwd(0)),
                   st_spec],
        out_shape=[jax.ShapeDtypeStruct((b, L, SSM_WIDTH), BF16), jax.ShapeDtypeStruct((b, L, SSM_WIDTH), BF16),
                   jax.ShapeDtypeStruct((b,) + st_shape, F32)],
        scratch_shapes=[pltpu.VMEM(st_shape, F32)],
        compiler_params=_cp("parallel", "arbitrary"),
        name="ssd_states",
    )(xbc, xbc, ex, ex, et, et, cs, cs, init)


def _ssd(xbc, dt_raw, dt_bias_row, a_row, init):
    y_diag, cs, ex, et = _ssd_chunks(xbc, dt_raw, dt_bias_row, a_row)
    y_f, y_b, fin = _ssd_states(xbc, ex, et, cs, init)
    return (y_diag, y_f, y_b), fin


def _ssd_body(xbc_ref, dtr_ref, dtb_ref, a_ref, e_ref, init_ref, y_ref, fin_ref, st_ref, *, reverse, nc, d):
    ci = pl.program_id(1)

    @pl.when(ci == 0)
    def _():
        st_ref[...] = init_ref[0]

    q = SSM_CHUNK
    gw = SSM_WIDTH // SSM_GROUPS
    hpg = SSM_HEADS // SSM_GROUPS
    xbc = xbc_ref[0]
    xs = xbc[:, :SSM_WIDTH]
    raw = dtr_ref[0] + dtb_ref[...]
    dt = jnp.maximum(raw, 0.0) + jnp.log1p(jnp.exp(-jnp.abs(raw)))
    da = dt * a_ref[...]
    li = lax.broadcasted_iota(jnp.int32, (q, q), 0)
    si = lax.broadcasted_iota(jnp.int32, (q, q), 1)
    mask = (li <= si) if reverse else (li >= si)
    acs = _dot(mask.astype(F32), da, HI)
    acs_t = acs.T
    e = e_ref[...]
    dtx = _dot(dt, e, HI)
    acsx = _dot(acs, e, HI)
    last = 0 if reverse else q - 1
    totx = acsx[last:last + 1, :]
    xd = xs * dtx
    xde = (xd * jnp.exp(totx - acsx)).astype(BF16)
    xdb = xd.astype(BF16)
    eacs = jnp.exp(acsx)
    etot = jnp.exp(totx)
    for g in range(SSM_GROUPS):
        bg = xbc[:, SSM_WIDTH + g * SSM_STATE:SSM_WIDTH + (g + 1) * SSM_STATE]
        cg = xbc[:, SSM_WIDTH + (SSM_GROUPS + g) * SSM_STATE:SSM_WIDTH + (SSM_GROUPS + g + 1) * SSM_STATE]
        cgb = cg.astype(BF16)
        gmat = _dot_t(cgb, bg.astype(BF16))
        st = st_ref[g]
        y_off = _dot(cgb, st.astype(BF16)) * eacs[:, g * gw:(g + 1) * gw]
        for j in range(hpg):
            h = g * hpg + j
            col = d * SSM_HEADS + h
            seg = acs[:, col:col + 1] - acs_t[col:col + 1, :]
            lm = jnp.where(mask, jnp.exp(seg), 0.0)
            lo = h * SSM_HEADDIM
            yd = _dot((gmat * lm).astype(BF16), xdb[:, lo:lo + SSM_HEADDIM])
            y_ref[0, :, lo:lo + SSM_HEADDIM] = yd + y_off[:, j * SSM_HEADDIM:(j + 1) * SSM_HEADDIM]
        st_ref[g] = st * etot[:, g * gw:(g + 1) * gw] + _dot(bg.T.astype(BF16), xde[:, g * gw:(g + 1) * gw])

    @pl.when(ci == nc - 1)
    def _():
        fin_ref[0] = st_ref[...]


def _ssd_scan(xbc, dt_raw, dt_bias_row, a_row, expand, init, *, d):
    b, L, _ = xbc.shape
    nc = L // SSM_CHUNK
    reverse = d == 1
    chunk = (lambda bi, c: (bi, nc - 1 - c, 0)) if reverse else (lambda bi, c: (bi, c, 0))
    full2 = lambda bi, c: (0, 0)
    st_shape = (SSM_GROUPS, SSM_STATE, SSM_WIDTH // SSM_GROUPS)
    st_spec = pl.BlockSpec((1,) + st_shape, lambda bi, c: (bi, 0, 0, 0))
    return pl.pallas_call(
        functools.partial(_ssd_body, reverse=reverse, nc=nc, d=d),
        grid=(b, nc),
        in_specs=[pl.BlockSpec((1, SSM_CHUNK, SSM_CONV_DIM), chunk),
                  pl.BlockSpec((1, SSM_CHUNK, LANES), chunk),
                  pl.BlockSpec((1, LANES), full2), pl.BlockSpec((1, LANES), full2),
                  pl.BlockSpec((LANES, SSM_WIDTH), full2), st_spec],
        out_specs=[pl.BlockSpec((1, SSM_CHUNK, SSM_WIDTH), chunk), st_spec],
        out_shape=[jax.ShapeDtypeStruct((b, L, SSM_WIDTH), F32),
                   jax.ShapeDtypeStruct((b,) + st_shape, F32)],
        scratch_shapes=[pltpu.VMEM(st_shape, F32)],
        compiler_params=_cp("parallel", "arbitrary"),
        name="ssd_scan",
    )(xbc, dt_raw, dt_bias_row, a_row, expand, init)


def _ssd_expand(d):
    e = np.zeros((LANES, SSM_WIDTH), np.float32)
    for h in range(SSM_HEADS):
        e[d * SSM_HEADS + h, h * SSM_HEADDIM:(h + 1) * SSM_HEADDIM] = 1.0
    return e


def _rope_tables(L):
    rows = L // GRID_W
    row = jnp.broadcast_to(jnp.arange(rows)[:, None], (rows, GRID_W)).reshape(L)
    col = jnp.broadcast_to(jnp.arange(GRID_W)[None, :], (rows, GRID_W)).reshape(L)
    nf = ATT_HEADDIM // 4
    inv = ROPE_BASE ** (-jnp.arange(nf, dtype=F32) / nf)
    ar = row.astype(F32)[:, None] * inv
    ac = col.astype(F32)[:, None] * inv
    cos = jnp.concatenate([jnp.cos(ar), jnp.cos(ar), jnp.cos(ac), jnp.cos(ac)], -1)
    sin = jnp.concatenate([-jnp.sin(ar), jnp.sin(ar), -jnp.sin(ac), jnp.sin(ac)], -1)
    return jnp.tile(cos, (1, ATT_HEADS)), jnp.tile(sin, (1, ATT_HEADS))


def _rope(x, cos, sin):
    w = x.shape[-1]
    quarter = ATT_HEADDIM // 4
    lane = lax.broadcasted_iota(jnp.int32, x.shape, x.ndim - 1)
    partner = jnp.where((lane // quarter) % 2 == 0, pltpu.roll(x, w - quarter, x.ndim - 1),
                        pltpu.roll(x, quarter, x.ndim - 1))
    return x * cos + partner * sin


def _rope_body(q_ref, kv_ref, cos_ref, sin_ref, qo_ref, ko_ref):
    cos = cos_ref[...]
    sin = sin_ref[...]
    qo_ref[0] = (_rope(q_ref[0].astype(F32), cos, sin) * (ATT_HEADDIM ** -0.5 * LOG2E)).astype(qo_ref.dtype)
    ko_ref[0] = _rope(kv_ref[0].astype(F32), cos[:, :ATT_KV], sin[:, :ATT_KV]).astype(ko_ref.dtype)


def _apply_rope(u_q, u_kv, cos, sin):
    b, L, _ = u_q.shape
    tl = 512
    row = lambda i, bi: (bi, i, 0)
    tab = lambda i, bi: (i, 0)
    return pl.pallas_call(
        _rope_body,
        grid=(L // tl, b),
        in_specs=[pl.BlockSpec((1, tl, ATT_WIDTH), row), pl.BlockSpec((1, tl, ATT_KV), row),
                  pl.BlockSpec((tl, ATT_WIDTH), tab), pl.BlockSpec((tl, ATT_WIDTH), tab)],
        out_specs=[pl.BlockSpec((1, tl, ATT_WIDTH), row), pl.BlockSpec((1, tl, ATT_KV), row)],
        out_shape=[jax.ShapeDtypeStruct((b, L, ATT_WIDTH), BF16), jax.ShapeDtypeStruct((b, L, ATT_KV), BF16)],
        compiler_params=_cp("parallel", "arbitrary"),
        name="rope",
    )(u_q, u_kv, cos, sin)


def _wattn_body(sink_ref, bias_ref, q_ref, kp_ref, kc_ref, kn_ref, vp_ref, vc_ref, vn_ref, kx_ref, vx_ref, z_ref,
                o_ref):
    hd = ATT_HEADDIM
    low_half = lax.broadcasted_iota(jnp.int32, (WINDOW, 2 * hd), 1) < hd
    q = q_ref[0]
    bias = bias_ref[0]
    z = z_ref[0].astype(F32)
    k_all, v_ext = [], []
    for g in range(ATT_KV_HEADS):
        ks = slice(g * hd, (g + 1) * hd)
        k_all.append(jnp.concatenate([r[0, :, ks] for r in (kp_ref, kc_ref, kn_ref, kx_ref)], 0).astype(BF16))
        v_all = jnp.concatenate([r[0, :, ks] for r in (vp_ref, vc_ref, vn_ref, vx_ref)], 0).astype(BF16)
        ones = jnp.ones_like(v_all)
        v_ext.append((jnp.concatenate([v_all, ones], 1), jnp.concatenate([ones, v_all], 1)))
    heads = range(ATT_HEADS)
    sinks = [sink_ref[h] * LOG2E for h in heads]
    scores = [_dot_t(q[:, h * hd:(h + 1) * hd].astype(BF16), k_all[h // ATT_GROUP]) + bias for h in heads]
    maxes = [jnp.maximum(jnp.max(s, -1, keepdims=True), sk) for s, sk in zip(scores, sinks)]
    probs = [jnp.exp2(s - m).astype(BF16) for s, m in zip(scores, maxes)]
    exts = [_dot(p, v_ext[h // ATT_GROUP][h % 2]) for h, p in zip(heads, probs)]
    outs = [e / (pltpu.roll(e, hd, 1) + jnp.exp2(sk - m)) for e, sk, m in zip(exts, sinks, maxes)]
    for pair in range(ATT_HEADS // 2):
        cs = slice(2 * pair * hd, (2 * pair + 2) * hd)
        o_ref[0, :, cs] = (jnp.where(low_half, outs[2 * pair], outs[2 * pair + 1]) * _silu(z[:, cs])).astype(o_ref.dtype)


def _window_attention(q_rot, u_kv, uc_kv, sinks, z_a):
    b, L, _ = q_rot.shape
    lc = uc_kv.shape[1]
    nb = L // WINDOW
    hd2 = ATT_KV
    cur = lambda bi, i: (bi, i, 0)
    prv = lambda bi, i: (bi, jnp.maximum(i - 1, 0), 0)
    nxt = lambda bi, i: (bi, jnp.minimum(i + 1, nb - 1), 0)
    vcur = lambda bi, i: (bi, i, 1)
    vprv = lambda bi, i: (bi, jnp.maximum(i - 1, 0), 1)
    vnxt = lambda bi, i: (bi, jnp.minimum(i + 1, nb - 1), 1)
    kblk = lambda f: pl.BlockSpec((1, WINDOW, hd2), f)
    nk = 3 * WINDOW + lc
    row = np.arange(WINDOW)[:, None]
    col = np.arange(nk)[None, :]
    in_prev = (col < WINDOW) & (col >= row)
    in_next = (col >= 2 * WINDOW) & (col < 3 * WINDOW) & (col - 2 * WINDOW <= row)
    always = ((col >= WINDOW) & (col < 2 * WINDOW)) | (col >= 3 * WINDOW)
    kinds = [always | in_next, always | in_prev | in_next, always | in_prev]
    if nb == 1:
        kinds = [always] * 3
    bias = jnp.asarray(np.where(np.stack(kinds), 0.0, NEG).astype(np.float32))
    kind = lambda bi, i: (jnp.where(i == 0, 0, jnp.where(i == nb - 1, 2, 1)), 0, 0)
    return pl.pallas_call(
        _wattn_body,
        grid=(b, nb),
        in_specs=[pl.BlockSpec(memory_space=pltpu.SMEM),
                  pl.BlockSpec((1, WINDOW, nk), kind),
                  pl.BlockSpec((1, WINDOW, ATT_WIDTH), cur),
                  kblk(prv), kblk(cur), kblk(nxt), kblk(vprv), kblk(vcur), kblk(vnxt),
                  pl.BlockSpec((1, lc, hd2), lambda bi, i: (bi, 0, 0)),
                  pl.BlockSpec((1, lc, hd2), lambda bi, i: (bi, 0, 1)),
                  pl.BlockSpec((1, WINDOW, ATT_WIDTH), cur)],
        out_specs=pl.BlockSpec((1, WINDOW, ATT_WIDTH), cur),
        out_shape=jax.ShapeDtypeStruct((b, L, ATT_WIDTH), BF16),
        compiler_params=_cp("parallel", "arbitrary"),
        name="window_attention",
    )(sinks, bias, q_rot, u_kv, u_kv, u_kv, u_kv, u_kv, u_kv, uc_kv, uc_kv, z_a)


def _cattn_body(sink_ref, q_ref, k_ref, v_ref, z_ref, o_ref):
    scale = ATT_HEADDIM ** -0.5
    q = q_ref[0]
    z = z_ref[0].astype(F32)
    for g in range(ATT_KV_HEADS):
        ks = slice(g * ATT_HEADDIM, (g + 1) * ATT_HEADDIM)
        k = k_ref[0, :, ks].astype(BF16)
        v = v_ref[0, :, ks].astype(BF16)
        for j in range(ATT_GROUP):
            h = g * ATT_GROUP + j
            hs = slice(h * ATT_HEADDIM, (h + 1) * ATT_HEADDIM)
            s = _dot_t(q[:, hs].astype(BF16), k) * scale
            sink = sink_ref[h]
            m = jnp.maximum(jnp.max(s, -1, keepdims=True), sink)
            p = jnp.exp(s - m)
            den = jnp.sum(p, -1, keepdims=True) + jnp.exp(sink - m)
            o_ref[0, :, hs] = (_dot(p.astype(BF16), v) / den * _silu(z[:, hs])).astype(o_ref.dtype)


def _ctx_attention(uc_q, uc_kv, sinks, z_ac):
    b, lc, _ = uc_q.shape
    blk = lambda bi: (bi, 0, 0)
    return pl.pallas_call(
        _cattn_body,
        grid=(b,),
        in_specs=[pl.BlockSpec(memory_space=pltpu.SMEM),
                  pl.BlockSpec((1, lc, ATT_WIDTH), blk),
                  pl.BlockSpec((1, lc, ATT_KV), lambda bi: (bi, 0, 0)),
                  pl.BlockSpec((1, lc, ATT_KV), lambda bi: (bi, 0, 1)),
                  pl.BlockSpec((1, lc, ATT_WIDTH), blk)],
        out_specs=pl.BlockSpec((1, lc, ATT_WIDTH), blk),
        out_shape=jax.ShapeDtypeStruct((b, lc, ATT_WIDTH), BF16),
        compiler_params=_cp("parallel"),
        name="ctx_attention",
    )(sinks, uc_q, uc_kv, uc_kv, z_ac)


def _out_body(h_ref, g_ref, yhy_ref, yd_ref, yf_ref, yb_ref, xs_ref, zs_ref, yat_ref, dsk_ref, nw_ref, w_ref,
              lg_ref, lb_ref, o_ref):
    gw = SSM_WIDTH // SSM_GROUPS
    y_scan = yd_ref[0].astype(F32) + yf_ref[0].astype(F32) + yb_ref[0].astype(F32)
    ys = (y_scan + xs_ref[0].astype(F32) * dsk_ref[...]) * _silu(zs_ref[0].astype(F32))
    acc = _dot(yhy_ref[0].astype(BF16), w_ref[0:HY_WIDTH, :])
    for g in range(SSM_GROUPS):
        seg = ys[:, g * gw:(g + 1) * gw]
        seg = seg * lax.rsqrt(jnp.mean(seg * seg, -1, keepdims=True) + RMS_EPS) * nw_ref[:, g * gw:(g + 1) * gw]
        lo = HY_WIDTH + g * gw
        acc = acc + _dot(seg.astype(BF16), w_ref[lo:lo + gw, :])
    acc = acc + _dot(yat_ref[0].astype(BF16), w_ref[HY_WIDTH + SSM_WIDTH:, :])
    r = DEEPNORM_ALPHA * h_ref[0] + g_ref[0] * acc
    mu = jnp.mean(r, -1, keepdims=True)
    rc = r - mu
    var = jnp.mean(rc * rc, -1, keepdims=True)
    o_ref[0] = rc * lax.rsqrt(var + LN_EPS) * lg_ref[...] + lb_ref[...]


def _out_projection(h, gate_mod, y_hy, y_ssd, xbc, z_s, y_at, d_skip, norm_w, w_out, ln_g, ln_b):
    b, L, d = h.shape
    tm = 256
    row = lambda bi, i: (bi, i, 0)
    vec = lambda bi, i: (bi, 0, 0)
    full = lambda bi, i: (0, 0)
    w512 = pl.BlockSpec((1, tm, SSM_WIDTH), row)
    return pl.pallas_call(
        _out_body,
        grid=(b, L // tm),
        in_specs=[pl.BlockSpec((1, tm, d), row), pl.BlockSpec((1, 1, d), vec),
                  w512, w512, w512, w512, w512, w512, w512,
                  pl.BlockSpec((1, SSM_WIDTH), full), pl.BlockSpec((1, SSM_WIDTH), full),
                  pl.BlockSpec(w_out.shape, full), pl.BlockSpec((1, d), full), pl.BlockSpec((1, d), full)],
        out_specs=pl.BlockSpec((1, tm, d), row),
        out_shape=jax.ShapeDtypeStruct((b, L, d), F32),
        compiler_params=_cp("parallel", "arbitrary"),
        name="out_projection",
    )(h, gate_mod, y_hy, *y_ssd, xbc, z_s, y_at, d_skip, norm_w, w_out.astype(BF16),
      ln_g.reshape(1, d), ln_b.reshape(1, d))


def _sequence_front(h, shift, scale, w_packed, ssm_conv_w, ssm_conv_b, rope_tables=None):
    u_hy3, u_hyg, u_xbc, u_zs, u_q, u_kv, u_za, u_dt = _in_projection(h, shift, scale, w_packed, rope_tables)
    xbc = _dwconv(u_xbc, ssm_conv_w, ssm_conv_b, act=True, split=SSM_CONV_DIM)[0]
    return dict(hy3=u_hy3, hy_gate=u_hyg, xbc=xbc, z_s=u_zs, q=u_q, kv=u_kv, z_a=u_za, dt=u_dt)


def kernel(x, c, ctx, c_ctx, w_mod, b_mod, w_in, hy_conv_w, hy_conv_b, hy_f_w1, hy_f_b1, hy_f_w2, hy_f_b2,
           hy_f_w3, hy_f_b3, hy_f_freq, hy_f_wout, hy_bias, ssm_conv_w, ssm_conv_b, ssm_dt_bias, ssm_a_log,
           ssm_d, ssm_norm_w, attn_sinks, w_out, ln_g, ln_b):
    b, L, d = x.shape
    lc = ctx.shape[1]
    cos, sin = _rope_tables(L)
    cc = jnp.concatenate([c, c_ctx[None], jnp.zeros((16 - b - 1, d), F32)], 0)
    zero_state = jnp.zeros((b, 2, SSM_GROUPS, SSM_STATE, SSM_WIDTH // SSM_GROUPS), F32)
    h_lat, h_ctx = x, ctx
    for i in range(DEPTH):
        ctx_needed = i < DEPTH - 1
        mod = _modulation(cc, w_mod[i], b_mod[i])
        sh, sc, g = (mod[:b, None, j * d:(j + 1) * d] for j in range(3))
        sh_c, sc_c, g_c = (jnp.broadcast_to(mod[b:b + 1, None, j * d:(j + 1) * d], (b, 1, d)) for j in range(3))
        w_packed = _pack_w_in(w_in[i])
        lat = _sequence_front(h_lat, sh, sc, w_packed, ssm_conv_w[i], ssm_conv_b[i], (cos, sin))
        cx = _sequence_front(h_ctx, sh_c, sc_c, w_packed, ssm_conv_w[i], ssm_conv_b[i])

        dt_bias_row = jnp.pad(ssm_dt_bias[i].reshape(1, -1), ((0, 0), (0, LANES - 2 * SSM_HEADS)))
        a_row = jnp.pad(-jnp.exp(ssm_a_log[i]).reshape(1, -1), ((0, 0), (0, LANES - 2 * SSM_HEADS)))
        ys_c, s_c = _ssd(cx["xbc"], cx["dt"], dt_bias_row, a_row, zero_state)
        ys, _ = _ssd(lat["xbc"], lat["dt"], dt_bias_row, a_row, s_c)

        filt = (hy_f_w1[i], hy_f_b1[i], hy_f_w2[i], hy_f_b2[i], hy_f_w3[i], hy_f_b3[i], hy_f_freq[i], hy_f_wout[i])
        kf = _hyena_spectrum(L, _hyena_filter_taps(L, *filt))
        y_hy = _hyena(lat["hy3"], lat["hy_gate"], hy_conv_w[i], hy_conv_b[i], kf, hy_bias[i])

        y_at = _window_attention(lat["q"], lat["kv"], cx["kv"], attn_sinks[i], lat["z_a"])

        d_skip = jnp.repeat(ssm_d[i], SSM_HEADDIM).reshape(1, SSM_WIDTH)
        norm_w = ssm_norm_w[i].reshape(1, SSM_WIDTH)
        new_lat = _out_projection(h_lat, g, y_hy, ys, lat["xbc"], lat["z_s"], y_at, d_skip, norm_w,
                                  w_out[i], ln_g[i], ln_b[i])
        if ctx_needed:
            kf_c = _hyena_spectrum(lc, _hyena_filter_taps(lc, *filt))
            y_hy_c = _hyena(cx["hy3"], cx["hy_gate"], hy_conv_w[i], hy_conv_b[i], kf_c, hy_bias[i])
            y_at_c = _ctx_attention(cx["q"], cx["kv"], attn_sinks[i], cx["z_a"])
            h_ctx = _out_projection(h_ctx, g_c, y_hy_c, ys_c, cx["xbc"], cx["z_s"], y_at_c, d_skip,
                                    norm_w, w_out[i], ln_g[i], ln_b[i])
        h_lat = new_lat
    return h_lat
```

```python
import functools
import math

import numpy as np
import jax
import jax.numpy as jnp
from jax import lax
from jax.experimental import pallas as pl
from jax.experimental.pallas import tpu as pltpu

F32 = jnp.float32
BF16 = jnp.bfloat16
HI = lax.Precision.HIGHEST

D_MODEL = 1024
DEPTH = 2
GRID_W = 64
HY_WIDTH = 512
HY_BANDS = 16
HY_EMB = 1 + 2 * HY_BANDS
HY_FILTER_HIDDEN = 64
HY_DECAY_TARGET = 1e-2
HY_FAST_PCT = 0.3
HY_SLOW_PCT = 1.5
SSM_WIDTH = 512
SSM_HEADS = 8
SSM_HEADDIM = 64
SSM_GROUPS = 2
SSM_STATE = 128
SSM_CHUNK = 128
SSM_CONV_DIM = SSM_WIDTH + 2 * SSM_GROUPS * SSM_STATE
ATT_WIDTH = 512
ATT_HEADS = 8
ATT_KV_HEADS = 2
ATT_HEADDIM = 64
ATT_GROUP = ATT_HEADS // ATT_KV_HEADS
ATT_KV = ATT_KV_HEADS * ATT_HEADDIM
WINDOW = 128
ROPE_BASE = 10000.0
HY_IN = 4 * HY_WIDTH
SSM_IN = SSM_CONV_DIM + SSM_WIDTH + 2 * SSM_HEADS
DEEPNORM_ALPHA = (2 * DEPTH) ** 0.25
LN_EPS = 1e-6
RMS_EPS = 1e-5

LANES = 128
VMEM_LIMIT = 56 * 1024 * 1024
NEG = -1e30
BIG = 1e30
LOG2E = math.log2(math.e)
FILTER_ROWS = 256
PROJ_ROWS = 512


def _cp(*sem):
    return pltpu.CompilerParams(dimension_semantics=sem, vmem_limit_bytes=VMEM_LIMIT)


def _silu(x):
    return (0.5 * x) * (1.0 + jnp.tanh(0.5 * x))


def _dot(a, b, precision=None):
    return jnp.dot(a, b, preferred_element_type=F32, precision=precision)


def _dot_t(a, b):
    return lax.dot_general(a, b, (((1,), (1,)), ((), ())), preferred_element_type=F32)


def _mod_body(c_ref, w_ref, b_ref, o_ref):
    o_ref[...] = _dot(_silu(c_ref[...]), w_ref[...], HI) + b_ref[...]


def _modulation(cc, w, b):
    rows, d = cc.shape
    n = w.shape[1]
    tn = 1024
    return pl.pallas_call(
        _mod_body,
        grid=(n // tn,),
        in_specs=[pl.BlockSpec((rows, d), lambda j: (0, 0)),
                  pl.BlockSpec((d, tn), lambda j: (0, j)),
                  pl.BlockSpec((1, tn), lambda j: (0, j))],
        out_specs=pl.BlockSpec((rows, tn), lambda j: (0, j)),
        out_shape=jax.ShapeDtypeStruct((rows, n), F32),
        compiler_params=_cp("arbitrary"),
        name="modulation",
    )(cc, w, b.reshape(1, n))


IN_SEGS = (3 * HY_WIDTH, HY_WIDTH, SSM_CONV_DIM, SSM_WIDTH, ATT_WIDTH, 2 * ATT_KV, ATT_WIDTH, LANES)
SEG_Q, SEG_KV = 4, 5
IN_DTYPES = (BF16,) * (len(IN_SEGS) - 1) + (F32,)
IN_CHUNK = 512


def _pack_w_in(w):
    o_ss = HY_IN
    o_at = HY_IN + SSM_IN
    dt = w[:, o_ss + SSM_CONV_DIM + SSM_WIDTH:o_at]
    parts = [w[:, :HY_IN], w[:, o_ss:o_ss + SSM_CONV_DIM + SSM_WIDTH], w[:, o_at:],
             dt, jnp.zeros((w.shape[0], LANES - dt.shape[1]), w.dtype)]
    return jnp.concatenate(parts, axis=1).astype(BF16)


def _inproj_body(h_ref, sh_ref, sc_ref, w_ref, *rest, rope):
    tabs, o_refs = (rest[:2], rest[2:]) if rope else ((), rest)
    x = h_ref[0]
    mu = jnp.mean(x, -1, keepdims=True)
    xc = x - mu
    var = jnp.mean(xc * xc, -1, keepdims=True)
    xm = (xc * lax.rsqrt(var + LN_EPS) * (1.0 + sc_ref[0]) + sh_ref[0]).astype(BF16)
    off = 0
    for seg, (o_ref, n) in enumerate(zip(o_refs, IN_SEGS)):
        for j in range(0, n, IN_CHUNK):
            w = min(IN_CHUNK, n - j)
            r = _dot(xm, w_ref[:, off + j:off + j + w])
            if rope and seg == SEG_Q:
                r = _rope(r, tabs[0][...], tabs[1][...]) * (ATT_HEADDIM ** -0.5 * LOG2E)
            if rope and seg == SEG_KV:
                o_ref[0, :, :ATT_KV] = _rope(r[:, :ATT_KV], tabs[0][:, :ATT_KV], tabs[1][:, :ATT_KV]).astype(o_ref.dtype)
                o_ref[0, :, ATT_KV:] = r[:, ATT_KV:].astype(o_ref.dtype)
            else:
                o_ref[0, :, j:j + w] = r.astype(o_ref.dtype)
        off += n


def _in_projection(h, shift, scale, w_packed, rope_tables=None):
    b, L, d = h.shape
    tm = min(L, PROJ_ROWS)
    n_all = w_packed.shape[1]
    row = lambda bi, i: (bi, i, 0)
    vec = lambda bi, i: (bi, 0, 0)
    rope = rope_tables is not None
    tab_specs = [pl.BlockSpec((tm, ATT_WIDTH), lambda bi, i: (i, 0))] * 2 if rope else []
    return pl.pallas_call(
        functools.partial(_inproj_body, rope=rope),
        grid=(b, L // tm),
        in_specs=[pl.BlockSpec((1, tm, d), row), pl.BlockSpec((1, 1, d), vec), pl.BlockSpec((1, 1, d), vec),
                  pl.BlockSpec((d, n_all), lambda bi, i: (0, 0))] + tab_specs,
        out_specs=[pl.BlockSpec((1, tm, n), row) for n in IN_SEGS],
        out_shape=[jax.ShapeDtypeStruct((b, L, n), dt) for n, dt in zip(IN_SEGS, IN_DTYPES)],
        compiler_params=_cp("parallel", "arbitrary"),
        name="in_projection",
    )(h, shift, scale, w_packed, *(rope_tables or ()))


def _dwconv_body(u_ref, w_ref, b_ref, o_ref, *, act):
    x = u_ref[0].astype(F32)
    L = x.shape[0]
    row = lax.broadcasted_iota(jnp.int32, x.shape, 0)
    prev = jnp.where(row == 0, 0.0, pltpu.roll(x, 1, 0))
    nxt = jnp.where(row == L - 1, 0.0, pltpu.roll(x, L - 1, 0))
    y = prev * w_ref[0:1, :] + x * w_ref[1:2, :] + nxt * w_ref[2:3, :] + b_ref[...]
    if act:
        y = _silu(y)
    o_ref[0, 0] = y.astype(o_ref.dtype)


def _dwconv(u, w, bias, *, act, split):
    b, L, c = u.shape
    tc = 256
    per = split // tc
    return pl.pallas_call(
        functools.partial(_dwconv_body, act=act),
        grid=(b, c // tc),
        in_specs=[pl.BlockSpec((1, L, tc), lambda bi, j: (bi, 0, j)),
                  pl.BlockSpec((3, tc), lambda bi, j: (0, j)),
                  pl.BlockSpec((1, tc), lambda bi, j: (0, j))],
        out_specs=pl.BlockSpec((1, 1, L, tc), lambda bi, j: (j // per, bi, 0, j % per)),
        out_shape=jax.ShapeDtypeStruct((c // split, b, L, split), BF16),
        compiler_params=_cp("parallel", "arbitrary"),
        name="dwconv",
    )(u, w, bias.reshape(1, c))


def _filter_features(L):
    t = jnp.linspace(0.0, 1.0, L, dtype=F32)[:, None]
    w = 2.0 * math.pi * jnp.arange(L, dtype=F32)[:, None] / L
    f = jnp.linspace(1e-4, HY_BANDS - 1, HY_BANDS, dtype=F32)[None]
    z = jnp.concatenate([t, jnp.cos(f * w), -jnp.sin(f * w)], -1)
    return jnp.pad(z, ((0, 0), (0, LANES - HY_EMB)))


def _pad_to(a, rows, cols):
    return jnp.pad(a, ((0, rows - a.shape[0]), (0, cols - a.shape[1])))


def _filter_body(z_ref, w1_ref, w2_ref, w3_ref, b_ref, fr_ref, wo_ref, ad_ref, o_ref, *, nblk):
    i = pl.program_id(0)

    @pl.when(i < nblk)
    def _():
        z = z_ref[...]
        fr = fr_ref[...]
        h = jnp.sin(fr * (_dot(z, w1_ref[...], HI) + b_ref[0:1, :]))
        h = jnp.sin(fr * (_dot(h, w2_ref[...], HI) + b_ref[1:2, :]))
        h = jnp.sin(fr * (_dot(h, w3_ref[...], HI) + b_ref[2:3, :]))
        win = jnp.exp(-z[:, 0:1] * ad_ref[...])
        h_hi = h.astype(BF16)
        h_lo = (h - h_hi.astype(F32)).astype(BF16)
        for j in range(4):
            cols = slice(j * HY_WIDTH, (j + 1) * HY_WIDTH)
            taps = _dot(h_hi, wo_ref[0, :, cols]) + _dot(h_lo, wo_ref[0, :, cols]) + _dot(h_hi, wo_ref[1, :, cols])
            o_ref[:, cols] = taps * win

    @pl.when(i == nblk)
    def _():
        o_ref[...] = jnp.zeros_like(o_ref)


def _hyena_filter_taps(L, w1, b1, w2, b2, w3, b3, freq, w_out):
    z = _filter_features(L)
    hp = LANES
    bias = jnp.stack([jnp.pad(b, (0, hp - b.shape[0])) for b in (b1, b2, b3)])
    bias = jnp.pad(bias, ((0, 5), (0, 0)))
    fr = jnp.pad(freq, (0, hp - freq.shape[0])).reshape(1, hp)
    max_decay = math.log(HY_DECAY_TARGET) / HY_FAST_PCT
    min_decay = math.log(HY_DECAY_TARGET) / HY_SLOW_PCT
    absd = jnp.abs(jnp.linspace(min_decay, max_decay, HY_WIDTH, dtype=F32)).reshape(1, HY_WIDTH)
    tl = FILTER_ROWS
    nblk = L // tl
    n = 4 * HY_WIDTH
    full = lambda i: (0, 0)
    return pl.pallas_call(
        functools.partial(_filter_body, nblk=nblk),
        grid=(nblk + 1,),
        in_specs=[pl.BlockSpec((tl, hp), lambda i: (jnp.minimum(i, nblk - 1), 0)),
                  pl.BlockSpec((hp, hp), full), pl.BlockSpec((hp, hp), full), pl.BlockSpec((hp, hp), full),
                  pl.BlockSpec((8, hp), full), pl.BlockSpec((1, hp), full),
                  pl.BlockSpec((2, hp, n), lambda i: (0, 0, 0)), pl.BlockSpec((1, HY_WIDTH), full)],
        out_specs=pl.BlockSpec((tl, n), lambda i: (i, 0)),
        out_shape=jax.ShapeDtypeStruct((L + tl, n), F32),
        compiler_params=_cp("arbitrary"),
        name="hyena_filter",
    )(z, _pad_to(w1, hp, hp), _pad_to(w2, hp, hp), _pad_to(w3, hp, hp), bias, fr, _split2(_pad_to(w_out, hp, n)),
      absd)


def _hy_cfg(L):
    n2 = 128 if L >= 2048 else 16
    n1 = 2 * L // n2
    k1n = n1 // 2 + 1
    jp = -(-2 * k1n // 16) * 16
    pitch = n2 + 8
    return dict(L=L, n2=n2, n1=n1, nh=n1 // 2, k1n=k1n, jp=jp, pitch=pitch)


@functools.lru_cache(maxsize=None)
def _dft_tables(L):
    cfg = _hy_cfg(L)
    n, n1, n2, nh, k1n, jp = 2 * L, cfg["n1"], cfg["n2"], cfg["nh"], cfg["k1n"], cfg["jp"]
    a_n1 = np.arange(n1)
    a_k1 = np.arange(k1n)
    th = 2 * np.pi * np.outer(a_k1, a_n1) / n1
    f1 = np.zeros((jp, n1))
    f1[0:2 * k1n:2] = np.cos(th)
    f1[1:2 * k1n:2] = -np.sin(th)
    a_n2 = np.arange(n2)
    m1 = np.zeros((k1n, 2 * n2, 2 * n2))
    for k in range(k1n):
        f = np.exp(-2j * np.pi * (np.outer(a_n2, a_n2) / n2 + a_n2[None, :] * k / n))
        m1[k] = np.block([[f.real, -f.imag], [f.imag, f.real]])
    ck = np.full(k1n, 2.0)
    ck[0] = 1.0
    ck[-1] = 1.0
    th6 = 2 * np.pi * np.outer(np.arange(nh), a_k1) / n1
    f6 = np.zeros((nh, jp))
    f6[:, 0:2 * k1n:2] = ck * np.cos(th6) / n
    f6[:, 1:2 * k1n:2] = -ck * np.sin(th6) / n
    as32 = lambda a: np.asarray(a, np.float32)
    return dict(f1=as32(f1), m1=as32(m1), f6=as32(f6))


def _first_stage(src_ref, a_scr, f1, cfg):
    n2, nh, jp, pitch = cfg["n2"], cfg["nh"], cfg["jp"], cfg["pitch"]
    bt = src_ref.shape[0]

    def step(i, carry):
        xs = jnp.concatenate([src_ref[t, pl.ds(2 * i + u, nh, stride=pitch), :] for t in range(bt) for u in range(2)], 1)
        r = _dot(f1, xs.astype(BF16))
        for t in range(bt):
            for u in range(2):
                lo = (2 * t + u) * LANES
                a_scr[t, pl.ds(2 * i + u, jp, stride=pitch), :] = r[:, lo:lo + LANES]
        return carry

    lax.fori_loop(0, n2 // 2, step, 0, unroll=min(16 // bt, n2 // 2))


def _k1_rows(a_scr, k, cfg):
    n2, pitch = cfg["n2"], cfg["pitch"]
    base = pl.multiple_of(2 * k * pitch, 8)
    parts = [jnp.concatenate([a_scr[t, pl.ds(base, n2), :], a_scr[t, pl.ds(base + pitch, n2), :]], 0)
             for t in range(a_scr.shape[0])]
    return base, jnp.concatenate(parts, 1)


def _split2(table):
    t = jnp.asarray(table)
    hi = t.astype(BF16)
    return jnp.stack([hi, (t - hi.astype(F32)).astype(BF16)])


def _spectrum_body(kf_ref, kb_ref, f1_ref, m1_ref, o_ref, a_scr, *, cfg):
    n2, nh, jp, pitch = cfg["n2"], cfg["nh"], cfg["jp"], cfg["pitch"]

    def dot3(m_hi, m_lo, x):
        x_hi = x.astype(BF16)
        x_lo = (x - x_hi.astype(F32)).astype(BF16)
        return _dot(m_hi, x_hi) + _dot(m_lo, x_hi) + _dot(m_hi, x_lo)

    def step(i, carry):
        cols = []
        for t in range(2):
            n = 2 * i + t
            cols.append(jnp.concatenate([kf_ref[pl.ds(n, nh, stride=n2), :],
                                         kb_ref[pl.ds(n2 - n, nh, stride=n2), :]], 0))
        r = dot3(f1_ref[0], f1_ref[1], jnp.concatenate(cols, 1))
        a_scr[0, pl.ds(2 * i, jp, stride=pitch), :] = r[:, :LANES]
        a_scr[0, pl.ds(2 * i + 1, jp, stride=pitch), :] = r[:, LANES:]
        return carry

    lax.fori_loop(0, n2 // 2, step, 0, unroll=min(16, n2 // 2))
    lag0 = pl.ds(0, jp, stride=pitch)
    a_scr[0, lag0, :] = a_scr[0, lag0, :] + f1_ref[0, :, 0:1].astype(F32) * kb_ref[0:1, :]

    def mid(k, carry):
        _, a = _k1_rows(a_scr, k, cfg)
        o_ref[0, k] = dot3(m1_ref[0, k], m1_ref[1, k], a).astype(BF16)
        return carry

    lax.fori_loop(0, cfg["k1n"], mid, 0, unroll=3)


def _hyena_spectrum(L, taps):
    c = HY_WIDTH
    cfg = _hy_cfg(L)
    tb = _dft_tables(L)
    n1, n2, nh, k1n = cfg["n1"], cfg["n2"], cfg["nh"], cfg["k1n"]
    f1 = _split2(np.concatenate([tb["f1"][:, :nh], tb["f1"][:, n1 - 1:nh - 1:-1]], 1))
    m1 = _split2(tb["m1"])
    nct = c // LANES
    rows = taps.shape[0]
    return pl.pallas_call(
        functools.partial(_spectrum_body, cfg=cfg),
        grid=(2, nct),
        in_specs=[pl.BlockSpec((rows, LANES), lambda cv, j: (0, 2 * cv * nct + j)),
                  pl.BlockSpec((rows, LANES), lambda cv, j: (0, (2 * cv + 1) * nct + j)),
                  pl.BlockSpec(f1.shape, lambda cv, j: (0, 0, 0)),
                  pl.BlockSpec(m1.shape, lambda cv, j: (0, 0, 0, 0))],
        out_specs=pl.BlockSpec((1, k1n, 2 * n2, LANES), lambda cv, j: (cv, 0, 0, j)),
        out_shape=jax.ShapeDtypeStruct((2, k1n, 2 * n2, c), BF16),
        scratch_shapes=[pltpu.VMEM((1, cfg["jp"] * cfg["pitch"], LANES), F32)],
        compiler_params=_cp("arbitrary", "arbitrary"),
        name="hyena_spectrum",
    )(taps, taps, f1, m1)


def _short_conv(u_ref, w_ref, b_ref, which):
    x = u_ref[...].astype(F32)
    L = x.shape[0]
    edge = lax.broadcasted_iota(jnp.int32, (8, x.shape[1]), 0)
    prev = pltpu.roll(x, 1, 0)
    prev = jnp.concatenate([jnp.where(edge == 0, 0.0, prev[:8]), prev[8:]], 0)
    nxt = pltpu.roll(x, L - 1, 0)
    nxt = jnp.concatenate([nxt[:L - 8], jnp.where(edge == 7, 0.0, nxt[L - 8:])], 0)
    sel = slice(which, which + 1)
    return prev * w_ref[0, sel, :] + x * w_ref[1, sel, :] + nxt * w_ref[2, sel, :] + b_ref[sel, :]


def _long_conv(src_scr, a_scr, y_scr, w_scr, f1_ref, f6_ref, m1_ref, kf_ref, conv, cfg):
    n2, nh, jp, pitch = cfg["n2"], cfg["nh"], cfg["jp"], cfg["pitch"]
    bt = src_scr.shape[0]
    _first_stage(src_scr, a_scr, f1_ref[...], cfg)

    def forward(k, carry):
        _, a = _k1_rows(a_scr, k, cfg)
        x = _dot(m1_ref[k], a.astype(BF16))
        kk = kf_ref[conv, k].astype(F32)
        kr, ki = kk[:n2], kk[n2:]
        cols = []
        for t in range(bt):
            xr, xi = x[:n2, t * LANES:(t + 1) * LANES], x[n2:, t * LANES:(t + 1) * LANES]
            cols.append(jnp.concatenate([xr * kr - xi * ki, xr * ki + xi * kr], 0))
        w_scr[k] = jnp.concatenate(cols, 1).astype(BF16)
        return carry

    def inverse(k, carry):
        base = pl.multiple_of(2 * k * pitch, 8)
        b = lax.dot_general(m1_ref[k], w_scr[k], (((0,), (0,)), ((), ())), preferred_element_type=F32)
        for t in range(bt):
            a_scr[t, pl.ds(base, n2), :] = b[:n2, t * LANES:(t + 1) * LANES]
            a_scr[t, pl.ds(base + pitch, n2), :] = b[n2:, t * LANES:(t + 1) * LANES]
        return carry

    lax.fori_loop(0, cfg["k1n"], forward, 0, unroll=3)
    lax.fori_loop(0, cfg["k1n"], inverse, 0, unroll=3)
    f6 = f6_ref[...]

    def last(i, carry):
        bs = jnp.concatenate([a_scr[t, pl.ds(2 * i + u, jp, stride=pitch), :] for t in range(bt) for u in range(2)], 1)
        y = _dot(f6, bs.astype(BF16))
        for t in range(bt):
            for u in range(2):
                lo = (2 * t + u) * LANES
                y_scr[t, pl.ds(2 * i + u, nh, stride=pitch), :] = y[:, lo:lo + LANES]
        return carry

    lax.fori_loop(0, n2 // 2, last, 0, unroll=min(16 // bt, n2 // 2))


def _hyena_body(v_ref, x1_ref, x2_ref, g_ref, cw_ref, cb_ref, hb_ref, f1_ref, f6_ref, m1_ref, kf_ref,
                o_ref, s_scr, a_scr, y_scr, w_scr, *, cfg):
    tabs = (w_scr, f1_ref, f6_ref, m1_ref, kf_ref)
    n2, nh, pitch = cfg["n2"], cfg["nh"], cfg["pitch"]
    slots = range(s_scr.shape[0])

    def put(scr, t, val):
        for n1 in range(nh):
            scr[t, n1 * pitch:n1 * pitch + n2, :] = val[n1 * n2:(n1 + 1) * n2]

    def get(scr, t):
        return jnp.concatenate([scr[t, n1 * pitch:n1 * pitch + n2, :] for n1 in range(nh)], 0)

    for t in slots:
        put(s_scr, t, _short_conv(v_ref.at[t], cw_ref, cb_ref, 0))
    _long_conv(s_scr, a_scr, y_scr, *tabs, 0, cfg)
    for t in slots:
        put(s_scr, t, _short_conv(x1_ref.at[t], cw_ref, cb_ref, 1) * (get(y_scr, t) + get(s_scr, t) * hb_ref[0:1, :]))
    _long_conv(s_scr, a_scr, y_scr, *tabs, 1, cfg)
    for t in slots:
        y = _short_conv(x2_ref.at[t], cw_ref, cb_ref, 2) * (get(y_scr, t) + get(s_scr, t) * hb_ref[1:2, :])
        o_ref[t] = (y * _silu(g_ref[t].astype(F32))).astype(o_ref.dtype)


def _hyena(u_hy3, u_gate, conv_w, conv_b, kf, hy_bias):
    b, L, _ = u_gate.shape
    c = HY_WIDTH
    cfg = _hy_cfg(L)
    tb = _dft_tables(L)
    nct = c // LANES
    bt = 2 if b % 2 == 0 else 1
    col = lambda off: pl.BlockSpec((bt, L, LANES), lambda j, bi: (bi, 0, off * nct + j))
    full = lambda a: pl.BlockSpec(a.shape, lambda j, bi: (0,) * a.ndim)
    f1 = jnp.asarray(tb["f1"][:, :cfg["nh"]]).astype(BF16)
    f6 = jnp.asarray(tb["f6"]).astype(BF16)
    m1 = jnp.asarray(tb["m1"]).astype(BF16)
    return pl.pallas_call(
        functools.partial(_hyena_body, cfg=cfg),
        grid=(nct, b // bt),
        in_specs=[col(0), col(1), col(2), col(0),
                  pl.BlockSpec((3, 3, LANES), lambda j, bi: (0, 0, j)),
                  pl.BlockSpec((3, LANES), lambda j, bi: (0, j)),
                  pl.BlockSpec((2, LANES), lambda j, bi: (0, j)),
                  full(f1), full(f6), full(m1),
                  pl.BlockSpec((2, cfg["k1n"], 2 * cfg["n2"], LANES), lambda j, bi: (0, 0, 0, j),
                               pipeline_mode=pl.Buffered(1))],
        out_specs=col(0),
        out_shape=jax.ShapeDtypeStruct((b, L, c), BF16),
        scratch_shapes=[pltpu.VMEM((bt, cfg["nh"] * cfg["pitch"], LANES), F32),
                        pltpu.VMEM((bt, cfg["jp"] * cfg["pitch"], LANES), F32),
                        pltpu.VMEM((bt, cfg["nh"] * cfg["pitch"], LANES), F32),
                        pltpu.VMEM((cfg["k1n"], 2 * cfg["n2"], bt * LANES), BF16)],
        compiler_params=_cp("arbitrary", "arbitrary"),
        name="hyena",
    )(u_hy3, u_hy3, u_hy3, u_gate, conv_w.reshape(3, 3, c), conv_b.reshape(3, c), hy_bias, f1, f6, m1, kf)


SSD_CPS = 2
SSD_SPS = 2
SPLIT_STRIDE = 2 * SSM_HEADS


def _pack3(x):
    hi = x.astype(BF16).astype(F32)
    r1 = x - hi
    mid = r1.astype(BF16).astype(F32)
    lo = (r1 - mid).astype(BF16).astype(F32)
    return (hi + pltpu.roll(mid, SPLIT_STRIDE, 1) + pltpu.roll(lo, 2 * SPLIT_STRIDE, 1)).astype(BF16)


def _unpack3(x3, used):
    return jnp.where(used, x3 + pltpu.roll(x3, LANES - SPLIT_STRIDE, 1) + pltpu.roll(x3, LANES - 2 * SPLIT_STRIDE, 1), 0.0)


@functools.lru_cache(maxsize=None)
def _ssd_spread_tables():
    col = np.zeros((LANES, 2 * SSM_HEADS * LANES), np.float32)
    head = np.zeros((LANES, 2 * SSM_WIDTH), np.float32)
    for c in range(2 * SSM_HEADS):
        d, h = divmod(c, SSM_HEADS)
        for piece in range(3):
            col[c + piece * SPLIT_STRIDE, c * LANES:(c + 1) * LANES] = 1.0
            lo = d * SSM_WIDTH + h * SSM_HEADDIM
            head[c + piece * SPLIT_STRIDE, lo:lo + SSM_HEADDIM] = 1.0
    return col, head


def _ssd_chunk_body(xbc_ref, dtr_ref, dtb_ref, a_ref, ecol_ref, ehead_ref, yd_ref, cs_ref, ex_ref, et_ref):
    q = SSM_CHUNK
    hpg = SSM_HEADS // SSM_GROUPS
    gw = SSM_WIDTH // SSM_GROUPS
    li = lax.broadcasted_iota(jnp.int32, (q, q), 0)
    si = lax.broadcasted_iota(jnp.int32, (q, q), 1)
    below = li > si
    diag = li == si
    fwd_lane = si < SSM_HEADS
    used = si < 2 * SSM_HEADS
    tril = (li >= si).astype(BF16)
    triu = (li <= si).astype(BF16)
    chunks = range(SSD_CPS)
    rows = [slice(c * q, (c + 1) * q) for c in chunks]
    ehead = ehead_ref[...]

    raws = [dtr_ref[0, r, :] + dtb_ref[...] for r in rows]
    dts = [jnp.where(used, jnp.maximum(x, 0.0) + jnp.log1p(jnp.exp(-jnp.abs(x))), 0.0) for x in raws]
    da3 = [_pack3(dt * a_ref[...]) for dt in dts]
    acs = [jnp.where(fwd_lane, _unpack3(_dot(tril, x), used), _unpack3(_dot(triu, x), used)) for x in da3]
    tots = [jnp.where(fwd_lane[0:1], a[q - 1:q, :], a[0:1, :]) for a in acs]
    ws = [dt * jnp.exp(t - a) for dt, t, a in zip(dts, tots, acs)]
    acs3 = [_pack3(a) for a in acs]
    colb = [_dot(x, ecol_ref[...]) for x in acs3]
    wx = [_dot(_pack3(w), ehead) for w in ws]
    for c in chunks:
        ex_ref[0, rows[c], :] = jnp.exp(_dot(acs3[c], ehead)).astype(BF16)
        et_ref[0, c] = jnp.exp(_dot(_pack3(jnp.broadcast_to(tots[c], (8, LANES))), ehead))
    rowt = [(a - jnp.where(dt > 0.0, jnp.log(dt), -BIG)).T for a, dt in zip(acs, dts)]
    dsum = [(dt + pltpu.roll(dt, LANES - SSM_HEADS, 1)).T for dt in dts]
    xbc = [xbc_ref[0, r, :] for r in rows]
    xsb = [x[:, :SSM_WIDTH].astype(BF16) for x in xbc]
    bgs = [[x[:, SSM_WIDTH + g * SSM_STATE:SSM_WIDTH + (g + 1) * SSM_STATE] for g in range(SSM_GROUPS)] for x in xbc]
    cgs = [[x[:, SSM_WIDTH + (SSM_GROUPS + g) * SSM_STATE:SSM_WIDTH + (SSM_GROUPS + g + 1) * SSM_STATE]
            for g in range(SSM_GROUPS)] for x in xbc]
    gmat = [[_dot_t(cgs[c][g].astype(BF16), bgs[c][g].astype(BF16)) for g in range(SSM_GROUPS)] for c in chunks]
    mats = []
    for c in chunks:
        for h in range(SSM_HEADS):
            hb = SSM_HEADS + h
            arg = jnp.where(below, colb[c][:, h * q:(h + 1) * q] - rowt[c][h:h + 1, :],
                            colb[c][:, hb * q:(hb + 1) * q] - rowt[c][hb:hb + 1, :])
            dec = jnp.where(diag, dsum[c][h:h + 1, :], jnp.exp(arg))
            mats.append((gmat[c][h // hpg] * dec).astype(BF16))
    for c in chunks:
        for h in range(SSM_HEADS):
            lo = h * SSM_HEADDIM
            yd_ref[0, rows[c], lo:lo + SSM_HEADDIM] = _dot(mats[c * SSM_HEADS + h],
                                                             xsb[c][:, lo:lo + SSM_HEADDIM]).astype(yd_ref.dtype)
    for c in chunks:
        for g in range(SSM_GROUPS):
            bgt = bgs[c][g].astype(F32).T.astype(BF16)
            xg = xbc[c][:, g * gw:(g + 1) * gw].astype(F32)
            for d in range(2):
                lo = d * SSM_WIDTH + g * gw
                cs_ref[0, c, d, g] = _dot(bgt, (xg * wx[c][:, lo:lo + gw]).astype(BF16))


def _ssd_chunks(xbc, dt_raw, dt_bias_row, a_row):
    b, L, _ = xbc.shape
    nc = L // SSM_CHUNK
    rows = SSD_CPS * SSM_CHUNK
    gw = SSM_WIDTH // SSM_GROUPS
    blk = lambda bi, i: (bi, i, 0)
    full2 = lambda bi, i: (0, 0)
    ecol, ehead = (jnp.asarray(t).astype(BF16) for t in _ssd_spread_tables())
    return pl.pallas_call(
        _ssd_chunk_body,
        grid=(b, nc // SSD_CPS),
        in_specs=[pl.BlockSpec((1, rows, SSM_CONV_DIM), blk), pl.BlockSpec((1, rows, LANES), blk),
                  pl.BlockSpec((1, LANES), full2), pl.BlockSpec((1, LANES), full2),
                  pl.BlockSpec(ecol.shape, full2), pl.BlockSpec(ehead.shape, full2)],
        out_specs=[pl.BlockSpec((1, rows, SSM_WIDTH), blk),
                   pl.BlockSpec((1, SSD_CPS, 2, SSM_GROUPS, SSM_STATE, gw), lambda bi, i: (bi, i, 0, 0, 0, 0)),
                   pl.BlockSpec((1, rows, 2 * SSM_WIDTH), blk),
                   pl.BlockSpec((1, SSD_CPS, 8, 2 * SSM_WIDTH), lambda bi, i: (bi, i, 0, 0))],
        out_shape=[jax.ShapeDtypeStruct((b, L, SSM_WIDTH), BF16),
                   jax.ShapeDtypeStruct((b, nc, 2, SSM_GROUPS, SSM_STATE, gw), F32),
                   jax.ShapeDtypeStruct((b, L, 2 * SSM_WIDTH), BF16),
                   jax.ShapeDtypeStruct((b, nc, 8, 2 * SSM_WIDTH), F32)],
        compiler_params=_cp("parallel", "arbitrary"),
        name="ssd_chunks",
    )(xbc, dt_raw, dt_bias_row, a_row, ecol, ehead)


def _ssd_state_body(cf_ref, cb_ref, xf_ref, xb_ref, ef_ref, eb_ref, sf_ref, sb_ref, init_ref, yf_ref, yb_ref, fin_ref,
                    st_ref, *, nsteps):
    ci = pl.program_id(1)

    @pl.when(ci == 0)
    def _():
        st_ref[...] = init_ref[0]

    q = SSM_CHUNK
    gw = SSM_WIDTH // SSM_GROUPS
    for s in range(SSD_SPS):
        dirs = ((cf_ref, xf_ref, ef_ref, sf_ref, yf_ref, s), (cb_ref, xb_ref, eb_ref, sb_ref, yb_ref, SSD_SPS - 1 - s))
        for d, (c_ref, x_ref, e_ref, s_ref, y_ref, j) in enumerate(dirs):
            rows = slice(j * q, (j + 1) * q)
            cmat = c_ref[0, rows, :].astype(BF16)
            for g in range(SSM_GROUPS):
                cols = slice(g * gw, (g + 1) * gw)
                st = st_ref[d, g]
                y_ref[0, rows, cols] = (_dot(cmat[:, g * SSM_STATE:(g + 1) * SSM_STATE], st.astype(BF16))
                                        * x_ref[0, rows, cols].astype(F32)).astype(y_ref.dtype)
                st_ref[d, g] = st * e_ref[0, j, 0:1, cols] + s_ref[0, j, 0, g]

    @pl.when(ci == nsteps - 1)
    def _():
        fin_ref[0] = st_ref[...]


def _ssd_states(xbc, ex, et, cs, init):
    b, L, _ = xbc.shape
    nc = L // SSM_CHUNK
    gw = SSM_WIDTH // SSM_GROUPS
    c_col = SSM_CONV_DIM // (SSM_GROUPS * SSM_STATE) - 1
    st_shape = (2, SSM_GROUPS, SSM_STATE, gw)
    st_spec = pl.BlockSpec((1,) + st_shape, lambda bi, c: (bi, 0, 0, 0, 0))
    nsteps = nc // SSD_SPS
    rows = SSD_SPS * SSM_CHUNK
    cs_blk = (1, SSD_SPS, 1, SSM_GROUPS, SSM_STATE, gw)
    fwd = lambda *tail: (lambda bi, c: (bi, c) + tail)
    bwd = lambda *tail: (lambda bi, c: (bi, nsteps - 1 - c) + tail)
    return pl.pallas_call(
        functools.partial(_ssd_state_body, nsteps=nsteps),
        grid=(b, nsteps),
        in_specs=[pl.BlockSpec((1, rows, SSM_GROUPS * SSM_STATE), fwd(c_col)),
                  pl.BlockSpec((1, rows, SSM_GROUPS * SSM_STATE), bwd(c_col)),
                  pl.BlockSpec((1, rows, SSM_WIDTH), fwd(0)),
                  pl.BlockSpec((1, rows, SSM_WIDTH), bwd(1)),
                  pl.BlockSpec((1, SSD_SPS, 8, SSM_WIDTH), fwd(0, 0)),
                  pl.BlockSpec((1, SSD_SPS, 8, SSM_WIDTH), bwd(0, 1)),
                  pl.BlockSpec(cs_blk, fwd(0, 0, 0, 0)),
                  pl.BlockSpec(cs_blk, bwd(1, 0, 0, 0)),
                  st_spec],
        out_specs=[pl.BlockSpec((1, rows, SSM_WIDTH), fwd(0)),
                   pl.BlockSpec((1, rows, SSM_WIDTH), bwd(0)),
                   st_spec],
        out_shape=[jax.ShapeDtypeStruct((b, L, SSM_WIDTH), BF16), jax.ShapeDtypeStruct((b, L, SSM_WIDTH), BF16),
                   jax.ShapeDtypeStruct((b,) + st_shape, F32)],
        scratch_shapes=[pltpu.VMEM(st_shape, F32)],
        compiler_params=_cp("parallel", "arbitrary"),
        name="ssd_states",
    )(xbc, xbc, ex, ex, et, et, cs, cs, init)


def _ssd(xbc, dt_raw, dt_bias_row, a_row, init):
    y_diag, cs, ex, et = _ssd_chunks(xbc, dt_raw, dt_bias_row, a_row)
    y_f, y_b, fin = _ssd_states(xbc, ex, et, cs, init)
    return (y_diag, y_f, y_b), fin


def _ssd_body(xbc_ref, dtr_ref, dtb_ref, a_ref, e_ref, init_ref, y_ref, fin_ref, st_ref, *, reverse, nc, d):
    ci = pl.program_id(1)

    @pl.when(ci == 0)
    def _():
        st_ref[...] = init_ref[0]

    q = SSM_CHUNK
    gw = SSM_WIDTH // SSM_GROUPS
    hpg = SSM_HEADS // SSM_GROUPS
    xbc = xbc_ref[0]
    xs = xbc[:, :SSM_WIDTH]
    raw = dtr_ref[0] + dtb_ref[...]
    dt = jnp.maximum(raw, 0.0) + jnp.log1p(jnp.exp(-jnp.abs(raw)))
    da = dt * a_ref[...]
    li = lax.broadcasted_iota(jnp.int32, (q, q), 0)
    si = lax.broadcasted_iota(jnp.int32, (q, q), 1)
    mask = (li <= si) if reverse else (li >= si)
    acs = _dot(mask.astype(F32), da, HI)
    acs_t = acs.T
    e = e_ref[...]
    dtx = _dot(dt, e, HI)
    acsx = _dot(acs, e, HI)
    last = 0 if reverse else q - 1
    totx = acsx[last:last + 1, :]
    xd = xs * dtx
    xde = (xd * jnp.exp(totx - acsx)).astype(BF16)
    xdb = xd.astype(BF16)
    eacs = jnp.exp(acsx)
    etot = jnp.exp(totx)
    for g in range(SSM_GROUPS):
        bg = xbc[:, SSM_WIDTH + g * SSM_STATE:SSM_WIDTH + (g + 1) * SSM_STATE]
        cg = xbc[:, SSM_WIDTH + (SSM_GROUPS + g) * SSM_STATE:SSM_WIDTH + (SSM_GROUPS + g + 1) * SSM_STATE]
        cgb = cg.astype(BF16)
        gmat = _dot_t(cgb, bg.astype(BF16))
        st = st_ref[g]
        y_off = _dot(cgb, st.astype(BF16)) * eacs[:, g * gw:(g + 1) * gw]
        for j in range(hpg):
            h = g * hpg + j
            col = d * SSM_HEADS + h
            seg = acs[:, col:col + 1] - acs_t[col:col + 1, :]
            lm = jnp.where(mask, jnp.exp(seg), 0.0)
            lo = h * SSM_HEADDIM
            yd = _dot((gmat * lm).astype(BF16), xdb[:, lo:lo + SSM_HEADDIM])
            y_ref[0, :, lo:lo + SSM_HEADDIM] = yd + y_off[:, j * SSM_HEADDIM:(j + 1) * SSM_HEADDIM]
        st_ref[g] = st * etot[:, g * gw:(g + 1) * gw] + _dot(bg.T.astype(BF16), xde[:, g * gw:(g + 1) * gw])

    @pl.when(ci == nc - 1)
    def _():
        fin_ref[0] = st_ref[...]


def _ssd_scan(xbc, dt_raw, dt_bias_row, a_row, expand, init, *, d):
    b, L, _ = xbc.shape
    nc = L // SSM_CHUNK
    reverse = d == 1
    chunk = (lambda bi, c: (bi, nc - 1 - c, 0)) if reverse else (lambda bi, c: (bi, c, 0))
    full2 = lambda bi, c: (0, 0)
    st_shape = (SSM_GROUPS, SSM_STATE, SSM_WIDTH // SSM_GROUPS)
    st_spec = pl.BlockSpec((1,) + st_shape, lambda bi, c: (bi, 0, 0, 0))
    return pl.pallas_call(
        functools.partial(_ssd_body, reverse=reverse, nc=nc, d=d),
        grid=(b, nc),
        in_specs=[pl.BlockSpec((1, SSM_CHUNK, SSM_CONV_DIM), chunk),
                  pl.BlockSpec((1, SSM_CHUNK, LANES), chunk),
                  pl.BlockSpec((1, LANES), full2), pl.BlockSpec((1, LANES), full2),
                  pl.BlockSpec((LANES, SSM_WIDTH), full2), st_spec],
        out_specs=[pl.BlockSpec((1, SSM_CHUNK, SSM_WIDTH), chunk), st_spec],
        out_shape=[jax.ShapeDtypeStruct((b, L, SSM_WIDTH), F32),
                   jax.ShapeDtypeStruct((b,) + st_shape, F32)],
        scratch_shapes=[pltpu.VMEM(st_shape, F32)],
        compiler_params=_cp("parallel", "arbitrary"),
        name="ssd_scan",
    )(xbc, dt_raw, dt_bias_row, a_row, expand, init)


def _ssd_expand(d):
    e = np.zeros((LANES, SSM_WIDTH), np.float32)
    for h in range(SSM_HEADS):
        e[d * SSM_HEADS + h, h * SSM_HEADDIM:(h + 1) * SSM_HEADDIM] = 1.0
    return e


def _rope_tables(L):
    rows = L // GRID_W
    row = jnp.broadcast_to(jnp.arange(rows)[:, None], (rows, GRID_W)).reshape(L)
    col = jnp.broadcast_to(jnp.arange(GRID_W)[None, :], (rows, GRID_W)).reshape(L)
    nf = ATT_HEADDIM // 4
    inv = ROPE_BASE ** (-jnp.arange(nf, dtype=F32) / nf)
    ar = row.astype(F32)[:, None] * inv
    ac = col.astype(F32)[:, None] * inv
    cos = jnp.concatenate([jnp.cos(ar), jnp.cos(ar), jnp.cos(ac), jnp.cos(ac)], -1)
    sin = jnp.concatenate([-jnp.sin(ar), jnp.sin(ar), -jnp.sin(ac), jnp.sin(ac)], -1)
    return jnp.tile(cos, (1, ATT_HEADS)), jnp.tile(sin, (1, ATT_HEADS))


def _rope(x, cos, sin):
    w = x.shape[-1]
    quarter = ATT_HEADDIM // 4
    lane = lax.broadcasted_iota(jnp.int32, x.shape, x.ndim - 1)
    partner = jnp.where((lane // quarter) % 2 == 0, pltpu.roll(x, w - quarter, x.ndim - 1),
                        pltpu.roll(x, quarter, x.ndim - 1))
    return x * cos + partner * sin


def _rope_body(q_ref, kv_ref, cos_ref, sin_ref, qo_ref, ko_ref):
    cos = cos_ref[...]
    sin = sin_ref[...]
    qo_ref[0] = (_rope(q_ref[0].astype(F32), cos, sin) * (ATT_HEADDIM ** -0.5 * LOG2E)).astype(qo_ref.dtype)
    ko_ref[0] = _rope(kv_ref[0].astype(F32), cos[:, :ATT_KV], sin[:, :ATT_KV]).astype(ko_ref.dtype)


def _apply_rope(u_q, u_kv, cos, sin):
    b, L, _ = u_q.shape
    tl = 512
    row = lambda i, bi: (bi, i, 0)
    tab = lambda i, bi: (i, 0)
    return pl.pallas_call(
        _rope_body,
        grid=(L // tl, b),
        in_specs=[pl.BlockSpec((1, tl, ATT_WIDTH), row), pl.BlockSpec((1, tl, ATT_KV), row),
                  pl.BlockSpec((tl, ATT_WIDTH), tab), pl.BlockSpec((tl, ATT_WIDTH), tab)],
        out_specs=[pl.BlockSpec((1, tl, ATT_WIDTH), row), pl.BlockSpec((1, tl, ATT_KV), row)],
        out_shape=[jax.ShapeDtypeStruct((b, L, ATT_WIDTH), BF16), jax.ShapeDtypeStruct((b, L, ATT_KV), BF16)],
        compiler_params=_cp("parallel", "arbitrary"),
        name="rope",
    )(u_q, u_kv, cos, sin)


def _wattn_body(sink_ref, bias_ref, q_ref, kp_ref, kc_ref, kn_ref, vp_ref, vc_ref, vn_ref, kx_ref, vx_ref, z_ref,
                o_ref):
    hd = ATT_HEADDIM
    low_half = lax.broadcasted_iota(jnp.int32, (WINDOW, 2 * hd), 1) < hd
    q = q_ref[0]
    bias = bias_ref[0]
    z = z_ref[0].astype(F32)
    k_all, v_ext = [], []
    for g in range(ATT_KV_HEADS):
        ks = slice(g * hd, (g + 1) * hd)
        k_all.append(jnp.concatenate([r[0, :, ks] for r in (kp_ref, kc_ref, kn_ref, kx_ref)], 0).astype(BF16))
        v_all = jnp.concatenate([r[0, :, ks] for r in (vp_ref, vc_ref, vn_ref, vx_ref)], 0).astype(BF16)
        ones = jnp.ones_like(v_all)
        v_ext.append((jnp.concatenate([v_all, ones], 1), jnp.concatenate([ones, v_all], 1)))
    heads = range(ATT_HEADS)
    sinks = [sink_ref[h] * LOG2E for h in heads]
    scores = [_dot_t(q[:, h * hd:(h + 1) * hd].astype(BF16), k_all[h // ATT_GROUP]) + bias for h in heads]
    maxes = [jnp.maximum(jnp.max(s, -1, keepdims=True), sk) for s, sk in zip(scores, sinks)]
    probs = [jnp.exp2(s - m).astype(BF16) for s, m in zip(scores, maxes)]
    exts = [_dot(p, v_ext[h // ATT_GROUP][h % 2]) for h, p in zip(heads, probs)]
    outs = [e / (pltpu.roll(e, hd, 1) + jnp.exp2(sk - m)) for e, sk, m in zip(exts, sinks, maxes)]
    for pair in range(ATT_HEADS // 2):
        cs = slice(2 * pair * hd, (2 * pair + 2) * hd)
        o_ref[0, :, cs] = (jnp.where(low_half, outs[2 * pair], outs[2 * pair + 1]) * _silu(z[:, cs])).astype(o_ref.dtype)


def _window_attention(q_rot, u_kv, uc_kv, sinks, z_a):
    b, L, _ = q_rot.shape
    lc = uc_kv.shape[1]
    nb = L // WINDOW
    hd2 = ATT_KV
    cur = lambda bi, i: (bi, i, 0)
    prv = lambda bi, i: (bi, jnp.maximum(i - 1, 0), 0)
    nxt = lambda bi, i: (bi, jnp.minimum(i + 1, nb - 1), 0)
    vcur = lambda bi, i: (bi, i, 1)
    vprv = lambda bi, i: (bi, jnp.maximum(i - 1, 0), 1)
    vnxt = lambda bi, i: (bi, jnp.minimum(i + 1, nb - 1), 1)
    kblk = lambda f: pl.BlockSpec((1, WINDOW, hd2), f)
    nk = 3 * WINDOW + lc
    row = np.arange(WINDOW)[:, None]
    col = np.arange(nk)[None, :]
    in_prev = (col < WINDOW) & (col >= row)
    in_next = (col >= 2 * WINDOW) & (col < 3 * WINDOW) & (col - 2 * WINDOW <= row)
    always = ((col >= WINDOW) & (col < 2 * WINDOW)) | (col >= 3 * WINDOW)
    kinds = [always | in_next, always | in_prev | in_next, always | in_prev]
    if nb == 1:
        kinds = [always] * 3
    bias = jnp.asarray(np.where(np.stack(kinds), 0.0, NEG).astype(np.float32))
    kind = lambda bi, i: (jnp.where(i == 0, 0, jnp.where(i == nb - 1, 2, 1)), 0, 0)
    return pl.pallas_call(
        _wattn_body,
        grid=(b, nb),
        in_specs=[pl.BlockSpec(memory_space=pltpu.SMEM),
                  pl.BlockSpec((1, WINDOW, nk), kind),
                  pl.BlockSpec((1, WINDOW, ATT_WIDTH), cur),
                  kblk(prv), kblk(cur), kblk(nxt), kblk(vprv), kblk(vcur), kblk(vnxt),
                  pl.BlockSpec((1, lc, hd2), lambda bi, i: (bi, 0, 0)),
                  pl.BlockSpec((1, lc, hd2), lambda bi, i: (bi, 0, 1)),
                  pl.BlockSpec((1, WINDOW, ATT_WIDTH), cur)],
        out_specs=pl.BlockSpec((1, WINDOW, ATT_WIDTH), cur),
        out_shape=jax.ShapeDtypeStruct((b, L, ATT_WIDTH), BF16),
        compiler_params=_cp("parallel", "arbitrary"),
        name="window_attention",
    )(sinks, bias, q_rot, u_kv, u_kv, u_kv, u_kv, u_kv, u_kv, uc_kv, uc_kv, z_a)


def _cattn_body(sink_ref, q_ref, k_ref, v_ref, z_ref, o_ref):
    scale = ATT_HEADDIM ** -0.5
    q = q_ref[0]
    z = z_ref[0].astype(F32)
    for g in range(ATT_KV_HEADS):
        ks = slice(g * ATT_HEADDIM, (g + 1) * ATT_HEADDIM)
        k = k_ref[0, :, ks].astype(BF16)
        v = v_ref[0, :, ks].astype(BF16)
        for j in range(ATT_GROUP):
            h = g * ATT_GROUP + j
            hs = slice(h * ATT_HEADDIM, (h + 1) * ATT_HEADDIM)
            s = _dot_t(q[:, hs].astype(BF16), k) * scale
            sink = sink_ref[h]
            m = jnp.maximum(jnp.max(s, -1, keepdims=True), sink)
            p = jnp.exp(s - m)
            den = jnp.sum(p, -1, keepdims=True) + jnp.exp(sink - m)
            o_ref[0, :, hs] = (_dot(p.astype(BF16), v) / den * _silu(z[:, hs])).astype(o_ref.dtype)


def _ctx_attention(uc_q, uc_kv, sinks, z_ac):
    b, lc, _ = uc_q.shape
    blk = lambda bi: (bi, 0, 0)
    return pl.pallas_call(
        _cattn_body,
        grid=(b,),
        in_specs=[pl.BlockSpec(memory_space=pltpu.SMEM),
                  pl.BlockSpec((1, lc, ATT_WIDTH), blk),
                  pl.BlockSpec((1, lc, ATT_KV), lambda bi: (bi, 0, 0)),
                  pl.BlockSpec((1, lc, ATT_KV), lambda bi: (bi, 0, 1)),
                  pl.BlockSpec((1, lc, ATT_WIDTH), blk)],
        out_specs=pl.BlockSpec((1, lc, ATT_WIDTH), blk),
        out_shape=jax.ShapeDtypeStruct((b, lc, ATT_WIDTH), BF16),
        compiler_params=_cp("parallel"),
        name="ctx_attention",
    )(sinks, uc_q, uc_kv, uc_kv, z_ac)


def _out_body(h_ref, g_ref, yhy_ref, yd_ref, yf_ref, yb_ref, xs_ref, zs_ref, yat_ref, dsk_ref, nw_ref, w_ref,
              lg_ref, lb_ref, o_ref):
    gw = SSM_WIDTH // SSM_GROUPS
    y_scan = yd_ref[0].astype(F32) + yf_ref[0].astype(F32) + yb_ref[0].astype(F32)
    ys = (y_scan + xs_ref[0].astype(F32) * dsk_ref[...]) * _silu(zs_ref[0].astype(F32))
    acc = _dot(yhy_ref[0].astype(BF16), w_ref[0:HY_WIDTH, :])
    for g in range(SSM_GROUPS):
        seg = ys[:, g * gw:(g + 1) * gw]
        seg = seg * lax.rsqrt(jnp.mean(seg * seg, -1, keepdims=True) + RMS_EPS) * nw_ref[:, g * gw:(g + 1) * gw]
        lo = HY_WIDTH + g * gw
        acc = acc + _dot(seg.astype(BF16), w_ref[lo:lo + gw, :])
    acc = acc + _dot(yat_ref[0].astype(BF16), w_ref[HY_WIDTH + SSM_WIDTH:, :])
    r = DEEPNORM_ALPHA * h_ref[0] + g_ref[0] * acc
    mu = jnp.mean(r, -1, keepdims=True)
    rc = r - mu
    var = jnp.mean(rc * rc, -1, keepdims=True)
    o_ref[0] = rc * lax.rsqrt(var + LN_EPS) * lg_ref[...] + lb_ref[...]


def _out_projection(h, gate_mod, y_hy, y_ssd, xbc, z_s, y_at, d_skip, norm_w, w_out, ln_g, ln_b):
    b, L, d = h.shape
    tm = min(L, PROJ_ROWS)
    row = lambda bi, i: (bi, i, 0)
    vec = lambda bi, i: (bi, 0, 0)
    full = lambda bi, i: (0, 0)
    w512 = pl.BlockSpec((1, tm, SSM_WIDTH), row)
    return pl.pallas_call(
        _out_body,
        grid=(b, L // tm),
        in_specs=[pl.BlockSpec((1, tm, d), row), pl.BlockSpec((1, 1, d), vec),
                  w512, w512, w512, w512, w512, w512, w512,
                  pl.BlockSpec((1, SSM_WIDTH), full), pl.BlockSpec((1, SSM_WIDTH), full),
                  pl.BlockSpec(w_out.shape, full), pl.BlockSpec((1, d), full), pl.BlockSpec((1, d), full)],
        out_specs=pl.BlockSpec((1, tm, d), row),
        out_shape=jax.ShapeDtypeStruct((b, L, d), F32),
        compiler_params=_cp("parallel", "arbitrary"),
        name="out_projection",
    )(h, gate_mod, y_hy, *y_ssd, xbc, z_s, y_at, d_skip, norm_w, w_out.astype(BF16),
      ln_g.reshape(1, d), ln_b.reshape(1, d))


def _sequence_front(h, shift, scale, w_packed, ssm_conv_w, ssm_conv_b, rope_tables=None):
    u_hy3, u_hyg, u_xbc, u_zs, u_q, u_kv, u_za, u_dt = _in_projection(h, shift, scale, w_packed, rope_tables)
    xbc = _dwconv(u_xbc, ssm_conv_w, ssm_conv_b, act=True, split=SSM_CONV_DIM)[0]
    return dict(hy3=u_hy3, hy_gate=u_hyg, xbc=xbc, z_s=u_zs, q=u_q, kv=u_kv, z_a=u_za, dt=u_dt)


def kernel(x, c, ctx, c_ctx, w_mod, b_mod, w_in, hy_conv_w, hy_conv_b, hy_f_w1, hy_f_b1, hy_f_w2, hy_f_b2,
           hy_f_w3, hy_f_b3, hy_f_freq, hy_f_wout, hy_bias, ssm_conv_w, ssm_conv_b, ssm_dt_bias, ssm_a_log,
           ssm_d, ssm_norm_w, attn_sinks, w_out, ln_g, ln_b):
    b, L, d = x.shape
    lc = ctx.shape[1]
    cos, sin = _rope_tables(L)
    cc = jnp.concatenate([c, c_ctx[None], jnp.zeros((16 - b - 1, d), F32)], 0)
    zero_state = jnp.zeros((b, 2, SSM_GROUPS, SSM_STATE, SSM_WIDTH // SSM_GROUPS), F32)
    h_lat, h_ctx = x, ctx
    for i in range(DEPTH):
        ctx_needed = i < DEPTH - 1
        mod = _modulation(cc, w_mod[i], b_mod[i])
        sh, sc, g = (mod[:b, None, j * d:(j + 1) * d] for j in range(3))
        sh_c, sc_c, g_c = (jnp.broadcast_to(mod[b:b + 1, None, j * d:(j + 1) * d], (b, 1, d)) for j in range(3))
        w_packed = _pack_w_in(w_in[i])
        lat = _sequence_front(h_lat, sh, sc, w_packed, ssm_conv_w[i], ssm_conv_b[i], (cos, sin))
        cx = _sequence_front(h_ctx, sh_c, sc_c, w_packed, ssm_conv_w[i], ssm_conv_b[i])

        dt_bias_row = jnp.pad(ssm_dt_bias[i].reshape(1, -1), ((0, 0), (0, LANES - 2 * SSM_HEADS)))
        a_row = jnp.pad(-jnp.exp(ssm_a_log[i]).reshape(1, -1), ((0, 0), (0, LANES - 2 * SSM_HEADS)))
        ys_c, s_c = _ssd(cx["xbc"], cx["dt"], dt_bias_row, a_row, zero_state)
        ys, _ = _ssd(lat["xbc"], lat["dt"], dt_bias_row, a_row, s_c)

        filt = (hy_f_w1[i], hy_f_b1[i], hy_f_w2[i], hy_f_b2[i], hy_f_w3[i], hy_f_b3[i], hy_f_freq[i], hy_f_wout[i])
        kf = _hyena_spectrum(L, _hyena_filter_taps(L, *filt))
        y_hy = _hyena(lat["hy3"], lat["hy_gate"], hy_conv_w[i], hy_conv_b[i], kf, hy_bias[i])

        y_at = _window_attention(lat["q"], lat["kv"], cx["kv"], attn_sinks[i], lat["z_a"])

        d_skip = jnp.repeat(ssm_d[i], SSM_HEADDIM).reshape(1, SSM_WIDTH)
        norm_w = ssm_norm_w[i].reshape(1, SSM_WIDTH)
        new_lat = _out_projection(h_lat, g, y_hy, ys, lat["xbc"], lat["z_s"], y_at, d_skip, norm_w,
                                  w_out[i], ln_g[i], ln_b[i])
        if ctx_needed:
            kf_c = _hyena_spectrum(lc, _hyena_filter_taps(lc, *filt))
            y_hy_c = _hyena(cx["hy3"], cx["hy_gate"], hy_conv_w[i], hy_conv_b[i], kf_c, hy_bias[i])
            y_at_c = _ctx_attention(cx["q"], cx["kv"], attn_sinks[i], cx["z_a"])
            h_ctx = _out_projection(h_ctx, g_c, y_hy_c, ys_c, cx["xbc"], cx["z_s"], y_at_c, d_skip,
                                    norm_w, w_out[i], ln_g[i], ln_b[i])
        h_lat = new_lat
    return h_lat
```

```python
import functools
import math

import numpy as np
import jax
import jax.numpy as jnp
from jax import lax
from jax.experimental import pallas as pl
from jax.experimental.pallas import tpu as pltpu

F32 = jnp.float32
BF16 = jnp.bfloat16
HI = lax.Precision.HIGHEST

D_MODEL = 1024
DEPTH = 2
GRID_W = 64
HY_WIDTH = 512
HY_BANDS = 16
HY_EMB = 1 + 2 * HY_BANDS
HY_FILTER_HIDDEN = 64
HY_DECAY_TARGET = 1e-2
HY_FAST_PCT = 0.3
HY_SLOW_PCT = 1.5
SSM_WIDTH = 512
SSM_HEADS = 8
SSM_HEADDIM = 64
SSM_GROUPS = 2
SSM_STATE = 128
SSM_CHUNK = 128
SSM_CONV_DIM = SSM_WIDTH + 2 * SSM_GROUPS * SSM_STATE
ATT_WIDTH = 512
ATT_HEADS = 8
ATT_KV_HEADS = 2
ATT_HEADDIM = 64
ATT_GROUP = ATT_HEADS // ATT_KV_HEADS
ATT_KV = ATT_KV_HEADS * ATT_HEADDIM
WINDOW = 128
ROPE_BASE = 10000.0
HY_IN = 4 * HY_WIDTH
SSM_IN = SSM_CONV_DIM + SSM_WIDTH + 2 * SSM_HEADS
DEEPNORM_ALPHA = (2 * DEPTH) ** 0.25
LN_EPS = 1e-6
RMS_EPS = 1e-5

LANES = 128
VMEM_LIMIT = 56 * 1024 * 1024
NEG = -1e30
BIG = 1e30
LOG2E = math.log2(math.e)
FILTER_ROWS = 256
PROJ_ROWS = 512


def _cp(*sem):
    return pltpu.CompilerParams(dimension_semantics=sem, vmem_limit_bytes=VMEM_LIMIT)


def _silu(x):
    return (0.5 * x) * (1.0 + jnp.tanh(0.5 * x))


def _dot(a, b, precision=None):
    return jnp.dot(a, b, preferred_element_type=F32, precision=precision)


def _dot_t(a, b):
    return lax.dot_general(a, b, (((1,), (1,)), ((), ())), preferred_element_type=F32)


def _mod_body(c_ref, w_ref, b_ref, o_ref):
    o_ref[...] = _dot(_silu(c_ref[...]), w_ref[...], HI) + b_ref[...]


def _modulation(cc, w, b):
    rows, d = cc.shape
    n = w.shape[1]
    tn = 1024
    return pl.pallas_call(
        _mod_body,
        grid=(n // tn,),
        in_specs=[pl.BlockSpec((rows, d), lambda j: (0, 0)),
                  pl.BlockSpec((d, tn), lambda j: (0, j)),
                  pl.BlockSpec((1, tn), lambda j: (0, j))],
        out_specs=pl.BlockSpec((rows, tn), lambda j: (0, j)),
        out_shape=jax.ShapeDtypeStruct((rows, n), F32),
        compiler_params=_cp("arbitrary"),
        name="modulation",
    )(cc, w, b.reshape(1, n))


IN_SEGS = (3 * HY_WIDTH, HY_WIDTH, SSM_CONV_DIM, SSM_WIDTH, ATT_WIDTH, 2 * ATT_KV, ATT_WIDTH, LANES)
SEG_Q, SEG_KV = 4, 5
IN_DTYPES = (BF16,) * (len(IN_SEGS) - 1) + (F32,)
IN_CHUNK = 512


def _pack_w_in(w):
    o_ss = HY_IN
    o_at = HY_IN + SSM_IN
    dt = w[:, o_ss + SSM_CONV_DIM + SSM_WIDTH:o_at]
    parts = [w[:, :HY_IN], w[:, o_ss:o_ss + SSM_CONV_DIM + SSM_WIDTH], w[:, o_at:],
             dt, jnp.zeros((w.shape[0], LANES - dt.shape[1]), w.dtype)]
    return jnp.concatenate(parts, axis=1).astype(BF16)


def _inproj_body(h_ref, sh_ref, sc_ref, w_ref, *rest, rope):
    tabs, o_refs = (rest[:2], rest[2:]) if rope else ((), rest)
    x = h_ref[0]
    mu = jnp.mean(x, -1, keepdims=True)
    xc = x - mu
    var = jnp.mean(xc * xc, -1, keepdims=True)
    xm = (xc * lax.rsqrt(var + LN_EPS) * (1.0 + sc_ref[0]) + sh_ref[0]).astype(BF16)
    off = 0
    for seg, (o_ref, n) in enumerate(zip(o_refs, IN_SEGS)):
        for j in range(0, n, IN_CHUNK):
            w = min(IN_CHUNK, n - j)
            r = _dot(xm, w_ref[:, off + j:off + j + w])
            if rope and seg == SEG_Q:
                r = _rope(r, tabs[0][...], tabs[1][...]) * (ATT_HEADDIM ** -0.5 * LOG2E)
            if rope and seg == SEG_KV:
                o_ref[0, :, :ATT_KV] = _rope(r[:, :ATT_KV], tabs[0][:, :ATT_KV], tabs[1][:, :ATT_KV]).astype(o_ref.dtype)
                o_ref[0, :, ATT_KV:] = r[:, ATT_KV:].astype(o_ref.dtype)
            else:
                o_ref[0, :, j:j + w] = r.astype(o_ref.dtype)
        off += n


def _in_projection(h, shift, scale, w_packed, rope_tables=None):
    b, L, d = h.shape
    tm = min(L, PROJ_ROWS)
    n_all = w_packed.shape[1]
    row = lambda bi, i: (bi, i, 0)
    vec = lambda bi, i: (bi, 0, 0)
    rope = rope_tables is not None
    tab_specs = [pl.BlockSpec((tm, ATT_WIDTH), lambda bi, i: (i, 0))] * 2 if rope else []
    return pl.pallas_call(
        functools.partial(_inproj_body, rope=rope),
        grid=(b, L // tm),
        in_specs=[pl.BlockSpec((1, tm, d), row), pl.BlockSpec((1, 1, d), vec), pl.BlockSpec((1, 1, d), vec),
                  pl.BlockSpec((d, n_all), lambda bi, i: (0, 0))] + tab_specs,
        out_specs=[pl.BlockSpec((1, tm, n), row) for n in IN_SEGS],
        out_shape=[jax.ShapeDtypeStruct((b, L, n), dt) for n, dt in zip(IN_SEGS, IN_DTYPES)],
        compiler_params=_cp("parallel", "arbitrary"),
        name="in_projection",
    )(h, shift, scale, w_packed, *(rope_tables or ()))


def _dwconv_body(u_ref, w_ref, b_ref, o_ref, *, act):
    x = u_ref[0].astype(F32)
    L = x.shape[0]
    row = lax.broadcasted_iota(jnp.int32, x.shape, 0)
    prev = jnp.where(row == 0, 0.0, pltpu.roll(x, 1, 0))
    nxt = jnp.where(row == L - 1, 0.0, pltpu.roll(x, L - 1, 0))
    y = prev * w_ref[0:1, :] + x * w_ref[1:2, :] + nxt * w_ref[2:3, :] + b_ref[...]
    if act:
        y = _silu(y)
    o_ref[0, 0] = y.astype(o_ref.dtype)


def _dwconv(u, w, bias, *, act, split):
    b, L, c = u.shape
    tc = 256
    per = split // tc
    return pl.pallas_call(
        functools.partial(_dwconv_body, act=act),
        grid=(b, c // tc),
        in_specs=[pl.BlockSpec((1, L, tc), lambda bi, j: (bi, 0, j)),
                  pl.BlockSpec((3, tc), lambda bi, j: (0, j)),
                  pl.BlockSpec((1, tc), lambda bi, j: (0, j))],
        out_specs=pl.BlockSpec((1, 1, L, tc), lambda bi, j: (j // per, bi, 0, j % per)),
        out_shape=jax.ShapeDtypeStruct((c // split, b, L, split), BF16),
        compiler_params=_cp("parallel", "arbitrary"),
        name="dwconv",
    )(u, w, bias.reshape(1, c))


def _filter_features(L):
    t = jnp.linspace(0.0, 1.0, L, dtype=F32)[:, None]
    w = 2.0 * math.pi * jnp.arange(L, dtype=F32)[:, None] / L
    f = jnp.linspace(1e-4, HY_BANDS - 1, HY_BANDS, dtype=F32)[None]
    z = jnp.concatenate([t, jnp.cos(f * w), -jnp.sin(f * w)], -1)
    return jnp.pad(z, ((0, 0), (0, LANES - HY_EMB)))


def _pad_to(a, rows, cols):
    return jnp.pad(a, ((0, rows - a.shape[0]), (0, cols - a.shape[1])))


def _filter_body(z_ref, w1_ref, w2_ref, w3_ref, b_ref, fr_ref, wo_ref, ad_ref, o_ref, *, nblk):
    i = pl.program_id(0)

    @pl.when(i < nblk)
    def _():
        z = z_ref[...]
        fr = fr_ref[...]
        h = jnp.sin(fr * (_dot(z, w1_ref[...], HI) + b_ref[0:1, :]))
        h = jnp.sin(fr * (_dot(h, w2_ref[...], HI) + b_ref[1:2, :]))
        h = jnp.sin(fr * (_dot(h, w3_ref[...], HI) + b_ref[2:3, :]))
        win = jnp.exp(-z[:, 0:1] * ad_ref[...])
        h_hi = h.astype(BF16)
        h_lo = (h - h_hi.astype(F32)).astype(BF16)
        for j in range(4):
            cols = slice(j * HY_WIDTH, (j + 1) * HY_WIDTH)
            taps = _dot(h_hi, wo_ref[0, :, cols]) + _dot(h_lo, wo_ref[0, :, cols]) + _dot(h_hi, wo_ref[1, :, cols])
            o_ref[:, cols] = taps * win

    @pl.when(i == nblk)
    def _():
        o_ref[...] = jnp.zeros_like(o_ref)


def _hyena_filter_taps(L, w1, b1, w2, b2, w3, b3, freq, w_out):
    z = _filter_features(L)
    hp = LANES
    bias = jnp.stack([jnp.pad(b, (0, hp - b.shape[0])) for b in (b1, b2, b3)])
    bias = jnp.pad(bias, ((0, 5), (0, 0)))
    fr = jnp.pad(freq, (0, hp - freq.shape[0])).reshape(1, hp)
    max_decay = math.log(HY_DECAY_TARGET) / HY_FAST_PCT
    min_decay = math.log(HY_DECAY_TARGET) / HY_SLOW_PCT
    absd = jnp.abs(jnp.linspace(min_decay, max_decay, HY_WIDTH, dtype=F32)).reshape(1, HY_WIDTH)
    tl = FILTER_ROWS
    nblk = L // tl
    n = 4 * HY_WIDTH
    full = lambda i: (0, 0)
    return pl.pallas_call(
        functools.partial(_filter_body, nblk=nblk),
        grid=(nblk + 1,),
        in_specs=[pl.BlockSpec((tl, hp), lambda i: (jnp.minimum(i, nblk - 1), 0)),
                  pl.BlockSpec((hp, hp), full), pl.BlockSpec((hp, hp), full), pl.BlockSpec((hp, hp), full),
                  pl.BlockSpec((8, hp), full), pl.BlockSpec((1, hp), full),
                  pl.BlockSpec((2, hp, n), lambda i: (0, 0, 0)), pl.BlockSpec((1, HY_WIDTH), full)],
        out_specs=pl.BlockSpec((tl, n), lambda i: (i, 0)),
        out_shape=jax.ShapeDtypeStruct((L + tl, n), F32),
        compiler_params=_cp("arbitrary"),
        name="hyena_filter",
    )(z, _pad_to(w1, hp, hp), _pad_to(w2, hp, hp), _pad_to(w3, hp, hp), bias, fr, _split2(_pad_to(w_out, hp, n)),
      absd)


def _hy_cfg(L):
    n2 = 128 if L >= 2048 else 16
    n1 = 2 * L // n2
    k1n = n1 // 2 + 1
    jp = -(-2 * k1n // 16) * 16
    pitch = n2 + 8
    return dict(L=L, n2=n2, n1=n1, nh=n1 // 2, k1n=k1n, jp=jp, pitch=pitch)


@functools.lru_cache(maxsize=None)
def _dft_tables(L):
    cfg = _hy_cfg(L)
    n, n1, n2, nh, k1n, jp = 2 * L, cfg["n1"], cfg["n2"], cfg["nh"], cfg["k1n"], cfg["jp"]
    a_n1 = np.arange(n1)
    a_k1 = np.arange(k1n)
    th = 2 * np.pi * np.outer(a_k1, a_n1) / n1
    f1 = np.zeros((jp, n1))
    f1[0:2 * k1n:2] = np.cos(th)
    f1[1:2 * k1n:2] = -np.sin(th)
    a_n2 = np.arange(n2)
    m1 = np.zeros((k1n, 2 * n2, 2 * n2))
    for k in range(k1n):
        f = np.exp(-2j * np.pi * (np.outer(a_n2, a_n2) / n2 + a_n2[None, :] * k / n))
        m1[k] = np.block([[f.real, -f.imag], [f.imag, f.real]])
    ck = np.full(k1n, 2.0)
    ck[0] = 1.0
    ck[-1] = 1.0
    th6 = 2 * np.pi * np.outer(np.arange(nh), a_k1) / n1
    f6 = np.zeros((nh, jp))
    f6[:, 0:2 * k1n:2] = ck * np.cos(th6) / n
    f6[:, 1:2 * k1n:2] = -ck * np.sin(th6) / n
    as32 = lambda a: np.asarray(a, np.float32)
    return dict(f1=as32(f1), m1=as32(m1), f6=as32(f6))


def _first_stage(src_ref, a_scr, f1, cfg):
    n2, nh, jp, pitch = cfg["n2"], cfg["nh"], cfg["jp"], cfg["pitch"]
    bt = src_ref.shape[0]

    def step(i, carry):
        xs = jnp.concatenate([src_ref[t, pl.ds(2 * i + u, nh, stride=pitch), :] for t in range(bt) for u in range(2)], 1)
        r = _dot(f1, xs.astype(BF16))
        for t in range(bt):
            for u in range(2):
                lo = (2 * t + u) * LANES
                a_scr[t, pl.ds(2 * i + u, jp, stride=pitch), :] = r[:, lo:lo + LANES]
        return carry

    lax.fori_loop(0, n2 // 2, step, 0, unroll=min(16 // bt, n2 // 2))


def _k1_rows(a_scr, k, cfg):
    n2, pitch = cfg["n2"], cfg["pitch"]
    base = pl.multiple_of(2 * k * pitch, 8)
    parts = [jnp.concatenate([a_scr[t, pl.ds(base, n2), :], a_scr[t, pl.ds(base + pitch, n2), :]], 0)
             for t in range(a_scr.shape[0])]
    return base, jnp.concatenate(parts, 1)


def _split2(table):
    t = jnp.asarray(table)
    hi = t.astype(BF16)
    return jnp.stack([hi, (t - hi.astype(F32)).astype(BF16)])


def _spectrum_body(kf_ref, kb_ref, f1_ref, m1_ref, o_ref, a_scr, *, cfg):
    n2, nh, jp, pitch = cfg["n2"], cfg["nh"], cfg["jp"], cfg["pitch"]

    def dot3(m_hi, m_lo, x):
        x_hi = x.astype(BF16)
        x_lo = (x - x_hi.astype(F32)).astype(BF16)
        return _dot(m_hi, x_hi) + _dot(m_lo, x_hi) + _dot(m_hi, x_lo)

    def step(i, carry):
        cols = []
        for t in range(2):
            n = 2 * i + t
            cols.append(jnp.concatenate([kf_ref[pl.ds(n, nh, stride=n2), :],
                                         kb_ref[pl.ds(n2 - n, nh, stride=n2), :]], 0))
        r = dot3(f1_ref[0], f1_ref[1], jnp.concatenate(cols, 1))
        a_scr[0, pl.ds(2 * i, jp, stride=pitch), :] = r[:, :LANES]
        a_scr[0, pl.ds(2 * i + 1, jp, stride=pitch), :] = r[:, LANES:]
        return carry

    lax.fori_loop(0, n2 // 2, step, 0, unroll=min(16, n2 // 2))
    lag0 = pl.ds(0, jp, stride=pitch)
    a_scr[0, lag0, :] = a_scr[0, lag0, :] + f1_ref[0, :, 0:1].astype(F32) * kb_ref[0:1, :]

    def mid(k, carry):
        _, a = _k1_rows(a_scr, k, cfg)
        o_ref[0, k] = dot3(m1_ref[0, k], m1_ref[1, k], a).astype(BF16)
        return carry

    lax.fori_loop(0, cfg["k1n"], mid, 0, unroll=3)


def _hyena_spectrum(L, taps):
    c = HY_WIDTH
    cfg = _hy_cfg(L)
    tb = _dft_tables(L)
    n1, n2, nh, k1n = cfg["n1"], cfg["n2"], cfg["nh"], cfg["k1n"]
    f1 = _split2(np.concatenate([tb["f1"][:, :nh], tb["f1"][:, n1 - 1:nh - 1:-1]], 1))
    m1 = _split2(tb["m1"])
    nct = c // LANES
    rows = taps.shape[0]
    return pl.pallas_call(
        functools.partial(_spectrum_body, cfg=cfg),
        grid=(2, nct),
        in_specs=[pl.BlockSpec((rows, LANES), lambda cv, j: (0, 2 * cv * nct + j)),
                  pl.BlockSpec((rows, LANES), lambda cv, j: (0, (2 * cv + 1) * nct + j)),
                  pl.BlockSpec(f1.shape, lambda cv, j: (0, 0, 0)),
                  pl.BlockSpec(m1.shape, lambda cv, j: (0, 0, 0, 0))],
        out_specs=pl.BlockSpec((1, k1n, 2 * n2, LANES), lambda cv, j: (cv, 0, 0, j)),
        out_shape=jax.ShapeDtypeStruct((2, k1n, 2 * n2, c), BF16),
        scratch_shapes=[pltpu.VMEM((1, cfg["jp"] * cfg["pitch"], LANES), F32)],
        compiler_params=_cp("arbitrary", "arbitrary"),
        name="hyena_spectrum",
    )(taps, taps, f1, m1)


def _short_conv(u_ref, w_ref, b_ref, which):
    x = u_ref[...].astype(F32)
    L = x.shape[0]
    edge = lax.broadcasted_iota(jnp.int32, (8, x.shape[1]), 0)
    prev = pltpu.roll(x, 1, 0)
    prev = jnp.concatenate([jnp.where(edge == 0, 0.0, prev[:8]), prev[8:]], 0)
    nxt = pltpu.roll(x, L - 1, 0)
    nxt = jnp.concatenate([nxt[:L - 8], jnp.where(edge == 7, 0.0, nxt[L - 8:])], 0)
    sel = slice(which, which + 1)
    return prev * w_ref[0, sel, :] + x * w_ref[1, sel, :] + nxt * w_ref[2, sel, :] + b_ref[sel, :]


def _long_conv(src_scr, a_scr, y_scr, w_scr, f1_ref, f6_ref, m1_ref, kf_ref, conv, cfg):
    n2, nh, jp, pitch = cfg["n2"], cfg["nh"], cfg["jp"], cfg["pitch"]
    bt = src_scr.shape[0]
    _first_stage(src_scr, a_scr, f1_ref[...], cfg)

    def forward(k, carry):
        _, a = _k1_rows(a_scr, k, cfg)
        x = _dot(m1_ref[k], a.astype(BF16))
        kk = kf_ref[conv, k].astype(F32)
        kr, ki = kk[:n2], kk[n2:]
        cols = []
        for t in range(bt):
            xr, xi = x[:n2, t * LANES:(t + 1) * LANES], x[n2:, t * LANES:(t + 1) * LANES]
            cols.append(jnp.concatenate([xr * kr - xi * ki, xr * ki + xi * kr], 0))
        w_scr[k] = jnp.concatenate(cols, 1).astype(BF16)
        return carry

    def inverse(k, carry):
        base = pl.multiple_of(2 * k * pitch, 8)
        b = lax.dot_general(m1_ref[k], w_scr[k], (((0,), (0,)), ((), ())), preferred_element_type=F32)
        for t in range(bt):
            a_scr[t, pl.ds(base, n2), :] = b[:n2, t * LANES:(t + 1) * LANES]
            a_scr[t, pl.ds(base + pitch, n2), :] = b[n2:, t * LANES:(t + 1) * LANES]
        return carry

    lax.fori_loop(0, cfg["k1n"], forward, 0, unroll=3)
    lax.fori_loop(0, cfg["k1n"], inverse, 0, unroll=3)
    f6 = f6_ref[...]

    def last(i, carry):
        bs = jnp.concatenate([a_scr[t, pl.ds(2 * i + u, jp, stride=pitch), :] for t in range(bt) for u in range(2)], 1)
        y = _dot(f6, bs.astype(BF16))
        for t in range(bt):
            for u in range(2):
                lo = (2 * t + u) * LANES
                y_scr[t, pl.ds(2 * i + u, nh, stride=pitch), :] = y[:, lo:lo + LANES]
        return carry

    lax.fori_loop(0, n2 // 2, last, 0, unroll=min(16 // bt, n2 // 2))


def _hyena_body(v_ref, x1_ref, x2_ref, g_ref, cw_ref, cb_ref, hb_ref, f1_ref, f6_ref, m1_ref, kf_ref,
                o_ref, s_scr, a_scr, y_scr, w_scr, *, cfg):
    tabs = (w_scr, f1_ref, f6_ref, m1_ref, kf_ref)
    n2, nh, pitch = cfg["n2"], cfg["nh"], cfg["pitch"]
    slots = range(s_scr.shape[0])

    def put(scr, t, val):
        for n1 in range(nh):
            scr[t, n1 * pitch:n1 * pitch + n2, :] = val[n1 * n2:(n1 + 1) * n2]

    def get(scr, t):
        return jnp.concatenate([scr[t, n1 * pitch:n1 * pitch + n2, :] for n1 in range(nh)], 0)

    for t in slots:
        put(s_scr, t, _short_conv(v_ref.at[t], cw_ref, cb_ref, 0))
    _long_conv(s_scr, a_scr, y_scr, *tabs, 0, cfg)
    for t in slots:
        put(s_scr, t, _short_conv(x1_ref.at[t], cw_ref, cb_ref, 1) * (get(y_scr, t) + get(s_scr, t) * hb_ref[0:1, :]))
    _long_conv(s_scr, a_scr, y_scr, *tabs, 1, cfg)
    for t in slots:
        y = _short_conv(x2_ref.at[t], cw_ref, cb_ref, 2) * (get(y_scr, t) + get(s_scr, t) * hb_ref[1:2, :])
        o_ref[t] = (y * _silu(g_ref[t].astype(F32))).astype(o_ref.dtype)


def _hyena(u_hy3, u_gate, conv_w, conv_b, kf, hy_bias):
    b, L, _ = u_gate.shape
    c = HY_WIDTH
    cfg = _hy_cfg(L)
    tb = _dft_tables(L)
    nct = c // LANES
    bt = 2 if b % 2 == 0 else 1
    col = lambda off: pl.BlockSpec((bt, L, LANES), lambda j, bi: (bi, 0, off * nct + j))
    full = lambda a: pl.BlockSpec(a.shape, lambda j, bi: (0,) * a.ndim)
    f1 = jnp.asarray(tb["f1"][:, :cfg["nh"]]).astype(BF16)
    f6 = jnp.asarray(tb["f6"]).astype(BF16)
    m1 = jnp.asarray(tb["m1"]).astype(BF16)
    return pl.pallas_call(
        functools.partial(_hyena_body, cfg=cfg),
        grid=(nct, b // bt),
        in_specs=[col(0), col(1), col(2), col(0),
                  pl.BlockSpec((3, 3, LANES), lambda j, bi: (0, 0, j)),
                  pl.BlockSpec((3, LANES), lambda j, bi: (0, j)),
                  pl.BlockSpec((2, LANES), lambda j, bi: (0, j)),
                  full(f1), full(f6), full(m1),
                  pl.BlockSpec((2, cfg["k1n"], 2 * cfg["n2"], LANES), lambda j, bi: (0, 0, 0, j),
                               pipeline_mode=pl.Buffered(1))],
        out_specs=col(0),
        out_shape=jax.ShapeDtypeStruct((b, L, c), BF16),
        scratch_shapes=[pltpu.VMEM((bt, cfg["nh"] * cfg["pitch"], LANES), F32),
                        pltpu.VMEM((bt, cfg["jp"] * cfg["pitch"], LANES), F32),
                        pltpu.VMEM((bt, cfg["nh"] * cfg["pitch"], LANES), F32),
                        pltpu.VMEM((cfg["k1n"], 2 * cfg["n2"], bt * LANES), BF16)],
        compiler_params=_cp("arbitrary", "arbitrary"),
        name="hyena",
    )(u_hy3, u_hy3, u_hy3, u_gate, conv_w.reshape(3, 3, c), conv_b.reshape(3, c), hy_bias, f1, f6, m1, kf)


SSD_CPS = 2
SSD_SPS = 2
SPLIT_STRIDE = 2 * SSM_HEADS


def _pack3(x):
    hi = x.astype(BF16).astype(F32)
    r1 = x - hi
    mid = r1.astype(BF16).astype(F32)
    lo = (r1 - mid).astype(BF16).astype(F32)
    return (hi + pltpu.roll(mid, SPLIT_STRIDE, 1) + pltpu.roll(lo, 2 * SPLIT_STRIDE, 1)).astype(BF16)


def _unpack3(x3, used):
    return jnp.where(used, x3 + pltpu.roll(x3, LANES - SPLIT_STRIDE, 1) + pltpu.roll(x3, LANES - 2 * SPLIT_STRIDE, 1), 0.0)


@functools.lru_cache(maxsize=None)
def _ssd_spread_tables():
    col = np.zeros((LANES, 2 * SSM_HEADS * LANES), np.float32)
    head = np.zeros((LANES, 2 * SSM_WIDTH), np.float32)
    for c in range(2 * SSM_HEADS):
        d, h = divmod(c, SSM_HEADS)
        for piece in range(3):
            col[c + piece * SPLIT_STRIDE, c * LANES:(c + 1) * LANES] = 1.0
            lo = d * SSM_WIDTH + h * SSM_HEADDIM
            head[c + piece * SPLIT_STRIDE, lo:lo + SSM_HEADDIM] = 1.0
    return col, head


def _ssd_chunk_body(xbc_ref, dtr_ref, dtb_ref, a_ref, ecol_ref, ehead_ref, yd_ref, cs_ref, ex_ref, et_ref):
    q = SSM_CHUNK
    hpg = SSM_HEADS // SSM_GROUPS
    gw = SSM_WIDTH // SSM_GROUPS
    li = lax.broadcasted_iota(jnp.int32, (q, q), 0)
    si = lax.broadcasted_iota(jnp.int32, (q, q), 1)
    below = li > si
    diag = li == si
    fwd_lane = si < SSM_HEADS
    used = si < 2 * SSM_HEADS
    tril = (li >= si).astype(BF16)
    triu = (li <= si).astype(BF16)
    chunks = range(SSD_CPS)
    rows = [slice(c * q, (c + 1) * q) for c in chunks]
    ehead = ehead_ref[...]

    raws = [dtr_ref[0, r, :] + dtb_ref[...] for r in rows]
    dts = [jnp.where(used, jnp.maximum(x, 0.0) + jnp.log1p(jnp.exp(-jnp.abs(x))), 0.0) for x in raws]
    da3 = [_pack3(dt * a_ref[...]) for dt in dts]
    acs = [jnp.where(fwd_lane, _unpack3(_dot(tril, x), used), _unpack3(_dot(triu, x), used)) for x in da3]
    tots = [jnp.where(fwd_lane[0:1], a[q - 1:q, :], a[0:1, :]) for a in acs]
    ws = [dt * jnp.exp(t - a) for dt, t, a in zip(dts, tots, acs)]
    acs3 = [_pack3(a) for a in acs]
    colb = [_dot(x, ecol_ref[...]) for x in acs3]
    wx = [_dot(_pack3(w), ehead) for w in ws]
    for c in chunks:
        ex_ref[0, rows[c], :] = jnp.exp(_dot(acs3[c], ehead)).astype(BF16)
        et_ref[0, c] = jnp.exp(_dot(_pack3(jnp.broadcast_to(tots[c], (8, LANES))), ehead))
    rowt = [(a - jnp.where(dt > 0.0, jnp.log(dt), -BIG)).T for a, dt in zip(acs, dts)]
    dsum = [(dt + pltpu.roll(dt, LANES - SSM_HEADS, 1)).T for dt in dts]
    xbc = [xbc_ref[0, r, :] for r in rows]
    xsb = [x[:, :SSM_WIDTH].astype(BF16) for x in xbc]
    bgs = [[x[:, SSM_WIDTH + g * SSM_STATE:SSM_WIDTH + (g + 1) * SSM_STATE] for g in range(SSM_GROUPS)] for x in xbc]
    cgs = [[x[:, SSM_WIDTH + (SSM_GROUPS + g) * SSM_STATE:SSM_WIDTH + (SSM_GROUPS + g + 1) * SSM_STATE]
            for g in range(SSM_GROUPS)] for x in xbc]
    gmat = [[_dot_t(cgs[c][g].astype(BF16), bgs[c][g].astype(BF16)) for g in range(SSM_GROUPS)] for c in chunks]
    mats = []
    for c in chunks:
        for h in range(SSM_HEADS):
            hb = SSM_HEADS + h
            arg = jnp.where(below, colb[c][:, h * q:(h + 1) * q] - rowt[c][h:h + 1, :],
                            colb[c][:, hb * q:(hb + 1) * q] - rowt[c][hb:hb + 1, :])
            dec = jnp.where(diag, dsum[c][h:h + 1, :], jnp.exp(arg))
            mats.append((gmat[c][h // hpg] * dec).astype(BF16))
    for c in chunks:
        for h in range(SSM_HEADS):
            lo = h * SSM_HEADDIM
            yd_ref[0, rows[c], lo:lo + SSM_HEADDIM] = _dot(mats[c * SSM_HEADS + h],
                                                             xsb[c][:, lo:lo + SSM_HEADDIM]).astype(yd_ref.dtype)
    for c in chunks:
        for g in range(SSM_GROUPS):
            bgt = bgs[c][g].astype(F32).T.astype(BF16)
            xg = xbc[c][:, g * gw:(g + 1) * gw].astype(F32)
            for d in range(2):
                lo = d * SSM_WIDTH + g * gw
                cs_ref[0, c, d, g] = _dot(bgt, (xg * wx[c][:, lo:lo + gw]).astype(BF16)).astype(cs_ref.dtype)


def _ssd_chunks(xbc, dt_raw, dt_bias_row, a_row):
    b, L, _ = xbc.shape
    nc = L // SSM_CHUNK
    rows = SSD_CPS * SSM_CHUNK
    gw = SSM_WIDTH // SSM_GROUPS
    blk = lambda bi, i: (bi, i, 0)
    full2 = lambda bi, i: (0, 0)
    ecol, ehead = (jnp.asarray(t).astype(BF16) for t in _ssd_spread_tables())
    return pl.pallas_call(
        _ssd_chunk_body,
        grid=(b, nc // SSD_CPS),
        in_specs=[pl.BlockSpec((1, rows, SSM_CONV_DIM), blk), pl.BlockSpec((1, rows, LANES), blk),
                  pl.BlockSpec((1, LANES), full2), pl.BlockSpec((1, LANES), full2),
                  pl.BlockSpec(ecol.shape, full2), pl.BlockSpec(ehead.shape, full2)],
        out_specs=[pl.BlockSpec((1, rows, SSM_WIDTH), blk),
                   pl.BlockSpec((1, SSD_CPS, 2, SSM_GROUPS, SSM_STATE, gw), lambda bi, i: (bi, i, 0, 0, 0, 0)),
                   pl.BlockSpec((1, rows, 2 * SSM_WIDTH), blk),
                   pl.BlockSpec((1, SSD_CPS, 8, 2 * SSM_WIDTH), lambda bi, i: (bi, i, 0, 0))],
        out_shape=[jax.ShapeDtypeStruct((b, L, SSM_WIDTH), BF16),
                   jax.ShapeDtypeStruct((b, nc, 2, SSM_GROUPS, SSM_STATE, gw), BF16),
                   jax.ShapeDtypeStruct((b, L, 2 * SSM_WIDTH), BF16),
                   jax.ShapeDtypeStruct((b, nc, 8, 2 * SSM_WIDTH), F32)],
        compiler_params=_cp("parallel", "arbitrary"),
        name="ssd_chunks",
    )(xbc, dt_raw, dt_bias_row, a_row, ecol, ehead)


def _ssd_state_body(cf_ref, cb_ref, xf_ref, xb_ref, ef_ref, eb_ref, sf_ref, sb_ref, init_ref, yf_ref, yb_ref, fin_ref,
                    st_ref, *, nsteps):
    ci = pl.program_id(1)

    @pl.when(ci == 0)
    def _():
        st_ref[...] = init_ref[0]

    q = SSM_CHUNK
    gw = SSM_WIDTH // SSM_GROUPS
    for s in range(SSD_SPS):
        dirs = ((cf_ref, xf_ref, ef_ref, sf_ref, yf_ref, s), (cb_ref, xb_ref, eb_ref, sb_ref, yb_ref, SSD_SPS - 1 - s))
        for d, (c_ref, x_ref, e_ref, s_ref, y_ref, j) in enumerate(dirs):
            rows = slice(j * q, (j + 1) * q)
            cmat = c_ref[0, rows, :].astype(BF16)
            for g in range(SSM_GROUPS):
                cols = slice(g * gw, (g + 1) * gw)
                st = st_ref[d, g]
                y_ref[0, rows, cols] = (_dot(cmat[:, g * SSM_STATE:(g + 1) * SSM_STATE], st.astype(BF16))
                                        * x_ref[0, rows, cols].astype(F32)).astype(y_ref.dtype)
                st_ref[d, g] = st * e_ref[0, j, 0:1, cols] + s_ref[0, j, 0, g].astype(F32)

    @pl.when(ci == nsteps - 1)
    def _():
        fin_ref[0] = st_ref[...]


def _ssd_states(xbc, ex, et, cs, init):
    b, L, _ = xbc.shape
    nc = L // SSM_CHUNK
    gw = SSM_WIDTH // SSM_GROUPS
    c_col = SSM_CONV_DIM // (SSM_GROUPS * SSM_STATE) - 1
    st_shape = (2, SSM_GROUPS, SSM_STATE, gw)
    st_spec = pl.BlockSpec((1,) + st_shape, lambda bi, c: (bi, 0, 0, 0, 0))
    nsteps = nc // SSD_SPS
    rows = SSD_SPS * SSM_CHUNK
    cs_blk = (1, SSD_SPS, 1, SSM_GROUPS, SSM_STATE, gw)
    fwd = lambda *tail: (lambda bi, c: (bi, c) + tail)
    bwd = lambda *tail: (lambda bi, c: (bi, nsteps - 1 - c) + tail)
    return pl.pallas_call(
        functools.partial(_ssd_state_body, nsteps=nsteps),
        grid=(b, nsteps),
        in_specs=[pl.BlockSpec((1, rows, SSM_GROUPS * SSM_STATE), fwd(c_col)),
                  pl.BlockSpec((1, rows, SSM_GROUPS * SSM_STATE), bwd(c_col)),
                  pl.BlockSpec((1, rows, SSM_WIDTH), fwd(0)),
                  pl.BlockSpec((1, rows, SSM_WIDTH), bwd(1)),
                  pl.BlockSpec((1, SSD_SPS, 8, SSM_WIDTH), fwd(0, 0)),
                  pl.BlockSpec((1, SSD_SPS, 8, SSM_WIDTH), bwd(0, 1)),
                  pl.BlockSpec(cs_blk, fwd(0, 0, 0, 0)),
                  pl.BlockSpec(cs_blk, bwd(1, 0, 0, 0)),
                  st_spec],
        out_specs=[pl.BlockSpec((1, rows, SSM_WIDTH), fwd(0)),
                   pl.BlockSpec((1, rows, SSM_WIDTH), bwd(0)),
                   st_spec],
        out_shape=[jax.ShapeDtypeStruct((b, L, SSM_WIDTH), BF16), jax.ShapeDtypeStruct((b, L, SSM_WIDTH), BF16),
                   jax.ShapeDtypeStruct((b,) + st_shape, F32)],
        scratch_shapes=[pltpu.VMEM(st_shape, F32)],
        compiler_params=_cp("parallel", "arbitrary"),
        name="ssd_states",
    )(xbc, xbc, ex, ex, et, et, cs, cs, init)


def _ssd(xbc, dt_raw, dt_bias_row, a_row, init):
    y_diag, cs, ex, et = _ssd_chunks(xbc, dt_raw, dt_bias_row, a_row)
    y_f, y_b, fin = _ssd_states(xbc, ex, et, cs, init)
    return (y_diag, y_f, y_b), fin


def _rope_tables(L):
    rows = L // GRID_W
    row = jnp.broadcast_to(jnp.arange(rows)[:, None], (rows, GRID_W)).reshape(L)
    col = jnp.broadcast_to(jnp.arange(GRID_W)[None, :], (rows, GRID_W)).reshape(L)
    nf = ATT_HEADDIM // 4
    inv = ROPE_BASE ** (-jnp.arange(nf, dtype=F32) / nf)
    ar = row.astype(F32)[:, None] * inv
    ac = col.astype(F32)[:, None] * inv
    cos = jnp.concatenate([jnp.cos(ar), jnp.cos(ar), jnp.cos(ac), jnp.cos(ac)], -1)
    sin = jnp.concatenate([-jnp.sin(ar), jnp.sin(ar), -jnp.sin(ac), jnp.sin(ac)], -1)
    return jnp.tile(cos, (1, ATT_HEADS)), jnp.tile(sin, (1, ATT_HEADS))


def _rope(x, cos, sin):
    w = x.shape[-1]
    quarter = ATT_HEADDIM // 4
    lane = lax.broadcasted_iota(jnp.int32, x.shape, x.ndim - 1)
    partner = jnp.where((lane // quarter) % 2 == 0, pltpu.roll(x, w - quarter, x.ndim - 1),
                        pltpu.roll(x, quarter, x.ndim - 1))
    return x * cos + partner * sin


def _wattn_body(sink_ref, bias_ref, q_ref, kp_ref, kc_ref, kn_ref, vp_ref, vc_ref, vn_ref, kx_ref, vx_ref, z_ref,
                o_ref):
    hd = ATT_HEADDIM
    low_half = lax.broadcasted_iota(jnp.int32, (WINDOW, 2 * hd), 1) < hd
    q = q_ref[0]
    bias = bias_ref[0]
    z = z_ref[0].astype(F32)
    k_all, v_ext = [], []
    for g in range(ATT_KV_HEADS):
        ks = slice(g * hd, (g + 1) * hd)
        k_all.append(jnp.concatenate([r[0, :, ks] for r in (kp_ref, kc_ref, kn_ref, kx_ref)], 0).astype(BF16))
        v_all = jnp.concatenate([r[0, :, ks] for r in (vp_ref, vc_ref, vn_ref, vx_ref)], 0).astype(BF16)
        ones = jnp.ones_like(v_all)
        v_ext.append((jnp.concatenate([v_all, ones], 1), jnp.concatenate([ones, v_all], 1)))
    heads = range(ATT_HEADS)
    sinks = [sink_ref[h] * LOG2E for h in heads]
    scores = [_dot_t(q[:, h * hd:(h + 1) * hd].astype(BF16), k_all[h // ATT_GROUP]) + bias for h in heads]
    maxes = [jnp.maximum(jnp.max(s, -1, keepdims=True), sk) for s, sk in zip(scores, sinks)]
    probs = [jnp.exp2(s - m).astype(BF16) for s, m in zip(scores, maxes)]
    exts = [_dot(p, v_ext[h // ATT_GROUP][h % 2]) for h, p in zip(heads, probs)]
    outs = [e / (pltpu.roll(e, hd, 1) + jnp.exp2(sk - m)) for e, sk, m in zip(exts, sinks, maxes)]
    for pair in range(ATT_HEADS // 2):
        cs = slice(2 * pair * hd, (2 * pair + 2) * hd)
        o_ref[0, :, cs] = (jnp.where(low_half, outs[2 * pair], outs[2 * pair + 1]) * _silu(z[:, cs])).astype(o_ref.dtype)


def _window_attention(q_rot, u_kv, uc_kv, sinks, z_a):
    b, L, _ = q_rot.shape
    lc = uc_kv.shape[1]
    nb = L // WINDOW
    hd2 = ATT_KV
    cur = lambda bi, i: (bi, i, 0)
    prv = lambda bi, i: (bi, jnp.maximum(i - 1, 0), 0)
    nxt = lambda bi, i: (bi, jnp.minimum(i + 1, nb - 1), 0)
    vcur = lambda bi, i: (bi, i, 1)
    vprv = lambda bi, i: (bi, jnp.maximum(i - 1, 0), 1)
    vnxt = lambda bi, i: (bi, jnp.minimum(i + 1, nb - 1), 1)
    kblk = lambda f: pl.BlockSpec((1, WINDOW, hd2), f)
    nk = 3 * WINDOW + lc
    row = np.arange(WINDOW)[:, None]
    col = np.arange(nk)[None, :]
    in_prev = (col < WINDOW) & (col >= row)
    in_next = (col >= 2 * WINDOW) & (col < 3 * WINDOW) & (col - 2 * WINDOW <= row)
    always = ((col >= WINDOW) & (col < 2 * WINDOW)) | (col >= 3 * WINDOW)
    kinds = [always | in_next, always | in_prev | in_next, always | in_prev]
    if nb == 1:
        kinds = [always] * 3
    bias = jnp.asarray(np.where(np.stack(kinds), 0.0, NEG).astype(np.float32))
    kind = lambda bi, i: (jnp.where(i == 0, 0, jnp.where(i == nb - 1, 2, 1)), 0, 0)
    return pl.pallas_call(
        _wattn_body,
        grid=(b, nb),
        in_specs=[pl.BlockSpec(memory_space=pltpu.SMEM),
                  pl.BlockSpec((1, WINDOW, nk), kind),
                  pl.BlockSpec((1, WINDOW, ATT_WIDTH), cur),
                  kblk(prv), kblk(cur), kblk(nxt), kblk(vprv), kblk(vcur), kblk(vnxt),
                  pl.BlockSpec((1, lc, hd2), lambda bi, i: (bi, 0, 0)),
                  pl.BlockSpec((1, lc, hd2), lambda bi, i: (bi, 0, 1)),
                  pl.BlockSpec((1, WINDOW, ATT_WIDTH), cur)],
        out_specs=pl.BlockSpec((1, WINDOW, ATT_WIDTH), cur),
        out_shape=jax.ShapeDtypeStruct((b, L, ATT_WIDTH), BF16),
        compiler_params=_cp("parallel", "arbitrary"),
        name="window_attention",
    )(sinks, bias, q_rot, u_kv, u_kv, u_kv, u_kv, u_kv, u_kv, uc_kv, uc_kv, z_a)


def _cattn_body(sink_ref, q_ref, k_ref, v_ref, z_ref, o_ref):
    scale = ATT_HEADDIM ** -0.5
    q = q_ref[0]
    z = z_ref[0].astype(F32)
    for g in range(ATT_KV_HEADS):
        ks = slice(g * ATT_HEADDIM, (g + 1) * ATT_HEADDIM)
        k = k_ref[0, :, ks].astype(BF16)
        v = v_ref[0, :, ks].astype(BF16)
        for j in range(ATT_GROUP):
            h = g * ATT_GROUP + j
            hs = slice(h * ATT_HEADDIM, (h + 1) * ATT_HEADDIM)
            s = _dot_t(q[:, hs].astype(BF16), k) * scale
            sink = sink_ref[h]
            m = jnp.maximum(jnp.max(s, -1, keepdims=True), sink)
            p = jnp.exp(s - m)
            den = jnp.sum(p, -1, keepdims=True) + jnp.exp(sink - m)
            o_ref[0, :, hs] = (_dot(p.astype(BF16), v) / den * _silu(z[:, hs])).astype(o_ref.dtype)


def _ctx_attention(uc_q, uc_kv, sinks, z_ac):
    b, lc, _ = uc_q.shape
    blk = lambda bi: (bi, 0, 0)
    return pl.pallas_call(
        _cattn_body,
        grid=(b,),
        in_specs=[pl.BlockSpec(memory_space=pltpu.SMEM),
                  pl.BlockSpec((1, lc, ATT_WIDTH), blk),
                  pl.BlockSpec((1, lc, ATT_KV), lambda bi: (bi, 0, 0)),
                  pl.BlockSpec((1, lc, ATT_KV), lambda bi: (bi, 0, 1)),
                  pl.BlockSpec((1, lc, ATT_WIDTH), blk)],
        out_specs=pl.BlockSpec((1, lc, ATT_WIDTH), blk),
        out_shape=jax.ShapeDtypeStruct((b, lc, ATT_WIDTH), BF16),
        compiler_params=_cp("parallel"),
        name="ctx_attention",
    )(sinks, uc_q, uc_kv, uc_kv, z_ac)


def _out_body(h_ref, g_ref, yhy_ref, yd_ref, yf_ref, yb_ref, xs_ref, zs_ref, yat_ref, dsk_ref, nw_ref, w_ref,
              lg_ref, lb_ref, o_ref):
    gw = SSM_WIDTH // SSM_GROUPS
    y_scan = yd_ref[0].astype(F32) + yf_ref[0].astype(F32) + yb_ref[0].astype(F32)
    ys = (y_scan + xs_ref[0].astype(F32) * dsk_ref[...]) * _silu(zs_ref[0].astype(F32))
    parts = [yhy_ref[0].astype(BF16)]
    for g in range(SSM_GROUPS):
        seg = ys[:, g * gw:(g + 1) * gw]
        seg = seg * lax.rsqrt(jnp.mean(seg * seg, -1, keepdims=True) + RMS_EPS) * nw_ref[:, g * gw:(g + 1) * gw]
        parts.append(seg.astype(BF16))
    parts.append(yat_ref[0].astype(BF16))
    acc = _dot(jnp.concatenate(parts, 1), w_ref[...])
    r = DEEPNORM_ALPHA * h_ref[0] + g_ref[0] * acc
    mu = jnp.mean(r, -1, keepdims=True)
    rc = r - mu
    var = jnp.mean(rc * rc, -1, keepdims=True)
    o_ref[0] = rc * lax.rsqrt(var + LN_EPS) * lg_ref[...] + lb_ref[...]


def _out_projection(h, gate_mod, y_hy, y_ssd, xbc, z_s, y_at, d_skip, norm_w, w_out, ln_g, ln_b):
    b, L, d = h.shape
    tm = min(L, PROJ_ROWS)
    row = lambda bi, i: (bi, i, 0)
    vec = lambda bi, i: (bi, 0, 0)
    full = lambda bi, i: (0, 0)
    w512 = pl.BlockSpec((1, tm, SSM_WIDTH), row)
    return pl.pallas_call(
        _out_body,
        grid=(b, L // tm),
        in_specs=[pl.BlockSpec((1, tm, d), row), pl.BlockSpec((1, 1, d), vec),
                  w512, w512, w512, w512, w512, w512, w512,
                  pl.BlockSpec((1, SSM_WIDTH), full), pl.BlockSpec((1, SSM_WIDTH), full),
                  pl.BlockSpec(w_out.shape, full), pl.BlockSpec((1, d), full), pl.BlockSpec((1, d), full)],
        out_specs=pl.BlockSpec((1, tm, d), row),
        out_shape=jax.ShapeDtypeStruct((b, L, d), F32),
        compiler_params=_cp("parallel", "arbitrary"),
        name="out_projection",
    )(h, gate_mod, y_hy, *y_ssd, xbc, z_s, y_at, d_skip, norm_w, w_out.astype(BF16),
      ln_g.reshape(1, d), ln_b.reshape(1, d))


def _sequence_front(h, shift, scale, w_packed, ssm_conv_w, ssm_conv_b, rope_tables=None):
    u_hy3, u_hyg, u_xbc, u_zs, u_q, u_kv, u_za, u_dt = _in_projection(h, shift, scale, w_packed, rope_tables)
    xbc = _dwconv(u_xbc, ssm_conv_w, ssm_conv_b, act=True, split=SSM_CONV_DIM)[0]
    return dict(hy3=u_hy3, hy_gate=u_hyg, xbc=xbc, z_s=u_zs, q=u_q, kv=u_kv, z_a=u_za, dt=u_dt)


def kernel(x, c, ctx, c_ctx, w_mod, b_mod, w_in, hy_conv_w, hy_conv_b, hy_f_w1, hy_f_b1, hy_f_w2, hy_f_b2,
           hy_f_w3, hy_f_b3, hy_f_freq, hy_f_wout, hy_bias, ssm_conv_w, ssm_conv_b, ssm_dt_bias, ssm_a_log,
           ssm_d, ssm_norm_w, attn_sinks, w_out, ln_g, ln_b):
    b, L, d = x.shape
    lc = ctx.shape[1]
    cos, sin = _rope_tables(L)
    cc = jnp.concatenate([c, c_ctx[None], jnp.zeros((16 - b - 1, d), F32)], 0)
    zero_state = jnp.zeros((b, 2, SSM_GROUPS, SSM_STATE, SSM_WIDTH // SSM_GROUPS), F32)
    h_lat, h_ctx = x, ctx
    for i in range(DEPTH):
        ctx_needed = i < DEPTH - 1
        mod = _modulation(cc, w_mod[i], b_mod[i])
        sh, sc, g = (mod[:b, None, j * d:(j + 1) * d] for j in range(3))
        sh_c, sc_c, g_c = (jnp.broadcast_to(mod[b:b + 1, None, j * d:(j + 1) * d], (b, 1, d)) for j in range(3))
        w_packed = _pack_w_in(w_in[i])
        lat = _sequence_front(h_lat, sh, sc, w_packed, ssm_conv_w[i], ssm_conv_b[i], (cos, sin))
        cx = _sequence_front(h_ctx, sh_c, sc_c, w_packed, ssm_conv_w[i], ssm_conv_b[i])

        dt_bias_row = jnp.pad(ssm_dt_bias[i].reshape(1, -1), ((0, 0), (0, LANES - 2 * SSM_HEADS)))
        a_row = jnp.pad(-jnp.exp(ssm_a_log[i]).reshape(1, -1), ((0, 0), (0, LANES - 2 * SSM_HEADS)))
        ys_c, s_c = _ssd(cx["xbc"], cx["dt"], dt_bias_row, a_row, zero_state)
        ys, _ = _ssd(lat["xbc"], lat["dt"], dt_bias_row, a_row, s_c)

        filt = (hy_f_w1[i], hy_f_b1[i], hy_f_w2[i], hy_f_b2[i], hy_f_w3[i], hy_f_b3[i], hy_f_freq[i], hy_f_wout[i])
        kf = _hyena_spectrum(L, _hyena_filter_taps(L, *filt))
        y_hy = _hyena(lat["hy3"], lat["hy_gate"], hy_conv_w[i], hy_conv_b[i], kf, hy_bias[i])

        y_at = _window_attention(lat["q"], lat["kv"], cx["kv"], attn_sinks[i], lat["z_a"])

        d_skip = jnp.repeat(ssm_d[i], SSM_HEADDIM).reshape(1, SSM_WIDTH)
        norm_w = ssm_norm_w[i].reshape(1, SSM_WIDTH)
        new_lat = _out_projection(h_lat, g, y_hy, ys, lat["xbc"], lat["z_s"], y_at, d_skip, norm_w,
                                  w_out[i], ln_g[i], ln_b[i])
        if ctx_needed:
            kf_c = _hyena_spectrum(lc, _hyena_filter_taps(lc, *filt))
            y_hy_c = _hyena(cx["hy3"], cx["hy_gate"], hy_conv_w[i], hy_conv_b[i], kf_c, hy_bias[i])
            y_at_c = _ctx_attention(cx["q"], cx["kv"], attn_sinks[i], cx["z_a"])
            h_ctx = _out_projection(h_ctx, g_c, y_hy_c, ys_c, cx["xbc"], cx["z_s"], y_at_c, d_skip,
                                    norm_w, w_out[i], ln_g[i], ln_b[i])
        h_lat = new_lat
    return h_lat
```

```python
import functools
import math

import numpy as np
import jax
import jax.numpy as jnp
from jax import lax
from jax.experimental import pallas as pl
from jax.experimental.pallas import tpu as pltpu

F32 = jnp.float32
BF16 = jnp.bfloat16
HI = lax.Precision.HIGHEST

D_MODEL = 1024
DEPTH = 2
GRID_W = 64
HY_WIDTH = 512
HY_BANDS = 16
HY_EMB = 1 + 2 * HY_BANDS
HY_FILTER_HIDDEN = 64
HY_DECAY_TARGET = 1e-2
HY_FAST_PCT = 0.3
HY_SLOW_PCT = 1.5
SSM_WIDTH = 512
SSM_HEADS = 8
SSM_HEADDIM = 64
SSM_GROUPS = 2
SSM_STATE = 128
SSM_CHUNK = 128
SSM_CONV_DIM = SSM_WIDTH + 2 * SSM_GROUPS * SSM_STATE
ATT_WIDTH = 512
ATT_HEADS = 8
ATT_KV_HEADS = 2
ATT_HEADDIM = 64
ATT_GROUP = ATT_HEADS // ATT_KV_HEADS
ATT_KV = ATT_KV_HEADS * ATT_HEADDIM
WINDOW = 128
ROPE_BASE = 10000.0
HY_IN = 4 * HY_WIDTH
SSM_IN = SSM_CONV_DIM + SSM_WIDTH + 2 * SSM_HEADS
DEEPNORM_ALPHA = (2 * DEPTH) ** 0.25
LN_EPS = 1e-6
RMS_EPS = 1e-5

LANES = 128
VMEM_LIMIT = 56 * 1024 * 1024
NEG = -1e30
BIG = 1e30
LOG2E = math.log2(math.e)
FILTER_ROWS = 256
PROJ_ROWS = 512
HY_STEP_ROWS = 8192


def _cp(*sem):
    return pltpu.CompilerParams(dimension_semantics=sem, vmem_limit_bytes=VMEM_LIMIT)


def _silu(x):
    return (0.5 * x) * (1.0 + jnp.tanh(0.5 * x))


def _dot(a, b, precision=None):
    return jnp.dot(a, b, preferred_element_type=F32, precision=precision)


def _dot_t(a, b):
    return lax.dot_general(a, b, (((1,), (1,)), ((), ())), preferred_element_type=F32)


def _mod_body(c_ref, w_ref, b_ref, o_ref):
    o_ref[...] = _dot(_silu(c_ref[...]), w_ref[...], HI) + b_ref[...]


def _modulation(cc, w, b):
    rows, d = cc.shape
    n = w.shape[1]
    tn = 1024
    return pl.pallas_call(
        _mod_body,
        grid=(n // tn,),
        in_specs=[pl.BlockSpec((rows, d), lambda j: (0, 0)),
                  pl.BlockSpec((d, tn), lambda j: (0, j)),
                  pl.BlockSpec((1, tn), lambda j: (0, j))],
        out_specs=pl.BlockSpec((rows, tn), lambda j: (0, j)),
        out_shape=jax.ShapeDtypeStruct((rows, n), F32),
        compiler_params=_cp("arbitrary"),
        name="modulation",
    )(cc, w, b.reshape(1, n))


IN_SEGS = (3 * HY_WIDTH, HY_WIDTH, SSM_CONV_DIM, SSM_WIDTH, ATT_WIDTH, 2 * ATT_KV, ATT_WIDTH, LANES)
SEG_Q, SEG_KV = 4, 5
IN_DTYPES = (BF16,) * (len(IN_SEGS) - 1) + (F32,)
IN_CHUNK = 512


def _pack_w_in(w):
    o_ss = HY_IN
    o_at = HY_IN + SSM_IN
    dt = w[:, o_ss + SSM_CONV_DIM + SSM_WIDTH:o_at]
    parts = [w[:, :HY_IN], w[:, o_ss:o_ss + SSM_CONV_DIM + SSM_WIDTH], w[:, o_at:],
             dt, jnp.zeros((w.shape[0], LANES - dt.shape[1]), w.dtype)]
    return jnp.concatenate(parts, axis=1).astype(BF16)


def _inproj_body(h_ref, sh_ref, sc_ref, w_ref, *rest, rope):
    tabs, o_refs = (rest[:2], rest[2:]) if rope else ((), rest)
    x = h_ref[0]
    mu = jnp.mean(x, -1, keepdims=True)
    xc = x - mu
    var = jnp.mean(xc * xc, -1, keepdims=True)
    xm = (xc * lax.rsqrt(var + LN_EPS) * (1.0 + sc_ref[0]) + sh_ref[0]).astype(BF16)
    off = 0
    for seg, (o_ref, n) in enumerate(zip(o_refs, IN_SEGS)):
        for j in range(0, n, IN_CHUNK):
            w = min(IN_CHUNK, n - j)
            r = _dot(xm, w_ref[:, off + j:off + j + w])
            if rope and seg == SEG_Q:
                r = _rope(r, tabs[0][...], tabs[1][...]) * (ATT_HEADDIM ** -0.5 * LOG2E)
            if rope and seg == SEG_KV:
                o_ref[0, :, :ATT_KV] = _rope(r[:, :ATT_KV], tabs[0][:, :ATT_KV], tabs[1][:, :ATT_KV]).astype(o_ref.dtype)
                o_ref[0, :, ATT_KV:] = r[:, ATT_KV:].astype(o_ref.dtype)
            else:
                o_ref[0, :, j:j + w] = r.astype(o_ref.dtype)
        off += n


def _in_projection(h, shift, scale, w_packed, rope_tables=None):
    b, L, d = h.shape
    tm = min(L, PROJ_ROWS)
    n_all = w_packed.shape[1]
    row = lambda bi, i: (bi, i, 0)
    vec = lambda bi, i: (bi, 0, 0)
    rope = rope_tables is not None
    tab_specs = [pl.BlockSpec((tm, ATT_WIDTH), lambda bi, i: (i, 0))] * 2 if rope else []
    return pl.pallas_call(
        functools.partial(_inproj_body, rope=rope),
        grid=(b, L // tm),
        in_specs=[pl.BlockSpec((1, tm, d), row), pl.BlockSpec((1, 1, d), vec), pl.BlockSpec((1, 1, d), vec),
                  pl.BlockSpec((d, n_all), lambda bi, i: (0, 0))] + tab_specs,
        out_specs=[pl.BlockSpec((1, tm, n), row) for n in IN_SEGS],
        out_shape=[jax.ShapeDtypeStruct((b, L, n), dt) for n, dt in zip(IN_SEGS, IN_DTYPES)],
        compiler_params=_cp("parallel", "arbitrary"),
        name="in_projection",
    )(h, shift, scale, w_packed, *(rope_tables or ()))


def _dwconv_body(u_ref, w_ref, b_ref, o_ref, *, act):
    x = u_ref[0].astype(F32)
    L = x.shape[0]
    row = lax.broadcasted_iota(jnp.int32, x.shape, 0)
    prev = jnp.where(row == 0, 0.0, pltpu.roll(x, 1, 0))
    nxt = jnp.where(row == L - 1, 0.0, pltpu.roll(x, L - 1, 0))
    y = prev * w_ref[0:1, :] + x * w_ref[1:2, :] + nxt * w_ref[2:3, :] + b_ref[...]
    if act:
        y = _silu(y)
    o_ref[0, 0] = y.astype(o_ref.dtype)


def _dwconv(u, w, bias, *, act, split):
    b, L, c = u.shape
    tc = 256
    per = split // tc
    return pl.pallas_call(
        functools.partial(_dwconv_body, act=act),
        grid=(b, c // tc),
        in_specs=[pl.BlockSpec((1, L, tc), lambda bi, j: (bi, 0, j)),
                  pl.BlockSpec((3, tc), lambda bi, j: (0, j)),
                  pl.BlockSpec((1, tc), lambda bi, j: (0, j))],
        out_specs=pl.BlockSpec((1, 1, L, tc), lambda bi, j: (j // per, bi, 0, j % per)),
        out_shape=jax.ShapeDtypeStruct((c // split, b, L, split), BF16),
        compiler_params=_cp("parallel", "arbitrary"),
        name="dwconv",
    )(u, w, bias.reshape(1, c))


def _filter_features(L):
    t = jnp.linspace(0.0, 1.0, L, dtype=F32)[:, None]
    w = 2.0 * math.pi * jnp.arange(L, dtype=F32)[:, None] / L
    f = jnp.linspace(1e-4, HY_BANDS - 1, HY_BANDS, dtype=F32)[None]
    z = jnp.concatenate([t, jnp.cos(f * w), -jnp.sin(f * w)], -1)
    return jnp.pad(z, ((0, 0), (0, LANES - HY_EMB)))


def _pad_to(a, rows, cols):
    return jnp.pad(a, ((0, rows - a.shape[0]), (0, cols - a.shape[1])))


def _filter_body(z_ref, w1_ref, w2_ref, w3_ref, b_ref, fr_ref, wo_ref, ad_ref, o_ref, *, nblk):
    i = pl.program_id(0)

    @pl.when(i < nblk)
    def _():
        z = z_ref[...]
        fr = fr_ref[...]
        h = jnp.sin(fr * (_dot(z, w1_ref[...], HI) + b_ref[0:1, :]))
        h = jnp.sin(fr * (_dot(h, w2_ref[...], HI) + b_ref[1:2, :]))
        h = jnp.sin(fr * (_dot(h, w3_ref[...], HI) + b_ref[2:3, :]))
        win = jnp.exp(-z[:, 0:1] * ad_ref[...])
        h_hi = h.astype(BF16)
        h_lo = (h - h_hi.astype(F32)).astype(BF16)
        for j in range(4):
            cols = slice(j * HY_WIDTH, (j + 1) * HY_WIDTH)
            taps = _dot(h_hi, wo_ref[0, :, cols]) + _dot(h_lo, wo_ref[0, :, cols]) + _dot(h_hi, wo_ref[1, :, cols])
            o_ref[:, cols] = taps * win

    @pl.when(i == nblk)
    def _():
        o_ref[...] = jnp.zeros_like(o_ref)


def _hyena_filter_taps(L, w1, b1, w2, b2, w3, b3, freq, w_out):
    z = _filter_features(L)
    hp = LANES
    bias = jnp.stack([jnp.pad(b, (0, hp - b.shape[0])) for b in (b1, b2, b3)])
    bias = jnp.pad(bias, ((0, 5), (0, 0)))
    fr = jnp.pad(freq, (0, hp - freq.shape[0])).reshape(1, hp)
    max_decay = math.log(HY_DECAY_TARGET) / HY_FAST_PCT
    min_decay = math.log(HY_DECAY_TARGET) / HY_SLOW_PCT
    absd = jnp.abs(jnp.linspace(min_decay, max_decay, HY_WIDTH, dtype=F32)).reshape(1, HY_WIDTH)
    tl = FILTER_ROWS
    nblk = L // tl
    n = 4 * HY_WIDTH
    full = lambda i: (0, 0)
    return pl.pallas_call(
        functools.partial(_filter_body, nblk=nblk),
        grid=(nblk + 1,),
        in_specs=[pl.BlockSpec((tl, hp), lambda i: (jnp.minimum(i, nblk - 1), 0)),
                  pl.BlockSpec((hp, hp), full), pl.BlockSpec((hp, hp), full), pl.BlockSpec((hp, hp), full),
                  pl.BlockSpec((8, hp), full), pl.BlockSpec((1, hp), full),
                  pl.BlockSpec((2, hp, n), lambda i: (0, 0, 0)), pl.BlockSpec((1, HY_WIDTH), full)],
        out_specs=pl.BlockSpec((tl, n), lambda i: (i, 0)),
        out_shape=jax.ShapeDtypeStruct((L + tl, n), F32),
        compiler_params=_cp("arbitrary"),
        name="hyena_filter",
    )(z, _pad_to(w1, hp, hp), _pad_to(w2, hp, hp), _pad_to(w3, hp, hp), bias, fr, _split2(_pad_to(w_out, hp, n)),
      absd)


def _hy_cfg(L):
    n2 = 128 if L >= 2048 else 16
    n1 = 2 * L // n2
    k1n = n1 // 2 + 1
    jp = -(-2 * k1n // 16) * 16
    pitch = n2 + 8
    return dict(L=L, n2=n2, n1=n1, nh=n1 // 2, k1n=k1n, jp=jp, pitch=pitch)


@functools.lru_cache(maxsize=None)
def _dft_tables(L):
    cfg = _hy_cfg(L)
    n, n1, n2, nh, k1n, jp = 2 * L, cfg["n1"], cfg["n2"], cfg["nh"], cfg["k1n"], cfg["jp"]
    a_n1 = np.arange(n1)
    a_k1 = np.arange(k1n)
    th = 2 * np.pi * np.outer(a_k1, a_n1) / n1
    f1 = np.zeros((jp, n1))
    f1[0:2 * k1n:2] = np.cos(th)
    f1[1:2 * k1n:2] = -np.sin(th)
    a_n2 = np.arange(n2)
    m1 = np.zeros((k1n, 2 * n2, 2 * n2))
    for k in range(k1n):
        f = np.exp(-2j * np.pi * (np.outer(a_n2, a_n2) / n2 + a_n2[None, :] * k / n))
        m1[k] = np.block([[f.real, -f.imag], [f.imag, f.real]])
    ck = np.full(k1n, 2.0)
    ck[0] = 1.0
    ck[-1] = 1.0
    th6 = 2 * np.pi * np.outer(np.arange(nh), a_k1) / n1
    f6 = np.zeros((nh, jp))
    f6[:, 0:2 * k1n:2] = ck * np.cos(th6) / n
    f6[:, 1:2 * k1n:2] = -ck * np.sin(th6) / n
    as32 = lambda a: np.asarray(a, np.float32)
    return dict(f1=as32(f1), m1=as32(m1), f6=as32(f6))


def _first_stage(src_ref, a_scr, f1, cfg):
    n2, nh, jp, pitch = cfg["n2"], cfg["nh"], cfg["jp"], cfg["pitch"]
    bt = src_ref.shape[0]

    def step(i, carry):
        xs = jnp.concatenate([src_ref[t, pl.ds(2 * i + u, nh, stride=pitch), :] for t in range(bt) for u in range(2)], 1)
        r = _dot(f1, xs.astype(BF16))
        for t in range(bt):
            for u in range(2):
                lo = (2 * t + u) * LANES
                a_scr[t, pl.ds(2 * i + u, jp, stride=pitch), :] = r[:, lo:lo + LANES]
        return carry

    lax.fori_loop(0, n2 // 2, step, 0, unroll=min(16 // bt, n2 // 2))


def _k1_rows(a_scr, k, cfg):
    n2, pitch = cfg["n2"], cfg["pitch"]
    base = pl.multiple_of(2 * k * pitch, 8)
    parts = [jnp.concatenate([a_scr[t, pl.ds(base, n2), :], a_scr[t, pl.ds(base + pitch, n2), :]], 0)
             for t in range(a_scr.shape[0])]
    return base, jnp.concatenate(parts, 1)


def _split2(table):
    t = jnp.asarray(table)
    hi = t.astype(BF16)
    return jnp.stack([hi, (t - hi.astype(F32)).astype(BF16)])


def _spectrum_body(kf_ref, kb_ref, f1_ref, m1_ref, o_ref, a_scr, *, cfg):
    n2, nh, jp, pitch = cfg["n2"], cfg["nh"], cfg["jp"], cfg["pitch"]

    def dot3(m_hi, m_lo, x):
        x_hi = x.astype(BF16)
        x_lo = (x - x_hi.astype(F32)).astype(BF16)
        return _dot(m_hi, x_hi) + _dot(m_lo, x_hi) + _dot(m_hi, x_lo)

    def step(i, carry):
        cols = []
        for t in range(2):
            n = 2 * i + t
            cols.append(jnp.concatenate([kf_ref[pl.ds(n, nh, stride=n2), :],
                                         kb_ref[pl.ds(n2 - n, nh, stride=n2), :]], 0))
        r = dot3(f1_ref[0], f1_ref[1], jnp.concatenate(cols, 1))
        a_scr[0, pl.ds(2 * i, jp, stride=pitch), :] = r[:, :LANES]
        a_scr[0, pl.ds(2 * i + 1, jp, stride=pitch), :] = r[:, LANES:]
        return carry

    lax.fori_loop(0, n2 // 2, step, 0, unroll=min(16, n2 // 2))
    lag0 = pl.ds(0, jp, stride=pitch)
    a_scr[0, lag0, :] = a_scr[0, lag0, :] + f1_ref[0, :, 0:1].astype(F32) * kb_ref[0:1, :]

    def mid(k, carry):
        _, a = _k1_rows(a_scr, k, cfg)
        o_ref[0, k] = dot3(m1_ref[0, k], m1_ref[1, k], a).astype(BF16)
        return carry

    lax.fori_loop(0, cfg["k1n"], mid, 0, unroll=3)


def _hyena_spectrum(L, taps):
    c = HY_WIDTH
    cfg = _hy_cfg(L)
    tb = _dft_tables(L)
    n1, n2, nh, k1n = cfg["n1"], cfg["n2"], cfg["nh"], cfg["k1n"]
    f1 = _split2(np.concatenate([tb["f1"][:, :nh], tb["f1"][:, n1 - 1:nh - 1:-1]], 1))
    m1 = _split2(tb["m1"])
    nct = c // LANES
    rows = taps.shape[0]
    return pl.pallas_call(
        functools.partial(_spectrum_body, cfg=cfg),
        grid=(2, nct),
        in_specs=[pl.BlockSpec((rows, LANES), lambda cv, j: (0, 2 * cv * nct + j)),
                  pl.BlockSpec((rows, LANES), lambda cv, j: (0, (2 * cv + 1) * nct + j)),
                  pl.BlockSpec(f1.shape, lambda cv, j: (0, 0, 0)),
                  pl.BlockSpec(m1.shape, lambda cv, j: (0, 0, 0, 0))],
        out_specs=pl.BlockSpec((1, k1n, 2 * n2, LANES), lambda cv, j: (cv, 0, 0, j)),
        out_shape=jax.ShapeDtypeStruct((2, k1n, 2 * n2, c), BF16),
        scratch_shapes=[pltpu.VMEM((1, cfg["jp"] * cfg["pitch"], LANES), F32)],
        compiler_params=_cp("arbitrary", "arbitrary"),
        name="hyena_spectrum",
    )(taps, taps, f1, m1)


def _short_conv(u_ref, w_ref, b_ref, which):
    x = u_ref[...].astype(F32)
    L = x.shape[0]
    edge = lax.broadcasted_iota(jnp.int32, (8, x.shape[1]), 0)
    prev = pltpu.roll(x, 1, 0)
    prev = jnp.concatenate([jnp.where(edge == 0, 0.0, prev[:8]), prev[8:]], 0)
    nxt = pltpu.roll(x, L - 1, 0)
    nxt = jnp.concatenate([nxt[:L - 8], jnp.where(edge == 7, 0.0, nxt[L - 8:])], 0)
    sel = slice(which, which + 1)
    return prev * w_ref[0, sel, :] + x * w_ref[1, sel, :] + nxt * w_ref[2, sel, :] + b_ref[sel, :]


def _long_conv(src_scr, a_scr, y_scr, w_scr, f1_ref, f6_ref, m1_ref, kf_ref, conv, cfg):
    n2, nh, jp, pitch = cfg["n2"], cfg["nh"], cfg["jp"], cfg["pitch"]
    bt = src_scr.shape[0]
    _first_stage(src_scr, a_scr, f1_ref[...], cfg)

    def forward(k, carry):
        _, a = _k1_rows(a_scr, k, cfg)
        x = _dot(m1_ref[k], a.astype(BF16))
        kk = kf_ref[conv, k].astype(F32)
        kr, ki = kk[:n2], kk[n2:]
        cols = []
        for t in range(bt):
            xr, xi = x[:n2, t * LANES:(t + 1) * LANES], x[n2:, t * LANES:(t + 1) * LANES]
            cols.append(jnp.concatenate([xr * kr - xi * ki, xr * ki + xi * kr], 0))
        w_scr[k] = jnp.concatenate(cols, 1).astype(BF16)
        return carry

    def inverse(k, carry):
        base = pl.multiple_of(2 * k * pitch, 8)
        b = lax.dot_general(m1_ref[k], w_scr[k], (((0,), (0,)), ((), ())), preferred_element_type=F32)
        for t in range(bt):
            a_scr[t, pl.ds(base, n2), :] = b[:n2, t * LANES:(t + 1) * LANES]
            a_scr[t, pl.ds(base + pitch, n2), :] = b[n2:, t * LANES:(t + 1) * LANES]
        return carry

    lax.fori_loop(0, cfg["k1n"], forward, 0, unroll=3)
    lax.fori_loop(0, cfg["k1n"], inverse, 0, unroll=3)
    f6 = f6_ref[...]

    def last(i, carry):
        bs = jnp.concatenate([a_scr[t, pl.ds(2 * i + u, jp, stride=pitch), :] for t in range(bt) for u in range(2)], 1)
        y = _dot(f6, bs.astype(BF16))
        for t in range(bt):
            for u in range(2):
                lo = (2 * t + u) * LANES
                y_scr[t, pl.ds(2 * i + u, nh, stride=pitch), :] = y[:, lo:lo + LANES]
        return carry

    lax.fori_loop(0, n2 // 2, last, 0, unroll=min(16 // bt, n2 // 2))


def _hyena_body(v_ref, x1_ref, x2_ref, g_ref, cw_ref, cb_ref, hb_ref, f1_ref, f6_ref, m1_ref, kf_ref,
                o_ref, s_scr, a_scr, y_scr, w_scr, *, cfg):
    tabs = (w_scr, f1_ref, f6_ref, m1_ref, kf_ref)
    n2, nh, pitch = cfg["n2"], cfg["nh"], cfg["pitch"]
    slots = range(s_scr.shape[0])

    def put(scr, t, val):
        for n1 in range(nh):
            scr[t, n1 * pitch:n1 * pitch + n2, :] = val[n1 * n2:(n1 + 1) * n2]

    def get(scr, t):
        return jnp.concatenate([scr[t, n1 * pitch:n1 * pitch + n2, :] for n1 in range(nh)], 0)

    for t in slots:
        put(s_scr, t, _short_conv(v_ref.at[t], cw_ref, cb_ref, 0))
    _long_conv(s_scr, a_scr, y_scr, *tabs, 0, cfg)
    for t in slots:
        put(s_scr, t, _short_conv(x1_ref.at[t], cw_ref, cb_ref, 1) * (get(y_scr, t) + get(s_scr, t) * hb_ref[0:1, :]))
    _long_conv(s_scr, a_scr, y_scr, *tabs, 1, cfg)
    for t in slots:
        y = _short_conv(x2_ref.at[t], cw_ref, cb_ref, 2) * (get(y_scr, t) + get(s_scr, t) * hb_ref[1:2, :])
        o_ref[t] = (y * _silu(g_ref[t].astype(F32))).astype(o_ref.dtype)


def _hyena(u_hy3, u_gate, conv_w, conv_b, kf, hy_bias):
    b, L, _ = u_gate.shape
    c = HY_WIDTH
    cfg = _hy_cfg(L)
    tb = _dft_tables(L)
    nct = c // LANES
    bt = max(t for t in (1, 2, 4, 8) if b % t == 0 and t * L <= max(L, HY_STEP_ROWS))
    col = lambda off: pl.BlockSpec((bt, L, LANES), lambda j, bi: (bi, 0, off * nct + j))
    full = lambda a: pl.BlockSpec(a.shape, lambda j, bi: (0,) * a.ndim)
    f1 = jnp.asarray(tb["f1"][:, :cfg["nh"]]).astype(BF16)
    f6 = jnp.asarray(tb["f6"]).astype(BF16)
    m1 = jnp.asarray(tb["m1"]).astype(BF16)
    return pl.pallas_call(
        functools.partial(_hyena_body, cfg=cfg),
        grid=(nct, b // bt),
        in_specs=[col(0), col(1), col(2), col(0),
                  pl.BlockSpec((3, 3, LANES), lambda j, bi: (0, 0, j)),
                  pl.BlockSpec((3, LANES), lambda j, bi: (0, j)),
                  pl.BlockSpec((2, LANES), lambda j, bi: (0, j)),
                  full(f1), full(f6), full(m1),
                  pl.BlockSpec((2, cfg["k1n"], 2 * cfg["n2"], LANES), lambda j, bi: (0, 0, 0, j),
                               pipeline_mode=pl.Buffered(1))],
        out_specs=col(0),
        out_shape=jax.ShapeDtypeStruct((b, L, c), BF16),
        scratch_shapes=[pltpu.VMEM((bt, cfg["nh"] * cfg["pitch"], LANES), F32),
                        pltpu.VMEM((bt, cfg["jp"] * cfg["pitch"], LANES), F32),
                        pltpu.VMEM((bt, cfg["nh"] * cfg["pitch"], LANES), F32),
                        pltpu.VMEM((cfg["k1n"], 2 * cfg["n2"], bt * LANES), BF16)],
        compiler_params=_cp("arbitrary", "arbitrary"),
        name="hyena",
    )(u_hy3, u_hy3, u_hy3, u_gate, conv_w.reshape(3, 3, c), conv_b.reshape(3, c), hy_bias, f1, f6, m1, kf)


SSD_CPS = 4
SSD_SPS = 2
SPLIT_STRIDE = 2 * SSM_HEADS


def _pack3(x):
    hi = x.astype(BF16).astype(F32)
    r1 = x - hi
    mid = r1.astype(BF16).astype(F32)
    lo = (r1 - mid).astype(BF16).astype(F32)
    return (hi + pltpu.roll(mid, SPLIT_STRIDE, 1) + pltpu.roll(lo, 2 * SPLIT_STRIDE, 1)).astype(BF16)


def _unpack3(x3, used):
    return jnp.where(used, x3 + pltpu.roll(x3, LANES - SPLIT_STRIDE, 1) + pltpu.roll(x3, LANES - 2 * SPLIT_STRIDE, 1), 0.0)


@functools.lru_cache(maxsize=None)
def _ssd_spread_tables():
    col = np.zeros((LANES, 2 * SSM_HEADS * LANES), np.float32)
    head = np.zeros((LANES, 2 * SSM_WIDTH), np.float32)
    for c in range(2 * SSM_HEADS):
        d, h = divmod(c, SSM_HEADS)
        for piece in range(3):
            col[c + piece * SPLIT_STRIDE, c * LANES:(c + 1) * LANES] = 1.0
            lo = d * SSM_WIDTH + h * SSM_HEADDIM
            head[c + piece * SPLIT_STRIDE, lo:lo + SSM_HEADDIM] = 1.0
    return col, head


def _ssd_chunk_body(xbc_ref, dtr_ref, dtb_ref, a_ref, ecol_ref, ehead_ref, yd_ref, cs_ref, ex_ref, et_ref):
    q = SSM_CHUNK
    hpg = SSM_HEADS // SSM_GROUPS
    gw = SSM_WIDTH // SSM_GROUPS
    li = lax.broadcasted_iota(jnp.int32, (q, q), 0)
    si = lax.broadcasted_iota(jnp.int32, (q, q), 1)
    below = li > si
    diag = li == si
    fwd_lane = si < SSM_HEADS
    used = si < 2 * SSM_HEADS
    tril = (li >= si).astype(BF16)
    triu = (li <= si).astype(BF16)
    chunks = range(cs_ref.shape[1])
    rows = [slice(c * q, (c + 1) * q) for c in chunks]
    ehead = ehead_ref[...]

    raws = [dtr_ref[0, r, :] + dtb_ref[...] for r in rows]
    dts = [jnp.where(used, jnp.maximum(x, 0.0) + jnp.log1p(jnp.exp(-jnp.abs(x))), 0.0) for x in raws]
    da3 = [_pack3(dt * a_ref[...]) for dt in dts]
    acs = [jnp.where(fwd_lane, _unpack3(_dot(tril, x), used), _unpack3(_dot(triu, x), used)) for x in da3]
    tots = [jnp.where(fwd_lane[0:1], a[q - 1:q, :], a[0:1, :]) for a in acs]
    ws = [dt * jnp.exp(t - a) for dt, t, a in zip(dts, tots, acs)]
    acs3 = [_pack3(a) for a in acs]
    colb = [_dot(x, ecol_ref[...]) for x in acs3]
    wx = [_dot(_pack3(w), ehead) for w in ws]
    for c in chunks:
        ex_ref[0, rows[c], :] = jnp.exp(_dot(acs3[c], ehead)).astype(BF16)
        et_ref[0, c] = jnp.exp(_dot(_pack3(jnp.broadcast_to(tots[c], (8, LANES))), ehead))
    rowt = [(a - jnp.where(dt > 0.0, jnp.log(dt), -BIG)).T for a, dt in zip(acs, dts)]
    dsum = [(dt + pltpu.roll(dt, LANES - SSM_HEADS, 1)).T for dt in dts]
    xbc = [xbc_ref[0, r, :] for r in rows]
    xsb = [x[:, :SSM_WIDTH].astype(BF16) for x in xbc]
    bgs = [[x[:, SSM_WIDTH + g * SSM_STATE:SSM_WIDTH + (g + 1) * SSM_STATE] for g in range(SSM_GROUPS)] for x in xbc]
    cgs = [[x[:, SSM_WIDTH + (SSM_GROUPS + g) * SSM_STATE:SSM_WIDTH + (SSM_GROUPS + g + 1) * SSM_STATE]
            for g in range(SSM_GROUPS)] for x in xbc]
    gmat = [[_dot_t(cgs[c][g].astype(BF16), bgs[c][g].astype(BF16)) for g in range(SSM_GROUPS)] for c in chunks]
    mats = []
    for c in chunks:
        for h in range(SSM_HEADS):
            hb = SSM_HEADS + h
            arg = jnp.where(below, colb[c][:, h * q:(h + 1) * q] - rowt[c][h:h + 1, :],
                            colb[c][:, hb * q:(hb + 1) * q] - rowt[c][hb:hb + 1, :])
            dec = jnp.where(diag, dsum[c][h:h + 1, :], jnp.exp(arg))
            mats.append((gmat[c][h // hpg] * dec).astype(BF16))
    for c in chunks:
        for h in range(SSM_HEADS):
            lo = h * SSM_HEADDIM
            yd_ref[0, rows[c], lo:lo + SSM_HEADDIM] = _dot(mats[c * SSM_HEADS + h],
                                                             xsb[c][:, lo:lo + SSM_HEADDIM]).astype(yd_ref.dtype)
    for c in chunks:
        for g in range(SSM_GROUPS):
            bgt = bgs[c][g].astype(F32).T.astype(BF16)
            xg = xbc[c][:, g * gw:(g + 1) * gw].astype(F32)
            for d in range(2):
                lo = d * SSM_WIDTH + g * gw
                cs_ref[0, c, d, g] = _dot(bgt, (xg * wx[c][:, lo:lo + gw]).astype(BF16)).astype(cs_ref.dtype)


def _ssd_chunks(xbc, dt_raw, dt_bias_row, a_row):
    b, L, _ = xbc.shape
    nc = L // SSM_CHUNK
    cps = math.gcd(nc, SSD_CPS)
    rows = cps * SSM_CHUNK
    gw = SSM_WIDTH // SSM_GROUPS
    blk = lambda bi, i: (bi, i, 0)
    full2 = lambda bi, i: (0, 0)
    ecol, ehead = (jnp.asarray(t).astype(BF16) for t in _ssd_spread_tables())
    return pl.pallas_call(
        _ssd_chunk_body,
        grid=(b, nc // cps),
        in_specs=[pl.BlockSpec((1, rows, SSM_CONV_DIM), blk), pl.BlockSpec((1, rows, LANES), blk),
                  pl.BlockSpec((1, LANES), full2), pl.BlockSpec((1, LANES), full2),
                  pl.BlockSpec(ecol.shape, full2), pl.BlockSpec(ehead.shape, full2)],
        out_specs=[pl.BlockSpec((1, rows, SSM_WIDTH), blk),
                   pl.BlockSpec((1, cps, 2, SSM_GROUPS, SSM_STATE, gw), lambda bi, i: (bi, i, 0, 0, 0, 0)),
                   pl.BlockSpec((1, rows, 2 * SSM_WIDTH), blk),
                   pl.BlockSpec((1, cps, 8, 2 * SSM_WIDTH), lambda bi, i: (bi, i, 0, 0))],
        out_shape=[jax.ShapeDtypeStruct((b, L, SSM_WIDTH), BF16),
                   jax.ShapeDtypeStruct((b, nc, 2, SSM_GROUPS, SSM_STATE, gw), BF16),
                   jax.ShapeDtypeStruct((b, L, 2 * SSM_WIDTH), BF16),
                   jax.ShapeDtypeStruct((b, nc, 8, 2 * SSM_WIDTH), F32)],
        compiler_params=_cp("parallel", "arbitrary"),
        name="ssd_chunks",
    )(xbc, dt_raw, dt_bias_row, a_row, ecol, ehead)


def _ssd_state_body(cf_ref, cb_ref, xf_ref, xb_ref, ef_ref, eb_ref, sf_ref, sb_ref, init_ref, yf_ref, yb_ref, fin_ref,
                    st_ref, *, nsteps):
    ci = pl.program_id(1)

    @pl.when(ci == 0)
    def _():
        st_ref[...] = init_ref[0]

    q = SSM_CHUNK
    gw = SSM_WIDTH // SSM_GROUPS
    for s in range(SSD_SPS):
        dirs = ((cf_ref, xf_ref, ef_ref, sf_ref, yf_ref, s), (cb_ref, xb_ref, eb_ref, sb_ref, yb_ref, SSD_SPS - 1 - s))
        for d, (c_ref, x_ref, e_ref, s_ref, y_ref, j) in enumerate(dirs):
            rows = slice(j * q, (j + 1) * q)
            cmat = c_ref[0, rows, :].astype(BF16)
            for g in range(SSM_GROUPS):
                cols = slice(g * gw, (g + 1) * gw)
                st = st_ref[d, g]
                y_ref[0, rows, cols] = (_dot(cmat[:, g * SSM_STATE:(g + 1) * SSM_STATE], st.astype(BF16))
                                        * x_ref[0, rows, cols].astype(F32)).astype(y_ref.dtype)
                st_ref[d, g] = st * e_ref[0, j, 0:1, cols] + s_ref[0, j, 0, g].astype(F32)

    @pl.when(ci == nsteps - 1)
    def _():
        fin_ref[0] = st_ref[...]


def _ssd_states(xbc, ex, et, cs, init):
    b, L, _ = xbc.shape
    nc = L // SSM_CHUNK
    gw = SSM_WIDTH // SSM_GROUPS
    c_col = SSM_CONV_DIM // (SSM_GROUPS * SSM_STATE) - 1
    st_shape = (2, SSM_GROUPS, SSM_STATE, gw)
    st_spec = pl.BlockSpec((1,) + st_shape, lambda bi, c: (bi, 0, 0, 0, 0))
    nsteps = nc // SSD_SPS
    rows = SSD_SPS * SSM_CHUNK
    cs_blk = (1, SSD_SPS, 1, SSM_GROUPS, SSM_STATE, gw)
    fwd = lambda *tail: (lambda bi, c: (bi, c) + tail)
    bwd = lambda *tail: (lambda bi, c: (bi, nsteps - 1 - c) + tail)
    return pl.pallas_call(
        functools.partial(_ssd_state_body, nsteps=nsteps),
        grid=(b, nsteps),
        in_specs=[pl.BlockSpec((1, rows, SSM_GROUPS * SSM_STATE), fwd(c_col)),
                  pl.BlockSpec((1, rows, SSM_GROUPS * SSM_STATE), bwd(c_col)),
                  pl.BlockSpec((1, rows, SSM_WIDTH), fwd(0)),
                  pl.BlockSpec((1, rows, SSM_WIDTH), bwd(1)),
                  pl.BlockSpec((1, SSD_SPS, 8, SSM_WIDTH), fwd(0, 0)),
                  pl.BlockSpec((1, SSD_SPS, 8, SSM_WIDTH), bwd(0, 1)),
                  pl.BlockSpec(cs_blk, fwd(0, 0, 0, 0)),
                  pl.BlockSpec(cs_blk, bwd(1, 0, 0, 0)),
                  st_spec],
        out_specs=[pl.BlockSpec((1, rows, SSM_WIDTH), fwd(0)),
                   pl.BlockSpec((1, rows, SSM_WIDTH), bwd(0)),
                   st_spec],
        out_shape=[jax.ShapeDtypeStruct((b, L, SSM_WIDTH), BF16), jax.ShapeDtypeStruct((b, L, SSM_WIDTH), BF16),
                   jax.ShapeDtypeStruct((b,) + st_shape, F32)],
        scratch_shapes=[pltpu.VMEM(st_shape, F32)],
        compiler_params=_cp("parallel", "arbitrary"),
        name="ssd_states",
    )(xbc, xbc, ex, ex, et, et, cs, cs, init)


def _ssd(xbc, dt_raw, dt_bias_row, a_row, init):
    y_diag, cs, ex, et = _ssd_chunks(xbc, dt_raw, dt_bias_row, a_row)
    y_f, y_b, fin = _ssd_states(xbc, ex, et, cs, init)
    return (y_diag, y_f, y_b), fin


def _rope_tables(L):
    rows = L // GRID_W
    row = jnp.broadcast_to(jnp.arange(rows)[:, None], (rows, GRID_W)).reshape(L)
    col = jnp.broadcast_to(jnp.arange(GRID_W)[None, :], (rows, GRID_W)).reshape(L)
    nf = ATT_HEADDIM // 4
    inv = ROPE_BASE ** (-jnp.arange(nf, dtype=F32) / nf)
    ar = row.astype(F32)[:, None] * inv
    ac = col.astype(F32)[:, None] * inv
    cos = jnp.concatenate([jnp.cos(ar), jnp.cos(ar), jnp.cos(ac), jnp.cos(ac)], -1)
    sin = jnp.concatenate([-jnp.sin(ar), jnp.sin(ar), -jnp.sin(ac), jnp.sin(ac)], -1)
    return jnp.tile(cos, (1, ATT_HEADS)), jnp.tile(sin, (1, ATT_HEADS))


def _rope(x, cos, sin):
    w = x.shape[-1]
    quarter = ATT_HEADDIM // 4
    lane = lax.broadcasted_iota(jnp.int32, x.shape, x.ndim - 1)
    partner = jnp.where((lane // quarter) % 2 == 0, pltpu.roll(x, w - quarter, x.ndim - 1),
                        pltpu.roll(x, quarter, x.ndim - 1))
    return x * cos + partner * sin


def _wattn_body(sink_ref, bias_ref, q_ref, kp_ref, kc_ref, kn_ref, vp_ref, vc_ref, vn_ref, kx_ref, vx_ref, z_ref,
                o_ref):
    hd = ATT_HEADDIM
    low_half = lax.broadcasted_iota(jnp.int32, (WINDOW, 2 * hd), 1) < hd
    q = q_ref[0]
    bias = bias_ref[0]
    z = z_ref[0].astype(F32)
    k_all, v_ext = [], []
    for g in range(ATT_KV_HEADS):
        ks = slice(g * hd, (g + 1) * hd)
        k_all.append(jnp.concatenate([r[0, :, ks] for r in (kp_ref, kc_ref, kn_ref, kx_ref)], 0).astype(BF16))
        v_all = jnp.concatenate([r[0, :, ks] for r in (vp_ref, vc_ref, vn_ref, vx_ref)], 0).astype(BF16)
        ones = jnp.ones_like(v_all)
        v_ext.append((jnp.concatenate([v_all, ones], 1), jnp.concatenate([ones, v_all], 1)))
    heads = range(ATT_HEADS)
    sinks = [sink_ref[h] * LOG2E for h in heads]
    scores = [_dot_t(q[:, h * hd:(h + 1) * hd].astype(BF16), k_all[h // ATT_GROUP]) + bias for h in heads]
    maxes = [jnp.maximum(jnp.max(s, -1, keepdims=True), sk) for s, sk in zip(scores, sinks)]
    probs = [jnp.exp2(s - m).astype(BF16) for s, m in zip(scores, maxes)]
    exts = [_dot(p, v_ext[h // ATT_GROUP][h % 2]) for h, p in zip(heads, probs)]
    outs = [e / (pltpu.roll(e, hd, 1) + jnp.exp2(sk - m)) for e, sk, m in zip(exts, sinks, maxes)]
    for pair in range(ATT_HEADS // 2):
        cs = slice(2 * pair * hd, (2 * pair + 2) * hd)
        o_ref[0, :, cs] = (jnp.where(low_half, outs[2 * pair], outs[2 * pair + 1]) * _silu(z[:, cs])).astype(o_ref.dtype)


def _window_attention(q_rot, u_kv, uc_kv, sinks, z_a):
    b, L, _ = q_rot.shape
    lc = uc_kv.shape[1]
    nb = L // WINDOW
    hd2 = ATT_KV
    cur = lambda bi, i: (bi, i, 0)
    prv = lambda bi, i: (bi, jnp.maximum(i - 1, 0), 0)
    nxt = lambda bi, i: (bi, jnp.minimum(i + 1, nb - 1), 0)
    vcur = lambda bi, i: (bi, i, 1)
    vprv = lambda bi, i: (bi, jnp.maximum(i - 1, 0), 1)
    vnxt = lambda bi, i: (bi, jnp.minimum(i + 1, nb - 1), 1)
    kblk = lambda f: pl.BlockSpec((1, WINDOW, hd2), f)
    nk = 3 * WINDOW + lc
    row = np.arange(WINDOW)[:, None]
    col = np.arange(nk)[None, :]
    in_prev = (col < WINDOW) & (col >= row)
    in_next = (col >= 2 * WINDOW) & (col < 3 * WINDOW) & (col - 2 * WINDOW <= row)
    always = ((col >= WINDOW) & (col < 2 * WINDOW)) | (col >= 3 * WINDOW)
    kinds = [always | in_next, always | in_prev | in_next, always | in_prev]
    if nb == 1:
        kinds = [always] * 3
    bias = jnp.asarray(np.where(np.stack(kinds), 0.0, NEG).astype(np.float32))
    kind = lambda bi, i: (jnp.where(i == 0, 0, jnp.where(i == nb - 1, 2, 1)), 0, 0)
    return pl.pallas_call(
        _wattn_body,
        grid=(b, nb),
        in_specs=[pl.BlockSpec(memory_space=pltpu.SMEM),
                  pl.BlockSpec((1, WINDOW, nk), kind),
                  pl.BlockSpec((1, WINDOW, ATT_WIDTH), cur),
                  kblk(prv), kblk(cur), kblk(nxt), kblk(vprv), kblk(vcur), kblk(vnxt),
                  pl.BlockSpec((1, lc, hd2), lambda bi, i: (bi, 0, 0)),
                  pl.BlockSpec((1, lc, hd2), lambda bi, i: (bi, 0, 1)),
                  pl.BlockSpec((1, WINDOW, ATT_WIDTH), cur)],
        out_specs=pl.BlockSpec((1, WINDOW, ATT_WIDTH), cur),
        out_shape=jax.ShapeDtypeStruct((b, L, ATT_WIDTH), BF16),
        compiler_params=_cp("parallel", "arbitrary"),
        name="window_attention",
    )(sinks, bias, q_rot, u_kv, u_kv, u_kv, u_kv, u_kv, u_kv, uc_kv, uc_kv, z_a)


def _cattn_body(sink_ref, q_ref, k_ref, v_ref, z_ref, o_ref):
    scale = ATT_HEADDIM ** -0.5
    q = q_ref[0]
    z = z_ref[0].astype(F32)
    for g in range(ATT_KV_HEADS):
        ks = slice(g * ATT_HEADDIM, (g + 1) * ATT_HEADDIM)
        k = k_ref[0, :, ks].astype(BF16)
        v = v_ref[0, :, ks].astype(BF16)
        for j in range(ATT_GROUP):
            h = g * ATT_GROUP + j
            hs = slice(h * ATT_HEADDIM, (h + 1) * ATT_HEADDIM)
            s = _dot_t(q[:, hs].astype(BF16), k) * scale
            sink = sink_ref[h]
            m = jnp.maximum(jnp.max(s, -1, keepdims=True), sink)
            p = jnp.exp(s - m)
            den = jnp.sum(p, -1, keepdims=True) + jnp.exp(sink - m)
            o_ref[0, :, hs] = (_dot(p.astype(BF16), v) / den * _silu(z[:, hs])).astype(o_ref.dtype)


def _ctx_attention(uc_q, uc_kv, sinks, z_ac):
    b, lc, _ = uc_q.shape
    blk = lambda bi: (bi, 0, 0)
    return pl.pallas_call(
        _cattn_body,
        grid=(b,),
        in_specs=[pl.BlockSpec(memory_space=pltpu.SMEM),
                  pl.BlockSpec((1, lc, ATT_WIDTH), blk),
                  pl.BlockSpec((1, lc, ATT_KV), lambda bi: (bi, 0, 0)),
                  pl.BlockSpec((1, lc, ATT_KV), lambda bi: (bi, 0, 1)),
                  pl.BlockSpec((1, lc, ATT_WIDTH), blk)],
        out_specs=pl.BlockSpec((1, lc, ATT_WIDTH), blk),
        out_shape=jax.ShapeDtypeStruct((b, lc, ATT_WIDTH), BF16),
        compiler_params=_cp("parallel"),
        name="ctx_attention",
    )(sinks, uc_q, uc_kv, uc_kv, z_ac)


def _out_body(h_ref, g_ref, yhy_ref, yd_ref, yf_ref, yb_ref, xs_ref, zs_ref, yat_ref, dsk_ref, nw_ref, w_ref,
              lg_ref, lb_ref, o_ref):
    gw = SSM_WIDTH // SSM_GROUPS
    y_scan = yd_ref[0].astype(F32) + yf_ref[0].astype(F32) + yb_ref[0].astype(F32)
    ys = (y_scan + xs_ref[0].astype(F32) * dsk_ref[...]) * _silu(zs_ref[0].astype(F32))
    parts = [yhy_ref[0].astype(BF16)]
    for g in range(SSM_GROUPS):
        seg = ys[:, g * gw:(g + 1) * gw]
        seg = seg * lax.rsqrt(jnp.mean(seg * seg, -1, keepdims=True) + RMS_EPS) * nw_ref[:, g * gw:(g + 1) * gw]
        parts.append(seg.astype(BF16))
    parts.append(yat_ref[0].astype(BF16))
    acc = _dot(jnp.concatenate(parts, 1), w_ref[...])
    r = DEEPNORM_ALPHA * h_ref[0] + g_ref[0] * acc
    mu = jnp.mean(r, -1, keepdims=True)
    rc = r - mu
    var = jnp.mean(rc * rc, -1, keepdims=True)
    o_ref[0] = rc * lax.rsqrt(var + LN_EPS) * lg_ref[...] + lb_ref[...]


def _out_projection(h, gate_mod, y_hy, y_ssd, xbc, z_s, y_at, d_skip, norm_w, w_out, ln_g, ln_b):
    b, L, d = h.shape
    tm = min(L, PROJ_ROWS)
    row = lambda bi, i: (bi, i, 0)
    vec = lambda bi, i: (bi, 0, 0)
    full = lambda bi, i: (0, 0)
    w512 = pl.BlockSpec((1, tm, SSM_WIDTH), row)
    return pl.pallas_call(
        _out_body,
        grid=(b, L // tm),
        in_specs=[pl.BlockSpec((1, tm, d), row), pl.BlockSpec((1, 1, d), vec),
                  w512, w512, w512, w512, w512, w512, w512,
                  pl.BlockSpec((1, SSM_WIDTH), full), pl.BlockSpec((1, SSM_WIDTH), full),
                  pl.BlockSpec(w_out.shape, full), pl.BlockSpec((1, d), full), pl.BlockSpec((1, d), full)],
        out_specs=pl.BlockSpec((1, tm, d), row),
        out_shape=jax.ShapeDtypeStruct((b, L, d), F32),
        compiler_params=_cp("parallel", "arbitrary"),
        name="out_projection",
    )(h, gate_mod, y_hy, *y_ssd, xbc, z_s, y_at, d_skip, norm_w, w_out.astype(BF16),
      ln_g.reshape(1, d), ln_b.reshape(1, d))


def _sequence_front(h, shift, scale, w_packed, ssm_conv_w, ssm_conv_b, rope_tables=None):
    u_hy3, u_hyg, u_xbc, u_zs, u_q, u_kv, u_za, u_dt = _in_projection(h, shift, scale, w_packed, rope_tables)
    xbc = _dwconv(u_xbc, ssm_conv_w, ssm_conv_b, act=True, split=SSM_CONV_DIM)[0]
    return dict(hy3=u_hy3, hy_gate=u_hyg, xbc=xbc, z_s=u_zs, q=u_q, kv=u_kv, z_a=u_za, dt=u_dt)


def kernel(x, c, ctx, c_ctx, w_mod, b_mod, w_in, hy_conv_w, hy_conv_b, hy_f_w1, hy_f_b1, hy_f_w2, hy_f_b2,
           hy_f_w3, hy_f_b3, hy_f_freq, hy_f_wout, hy_bias, ssm_conv_w, ssm_conv_b, ssm_dt_bias, ssm_a_log,
           ssm_d, ssm_norm_w, attn_sinks, w_out, ln_g, ln_b):
    b, L, d = x.shape
    lc = ctx.shape[1]
    cos, sin = _rope_tables(L)
    cc = jnp.concatenate([c, c_ctx[None], jnp.zeros((16 - b - 1, d), F32)], 0)
    zero_state = jnp.zeros((b, 2, SSM_GROUPS, SSM_STATE, SSM_WIDTH // SSM_GROUPS), F32)
    h_lat, h_ctx = x, ctx
    for i in range(DEPTH):
        ctx_needed = i < DEPTH - 1
        mod = _modulation(cc, w_mod[i], b_mod[i])
        sh, sc, g = (mod[:b, None, j * d:(j + 1) * d] for j in range(3))
        sh_c, sc_c, g_c = (jnp.broadcast_to(mod[b:b + 1, None, j * d:(j + 1) * d], (b, 1, d)) for j in range(3))
        w_packed = _pack_w_in(w_in[i])
        lat = _sequence_front(h_lat, sh, sc, w_packed, ssm_conv_w[i], ssm_conv_b[i], (cos, sin))
        cx = _sequence_front(h_ctx, sh_c, sc_c, w_packed, ssm_conv_w[i], ssm_conv_b[i])

        dt_bias_row = jnp.pad(ssm_dt_bias[i].reshape(1, -1), ((0, 0), (0, LANES - 2 * SSM_HEADS)))
        a_row = jnp.pad(-jnp.exp(ssm_a_log[i]).reshape(1, -1), ((0, 0), (0, LANES - 2 * SSM_HEADS)))
        ys_c, s_c = _ssd(cx["xbc"], cx["dt"], dt_bias_row, a_row, zero_state)
        ys, _ = _ssd(lat["xbc"], lat["dt"], dt_bias_row, a_row, s_c)

        filt = (hy_f_w1[i], hy_f_b1[i], hy_f_w2[i], hy_f_b2[i], hy_f_w3[i], hy_f_b3[i], hy_f_freq[i], hy_f_wout[i])
        kf = _hyena_spectrum(L, _hyena_filter_taps(L, *filt))
        y_hy = _hyena(lat["hy3"], lat["hy_gate"], hy_conv_w[i], hy_conv_b[i], kf, hy_bias[i])

        y_at = _window_attention(lat["q"], lat["kv"], cx["kv"], attn_sinks[i], lat["z_a"])

        d_skip = jnp.repeat(ssm_d[i], SSM_HEADDIM).reshape(1, SSM_WIDTH)
        norm_w = ssm_norm_w[i].reshape(1, SSM_WIDTH)
        new_lat = _out_projection(h_lat, g, y_hy, ys, lat["xbc"], lat["z_s"], y_at, d_skip, norm_w,
                                  w_out[i], ln_g[i], ln_b[i])
        if ctx_needed:
            kf_c = _hyena_spectrum(lc, _hyena_filter_taps(lc, *filt))
            y_hy_c = _hyena(cx["hy3"], cx["hy_gate"], hy_conv_w[i], hy_conv_b[i], kf_c, hy_bias[i])
            y_at_c = _ctx_attention(cx["q"], cx["kv"], attn_sinks[i], cx["z_a"])
            h_ctx = _out_projection(h_ctx, g_c, y_hy_c, ys_c, cx["xbc"], cx["z_s"], y_at_c, d_skip,
                                    norm_w, w_out[i], ln_g[i], ln_b[i])
        h_lat = new_lat
    return h_lat
```

```python
import functools
import math

import numpy as np
import jax
import jax.numpy as jnp
from jax import lax
from jax.experimental import pallas as pl
from jax.experimental.pallas import tpu as pltpu

F32 = jnp.float32
BF16 = jnp.bfloat16
HI = lax.Precision.HIGHEST

D_MODEL = 1024
DEPTH = 2
GRID_W = 64
HY_WIDTH = 512
HY_BANDS = 16
HY_EMB = 1 + 2 * HY_BANDS
HY_FILTER_HIDDEN = 64
HY_DECAY_TARGET = 1e-2
HY_FAST_PCT = 0.3
HY_SLOW_PCT = 1.5
SSM_WIDTH = 512
SSM_HEADS = 8
SSM_HEADDIM = 64
SSM_GROUPS = 2
SSM_STATE = 128
SSM_CHUNK = 128
SSM_CONV_DIM = SSM_WIDTH + 2 * SSM_GROUPS * SSM_STATE
ATT_WIDTH = 512
ATT_HEADS = 8
ATT_KV_HEADS = 2
ATT_HEADDIM = 64
ATT_GROUP = ATT_HEADS // ATT_KV_HEADS
ATT_KV = ATT_KV_HEADS * ATT_HEADDIM
WINDOW = 128
ROPE_BASE = 10000.0
HY_IN = 4 * HY_WIDTH
SSM_IN = SSM_CONV_DIM + SSM_WIDTH + 2 * SSM_HEADS
DEEPNORM_ALPHA = (2 * DEPTH) ** 0.25
LN_EPS = 1e-6
RMS_EPS = 1e-5

LANES = 128
VMEM_LIMIT = 56 * 1024 * 1024
NEG = -1e30
BIG = 1e30
LOG2E = math.log2(math.e)
FILTER_ROWS = 256
PROJ_ROWS = 512
HY_STEP_ROWS = 8192
SECOND_STAGE_UNROLL = 11


def _cp(*sem):
    return pltpu.CompilerParams(dimension_semantics=sem, vmem_limit_bytes=VMEM_LIMIT)


def _silu(x):
    return (0.5 * x) * (1.0 + jnp.tanh(0.5 * x))


def _dot(a, b, precision=None):
    return jnp.dot(a, b, preferred_element_type=F32, precision=precision)


def _dot_t(a, b):
    return lax.dot_general(a, b, (((1,), (1,)), ((), ())), preferred_element_type=F32)


def _mod_body(c_ref, w_ref, b_ref, o_ref):
    o_ref[...] = _dot(_silu(c_ref[...]), w_ref[...], HI) + b_ref[...]


def _modulation(cc, w, b):
    rows, d = cc.shape
    n = w.shape[1]
    tn = 1024
    return pl.pallas_call(
        _mod_body,
        grid=(n // tn,),
        in_specs=[pl.BlockSpec((rows, d), lambda j: (0, 0)),
                  pl.BlockSpec((d, tn), lambda j: (0, j)),
                  pl.BlockSpec((1, tn), lambda j: (0, j))],
        out_specs=pl.BlockSpec((rows, tn), lambda j: (0, j)),
        out_shape=jax.ShapeDtypeStruct((rows, n), F32),
        compiler_params=_cp("arbitrary"),
        name="modulation",
    )(cc, w, b.reshape(1, n))


IN_SEGS = (3 * HY_WIDTH, HY_WIDTH, SSM_CONV_DIM, SSM_WIDTH, ATT_WIDTH, 2 * ATT_KV, ATT_WIDTH, LANES)
SEG_Q, SEG_KV = 4, 5
IN_DTYPES = (BF16,) * (len(IN_SEGS) - 1) + (F32,)
IN_CHUNK = 512


def _pack_w_in(w):
    o_ss = HY_IN
    o_at = HY_IN + SSM_IN
    dt = w[:, o_ss + SSM_CONV_DIM + SSM_WIDTH:o_at]
    parts = [w[:, :HY_IN], w[:, o_ss:o_ss + SSM_CONV_DIM + SSM_WIDTH], w[:, o_at:],
             dt, jnp.zeros((w.shape[0], LANES - dt.shape[1]), w.dtype)]
    return jnp.concatenate(parts, axis=1).astype(BF16)


def _inproj_body(h_ref, sh_ref, sc_ref, w_ref, *rest, rope):
    tabs, o_refs = (rest[:2], rest[2:]) if rope else ((), rest)
    x = h_ref[0]
    mu = jnp.mean(x, -1, keepdims=True)
    xc = x - mu
    var = jnp.mean(xc * xc, -1, keepdims=True)
    xm = (xc * lax.rsqrt(var + LN_EPS) * (1.0 + sc_ref[0]) + sh_ref[0]).astype(BF16)
    off = 0
    for seg, (o_ref, n) in enumerate(zip(o_refs, IN_SEGS)):
        for j in range(0, n, IN_CHUNK):
            w = min(IN_CHUNK, n - j)
            r = _dot(xm, w_ref[:, off + j:off + j + w])
            if rope and seg == SEG_Q:
                r = _rope(r, tabs[0][...], tabs[1][...]) * (ATT_HEADDIM ** -0.5 * LOG2E)
            if rope and seg == SEG_KV:
                o_ref[0, :, :ATT_KV] = _rope(r[:, :ATT_KV], tabs[0][:, :ATT_KV], tabs[1][:, :ATT_KV]).astype(o_ref.dtype)
                o_ref[0, :, ATT_KV:] = r[:, ATT_KV:].astype(o_ref.dtype)
            else:
                o_ref[0, :, j:j + w] = r.astype(o_ref.dtype)
        off += n


def _in_projection(h, shift, scale, w_packed, rope_tables=None):
    b, L, d = h.shape
    tm = min(L, PROJ_ROWS)
    n_all = w_packed.shape[1]
    row = lambda bi, i: (bi, i, 0)
    vec = lambda bi, i: (bi, 0, 0)
    rope = rope_tables is not None
    tab_specs = [pl.BlockSpec((tm, ATT_WIDTH), lambda bi, i: (i, 0))] * 2 if rope else []
    return pl.pallas_call(
        functools.partial(_inproj_body, rope=rope),
        grid=(b, L // tm),
        in_specs=[pl.BlockSpec((1, tm, d), row), pl.BlockSpec((1, 1, d), vec), pl.BlockSpec((1, 1, d), vec),
                  pl.BlockSpec((d, n_all), lambda bi, i: (0, 0))] + tab_specs,
        out_specs=[pl.BlockSpec((1, tm, n), row) for n in IN_SEGS],
        out_shape=[jax.ShapeDtypeStruct((b, L, n), dt) for n, dt in zip(IN_SEGS, IN_DTYPES)],
        compiler_params=_cp("parallel", "arbitrary"),
        name="in_projection",
    )(h, shift, scale, w_packed, *(rope_tables or ()))


def _dwconv_body(u_ref, w_ref, b_ref, o_ref, *, act):
    x = u_ref[0].astype(F32)
    L = x.shape[0]
    row = lax.broadcasted_iota(jnp.int32, x.shape, 0)
    prev = jnp.where(row == 0, 0.0, pltpu.roll(x, 1, 0))
    nxt = jnp.where(row == L - 1, 0.0, pltpu.roll(x, L - 1, 0))
    y = prev * w_ref[0:1, :] + x * w_ref[1:2, :] + nxt * w_ref[2:3, :] + b_ref[...]
    if act:
        y = _silu(y)
    o_ref[0, 0] = y.astype(o_ref.dtype)


def _dwconv(u, w, bias, *, act, split):
    b, L, c = u.shape
    tc = 256
    per = split // tc
    return pl.pallas_call(
        functools.partial(_dwconv_body, act=act),
        grid=(b, c // tc),
        in_specs=[pl.BlockSpec((1, L, tc), lambda bi, j: (bi, 0, j)),
                  pl.BlockSpec((3, tc), lambda bi, j: (0, j)),
                  pl.BlockSpec((1, tc), lambda bi, j: (0, j))],
        out_specs=pl.BlockSpec((1, 1, L, tc), lambda bi, j: (j // per, bi, 0, j % per)),
        out_shape=jax.ShapeDtypeStruct((c // split, b, L, split), BF16),
        compiler_params=_cp("parallel", "arbitrary"),
        name="dwconv",
    )(u, w, bias.reshape(1, c))


def _filter_features(L):
    t = jnp.linspace(0.0, 1.0, L, dtype=F32)[:, None]
    w = 2.0 * math.pi * jnp.arange(L, dtype=F32)[:, None] / L
    f = jnp.linspace(1e-4, HY_BANDS - 1, HY_BANDS, dtype=F32)[None]
    z = jnp.concatenate([t, jnp.cos(f * w), -jnp.sin(f * w)], -1)
    return jnp.pad(z, ((0, 0), (0, LANES - HY_EMB)))


def _pad_to(a, rows, cols):
    return jnp.pad(a, ((0, rows - a.shape[0]), (0, cols - a.shape[1])))


def _filter_body(z_ref, w1_ref, w2_ref, w3_ref, b_ref, fr_ref, wo_ref, ad_ref, o_ref, *, nblk):
    i = pl.program_id(0)

    @pl.when(i < nblk)
    def _():
        z = z_ref[...]
        fr = fr_ref[...]
        h = jnp.sin(fr * (_dot(z, w1_ref[...], HI) + b_ref[0:1, :]))
        h = jnp.sin(fr * (_dot(h, w2_ref[...], HI) + b_ref[1:2, :]))
        h = jnp.sin(fr * (_dot(h, w3_ref[...], HI) + b_ref[2:3, :]))
        win = jnp.exp(-z[:, 0:1] * ad_ref[...])
        h_hi = h.astype(BF16)
        h_lo = (h - h_hi.astype(F32)).astype(BF16)
        for j in range(4):
            cols = slice(j * HY_WIDTH, (j + 1) * HY_WIDTH)
            taps = _dot(h_hi, wo_ref[0, :, cols]) + _dot(h_lo, wo_ref[0, :, cols]) + _dot(h_hi, wo_ref[1, :, cols])
            o_ref[:, cols] = taps * win

    @pl.when(i == nblk)
    def _():
        o_ref[...] = jnp.zeros_like(o_ref)


def _hyena_filter_taps(L, w1, b1, w2, b2, w3, b3, freq, w_out):
    z = _filter_features(L)
    hp = LANES
    bias = jnp.stack([jnp.pad(b, (0, hp - b.shape[0])) for b in (b1, b2, b3)])
    bias = jnp.pad(bias, ((0, 5), (0, 0)))
    fr = jnp.pad(freq, (0, hp - freq.shape[0])).reshape(1, hp)
    max_decay = math.log(HY_DECAY_TARGET) / HY_FAST_PCT
    min_decay = math.log(HY_DECAY_TARGET) / HY_SLOW_PCT
    absd = jnp.abs(jnp.linspace(min_decay, max_decay, HY_WIDTH, dtype=F32)).reshape(1, HY_WIDTH)
    tl = FILTER_ROWS
    nblk = L // tl
    n = 4 * HY_WIDTH
    full = lambda i: (0, 0)
    return pl.pallas_call(
        functools.partial(_filter_body, nblk=nblk),
        grid=(nblk + 1,),
        in_specs=[pl.BlockSpec((tl, hp), lambda i: (jnp.minimum(i, nblk - 1), 0)),
                  pl.BlockSpec((hp, hp), full), pl.BlockSpec((hp, hp), full), pl.BlockSpec((hp, hp), full),
                  pl.BlockSpec((8, hp), full), pl.BlockSpec((1, hp), full),
                  pl.BlockSpec((2, hp, n), lambda i: (0, 0, 0)), pl.BlockSpec((1, HY_WIDTH), full)],
        out_specs=pl.BlockSpec((tl, n), lambda i: (i, 0)),
        out_shape=jax.ShapeDtypeStruct((L + tl, n), F32),
        compiler_params=_cp("arbitrary"),
        name="hyena_filter",
    )(z, _pad_to(w1, hp, hp), _pad_to(w2, hp, hp), _pad_to(w3, hp, hp), bias, fr, _split2(_pad_to(w_out, hp, n)),
      absd)


def _hy_cfg(L):
    n2 = 128 if L >= 2048 else 16
    n1 = 2 * L // n2
    k1n = n1 // 2 + 1
    jp = -(-2 * k1n // 16) * 16
    pitch = n2 + 8
    return dict(L=L, n2=n2, n1=n1, nh=n1 // 2, k1n=k1n, jp=jp, pitch=pitch)


@functools.lru_cache(maxsize=None)
def _dft_tables(L):
    cfg = _hy_cfg(L)
    n, n1, n2, nh, k1n, jp = 2 * L, cfg["n1"], cfg["n2"], cfg["nh"], cfg["k1n"], cfg["jp"]
    a_n1 = np.arange(n1)
    a_k1 = np.arange(k1n)
    th = 2 * np.pi * np.outer(a_k1, a_n1) / n1
    f1 = np.zeros((jp, n1))
    f1[0:2 * k1n:2] = np.cos(th)
    f1[1:2 * k1n:2] = -np.sin(th)
    a_n2 = np.arange(n2)
    m1 = np.zeros((k1n, 2 * n2, 2 * n2))
    for k in range(k1n):
        f = np.exp(-2j * np.pi * (np.outer(a_n2, a_n2) / n2 + a_n2[None, :] * k / n))
        m1[k] = np.block([[f.real, -f.imag], [f.imag, f.real]])
    ck = np.full(k1n, 2.0)
    ck[0] = 1.0
    ck[-1] = 1.0
    th6 = 2 * np.pi * np.outer(np.arange(nh), a_k1) / n1
    f6 = np.zeros((nh, jp))
    f6[:, 0:2 * k1n:2] = ck * np.cos(th6) / n
    f6[:, 1:2 * k1n:2] = -ck * np.sin(th6) / n
    as32 = lambda a: np.asarray(a, np.float32)
    return dict(f1=as32(f1), m1=as32(m1), f6=as32(f6))


def _first_stage(src_ref, a_scr, f1, cfg):
    n2, nh, jp, pitch = cfg["n2"], cfg["nh"], cfg["jp"], cfg["pitch"]
    bt = src_ref.shape[0]

    def step(i, carry):
        xs = jnp.concatenate([src_ref[t, pl.ds(2 * i + u, nh, stride=pitch), :] for t in range(bt) for u in range(2)], 1)
        r = _dot(f1, xs.astype(BF16))
        for t in range(bt):
            for u in range(2):
                lo = (2 * t + u) * LANES
                a_scr[t, pl.ds(2 * i + u, jp, stride=pitch), :] = r[:, lo:lo + LANES]
        return carry

    lax.fori_loop(0, n2 // 2, step, 0, unroll=min(16 // bt, n2 // 2))


def _k1_rows(a_scr, k, cfg):
    n2, pitch = cfg["n2"], cfg["pitch"]
    base = pl.multiple_of(2 * k * pitch, 8)
    parts = [jnp.concatenate([a_scr[t, pl.ds(base, n2), :], a_scr[t, pl.ds(base + pitch, n2), :]], 0)
             for t in range(a_scr.shape[0])]
    return base, jnp.concatenate(parts, 1)


def _split2(table):
    t = jnp.asarray(table)
    hi = t.astype(BF16)
    return jnp.stack([hi, (t - hi.astype(F32)).astype(BF16)])


def _spectrum_body(kf_ref, kb_ref, f1_ref, m1_ref, o_ref, a_scr, *, cfg):
    n2, nh, jp, pitch = cfg["n2"], cfg["nh"], cfg["jp"], cfg["pitch"]

    def dot3(m_hi, m_lo, x):
        x_hi = x.astype(BF16)
        x_lo = (x - x_hi.astype(F32)).astype(BF16)
        return _dot(m_hi, x_hi) + _dot(m_lo, x_hi) + _dot(m_hi, x_lo)

    def step(i, carry):
        cols = []
        for t in range(2):
            n = 2 * i + t
            cols.append(jnp.concatenate([kf_ref[pl.ds(n, nh, stride=n2), :],
                                         kb_ref[pl.ds(n2 - n, nh, stride=n2), :]], 0))
        r = dot3(f1_ref[0], f1_ref[1], jnp.concatenate(cols, 1))
        a_scr[0, pl.ds(2 * i, jp, stride=pitch), :] = r[:, :LANES]
        a_scr[0, pl.ds(2 * i + 1, jp, stride=pitch), :] = r[:, LANES:]
        return carry

    lax.fori_loop(0, n2 // 2, step, 0, unroll=min(16, n2 // 2))
    lag0 = pl.ds(0, jp, stride=pitch)
    a_scr[0, lag0, :] = a_scr[0, lag0, :] + f1_ref[0, :, 0:1].astype(F32) * kb_ref[0:1, :]

    def mid(k, carry):
        _, a = _k1_rows(a_scr, k, cfg)
        o_ref[0, k] = dot3(m1_ref[0, k], m1_ref[1, k], a).astype(BF16)
        return carry

    lax.fori_loop(0, cfg["k1n"], mid, 0, unroll=3)


def _hyena_spectrum(L, taps):
    c = HY_WIDTH
    cfg = _hy_cfg(L)
    tb = _dft_tables(L)
    n1, n2, nh, k1n = cfg["n1"], cfg["n2"], cfg["nh"], cfg["k1n"]
    f1 = _split2(np.concatenate([tb["f1"][:, :nh], tb["f1"][:, n1 - 1:nh - 1:-1]], 1))
    m1 = _split2(tb["m1"])
    nct = c // LANES
    rows = taps.shape[0]
    return pl.pallas_call(
        functools.partial(_spectrum_body, cfg=cfg),
        grid=(2, nct),
        in_specs=[pl.BlockSpec((rows, LANES), lambda cv, j: (0, 2 * cv * nct + j)),
                  pl.BlockSpec((rows, LANES), lambda cv, j: (0, (2 * cv + 1) * nct + j)),
                  pl.BlockSpec(f1.shape, lambda cv, j: (0, 0, 0)),
                  pl.BlockSpec(m1.shape, lambda cv, j: (0, 0, 0, 0))],
        out_specs=pl.BlockSpec((1, k1n, 2 * n2, LANES), lambda cv, j: (cv, 0, 0, j)),
        out_shape=jax.ShapeDtypeStruct((2, k1n, 2 * n2, c), BF16),
        scratch_shapes=[pltpu.VMEM((1, cfg["jp"] * cfg["pitch"], LANES), F32)],
        compiler_params=_cp("arbitrary", "arbitrary"),
        name="hyena_spectrum",
    )(taps, taps, f1, m1)


def _short_conv(u_ref, w_ref, b_ref, which):
    x = u_ref[...].astype(F32)
    L = x.shape[0]
    edge = lax.broadcasted_iota(jnp.int32, (8, x.shape[1]), 0)
    prev = pltpu.roll(x, 1, 0)
    prev = jnp.concatenate([jnp.where(edge == 0, 0.0, prev[:8]), prev[8:]], 0)
    nxt = pltpu.roll(x, L - 1, 0)
    nxt = jnp.concatenate([nxt[:L - 8], jnp.where(edge == 7, 0.0, nxt[L - 8:])], 0)
    sel = slice(which, which + 1)
    return prev * w_ref[0, sel, :] + x * w_ref[1, sel, :] + nxt * w_ref[2, sel, :] + b_ref[sel, :]


def _long_conv(src_scr, a_scr, y_scr, w_scr, f1_ref, f6_ref, m1_ref, kf_ref, conv, cfg):
    n2, nh, jp, pitch = cfg["n2"], cfg["nh"], cfg["jp"], cfg["pitch"]
    bt = src_scr.shape[0]
    _first_stage(src_scr, a_scr, f1_ref[...], cfg)

    def forward(k, carry):
        _, a = _k1_rows(a_scr, k, cfg)
        x = _dot(m1_ref[k], a.astype(BF16))
        kk = kf_ref[conv, k].astype(F32)
        kr, ki = kk[:n2], kk[n2:]
        cols = []
        for t in range(bt):
            xr, xi = x[:n2, t * LANES:(t + 1) * LANES], x[n2:, t * LANES:(t + 1) * LANES]
            cols.append(jnp.concatenate([xr * kr - xi * ki, xr * ki + xi * kr], 0))
        w_scr[k] = jnp.concatenate(cols, 1).astype(BF16)
        return carry

    def inverse(k, carry):
        base = pl.multiple_of(2 * k * pitch, 8)
        b = lax.dot_general(m1_ref[k], w_scr[k], (((0,), (0,)), ((), ())), preferred_element_type=F32)
        for t in range(bt):
            a_scr[t, pl.ds(base, n2), :] = b[:n2, t * LANES:(t + 1) * LANES]
            a_scr[t, pl.ds(base + pitch, n2), :] = b[n2:, t * LANES:(t + 1) * LANES]
        return carry

    lax.fori_loop(0, cfg["k1n"], forward, 0, unroll=SECOND_STAGE_UNROLL)
    lax.fori_loop(0, cfg["k1n"], inverse, 0, unroll=SECOND_STAGE_UNROLL)
    f6 = f6_ref[...]

    def last(i, carry):
        bs = jnp.concatenate([a_scr[t, pl.ds(2 * i + u, jp, stride=pitch), :] for t in range(bt) for u in range(2)], 1)
        y = _dot(f6, bs.astype(BF16))
        for t in range(bt):
            for u in range(2):
                lo = (2 * t + u) * LANES
                y_scr[t, pl.ds(2 * i + u, nh, stride=pitch), :] = y[:, lo:lo + LANES]
        return carry

    lax.fori_loop(0, n2 // 2, last, 0, unroll=min(16 // bt, n2 // 2))


def _hyena_body(v_ref, x1_ref, x2_ref, g_ref, cw_ref, cb_ref, hb_ref, f1_ref, f6_ref, m1_ref, kf_ref,
                o_ref, s_scr, a_scr, y_scr, w_scr, *, cfg):
    tabs = (w_scr, f1_ref, f6_ref, m1_ref, kf_ref)
    n2, nh, pitch = cfg["n2"], cfg["nh"], cfg["pitch"]
    slots = range(s_scr.shape[0])

    def put(scr, t, val):
        for n1 in range(nh):
            scr[t, n1 * pitch:n1 * pitch + n2, :] = val[n1 * n2:(n1 + 1) * n2]

    def get(scr, t):
        return jnp.concatenate([scr[t, n1 * pitch:n1 * pitch + n2, :] for n1 in range(nh)], 0)

    for t in slots:
        put(s_scr, t, _short_conv(v_ref.at[t], cw_ref, cb_ref, 0))
    _long_conv(s_scr, a_scr, y_scr, *tabs, 0, cfg)
    for t in slots:
        put(s_scr, t, _short_conv(x1_ref.at[t], cw_ref, cb_ref, 1) * (get(y_scr, t) + get(s_scr, t) * hb_ref[0:1, :]))
    _long_conv(s_scr, a_scr, y_scr, *tabs, 1, cfg)
    for t in slots:
        y = _short_conv(x2_ref.at[t], cw_ref, cb_ref, 2) * (get(y_scr, t) + get(s_scr, t) * hb_ref[1:2, :])
        o_ref[t] = (y * _silu(g_ref[t].astype(F32))).astype(o_ref.dtype)


def _hyena(u_hy3, u_gate, conv_w, conv_b, kf, hy_bias):
    b, L, _ = u_gate.shape
    c = HY_WIDTH
    cfg = _hy_cfg(L)
    tb = _dft_tables(L)
    nct = c // LANES
    bt = max(t for t in (1, 2, 4, 8) if b % t == 0 and t * L <= max(L, HY_STEP_ROWS))
    col = lambda off: pl.BlockSpec((bt, L, LANES), lambda j, bi: (bi, 0, off * nct + j))
    full = lambda a: pl.BlockSpec(a.shape, lambda j, bi: (0,) * a.ndim)
    f1 = jnp.asarray(tb["f1"][:, :cfg["nh"]]).astype(BF16)
    f6 = jnp.asarray(tb["f6"]).astype(BF16)
    m1 = jnp.asarray(tb["m1"]).astype(BF16)
    return pl.pallas_call(
        functools.partial(_hyena_body, cfg=cfg),
        grid=(nct, b // bt),
        in_specs=[col(0), col(1), col(2), col(0),
                  pl.BlockSpec((3, 3, LANES), lambda j, bi: (0, 0, j)),
                  pl.BlockSpec((3, LANES), lambda j, bi: (0, j)),
                  pl.BlockSpec((2, LANES), lambda j, bi: (0, j)),
                  full(f1), full(f6), full(m1),
                  pl.BlockSpec((2, cfg["k1n"], 2 * cfg["n2"], LANES), lambda j, bi: (0, 0, 0, j),
                               pipeline_mode=pl.Buffered(1))],
        out_specs=col(0),
        out_shape=jax.ShapeDtypeStruct((b, L, c), BF16),
        scratch_shapes=[pltpu.VMEM((bt, cfg["nh"] * cfg["pitch"], LANES), F32),
                        pltpu.VMEM((bt, cfg["jp"] * cfg["pitch"], LANES), F32),
                        pltpu.VMEM((bt, cfg["nh"] * cfg["pitch"], LANES), F32),
                        pltpu.VMEM((cfg["k1n"], 2 * cfg["n2"], bt * LANES), BF16)],
        compiler_params=_cp("arbitrary", "arbitrary"),
        name="hyena",
    )(u_hy3, u_hy3, u_hy3, u_gate, conv_w.reshape(3, 3, c), conv_b.reshape(3, c), hy_bias, f1, f6, m1, kf)


SSD_CPS = 4
SSD_SPS = 2
SPLIT_STRIDE = 2 * SSM_HEADS


def _pack3(x):
    hi = x.astype(BF16).astype(F32)
    r1 = x - hi
    mid = r1.astype(BF16).astype(F32)
    lo = (r1 - mid).astype(BF16).astype(F32)
    return (hi + pltpu.roll(mid, SPLIT_STRIDE, 1) + pltpu.roll(lo, 2 * SPLIT_STRIDE, 1)).astype(BF16)


def _unpack3(x3, used):
    return jnp.where(used, x3 + pltpu.roll(x3, LANES - SPLIT_STRIDE, 1) + pltpu.roll(x3, LANES - 2 * SPLIT_STRIDE, 1), 0.0)


@functools.lru_cache(maxsize=None)
def _ssd_spread_tables():
    col = np.zeros((LANES, 2 * SSM_HEADS * LANES), np.float32)
    head = np.zeros((LANES, 2 * SSM_WIDTH), np.float32)
    for c in range(2 * SSM_HEADS):
        d, h = divmod(c, SSM_HEADS)
        for piece in range(3):
            col[c + piece * SPLIT_STRIDE, c * LANES:(c + 1) * LANES] = 1.0
            lo = d * SSM_WIDTH + h * SSM_HEADDIM
            head[c + piece * SPLIT_STRIDE, lo:lo + SSM_HEADDIM] = 1.0
    return col, head


def _ssd_chunk_body(xbc_ref, dtr_ref, dtb_ref, a_ref, ecol_ref, ehead_ref, yd_ref, cs_ref, ex_ref, et_ref):
    q = SSM_CHUNK
    hpg = SSM_HEADS // SSM_GROUPS
    gw = SSM_WIDTH // SSM_GROUPS
    li = lax.broadcasted_iota(jnp.int32, (q, q), 0)
    si = lax.broadcasted_iota(jnp.int32, (q, q), 1)
    below = li > si
    diag = li == si
    fwd_lane = si < SSM_HEADS
    used = si < 2 * SSM_HEADS
    tril = (li >= si).astype(BF16)
    triu = (li <= si).astype(BF16)
    chunks = range(cs_ref.shape[1])
    rows = [slice(c * q, (c + 1) * q) for c in chunks]
    ehead = ehead_ref[...]

    raws = [dtr_ref[0, r, :] + dtb_ref[...] for r in rows]
    dts = [jnp.where(used, jnp.maximum(x, 0.0) + jnp.log1p(jnp.exp(-jnp.abs(x))), 0.0) for x in raws]
    da3 = [_pack3(dt * a_ref[...]) for dt in dts]
    acs = [jnp.where(fwd_lane, _unpack3(_dot(tril, x), used), _unpack3(_dot(triu, x), used)) for x in da3]
    tots = [jnp.where(fwd_lane[0:1], a[q - 1:q, :], a[0:1, :]) for a in acs]
    ws = [dt * jnp.exp(t - a) for dt, t, a in zip(dts, tots, acs)]
    acs3 = [_pack3(a) for a in acs]
    colb = [_dot(x, ecol_ref[...]) for x in acs3]
    wx = [_dot(_pack3(w), ehead) for w in ws]
    for c in chunks:
        ex_ref[0, rows[c], :] = jnp.exp(_dot(acs3[c], ehead)).astype(BF16)
        et_ref[0, c] = jnp.exp(_dot(_pack3(jnp.broadcast_to(tots[c], (8, LANES))), ehead))
    rowt = [(a - jnp.where(dt > 0.0, jnp.log(dt), -BIG)).T for a, dt in zip(acs, dts)]
    dsum = [(dt + pltpu.roll(dt, LANES - SSM_HEADS, 1)).T for dt in dts]
    xbc = [xbc_ref[0, r, :] for r in rows]
    xsb = [x[:, :SSM_WIDTH].astype(BF16) for x in xbc]
    bgs = [[x[:, SSM_WIDTH + g * SSM_STATE:SSM_WIDTH + (g + 1) * SSM_STATE] for g in range(SSM_GROUPS)] for x in xbc]
    cgs = [[x[:, SSM_WIDTH + (SSM_GROUPS + g) * SSM_STATE:SSM_WIDTH + (SSM_GROUPS + g + 1) * SSM_STATE]
            for g in range(SSM_GROUPS)] for x in xbc]
    gmat = [[_dot_t(cgs[c][g].astype(BF16), bgs[c][g].astype(BF16)) for g in range(SSM_GROUPS)] for c in chunks]
    mats = []
    for c in chunks:
        for h in range(SSM_HEADS):
            hb = SSM_HEADS + h
            arg = jnp.where(below, colb[c][:, h * q:(h + 1) * q] - rowt[c][h:h + 1, :],
                            colb[c][:, hb * q:(hb + 1) * q] - rowt[c][hb:hb + 1, :])
            dec = jnp.where(diag, dsum[c][h:h + 1, :], jnp.exp(arg))
            mats.append((gmat[c][h // hpg] * dec).astype(BF16))
    for c in chunks:
        for h in range(SSM_HEADS):
            lo = h * SSM_HEADDIM
            yd_ref[0, rows[c], lo:lo + SSM_HEADDIM] = _dot(mats[c * SSM_HEADS + h],
                                                             xsb[c][:, lo:lo + SSM_HEADDIM]).astype(yd_ref.dtype)
    for c in chunks:
        for g in range(SSM_GROUPS):
            bgt = bgs[c][g].astype(F32).T.astype(BF16)
            xg = xbc[c][:, g * gw:(g + 1) * gw].astype(F32)
            for d in range(2):
                lo = d * SSM_WIDTH + g * gw
                cs_ref[0, c, d, g] = _dot(bgt, (xg * wx[c][:, lo:lo + gw]).astype(BF16)).astype(cs_ref.dtype)


def _ssd_chunks(xbc, dt_raw, dt_bias_row, a_row):
    b, L, _ = xbc.shape
    nc = L // SSM_CHUNK
    cps = math.gcd(nc, SSD_CPS)
    rows = cps * SSM_CHUNK
    gw = SSM_WIDTH // SSM_GROUPS
    blk = lambda bi, i: (bi, i, 0)
    full2 = lambda bi, i: (0, 0)
    ecol, ehead = (jnp.asarray(t).astype(BF16) for t in _ssd_spread_tables())
    return pl.pallas_call(
        _ssd_chunk_body,
        grid=(b, nc // cps),
        in_specs=[pl.BlockSpec((1, rows, SSM_CONV_DIM), blk), pl.BlockSpec((1, rows, LANES), blk),
                  pl.BlockSpec((1, LANES), full2), pl.BlockSpec((1, LANES), full2),
                  pl.BlockSpec(ecol.shape, full2), pl.BlockSpec(ehead.shape, full2)],
        out_specs=[pl.BlockSpec((1, rows, SSM_WIDTH), blk),
                   pl.BlockSpec((1, cps, 2, SSM_GROUPS, SSM_STATE, gw), lambda bi, i: (bi, i, 0, 0, 0, 0)),
                   pl.BlockSpec((1, rows, 2 * SSM_WIDTH), blk),
                   pl.BlockSpec((1, cps, 8, 2 * SSM_WIDTH), lambda bi, i: (bi, i, 0, 0))],
        out_shape=[jax.ShapeDtypeStruct((b, L, SSM_WIDTH), BF16),
                   jax.ShapeDtypeStruct((b, nc, 2, SSM_GROUPS, SSM_STATE, gw), BF16),
                   jax.ShapeDtypeStruct((b, L, 2 * SSM_WIDTH), BF16),
                   jax.ShapeDtypeStruct((b, nc, 8, 2 * SSM_WIDTH), F32)],
        compiler_params=_cp("parallel", "arbitrary"),
        name="ssd_chunks",
    )(xbc, dt_raw, dt_bias_row, a_row, ecol, ehead)


def _ssd_state_body(cf_ref, cb_ref, xf_ref, xb_ref, ef_ref, eb_ref, sf_ref, sb_ref, init_ref, yf_ref, yb_ref, fin_ref,
                    st_ref, *, nsteps):
    ci = pl.program_id(1)

    @pl.when(ci == 0)
    def _():
        st_ref[...] = init_ref[0]

    q = SSM_CHUNK
    gw = SSM_WIDTH // SSM_GROUPS
    for s in range(SSD_SPS):
        dirs = ((cf_ref, xf_ref, ef_ref, sf_ref, yf_ref, s), (cb_ref, xb_ref, eb_ref, sb_ref, yb_ref, SSD_SPS - 1 - s))
        for d, (c_ref, x_ref, e_ref, s_ref, y_ref, j) in enumerate(dirs):
            rows = slice(j * q, (j + 1) * q)
            cmat = c_ref[0, rows, :].astype(BF16)
            for g in range(SSM_GROUPS):
                cols = slice(g * gw, (g + 1) * gw)
                st = st_ref[d, g]
                y_ref[0, rows, cols] = (_dot(cmat[:, g * SSM_STATE:(g + 1) * SSM_STATE], st.astype(BF16))
                                        * x_ref[0, rows, cols].astype(F32)).astype(y_ref.dtype)
                st_ref[d, g] = st * e_ref[0, j, 0:1, cols] + s_ref[0, j, 0, g].astype(F32)

    @pl.when(ci == nsteps - 1)
    def _():
        fin_ref[0] = st_ref[...]


def _ssd_states(xbc, ex, et, cs, init):
    b, L, _ = xbc.shape
    nc = L // SSM_CHUNK
    gw = SSM_WIDTH // SSM_GROUPS
    c_col = SSM_CONV_DIM // (SSM_GROUPS * SSM_STATE) - 1
    st_shape = (2, SSM_GROUPS, SSM_STATE, gw)
    st_spec = pl.BlockSpec((1,) + st_shape, lambda bi, c: (bi, 0, 0, 0, 0))
    nsteps = nc // SSD_SPS
    rows = SSD_SPS * SSM_CHUNK
    cs_blk = (1, SSD_SPS, 1, SSM_GROUPS, SSM_STATE, gw)
    fwd = lambda *tail: (lambda bi, c: (bi, c) + tail)
    bwd = lambda *tail: (lambda bi, c: (bi, nsteps - 1 - c) + tail)
    return pl.pallas_call(
        functools.partial(_ssd_state_body, nsteps=nsteps),
        grid=(b, nsteps),
        in_specs=[pl.BlockSpec((1, rows, SSM_GROUPS * SSM_STATE), fwd(c_col)),
                  pl.BlockSpec((1, rows, SSM_GROUPS * SSM_STATE), bwd(c_col)),
                  pl.BlockSpec((1, rows, SSM_WIDTH), fwd(0)),
                  pl.BlockSpec((1, rows, SSM_WIDTH), bwd(1)),
                  pl.BlockSpec((1, SSD_SPS, 8, SSM_WIDTH), fwd(0, 0)),
                  pl.BlockSpec((1, SSD_SPS, 8, SSM_WIDTH), bwd(0, 1)),
                  pl.BlockSpec(cs_blk, fwd(0, 0, 0, 0)),
                  pl.BlockSpec(cs_blk, bwd(1, 0, 0, 0)),
                  st_spec],
        out_specs=[pl.BlockSpec((1, rows, SSM_WIDTH), fwd(0)),
                   pl.BlockSpec((1, rows, SSM_WIDTH), bwd(0)),
                   st_spec],
        out_shape=[jax.ShapeDtypeStruct((b, L, SSM_WIDTH), BF16), jax.ShapeDtypeStruct((b, L, SSM_WIDTH), BF16),
                   jax.ShapeDtypeStruct((b,) + st_shape, F32)],
        scratch_shapes=[pltpu.VMEM(st_shape, F32)],
        compiler_params=_cp("parallel", "arbitrary"),
        name="ssd_states",
    )(xbc, xbc, ex, ex, et, et, cs, cs, init)


def _ssd(xbc, dt_raw, dt_bias_row, a_row, init):
    y_diag, cs, ex, et = _ssd_chunks(xbc, dt_raw, dt_bias_row, a_row)
    y_f, y_b, fin = _ssd_states(xbc, ex, et, cs, init)
    return (y_diag, y_f, y_b), fin


def _rope_tables(L):
    rows = L // GRID_W
    row = jnp.broadcast_to(jnp.arange(rows)[:, None], (rows, GRID_W)).reshape(L)
    col = jnp.broadcast_to(jnp.arange(GRID_W)[None, :], (rows, GRID_W)).reshape(L)
    nf = ATT_HEADDIM // 4
    inv = ROPE_BASE ** (-jnp.arange(nf, dtype=F32) / nf)
    ar = row.astype(F32)[:, None] * inv
    ac = col.astype(F32)[:, None] * inv
    cos = jnp.concatenate([jnp.cos(ar), jnp.cos(ar), jnp.cos(ac), jnp.cos(ac)], -1)
    sin = jnp.concatenate([-jnp.sin(ar), jnp.sin(ar), -jnp.sin(ac), jnp.sin(ac)], -1)
    return jnp.tile(cos, (1, ATT_HEADS)), jnp.tile(sin, (1, ATT_HEADS))


def _rope(x, cos, sin):
    w = x.shape[-1]
    quarter = ATT_HEADDIM // 4
    lane = lax.broadcasted_iota(jnp.int32, x.shape, x.ndim - 1)
    partner = jnp.where((lane // quarter) % 2 == 0, pltpu.roll(x, w - quarter, x.ndim - 1),
                        pltpu.roll(x, quarter, x.ndim - 1))
    return x * cos + partner * sin


def _wattn_body(sink_ref, bias_ref, q_ref, kp_ref, kc_ref, kn_ref, vp_ref, vc_ref, vn_ref, kx_ref, vx_ref, z_ref,
                o_ref):
    hd = ATT_HEADDIM
    low_half = lax.broadcasted_iota(jnp.int32, (WINDOW, 2 * hd), 1) < hd
    q = q_ref[0]
    bias = bias_ref[0]
    z = z_ref[0].astype(F32)
    k_all, v_ext = [], []
    for g in range(ATT_KV_HEADS):
        ks = slice(g * hd, (g + 1) * hd)
        k_all.append(jnp.concatenate([r[0, :, ks] for r in (kp_ref, kc_ref, kn_ref, kx_ref)], 0).astype(BF16))
        v_all = jnp.concatenate([r[0, :, ks] for r in (vp_ref, vc_ref, vn_ref, vx_ref)], 0).astype(BF16)
        ones = jnp.ones_like(v_all)
        v_ext.append((jnp.concatenate([v_all, ones], 1), jnp.concatenate([ones, v_all], 1)))
    heads = range(ATT_HEADS)
    sinks = [sink_ref[h] * LOG2E for h in heads]
    scores = [_dot_t(q[:, h * hd:(h + 1) * hd].astype(BF16), k_all[h // ATT_GROUP]) + bias for h in heads]
    maxes = [jnp.maximum(jnp.max(s, -1, keepdims=True), sk) for s, sk in zip(scores, sinks)]
    probs = [jnp.exp2(s - m).astype(BF16) for s, m in zip(scores, maxes)]
    exts = [_dot(p, v_ext[h // ATT_GROUP][h % 2]) for h, p in zip(heads, probs)]
    outs = [e / (pltpu.roll(e, hd, 1) + jnp.exp2(sk - m)) for e, sk, m in zip(exts, sinks, maxes)]
    for pair in range(ATT_HEADS // 2):
        cs = slice(2 * pair * hd, (2 * pair + 2) * hd)
        o_ref[0, :, cs] = (jnp.where(low_half, outs[2 * pair], outs[2 * pair + 1]) * _silu(z[:, cs])).astype(o_ref.dtype)


def _window_attention(q_rot, u_kv, uc_kv, sinks, z_a):
    b, L, _ = q_rot.shape
    lc = uc_kv.shape[1]
    nb = L // WINDOW
    hd2 = ATT_KV
    cur = lambda bi, i: (bi, i, 0)
    prv = lambda bi, i: (bi, jnp.maximum(i - 1, 0), 0)
    nxt = lambda bi, i: (bi, jnp.minimum(i + 1, nb - 1), 0)
    vcur = lambda bi, i: (bi, i, 1)
    vprv = lambda bi, i: (bi, jnp.maximum(i - 1, 0), 1)
    vnxt = lambda bi, i: (bi, jnp.minimum(i + 1, nb - 1), 1)
    kblk = lambda f: pl.BlockSpec((1, WINDOW, hd2), f)
    nk = 3 * WINDOW + lc
    row = np.arange(WINDOW)[:, None]
    col = np.arange(nk)[None, :]
    in_prev = (col < WINDOW) & (col >= row)
    in_next = (col >= 2 * WINDOW) & (col < 3 * WINDOW) & (col - 2 * WINDOW <= row)
    always = ((col >= WINDOW) & (col < 2 * WINDOW)) | (col >= 3 * WINDOW)
    kinds = [always | in_next, always | in_prev | in_next, always | in_prev]
    if nb == 1:
        kinds = [always] * 3
    bias = jnp.asarray(np.where(np.stack(kinds), 0.0, NEG).astype(np.float32))
    kind = lambda bi, i: (jnp.where(i == 0, 0, jnp.where(i == nb - 1, 2, 1)), 0, 0)
    return pl.pallas_call(
        _wattn_body,
        grid=(b, nb),
        in_specs=[pl.BlockSpec(memory_space=pltpu.SMEM),
                  pl.BlockSpec((1, WINDOW, nk), kind),
                  pl.BlockSpec((1, WINDOW, ATT_WIDTH), cur),
                  kblk(prv), kblk(cur), kblk(nxt), kblk(vprv), kblk(vcur), kblk(vnxt),
                  pl.BlockSpec((1, lc, hd2), lambda bi, i: (bi, 0, 0)),
                  pl.BlockSpec((1, lc, hd2), lambda bi, i: (bi, 0, 1)),
                  pl.BlockSpec((1, WINDOW, ATT_WIDTH), cur)],
        out_specs=pl.BlockSpec((1, WINDOW, ATT_WIDTH), cur),
        out_shape=jax.ShapeDtypeStruct((b, L, ATT_WIDTH), BF16),
        compiler_params=_cp("parallel", "arbitrary"),
        name="window_attention",
    )(sinks, bias, q_rot, u_kv, u_kv, u_kv, u_kv, u_kv, u_kv, uc_kv, uc_kv, z_a)


def _cattn_body(sink_ref, q_ref, k_ref, v_ref, z_ref, o_ref):
    scale = ATT_HEADDIM ** -0.5
    q = q_ref[0]
    z = z_ref[0].astype(F32)
    for g in range(ATT_KV_HEADS):
        ks = slice(g * ATT_HEADDIM, (g + 1) * ATT_HEADDIM)
        k = k_ref[0, :, ks].astype(BF16)
        v = v_ref[0, :, ks].astype(BF16)
        for j in range(ATT_GROUP):
            h = g * ATT_GROUP + j
            hs = slice(h * ATT_HEADDIM, (h + 1) * ATT_HEADDIM)
            s = _dot_t(q[:, hs].astype(BF16), k) * scale
            sink = sink_ref[h]
            m = jnp.maximum(jnp.max(s, -1, keepdims=True), sink)
            p = jnp.exp(s - m)
            den = jnp.sum(p, -1, keepdims=True) + jnp.exp(sink - m)
            o_ref[0, :, hs] = (_dot(p.astype(BF16), v) / den * _silu(z[:, hs])).astype(o_ref.dtype)


def _ctx_attention(uc_q, uc_kv, sinks, z_ac):
    b, lc, _ = uc_q.shape
    blk = lambda bi: (bi, 0, 0)
    return pl.pallas_call(
        _cattn_body,
        grid=(b,),
        in_specs=[pl.BlockSpec(memory_space=pltpu.SMEM),
                  pl.BlockSpec((1, lc, ATT_WIDTH), blk),
                  pl.BlockSpec((1, lc, ATT_KV), lambda bi: (bi, 0, 0)),
                  pl.BlockSpec((1, lc, ATT_KV), lambda bi: (bi, 0, 1)),
                  pl.BlockSpec((1, lc, ATT_WIDTH), blk)],
        out_specs=pl.BlockSpec((1, lc, ATT_WIDTH), blk),
        out_shape=jax.ShapeDtypeStruct((b, lc, ATT_WIDTH), BF16),
        compiler_params=_cp("parallel"),
        name="ctx_attention",
    )(sinks, uc_q, uc_kv, uc_kv, z_ac)


def _out_body(h_ref, g_ref, yhy_ref, yd_ref, yf_ref, yb_ref, xs_ref, zs_ref, yat_ref, dsk_ref, nw_ref, w_ref,
              lg_ref, lb_ref, o_ref):
    gw = SSM_WIDTH // SSM_GROUPS
    y_scan = yd_ref[0].astype(F32) + yf_ref[0].astype(F32) + yb_ref[0].astype(F32)
    ys = (y_scan + xs_ref[0].astype(F32) * dsk_ref[...]) * _silu(zs_ref[0].astype(F32))
    parts = [yhy_ref[0].astype(BF16)]
    for g in range(SSM_GROUPS):
        seg = ys[:, g * gw:(g + 1) * gw]
        seg = seg * lax.rsqrt(jnp.mean(seg * seg, -1, keepdims=True) + RMS_EPS) * nw_ref[:, g * gw:(g + 1) * gw]
        parts.append(seg.astype(BF16))
    parts.append(yat_ref[0].astype(BF16))
    acc = _dot(jnp.concatenate(parts, 1), w_ref[...])
    r = DEEPNORM_ALPHA * h_ref[0] + g_ref[0] * acc
    mu = jnp.mean(r, -1, keepdims=True)
    rc = r - mu
    var = jnp.mean(rc * rc, -1, keepdims=True)
    o_ref[0] = rc * lax.rsqrt(var + LN_EPS) * lg_ref[...] + lb_ref[...]


def _out_projection(h, gate_mod, y_hy, y_ssd, xbc, z_s, y_at, d_skip, norm_w, w_out, ln_g, ln_b):
    b, L, d = h.shape
    tm = min(L, PROJ_ROWS)
    row = lambda bi, i: (bi, i, 0)
    vec = lambda bi, i: (bi, 0, 0)
    full = lambda bi, i: (0, 0)
    w512 = pl.BlockSpec((1, tm, SSM_WIDTH), row)
    return pl.pallas_call(
        _out_body,
        grid=(b, L // tm),
        in_specs=[pl.BlockSpec((1, tm, d), row), pl.BlockSpec((1, 1, d), vec),
                  w512, w512, w512, w512, w512, w512, w512,
                  pl.BlockSpec((1, SSM_WIDTH), full), pl.BlockSpec((1, SSM_WIDTH), full),
                  pl.BlockSpec(w_out.shape, full), pl.BlockSpec((1, d), full), pl.BlockSpec((1, d), full)],
        out_specs=pl.BlockSpec((1, tm, d), row),
        out_shape=jax.ShapeDtypeStruct((b, L, d), F32),
        compiler_params=_cp("parallel", "arbitrary"),
        name="out_projection",
    )(h, gate_mod, y_hy, *y_ssd, xbc, z_s, y_at, d_skip, norm_w, w_out.astype(BF16),
      ln_g.reshape(1, d), ln_b.reshape(1, d))


def _sequence_front(h, shift, scale, w_packed, ssm_conv_w, ssm_conv_b, rope_tables=None):
    u_hy3, u_hyg, u_xbc, u_zs, u_q, u_kv, u_za, u_dt = _in_projection(h, shift, scale, w_packed, rope_tables)
    xbc = _dwconv(u_xbc, ssm_conv_w, ssm_conv_b, act=True, split=SSM_CONV_DIM)[0]
    return dict(hy3=u_hy3, hy_gate=u_hyg, xbc=xbc, z_s=u_zs, q=u_q, kv=u_kv, z_a=u_za, dt=u_dt)


def kernel(x, c, ctx, c_ctx, w_mod, b_mod, w_in, hy_conv_w, hy_conv_b, hy_f_w1, hy_f_b1, hy_f_w2, hy_f_b2,
           hy_f_w3, hy_f_b3, hy_f_freq, hy_f_wout, hy_bias, ssm_conv_w, ssm_conv_b, ssm_dt_bias, ssm_a_log,
           ssm_d, ssm_norm_w, attn_sinks, w_out, ln_g, ln_b):
    b, L, d = x.shape
    lc = ctx.shape[1]
    cos, sin = _rope_tables(L)
    cc = jnp.concatenate([c, c_ctx[None], jnp.zeros((16 - b - 1, d), F32)], 0)
    zero_state = jnp.zeros((b, 2, SSM_GROUPS, SSM_STATE, SSM_WIDTH // SSM_GROUPS), F32)
    h_lat, h_ctx = x, ctx
    for i in range(DEPTH):
        ctx_needed = i < DEPTH - 1
        mod = _modulation(cc, w_mod[i], b_mod[i])
        sh, sc, g = (mod[:b, None, j * d:(j + 1) * d] for j in range(3))
        sh_c, sc_c, g_c = (jnp.broadcast_to(mod[b:b + 1, None, j * d:(j + 1) * d], (b, 1, d)) for j in range(3))
        w_packed = _pack_w_in(w_in[i])
        lat = _sequence_front(h_lat, sh, sc, w_packed, ssm_conv_w[i], ssm_conv_b[i], (cos, sin))
        cx = _sequence_front(h_ctx, sh_c, sc_c, w_packed, ssm_conv_w[i], ssm_conv_b[i])

        dt_bias_row = jnp.pad(ssm_dt_bias[i].reshape(1, -1), ((0, 0), (0, LANES - 2 * SSM_HEADS)))
        a_row = jnp.pad(-jnp.exp(ssm_a_log[i]).reshape(1, -1), ((0, 0), (0, LANES - 2 * SSM_HEADS)))
        ys_c, s_c = _ssd(cx["xbc"], cx["dt"], dt_bias_row, a_row, zero_state)
        ys, _ = _ssd(lat["xbc"], lat["dt"], dt_bias_row, a_row, s_c)

        filt = (hy_f_w1[i], hy_f_b1[i], hy_f_w2[i], hy_f_b2[i], hy_f_w3[i], hy_f_b3[i], hy_f_freq[i], hy_f_wout[i])
        kf = _hyena_spectrum(L, _hyena_filter_taps(L, *filt))
        y_hy = _hyena(lat["hy3"], lat["hy_gate"], hy_conv_w[i], hy_conv_b[i], kf, hy_bias[i])

        y_at = _window_attention(lat["q"], lat["kv"], cx["kv"], attn_sinks[i], lat["z_a"])

        d_skip = jnp.repeat(ssm_d[i], SSM_HEADDIM).reshape(1, SSM_WIDTH)
        norm_w = ssm_norm_w[i].reshape(1, SSM_WIDTH)
        new_lat = _out_projection(h_lat, g, y_hy, ys, lat["xbc"], lat["z_s"], y_at, d_skip, norm_w,
                                  w_out[i], ln_g[i], ln_b[i])
        if ctx_needed:
            kf_c = _hyena_spectrum(lc, _hyena_filter_taps(lc, *filt))
            y_hy_c = _hyena(cx["hy3"], cx["hy_gate"], hy_conv_w[i], hy_conv_b[i], kf_c, hy_bias[i])
            y_at_c = _ctx_attention(cx["q"], cx["kv"], attn_sinks[i], cx["z_a"])
            h_ctx = _out_projection(h_ctx, g_c, y_hy_c, ys_c, cx["xbc"], cx["z_s"], y_at_c, d_skip,
                                    norm_w, w_out[i], ln_g[i], ln_b[i])
        h_lat = new_lat
    return h_lat
```

```python
import functools
import math

import numpy as np
import jax
import jax.numpy as jnp
from jax import lax
from jax.experimental import pallas as pl
from jax.experimental.pallas import tpu as pltpu

F32 = jnp.float32
BF16 = jnp.bfloat16
HI = lax.Precision.HIGHEST

D_MODEL = 1024
DEPTH = 2
GRID_W = 64
HY_WIDTH = 512
HY_BANDS = 16
HY_EMB = 1 + 2 * HY_BANDS
HY_FILTER_HIDDEN = 64
HY_DECAY_TARGET = 1e-2
HY_FAST_PCT = 0.3
HY_SLOW_PCT = 1.5
SSM_WIDTH = 512
SSM_HEADS = 8
SSM_HEADDIM = 64
SSM_GROUPS = 2
SSM_STATE = 128
SSM_CHUNK = 128
SSM_CONV_DIM = SSM_WIDTH + 2 * SSM_GROUPS * SSM_STATE
ATT_WIDTH = 512
ATT_HEADS = 8
ATT_KV_HEADS = 2
ATT_HEADDIM = 64
ATT_GROUP = ATT_HEADS // ATT_KV_HEADS
ATT_KV = ATT_KV_HEADS * ATT_HEADDIM
WINDOW = 128
ROPE_BASE = 10000.0
HY_IN = 4 * HY_WIDTH
SSM_IN = SSM_CONV_DIM + SSM_WIDTH + 2 * SSM_HEADS
DEEPNORM_ALPHA = (2 * DEPTH) ** 0.25
LN_EPS = 1e-6
RMS_EPS = 1e-5

LANES = 128
VMEM_LIMIT = 56 * 1024 * 1024
NEG = -1e30
BIG = 1e30
LOG2E = math.log2(math.e)
FILTER_ROWS = 256
PROJ_ROWS = 512
HY_STEP_ROWS = 8192
SECOND_STAGE_UNROLL = 11


def _cp(*sem):
    return pltpu.CompilerParams(dimension_semantics=sem, vmem_limit_bytes=VMEM_LIMIT)


def _silu(x):
    return (0.5 * x) * (1.0 + jnp.tanh(0.5 * x))


def _dot(a, b, precision=None):
    return jnp.dot(a, b, preferred_element_type=F32, precision=precision)


def _dot_t(a, b):
    return lax.dot_general(a, b, (((1,), (1,)), ((), ())), preferred_element_type=F32)


def _mod_body(c_ref, w_ref, b_ref, o_ref):
    o_ref[...] = _dot(_silu(c_ref[...]), w_ref[...], HI) + b_ref[...]


def _modulation(cc, w, b):
    rows, d = cc.shape
    n = w.shape[1]
    tn = 1024
    return pl.pallas_call(
        _mod_body,
        grid=(n // tn,),
        in_specs=[pl.BlockSpec((rows, d), lambda j: (0, 0)),
                  pl.BlockSpec((d, tn), lambda j: (0, j)),
                  pl.BlockSpec((1, tn), lambda j: (0, j))],
        out_specs=pl.BlockSpec((rows, tn), lambda j: (0, j)),
        out_shape=jax.ShapeDtypeStruct((rows, n), F32),
        compiler_params=_cp("arbitrary"),
        name="modulation",
    )(cc, w, b.reshape(1, n))


IN_SEGS = (3 * HY_WIDTH, HY_WIDTH, SSM_CONV_DIM, SSM_WIDTH, ATT_WIDTH, 2 * ATT_KV, ATT_WIDTH, LANES)
SEG_Q, SEG_KV = 4, 5
IN_DTYPES = (BF16,) * (len(IN_SEGS) - 1) + (F32,)
IN_CHUNK = 512


def _pack_w_in(w):
    o_ss = HY_IN
    o_at = HY_IN + SSM_IN
    dt = w[:, o_ss + SSM_CONV_DIM + SSM_WIDTH:o_at]
    parts = [w[:, :HY_IN], w[:, o_ss:o_ss + SSM_CONV_DIM + SSM_WIDTH], w[:, o_at:],
             dt, jnp.zeros((w.shape[0], LANES - dt.shape[1]), w.dtype)]
    return jnp.concatenate(parts, axis=1).astype(BF16)


def _inproj_body(h_ref, sh_ref, sc_ref, w_ref, *rest, rope):
    tabs, o_refs = (rest[:2], rest[2:]) if rope else ((), rest)
    x = h_ref[0]
    mu = jnp.mean(x, -1, keepdims=True)
    xc = x - mu
    var = jnp.mean(xc * xc, -1, keepdims=True)
    xm = (xc * lax.rsqrt(var + LN_EPS) * (1.0 + sc_ref[0]) + sh_ref[0]).astype(BF16)
    off = 0
    for seg, (o_ref, n) in enumerate(zip(o_refs, IN_SEGS)):
        for j in range(0, n, IN_CHUNK):
            w = min(IN_CHUNK, n - j)
            r = _dot(xm, w_ref[:, off + j:off + j + w])
            if rope and seg == SEG_Q:
                r = _rope(r, tabs[0][...], tabs[1][...]) * (ATT_HEADDIM ** -0.5 * LOG2E)
            if rope and seg == SEG_KV:
                o_ref[0, :, :ATT_KV] = _rope(r[:, :ATT_KV], tabs[0][:, :ATT_KV], tabs[1][:, :ATT_KV]).astype(o_ref.dtype)
                o_ref[0, :, ATT_KV:] = r[:, ATT_KV:].astype(o_ref.dtype)
            else:
                o_ref[0, :, j:j + w] = r.astype(o_ref.dtype)
        off += n


def _in_projection(h, shift, scale, w_packed, rope_tables=None):
    b, L, d = h.shape
    tm = min(L, PROJ_ROWS)
    n_all = w_packed.shape[1]
    row = lambda bi, i: (bi, i, 0)
    vec = lambda bi, i: (bi, 0, 0)
    rope = rope_tables is not None
    tab_specs = [pl.BlockSpec((tm, ATT_WIDTH), lambda bi, i: (i, 0))] * 2 if rope else []
    return pl.pallas_call(
        functools.partial(_inproj_body, rope=rope),
        grid=(b, L // tm),
        in_specs=[pl.BlockSpec((1, tm, d), row), pl.BlockSpec((1, 1, d), vec), pl.BlockSpec((1, 1, d), vec),
                  pl.BlockSpec((d, n_all), lambda bi, i: (0, 0))] + tab_specs,
        out_specs=[pl.BlockSpec((1, tm, n), row) for n in IN_SEGS],
        out_shape=[jax.ShapeDtypeStruct((b, L, n), dt) for n, dt in zip(IN_SEGS, IN_DTYPES)],
        compiler_params=_cp("parallel", "arbitrary"),
        name="in_projection",
    )(h, shift, scale, w_packed, *(rope_tables or ()))


def _dwconv_body(u_ref, w_ref, b_ref, o_ref, *, act):
    x = u_ref[0].astype(F32)
    L = x.shape[0]
    row = lax.broadcasted_iota(jnp.int32, x.shape, 0)
    prev = jnp.where(row == 0, 0.0, pltpu.roll(x, 1, 0))
    nxt = jnp.where(row == L - 1, 0.0, pltpu.roll(x, L - 1, 0))
    y = prev * w_ref[0:1, :] + x * w_ref[1:2, :] + nxt * w_ref[2:3, :] + b_ref[...]
    if act:
        y = _silu(y)
    o_ref[0, 0] = y.astype(o_ref.dtype)


def _dwconv(u, w, bias, *, act, split):
    b, L, c = u.shape
    tc = 256
    per = split // tc
    return pl.pallas_call(
        functools.partial(_dwconv_body, act=act),
        grid=(b, c // tc),
        in_specs=[pl.BlockSpec((1, L, tc), lambda bi, j: (bi, 0, j)),
                  pl.BlockSpec((3, tc), lambda bi, j: (0, j)),
                  pl.BlockSpec((1, tc), lambda bi, j: (0, j))],
        out_specs=pl.BlockSpec((1, 1, L, tc), lambda bi, j: (j // per, bi, 0, j % per)),
        out_shape=jax.ShapeDtypeStruct((c // split, b, L, split), BF16),
        compiler_params=_cp("parallel", "arbitrary"),
        name="dwconv",
    )(u, w, bias.reshape(1, c))


def _filter_features(L):
    t = jnp.linspace(0.0, 1.0, L, dtype=F32)[:, None]
    w = 2.0 * math.pi * jnp.arange(L, dtype=F32)[:, None] / L
    f = jnp.linspace(1e-4, HY_BANDS - 1, HY_BANDS, dtype=F32)[None]
    z = jnp.concatenate([t, jnp.cos(f * w), -jnp.sin(f * w)], -1)
    return jnp.pad(z, ((0, 0), (0, LANES - HY_EMB)))


def _pad_to(a, rows, cols):
    return jnp.pad(a, ((0, rows - a.shape[0]), (0, cols - a.shape[1])))


def _filter_body(z_ref, w1_ref, w2_ref, w3_ref, b_ref, fr_ref, wo_ref, ad_ref, o_ref, *, nblk):
    i = pl.program_id(0)

    @pl.when(i < nblk)
    def _():
        z = z_ref[...]
        fr = fr_ref[...]
        h = jnp.sin(fr * (_dot(z, w1_ref[...], HI) + b_ref[0:1, :]))
        h = jnp.sin(fr * (_dot(h, w2_ref[...], HI) + b_ref[1:2, :]))
        h = jnp.sin(fr * (_dot(h, w3_ref[...], HI) + b_ref[2:3, :]))
        win = jnp.exp(-z[:, 0:1] * ad_ref[...])
        h_hi = h.astype(BF16)
        h_lo = (h - h_hi.astype(F32)).astype(BF16)
        for j in range(4):
            cols = slice(j * HY_WIDTH, (j + 1) * HY_WIDTH)
            taps = _dot(h_hi, wo_ref[0, :, cols]) + _dot(h_lo, wo_ref[0, :, cols]) + _dot(h_hi, wo_ref[1, :, cols])
            o_ref[:, cols] = taps * win

    @pl.when(i == nblk)
    def _():
        o_ref[...] = jnp.zeros_like(o_ref)


def _hyena_filter_taps(L, w1, b1, w2, b2, w3, b3, freq, w_out):
    z = _filter_features(L)
    hp = LANES
    bias = jnp.stack([jnp.pad(b, (0, hp - b.shape[0])) for b in (b1, b2, b3)])
    bias = jnp.pad(bias, ((0, 5), (0, 0)))
    fr = jnp.pad(freq, (0, hp - freq.shape[0])).reshape(1, hp)
    max_decay = math.log(HY_DECAY_TARGET) / HY_FAST_PCT
    min_decay = math.log(HY_DECAY_TARGET) / HY_SLOW_PCT
    absd = jnp.abs(jnp.linspace(min_decay, max_decay, HY_WIDTH, dtype=F32)).reshape(1, HY_WIDTH)
    tl = FILTER_ROWS
    nblk = L // tl
    n = 4 * HY_WIDTH
    full = lambda i: (0, 0)
    return pl.pallas_call(
        functools.partial(_filter_body, nblk=nblk),
        grid=(nblk + 1,),
        in_specs=[pl.BlockSpec((tl, hp), lambda i: (jnp.minimum(i, nblk - 1), 0)),
                  pl.BlockSpec((hp, hp), full), pl.BlockSpec((hp, hp), full), pl.BlockSpec((hp, hp), full),
                  pl.BlockSpec((8, hp), full), pl.BlockSpec((1, hp), full),
                  pl.BlockSpec((2, hp, n), lambda i: (0, 0, 0)), pl.BlockSpec((1, HY_WIDTH), full)],
        out_specs=pl.BlockSpec((tl, n), lambda i: (i, 0)),
        out_shape=jax.ShapeDtypeStruct((L + tl, n), F32),
        compiler_params=_cp("arbitrary"),
        name="hyena_filter",
    )(z, _pad_to(w1, hp, hp), _pad_to(w2, hp, hp), _pad_to(w3, hp, hp), bias, fr, _split2(_pad_to(w_out, hp, n)),
      absd)


def _hy_cfg(L):
    n2 = 128 if L >= 2048 else 16
    n1 = 2 * L // n2
    k1n = n1 // 2 + 1
    jp = -(-2 * k1n // 16) * 16
    pitch = n2 + 8
    return dict(L=L, n2=n2, n1=n1, nh=n1 // 2, k1n=k1n, jp=jp, pitch=pitch)


@functools.lru_cache(maxsize=None)
def _dft_tables(L):
    cfg = _hy_cfg(L)
    n, n1, n2, nh, k1n, jp = 2 * L, cfg["n1"], cfg["n2"], cfg["nh"], cfg["k1n"], cfg["jp"]
    a_n1 = np.arange(n1)
    a_k1 = np.arange(k1n)
    th = 2 * np.pi * np.outer(a_k1, a_n1) / n1
    f1 = np.zeros((jp, n1))
    f1[0:2 * k1n:2] = np.cos(th)
    f1[1:2 * k1n:2] = -np.sin(th)
    a_n2 = np.arange(n2)
    m1 = np.zeros((k1n, 2 * n2, 2 * n2))
    for k in range(k1n):
        f = np.exp(-2j * np.pi * (np.outer(a_n2, a_n2) / n2 + a_n2[None, :] * k / n))
        m1[k] = np.block([[f.real, -f.imag], [f.imag, f.real]])
    ck = np.full(k1n, 2.0)
    ck[0] = 1.0
    ck[-1] = 1.0
    th6 = 2 * np.pi * np.outer(np.arange(nh), a_k1) / n1
    f6 = np.zeros((nh, jp))
    f6[:, 0:2 * k1n:2] = ck * np.cos(th6) / n
    f6[:, 1:2 * k1n:2] = -ck * np.sin(th6) / n
    as32 = lambda a: np.asarray(a, np.float32)
    return dict(f1=as32(f1), m1=as32(m1), f6=as32(f6))


def _first_stage(src_ref, a_scr, f1, cfg):
    n2, nh, jp, pitch = cfg["n2"], cfg["nh"], cfg["jp"], cfg["pitch"]
    bt = src_ref.shape[0]

    def step(i, carry):
        xs = jnp.concatenate([src_ref[t, pl.ds(2 * i + u, nh, stride=pitch), :] for t in range(bt) for u in range(2)], 1)
        r = _dot(f1, xs.astype(BF16))
        for t in range(bt):
            for u in range(2):
                lo = (2 * t + u) * LANES
                a_scr[t, pl.ds(2 * i + u, jp, stride=pitch), :] = r[:, lo:lo + LANES]
        return carry

    lax.fori_loop(0, n2 // 2, step, 0, unroll=min(16 // bt, n2 // 2))


def _k1_rows(a_scr, k, cfg):
    n2, pitch = cfg["n2"], cfg["pitch"]
    base = pl.multiple_of(2 * k * pitch, 8)
    parts = [jnp.concatenate([a_scr[t, pl.ds(base, n2), :], a_scr[t, pl.ds(base + pitch, n2), :]], 0)
             for t in range(a_scr.shape[0])]
    return base, jnp.concatenate(parts, 1)


def _split2(table):
    t = jnp.asarray(table)
    hi = t.astype(BF16)
    return jnp.stack([hi, (t - hi.astype(F32)).astype(BF16)])


def _spectrum_body(kf_ref, kb_ref, f1_ref, m1_ref, o_ref, a_scr, *, cfg):
    n2, nh, jp, pitch = cfg["n2"], cfg["nh"], cfg["jp"], cfg["pitch"]

    def dot3(m_hi, m_lo, x):
        x_hi = x.astype(BF16)
        x_lo = (x - x_hi.astype(F32)).astype(BF16)
        return _dot(m_hi, x_hi) + _dot(m_lo, x_hi) + _dot(m_hi, x_lo)

    def step(i, carry):
        cols = []
        for t in range(2):
            n = 2 * i + t
            cols.append(jnp.concatenate([kf_ref[pl.ds(n, nh, stride=n2), :],
                                         kb_ref[pl.ds(n2 - n, nh, stride=n2), :]], 0))
        r = dot3(f1_ref[0], f1_ref[1], jnp.concatenate(cols, 1))
        a_scr[0, pl.ds(2 * i, jp, stride=pitch), :] = r[:, :LANES]
        a_scr[0, pl.ds(2 * i + 1, jp, stride=pitch), :] = r[:, LANES:]
        return carry

    lax.fori_loop(0, n2 // 2, step, 0, unroll=min(16, n2 // 2))
    lag0 = pl.ds(0, jp, stride=pitch)
    a_scr[0, lag0, :] = a_scr[0, lag0, :] + f1_ref[0, :, 0:1].astype(F32) * kb_ref[0:1, :]

    def mid(k, carry):
        _, a = _k1_rows(a_scr, k, cfg)
        o_ref[0, k] = dot3(m1_ref[0, k], m1_ref[1, k], a).astype(BF16)
        return carry

    lax.fori_loop(0, cfg["k1n"], mid, 0, unroll=SECOND_STAGE_UNROLL)


def _hyena_spectrum(L, taps):
    c = HY_WIDTH
    cfg = _hy_cfg(L)
    tb = _dft_tables(L)
    n1, n2, nh, k1n = cfg["n1"], cfg["n2"], cfg["nh"], cfg["k1n"]
    f1 = _split2(np.concatenate([tb["f1"][:, :nh], tb["f1"][:, n1 - 1:nh - 1:-1]], 1))
    m1 = _split2(tb["m1"])
    nct = c // LANES
    rows = taps.shape[0]
    return pl.pallas_call(
        functools.partial(_spectrum_body, cfg=cfg),
        grid=(2, nct),
        in_specs=[pl.BlockSpec((rows, LANES), lambda cv, j: (0, 2 * cv * nct + j)),
                  pl.BlockSpec((rows, LANES), lambda cv, j: (0, (2 * cv + 1) * nct + j)),
                  pl.BlockSpec(f1.shape, lambda cv, j: (0, 0, 0)),
                  pl.BlockSpec(m1.shape, lambda cv, j: (0, 0, 0, 0))],
        out_specs=pl.BlockSpec((1, k1n, 2 * n2, LANES), lambda cv, j: (cv, 0, 0, j)),
        out_shape=jax.ShapeDtypeStruct((2, k1n, 2 * n2, c), BF16),
        scratch_shapes=[pltpu.VMEM((1, cfg["jp"] * cfg["pitch"], LANES), F32)],
        compiler_params=_cp("arbitrary", "arbitrary"),
        name="hyena_spectrum",
    )(taps, taps, f1, m1)


def _short_conv(u_ref, w_ref, b_ref, which):
    x = u_ref[...].astype(F32)
    L = x.shape[0]
    edge = lax.broadcasted_iota(jnp.int32, (8, x.shape[1]), 0)
    prev = pltpu.roll(x, 1, 0)
    prev = jnp.concatenate([jnp.where(edge == 0, 0.0, prev[:8]), prev[8:]], 0)
    nxt = pltpu.roll(x, L - 1, 0)
    nxt = jnp.concatenate([nxt[:L - 8], jnp.where(edge == 7, 0.0, nxt[L - 8:])], 0)
    sel = slice(which, which + 1)
    return prev * w_ref[0, sel, :] + x * w_ref[1, sel, :] + nxt * w_ref[2, sel, :] + b_ref[sel, :]


def _long_conv(src_scr, a_scr, y_scr, w_scr, f1_ref, f6_ref, m1_ref, kf_ref, conv, cfg):
    n2, nh, jp, pitch = cfg["n2"], cfg["nh"], cfg["jp"], cfg["pitch"]
    bt = src_scr.shape[0]
    _first_stage(src_scr, a_scr, f1_ref[...], cfg)

    def forward(k, carry):
        _, a = _k1_rows(a_scr, k, cfg)
        x = _dot(m1_ref[k], a.astype(BF16))
        kk = kf_ref[conv, k].astype(F32)
        kr, ki = kk[:n2], kk[n2:]
        cols = []
        for t in range(bt):
            xr, xi = x[:n2, t * LANES:(t + 1) * LANES], x[n2:, t * LANES:(t + 1) * LANES]
            cols.append(jnp.concatenate([xr * kr - xi * ki, xr * ki + xi * kr], 0))
        w_scr[k] = jnp.concatenate(cols, 1).astype(BF16)
        return carry

    def inverse(k, carry):
        base = pl.multiple_of(2 * k * pitch, 8)
        b = lax.dot_general(m1_ref[k], w_scr[k], (((0,), (0,)), ((), ())), preferred_element_type=F32)
        for t in range(bt):
            a_scr[t, pl.ds(base, n2), :] = b[:n2, t * LANES:(t + 1) * LANES]
            a_scr[t, pl.ds(base + pitch, n2), :] = b[n2:, t * LANES:(t + 1) * LANES]
        return carry

    lax.fori_loop(0, cfg["k1n"], forward, 0, unroll=SECOND_STAGE_UNROLL)
    lax.fori_loop(0, cfg["k1n"], inverse, 0, unroll=SECOND_STAGE_UNROLL)
    f6 = f6_ref[...]

    def last(i, carry):
        bs = jnp.concatenate([a_scr[t, pl.ds(2 * i + u, jp, stride=pitch), :] for t in range(bt) for u in range(2)], 1)
        y = _dot(f6, bs.astype(BF16))
        for t in range(bt):
            for u in range(2):
                lo = (2 * t + u) * LANES
                y_scr[t, pl.ds(2 * i + u, nh, stride=pitch), :] = y[:, lo:lo + LANES]
        return carry

    lax.fori_loop(0, n2 // 2, last, 0, unroll=min(16 // bt, n2 // 2))


def _hyena_body(v_ref, x1_ref, x2_ref, g_ref, cw_ref, cb_ref, hb_ref, f1_ref, f6_ref, m1_ref, kf_ref,
                o_ref, s_scr, a_scr, y_scr, w_scr, *, cfg):
    tabs = (w_scr, f1_ref, f6_ref, m1_ref, kf_ref)
    n2, nh, pitch = cfg["n2"], cfg["nh"], cfg["pitch"]
    slots = range(s_scr.shape[0])

    def put(scr, t, val):
        for n1 in range(nh):
            scr[t, n1 * pitch:n1 * pitch + n2, :] = val[n1 * n2:(n1 + 1) * n2]

    def get(scr, t):
        return jnp.concatenate([scr[t, n1 * pitch:n1 * pitch + n2, :] for n1 in range(nh)], 0)

    for t in slots:
        put(s_scr, t, _short_conv(v_ref.at[t], cw_ref, cb_ref, 0))
    _long_conv(s_scr, a_scr, y_scr, *tabs, 0, cfg)
    for t in slots:
        put(s_scr, t, _short_conv(x1_ref.at[t], cw_ref, cb_ref, 1) * (get(y_scr, t) + get(s_scr, t) * hb_ref[0:1, :]))
    _long_conv(s_scr, a_scr, y_scr, *tabs, 1, cfg)
    for t in slots:
        y = _short_conv(x2_ref.at[t], cw_ref, cb_ref, 2) * (get(y_scr, t) + get(s_scr, t) * hb_ref[1:2, :])
        o_ref[t] = (y * _silu(g_ref[t].astype(F32))).astype(o_ref.dtype)


def _hyena(u_hy3, u_gate, conv_w, conv_b, kf, hy_bias):
    b, L, _ = u_gate.shape
    c = HY_WIDTH
    cfg = _hy_cfg(L)
    tb = _dft_tables(L)
    nct = c // LANES
    bt = max(t for t in (1, 2, 4, 8) if b % t == 0 and t * L <= max(L, HY_STEP_ROWS))
    col = lambda off: pl.BlockSpec((bt, L, LANES), lambda j, bi: (bi, 0, off * nct + j))
    full = lambda a: pl.BlockSpec(a.shape, lambda j, bi: (0,) * a.ndim)
    f1 = jnp.asarray(tb["f1"][:, :cfg["nh"]]).astype(BF16)
    f6 = jnp.asarray(tb["f6"]).astype(BF16)
    m1 = jnp.asarray(tb["m1"]).astype(BF16)
    return pl.pallas_call(
        functools.partial(_hyena_body, cfg=cfg),
        grid=(nct, b // bt),
        in_specs=[col(0), col(1), col(2), col(0),
                  pl.BlockSpec((3, 3, LANES), lambda j, bi: (0, 0, j)),
                  pl.BlockSpec((3, LANES), lambda j, bi: (0, j)),
                  pl.BlockSpec((2, LANES), lambda j, bi: (0, j)),
                  full(f1), full(f6), full(m1),
                  pl.BlockSpec((2, cfg["k1n"], 2 * cfg["n2"], LANES), lambda j, bi: (0, 0, 0, j),
                               pipeline_mode=pl.Buffered(1))],
        out_specs=col(0),
        out_shape=jax.ShapeDtypeStruct((b, L, c), BF16),
        scratch_shapes=[pltpu.VMEM((bt, cfg["nh"] * cfg["pitch"], LANES), F32),
                        pltpu.VMEM((bt, cfg["jp"] * cfg["pitch"], LANES), F32),
                        pltpu.VMEM((bt, cfg["nh"] * cfg["pitch"], LANES), F32),
                        pltpu.VMEM((cfg["k1n"], 2 * cfg["n2"], bt * LANES), BF16)],
        compiler_params=_cp("arbitrary", "arbitrary"),
        name="hyena",
    )(u_hy3, u_hy3, u_hy3, u_gate, conv_w.reshape(3, 3, c), conv_b.reshape(3, c), hy_bias, f1, f6, m1, kf)


SSD_CPS = 4
SSD_SPS = 4
SPLIT_STRIDE = 2 * SSM_HEADS


def _pack3(x):
    hi = x.astype(BF16).astype(F32)
    r1 = x - hi
    mid = r1.astype(BF16).astype(F32)
    lo = (r1 - mid).astype(BF16).astype(F32)
    return (hi + pltpu.roll(mid, SPLIT_STRIDE, 1) + pltpu.roll(lo, 2 * SPLIT_STRIDE, 1)).astype(BF16)


def _unpack3(x3, used):
    return jnp.where(used, x3 + pltpu.roll(x3, LANES - SPLIT_STRIDE, 1) + pltpu.roll(x3, LANES - 2 * SPLIT_STRIDE, 1), 0.0)


@functools.lru_cache(maxsize=None)
def _ssd_spread_tables():
    col = np.zeros((LANES, 2 * SSM_HEADS * LANES), np.float32)
    head = np.zeros((LANES, 2 * SSM_WIDTH), np.float32)
    for c in range(2 * SSM_HEADS):
        d, h = divmod(c, SSM_HEADS)
        for piece in range(3):
            col[c + piece * SPLIT_STRIDE, c * LANES:(c + 1) * LANES] = 1.0
            lo = d * SSM_WIDTH + h * SSM_HEADDIM
            head[c + piece * SPLIT_STRIDE, lo:lo + SSM_HEADDIM] = 1.0
    return col, head


def _ssd_chunk_body(xbc_ref, dtr_ref, dtb_ref, a_ref, ecol_ref, ehead_ref, yd_ref, cs_ref, ex_ref, et_ref):
    q = SSM_CHUNK
    hpg = SSM_HEADS // SSM_GROUPS
    gw = SSM_WIDTH // SSM_GROUPS
    li = lax.broadcasted_iota(jnp.int32, (q, q), 0)
    si = lax.broadcasted_iota(jnp.int32, (q, q), 1)
    below = li > si
    diag = li == si
    fwd_lane = si < SSM_HEADS
    used = si < 2 * SSM_HEADS
    tril = (li >= si).astype(BF16)
    triu = (li <= si).astype(BF16)
    chunks = range(cs_ref.shape[1])
    rows = [slice(c * q, (c + 1) * q) for c in chunks]
    ehead = ehead_ref[...]

    raws = [dtr_ref[0, r, :] + dtb_ref[...] for r in rows]
    dts = [jnp.where(used, jnp.maximum(x, 0.0) + jnp.log1p(jnp.exp(-jnp.abs(x))), 0.0) for x in raws]
    da3 = [_pack3(dt * a_ref[...]) for dt in dts]
    acs = [jnp.where(fwd_lane, _unpack3(_dot(tril, x), used), _unpack3(_dot(triu, x), used)) for x in da3]
    tots = [jnp.where(fwd_lane[0:1], a[q - 1:q, :], a[0:1, :]) for a in acs]
    ws = [dt * jnp.exp(t - a) for dt, t, a in zip(dts, tots, acs)]
    acs3 = [_pack3(a) for a in acs]
    colb = [_dot(x, ecol_ref[...]) for x in acs3]
    wx = [_dot(_pack3(w), ehead) for w in ws]
    for c in chunks:
        ex_ref[0, rows[c], :] = jnp.exp(_dot(acs3[c], ehead)).astype(BF16)
        et_ref[0, c] = jnp.exp(_dot(_pack3(jnp.broadcast_to(tots[c], (8, LANES))), ehead))
    rowt = [(a - jnp.where(dt > 0.0, jnp.log(dt), -BIG)).T for a, dt in zip(acs, dts)]
    dsum = [(dt + pltpu.roll(dt, LANES - SSM_HEADS, 1)).T for dt in dts]
    xbc = [xbc_ref[0, r, :] for r in rows]
    xsb = [x[:, :SSM_WIDTH].astype(BF16) for x in xbc]
    bgs = [[x[:, SSM_WIDTH + g * SSM_STATE:SSM_WIDTH + (g + 1) * SSM_STATE] for g in range(SSM_GROUPS)] for x in xbc]
    cgs = [[x[:, SSM_WIDTH + (SSM_GROUPS + g) * SSM_STATE:SSM_WIDTH + (SSM_GROUPS + g + 1) * SSM_STATE]
            for g in range(SSM_GROUPS)] for x in xbc]
    gmat = [[_dot_t(cgs[c][g].astype(BF16), bgs[c][g].astype(BF16)) for g in range(SSM_GROUPS)] for c in chunks]
    mats = []
    for c in chunks:
        for h in range(SSM_HEADS):
            hb = SSM_HEADS + h
            arg = jnp.where(below, colb[c][:, h * q:(h + 1) * q] - rowt[c][h:h + 1, :],
                            colb[c][:, hb * q:(hb + 1) * q] - rowt[c][hb:hb + 1, :])
            dec = jnp.where(diag, dsum[c][h:h + 1, :], jnp.exp(arg))
            mats.append((gmat[c][h // hpg] * dec).astype(BF16))
    for c in chunks:
        for h in range(SSM_HEADS):
            lo = h * SSM_HEADDIM
            yd_ref[0, rows[c], lo:lo + SSM_HEADDIM] = _dot(mats[c * SSM_HEADS + h],
                                                             xsb[c][:, lo:lo + SSM_HEADDIM]).astype(yd_ref.dtype)
    for c in chunks:
        for g in range(SSM_GROUPS):
            bgt = bgs[c][g].astype(F32).T.astype(BF16)
            xg = xbc[c][:, g * gw:(g + 1) * gw].astype(F32)
            for d in range(2):
                lo = d * SSM_WIDTH + g * gw
                cs_ref[0, c, d, g] = _dot(bgt, (xg * wx[c][:, lo:lo + gw]).astype(BF16)).astype(cs_ref.dtype)


def _ssd_chunks(xbc, dt_raw, dt_bias_row, a_row):
    b, L, _ = xbc.shape
    nc = L // SSM_CHUNK
    cps = math.gcd(nc, SSD_CPS)
    rows = cps * SSM_CHUNK
    gw = SSM_WIDTH // SSM_GROUPS
    blk = lambda bi, i: (bi, i, 0)
    full2 = lambda bi, i: (0, 0)
    ecol, ehead = (jnp.asarray(t).astype(BF16) for t in _ssd_spread_tables())
    return pl.pallas_call(
        _ssd_chunk_body,
        grid=(b, nc // cps),
        in_specs=[pl.BlockSpec((1, rows, SSM_CONV_DIM), blk), pl.BlockSpec((1, rows, LANES), blk),
                  pl.BlockSpec((1, LANES), full2), pl.BlockSpec((1, LANES), full2),
                  pl.BlockSpec(ecol.shape, full2), pl.BlockSpec(ehead.shape, full2)],
        out_specs=[pl.BlockSpec((1, rows, SSM_WIDTH), blk),
                   pl.BlockSpec((1, cps, 2, SSM_GROUPS, SSM_STATE, gw), lambda bi, i: (bi, i, 0, 0, 0, 0)),
                   pl.BlockSpec((1, rows, 2 * SSM_WIDTH), blk),
                   pl.BlockSpec((1, cps, 8, 2 * SSM_WIDTH), lambda bi, i: (bi, i, 0, 0))],
        out_shape=[jax.ShapeDtypeStruct((b, L, SSM_WIDTH), BF16),
                   jax.ShapeDtypeStruct((b, nc, 2, SSM_GROUPS, SSM_STATE, gw), BF16),
                   jax.ShapeDtypeStruct((b, L, 2 * SSM_WIDTH), BF16),
                   jax.ShapeDtypeStruct((b, nc, 8, 2 * SSM_WIDTH), F32)],
        compiler_params=_cp("parallel", "arbitrary"),
        name="ssd_chunks",
    )(xbc, dt_raw, dt_bias_row, a_row, ecol, ehead)


def _ssd_state_body(cf_ref, cb_ref, xf_ref, xb_ref, ef_ref, eb_ref, sf_ref, sb_ref, init_ref, yf_ref, yb_ref, fin_ref,
                    st_ref, *, nsteps):
    ci = pl.program_id(1)

    @pl.when(ci == 0)
    def _():
        st_ref[...] = init_ref[0]

    q = SSM_CHUNK
    gw = SSM_WIDTH // SSM_GROUPS
    sps = sf_ref.shape[1]
    for s in range(sps):
        dirs = ((cf_ref, xf_ref, ef_ref, sf_ref, yf_ref, s), (cb_ref, xb_ref, eb_ref, sb_ref, yb_ref, sps - 1 - s))
        for d, (c_ref, x_ref, e_ref, s_ref, y_ref, j) in enumerate(dirs):
            rows = slice(j * q, (j + 1) * q)
            cmat = c_ref[0, rows, :].astype(BF16)
            for g in range(SSM_GROUPS):
                cols = slice(g * gw, (g + 1) * gw)
                st = st_ref[d, g]
                y_ref[0, rows, cols] = (_dot(cmat[:, g * SSM_STATE:(g + 1) * SSM_STATE], st.astype(BF16))
                                        * x_ref[0, rows, cols].astype(F32)).astype(y_ref.dtype)
                st_ref[d, g] = st * e_ref[0, j, 0:1, cols] + s_ref[0, j, 0, g].astype(F32)

    @pl.when(ci == nsteps - 1)
    def _():
        fin_ref[0] = st_ref[...]


def _ssd_states(xbc, ex, et, cs, init):
    b, L, _ = xbc.shape
    nc = L // SSM_CHUNK
    gw = SSM_WIDTH // SSM_GROUPS
    c_col = SSM_CONV_DIM // (SSM_GROUPS * SSM_STATE) - 1
    st_shape = (2, SSM_GROUPS, SSM_STATE, gw)
    st_spec = pl.BlockSpec((1,) + st_shape, lambda bi, c: (bi, 0, 0, 0, 0))
    sps = math.gcd(nc, SSD_SPS)
    nsteps = nc // sps
    rows = sps * SSM_CHUNK
    cs_blk = (1, sps, 1, SSM_GROUPS, SSM_STATE, gw)
    fwd = lambda *tail: (lambda bi, c: (bi, c) + tail)
    bwd = lambda *tail: (lambda bi, c: (bi, nsteps - 1 - c) + tail)
    return pl.pallas_call(
        functools.partial(_ssd_state_body, nsteps=nsteps),
        grid=(b, nsteps),
        in_specs=[pl.BlockSpec((1, rows, SSM_GROUPS * SSM_STATE), fwd(c_col)),
                  pl.BlockSpec((1, rows, SSM_GROUPS * SSM_STATE), bwd(c_col)),
                  pl.BlockSpec((1, rows, SSM_WIDTH), fwd(0)),
                  pl.BlockSpec((1, rows, SSM_WIDTH), bwd(1)),
                  pl.BlockSpec((1, sps, 8, SSM_WIDTH), fwd(0, 0)),
                  pl.BlockSpec((1, sps, 8, SSM_WIDTH), bwd(0, 1)),
                  pl.BlockSpec(cs_blk, fwd(0, 0, 0, 0)),
                  pl.BlockSpec(cs_blk, bwd(1, 0, 0, 0)),
                  st_spec],
        out_specs=[pl.BlockSpec((1, rows, SSM_WIDTH), fwd(0)),
                   pl.BlockSpec((1, rows, SSM_WIDTH), bwd(0)),
                   st_spec],
        out_shape=[jax.ShapeDtypeStruct((b, L, SSM_WIDTH), BF16), jax.ShapeDtypeStruct((b, L, SSM_WIDTH), BF16),
                   jax.ShapeDtypeStruct((b,) + st_shape, F32)],
        scratch_shapes=[pltpu.VMEM(st_shape, F32)],
        compiler_params=_cp("parallel", "arbitrary"),
        name="ssd_states",
    )(xbc, xbc, ex, ex, et, et, cs, cs, init)


def _ssd(xbc, dt_raw, dt_bias_row, a_row, init):
    y_diag, cs, ex, et = _ssd_chunks(xbc, dt_raw, dt_bias_row, a_row)
    y_f, y_b, fin = _ssd_states(xbc, ex, et, cs, init)
    return (y_diag, y_f, y_b), fin


def _rope_tables(L):
    rows = L // GRID_W
    row = jnp.broadcast_to(jnp.arange(rows)[:, None], (rows, GRID_W)).reshape(L)
    col = jnp.broadcast_to(jnp.arange(GRID_W)[None, :], (rows, GRID_W)).reshape(L)
    nf = ATT_HEADDIM // 4
    inv = ROPE_BASE ** (-jnp.arange(nf, dtype=F32) / nf)
    ar = row.astype(F32)[:, None] * inv
    ac = col.astype(F32)[:, None] * inv
    cos = jnp.concatenate([jnp.cos(ar), jnp.cos(ar), jnp.cos(ac), jnp.cos(ac)], -1)
    sin = jnp.concatenate([-jnp.sin(ar), jnp.sin(ar), -jnp.sin(ac), jnp.sin(ac)], -1)
    return jnp.tile(cos, (1, ATT_HEADS)), jnp.tile(sin, (1, ATT_HEADS))


def _rope(x, cos, sin):
    w = x.shape[-1]
    quarter = ATT_HEADDIM // 4
    lane = lax.broadcasted_iota(jnp.int32, x.shape, x.ndim - 1)
    partner = jnp.where((lane // quarter) % 2 == 0, pltpu.roll(x, w - quarter, x.ndim - 1),
                        pltpu.roll(x, quarter, x.ndim - 1))
    return x * cos + partner * sin


def _wattn_body(sink_ref, bias_ref, q_ref, kp_ref, kc_ref, kn_ref, vp_ref, vc_ref, vn_ref, kx_ref, vx_ref, z_ref,
                o_ref):
    hd = ATT_HEADDIM
    low_half = lax.broadcasted_iota(jnp.int32, (WINDOW, 2 * hd), 1) < hd
    q = q_ref[0]
    bias = bias_ref[0]
    z = z_ref[0].astype(F32)
    k_all, v_ext = [], []
    for g in range(ATT_KV_HEADS):
        ks = slice(g * hd, (g + 1) * hd)
        k_all.append(jnp.concatenate([r[0, :, ks] for r in (kp_ref, kc_ref, kn_ref, kx_ref)], 0).astype(BF16))
        v_all = jnp.concatenate([r[0, :, ks] for r in (vp_ref, vc_ref, vn_ref, vx_ref)], 0).astype(BF16)
        ones = jnp.ones_like(v_all)
        v_ext.append((jnp.concatenate([v_all, ones], 1), jnp.concatenate([ones, v_all], 1)))
    heads = range(ATT_HEADS)
    sinks = [sink_ref[h] * LOG2E for h in heads]
    scores = [_dot_t(q[:, h * hd:(h + 1) * hd].astype(BF16), k_all[h // ATT_GROUP]) + bias for h in heads]
    maxes = [jnp.maximum(jnp.max(s, -1, keepdims=True), sk) for s, sk in zip(scores, sinks)]
    probs = [jnp.exp2(s - m).astype(BF16) for s, m in zip(scores, maxes)]
    exts = [_dot(p, v_ext[h // ATT_GROUP][h % 2]) for h, p in zip(heads, probs)]
    outs = [e / (pltpu.roll(e, hd, 1) + jnp.exp2(sk - m)) for e, sk, m in zip(exts, sinks, maxes)]
    for pair in range(ATT_HEADS // 2):
        cs = slice(2 * pair * hd, (2 * pair + 2) * hd)
        o_ref[0, :, cs] = (jnp.where(low_half, outs[2 * pair], outs[2 * pair + 1]) * _silu(z[:, cs])).astype(o_ref.dtype)


def _window_attention(q_rot, u_kv, uc_kv, sinks, z_a):
    b, L, _ = q_rot.shape
    lc = uc_kv.shape[1]
    nb = L // WINDOW
    hd2 = ATT_KV
    cur = lambda bi, i: (bi, i, 0)
    prv = lambda bi, i: (bi, jnp.maximum(i - 1, 0), 0)
    nxt = lambda bi, i: (bi, jnp.minimum(i + 1, nb - 1), 0)
    vcur = lambda bi, i: (bi, i, 1)
    vprv = lambda bi, i: (bi, jnp.maximum(i - 1, 0), 1)
    vnxt = lambda bi, i: (bi, jnp.minimum(i + 1, nb - 1), 1)
    kblk = lambda f: pl.BlockSpec((1, WINDOW, hd2), f)
    nk = 3 * WINDOW + lc
    row = np.arange(WINDOW)[:, None]
    col = np.arange(nk)[None, :]
    in_prev = (col < WINDOW) & (col >= row)
    in_next = (col >= 2 * WINDOW) & (col < 3 * WINDOW) & (col - 2 * WINDOW <= row)
    always = ((col >= WINDOW) & (col < 2 * WINDOW)) | (col >= 3 * WINDOW)
    kinds = [always | in_next, always | in_prev | in_next, always | in_prev]
    if nb == 1:
        kinds = [always] * 3
    bias = jnp.asarray(np.where(np.stack(kinds), 0.0, NEG).astype(np.float32))
    kind = lambda bi, i: (jnp.where(i == 0, 0, jnp.where(i == nb - 1, 2, 1)), 0, 0)
    return pl.pallas_call(
        _wattn_body,
        grid=(b, nb),
        in_specs=[pl.BlockSpec(memory_space=pltpu.SMEM),
                  pl.BlockSpec((1, WINDOW, nk), kind),
                  pl.BlockSpec((1, WINDOW, ATT_WIDTH), cur),
                  kblk(prv), kblk(cur), kblk(nxt), kblk(vprv), kblk(vcur), kblk(vnxt),
                  pl.BlockSpec((1, lc, hd2), lambda bi, i: (bi, 0, 0)),
                  pl.BlockSpec((1, lc, hd2), lambda bi, i: (bi, 0, 1)),
                  pl.BlockSpec((1, WINDOW, ATT_WIDTH), cur)],
        out_specs=pl.BlockSpec((1, WINDOW, ATT_WIDTH), cur),
        out_shape=jax.ShapeDtypeStruct((b, L, ATT_WIDTH), BF16),
        compiler_params=_cp("parallel", "arbitrary"),
        name="window_attention",
    )(sinks, bias, q_rot, u_kv, u_kv, u_kv, u_kv, u_kv, u_kv, uc_kv, uc_kv, z_a)


def _cattn_body(sink_ref, q_ref, k_ref, v_ref, z_ref, o_ref):
    scale = ATT_HEADDIM ** -0.5
    q = q_ref[0]
    z = z_ref[0].astype(F32)
    for g in range(ATT_KV_HEADS):
        ks = slice(g * ATT_HEADDIM, (g + 1) * ATT_HEADDIM)
        k = k_ref[0, :, ks].astype(BF16)
        v = v_ref[0, :, ks].astype(BF16)
        for j in range(ATT_GROUP):
            h = g * ATT_GROUP + j
            hs = slice(h * ATT_HEADDIM, (h + 1) * ATT_HEADDIM)
            s = _dot_t(q[:, hs].astype(BF16), k) * scale
            sink = sink_ref[h]
            m = jnp.maximum(jnp.max(s, -1, keepdims=True), sink)
            p = jnp.exp(s - m)
            den = jnp.sum(p, -1, keepdims=True) + jnp.exp(sink - m)
            o_ref[0, :, hs] = (_dot(p.astype(BF16), v) / den * _silu(z[:, hs])).astype(o_ref.dtype)


def _ctx_attention(uc_q, uc_kv, sinks, z_ac):
    b, lc, _ = uc_q.shape
    blk = lambda bi: (bi, 0, 0)
    return pl.pallas_call(
        _cattn_body,
        grid=(b,),
        in_specs=[pl.BlockSpec(memory_space=pltpu.SMEM),
                  pl.BlockSpec((1, lc, ATT_WIDTH), blk),
                  pl.BlockSpec((1, lc, ATT_KV), lambda bi: (bi, 0, 0)),
                  pl.BlockSpec((1, lc, ATT_KV), lambda bi: (bi, 0, 1)),
                  pl.BlockSpec((1, lc, ATT_WIDTH), blk)],
        out_specs=pl.BlockSpec((1, lc, ATT_WIDTH), blk),
        out_shape=jax.ShapeDtypeStruct((b, lc, ATT_WIDTH), BF16),
        compiler_params=_cp("parallel"),
        name="ctx_attention",
    )(sinks, uc_q, uc_kv, uc_kv, z_ac)


def _out_body(h_ref, g_ref, yhy_ref, yd_ref, yf_ref, yb_ref, xs_ref, zs_ref, yat_ref, dsk_ref, nw_ref, w_ref,
              lg_ref, lb_ref, o_ref):
    gw = SSM_WIDTH // SSM_GROUPS
    y_scan = yd_ref[0].astype(F32) + yf_ref[0].astype(F32) + yb_ref[0].astype(F32)
    ys = (y_scan + xs_ref[0].astype(F32) * dsk_ref[...]) * _silu(zs_ref[0].astype(F32))
    parts = [yhy_ref[0].astype(BF16)]
    for g in range(SSM_GROUPS):
        seg = ys[:, g * gw:(g + 1) * gw]
        seg = seg * lax.rsqrt(jnp.mean(seg * seg, -1, keepdims=True) + RMS_EPS) * nw_ref[:, g * gw:(g + 1) * gw]
        parts.append(seg.astype(BF16))
    parts.append(yat_ref[0].astype(BF16))
    acc = _dot(jnp.concatenate(parts, 1), w_ref[...])
    r = DEEPNORM_ALPHA * h_ref[0] + g_ref[0] * acc
    mu = jnp.mean(r, -1, keepdims=True)
    rc = r - mu
    var = jnp.mean(rc * rc, -1, keepdims=True)
    o_ref[0] = rc * lax.rsqrt(var + LN_EPS) * lg_ref[...] + lb_ref[...]


def _out_projection(h, gate_mod, y_hy, y_ssd, xbc, z_s, y_at, d_skip, norm_w, w_out, ln_g, ln_b):
    b, L, d = h.shape
    tm = min(L, PROJ_ROWS)
    row = lambda bi, i: (bi, i, 0)
    vec = lambda bi, i: (bi, 0, 0)
    full = lambda bi, i: (0, 0)
    w512 = pl.BlockSpec((1, tm, SSM_WIDTH), row)
    return pl.pallas_call(
        _out_body,
        grid=(b, L // tm),
        in_specs=[pl.BlockSpec((1, tm, d), row), pl.BlockSpec((1, 1, d), vec),
                  w512, w512, w512, w512, w512, w512, w512,
                  pl.BlockSpec((1, SSM_WIDTH), full), pl.BlockSpec((1, SSM_WIDTH), full),
                  pl.BlockSpec(w_out.shape, full), pl.BlockSpec((1, d), full), pl.BlockSpec((1, d), full)],
        out_specs=pl.BlockSpec((1, tm, d), row),
        out_shape=jax.ShapeDtypeStruct((b, L, d), F32),
        compiler_params=_cp("parallel", "arbitrary"),
        name="out_projection",
    )(h, gate_mod, y_hy, *y_ssd, xbc, z_s, y_at, d_skip, norm_w, w_out.astype(BF16),
      ln_g.reshape(1, d), ln_b.reshape(1, d))


def _sequence_front(h, shift, scale, w_packed, ssm_conv_w, ssm_conv_b, rope_tables=None):
    u_hy3, u_hyg, u_xbc, u_zs, u_q, u_kv, u_za, u_dt = _in_projection(h, shift, scale, w_packed, rope_tables)
    xbc = _dwconv(u_xbc, ssm_conv_w, ssm_conv_b, act=True, split=SSM_CONV_DIM)[0]
    return dict(hy3=u_hy3, hy_gate=u_hyg, xbc=xbc, z_s=u_zs, q=u_q, kv=u_kv, z_a=u_za, dt=u_dt)


def kernel(x, c, ctx, c_ctx, w_mod, b_mod, w_in, hy_conv_w, hy_conv_b, hy_f_w1, hy_f_b1, hy_f_w2, hy_f_b2,
           hy_f_w3, hy_f_b3, hy_f_freq, hy_f_wout, hy_bias, ssm_conv_w, ssm_conv_b, ssm_dt_bias, ssm_a_log,
           ssm_d, ssm_norm_w, attn_sinks, w_out, ln_g, ln_b):
    b, L, d = x.shape
    lc = ctx.shape[1]
    cos, sin = _rope_tables(L)
    cc = jnp.concatenate([c, c_ctx[None], jnp.zeros((16 - b - 1, d), F32)], 0)
    zero_state = jnp.zeros((b, 2, SSM_GROUPS, SSM_STATE, SSM_WIDTH // SSM_GROUPS), F32)
    h_lat, h_ctx = x, ctx
    for i in range(DEPTH):
        ctx_needed = i < DEPTH - 1
        mod = _modulation(cc, w_mod[i], b_mod[i])
        sh, sc, g = (mod[:b, None, j * d:(j + 1) * d] for j in range(3))
        sh_c, sc_c, g_c = (jnp.broadcast_to(mod[b:b + 1, None, j * d:(j + 1) * d], (b, 1, d)) for j in range(3))
        w_packed = _pack_w_in(w_in[i])
        lat = _sequence_front(h_lat, sh, sc, w_packed, ssm_conv_w[i], ssm_conv_b[i], (cos, sin))
        cx = _sequence_front(h_ctx, sh_c, sc_c, w_packed, ssm_conv_w[i], ssm_conv_b[i])

        dt_bias_row = jnp.pad(ssm_dt_bias[i].reshape(1, -1), ((0, 0), (0, LANES - 2 * SSM_HEADS)))
        a_row = jnp.pad(-jnp.exp(ssm_a_log[i]).reshape(1, -1), ((0, 0), (0, LANES - 2 * SSM_HEADS)))
        ys_c, s_c = _ssd(cx["xbc"], cx["dt"], dt_bias_row, a_row, zero_state)
        ys, _ = _ssd(lat["xbc"], lat["dt"], dt_bias_row, a_row, s_c)

        filt = (hy_f_w1[i], hy_f_b1[i], hy_f_w2[i], hy_f_b2[i], hy_f_w3[i], hy_f_b3[i], hy_f_freq[i], hy_f_wout[i])
        kf = _hyena_spectrum(L, _hyena_filter_taps(L, *filt))
        y_hy = _hyena(lat["hy3"], lat["hy_gate"], hy_conv_w[i], hy_conv_b[i], kf, hy_bias[i])

        y_at = _window_attention(lat["q"], lat["kv"], cx["kv"], attn_sinks[i], lat["z_a"])

        d_skip = jnp.repeat(ssm_d[i], SSM_HEADDIM).reshape(1, SSM_WIDTH)
        norm_w = ssm_norm_w[i].reshape(1, SSM_WIDTH)
        new_lat = _out_projection(h_lat, g, y_hy, ys, lat["xbc"], lat["z_s"], y_at, d_skip, norm_w,
                                  w_out[i], ln_g[i], ln_b[i])
        if ctx_needed:
            kf_c = _hyena_spectrum(lc, _hyena_filter_taps(lc, *filt))
            y_hy_c = _hyena(cx["hy3"], cx["hy_gate"], hy_conv_w[i], hy_conv_b[i], kf_c, hy_bias[i])
            y_at_c = _ctx_attention(cx["q"], cx["kv"], attn_sinks[i], cx["z_a"])
            h_ctx = _out_projection(h_ctx, g_c, y_hy_c, ys_c, cx["xbc"], cx["z_s"], y_at_c, d_skip,
                                    norm_w, w_out[i], ln_g[i], ln_b[i])
        h_lat = new_lat
    return h_lat
```

```python
import functools
import math

import numpy as np
import jax
import jax.numpy as jnp
from jax import lax
from jax.experimental import pallas as pl
from jax.experimental.pallas import tpu as pltpu

F32 = jnp.float32
BF16 = jnp.bfloat16
HI = lax.Precision.HIGHEST

D_MODEL = 1024
DEPTH = 2
GRID_W = 64
HY_WIDTH = 512
HY_BANDS = 16
HY_EMB = 1 + 2 * HY_BANDS
HY_FILTER_HIDDEN = 64
HY_DECAY_TARGET = 1e-2
HY_FAST_PCT = 0.3
HY_SLOW_PCT = 1.5
SSM_WIDTH = 512
SSM_HEADS = 8
SSM_HEADDIM = 64
SSM_GROUPS = 2
SSM_STATE = 128
SSM_CHUNK = 128
SSM_CONV_DIM = SSM_WIDTH + 2 * SSM_GROUPS * SSM_STATE
ATT_WIDTH = 512
ATT_HEADS = 8
ATT_KV_HEADS = 2
ATT_HEADDIM = 64
ATT_GROUP = ATT_HEADS // ATT_KV_HEADS
ATT_KV = ATT_KV_HEADS * ATT_HEADDIM
WINDOW = 128
ROPE_BASE = 10000.0
HY_IN = 4 * HY_WIDTH
SSM_IN = SSM_CONV_DIM + SSM_WIDTH + 2 * SSM_HEADS
DEEPNORM_ALPHA = (2 * DEPTH) ** 0.25
LN_EPS = 1e-6
RMS_EPS = 1e-5

LANES = 128
VMEM_LIMIT = 56 * 1024 * 1024
NEG = -1e30
BIG = 1e30
LOG2E = math.log2(math.e)
FILTER_ROWS = 256
PROJ_ROWS = 512
HY_STEP_ROWS = 8192
SECOND_STAGE_UNROLL = 11


def _cp(*sem):
    return pltpu.CompilerParams(dimension_semantics=sem, vmem_limit_bytes=VMEM_LIMIT)


def _silu(x):
    return (0.5 * x) * (1.0 + jnp.tanh(0.5 * x))


def _dot(a, b, precision=None):
    return jnp.dot(a, b, preferred_element_type=F32, precision=precision)


def _dot_t(a, b):
    return lax.dot_general(a, b, (((1,), (1,)), ((), ())), preferred_element_type=F32)


def _mod_body(c_ref, w_ref, b_ref, o_ref):
    o_ref[...] = _dot(_silu(c_ref[...]), w_ref[...], HI) + b_ref[...]


def _modulation(cc, w, b):
    rows, d = cc.shape
    n = w.shape[1]
    tn = 1024
    return pl.pallas_call(
        _mod_body,
        grid=(n // tn,),
        in_specs=[pl.BlockSpec((rows, d), lambda j: (0, 0)),
                  pl.BlockSpec((d, tn), lambda j: (0, j)),
                  pl.BlockSpec((1, tn), lambda j: (0, j))],
        out_specs=pl.BlockSpec((rows, tn), lambda j: (0, j)),
        out_shape=jax.ShapeDtypeStruct((rows, n), F32),
        compiler_params=_cp("arbitrary"),
        name="modulation",
    )(cc, w, b.reshape(1, n))


IN_SEGS = (3 * HY_WIDTH, HY_WIDTH, SSM_CONV_DIM, SSM_WIDTH, ATT_WIDTH, 2 * ATT_KV, ATT_WIDTH, LANES)
SEG_Q, SEG_KV = 4, 5
IN_DTYPES = (BF16,) * (len(IN_SEGS) - 1) + (F32,)
IN_CHUNK = 512


def _pack_w_in(w):
    o_ss = HY_IN
    o_at = HY_IN + SSM_IN
    dt = w[:, o_ss + SSM_CONV_DIM + SSM_WIDTH:o_at]
    parts = [w[:, :HY_IN], w[:, o_ss:o_ss + SSM_CONV_DIM + SSM_WIDTH], w[:, o_at:],
             dt, jnp.zeros((w.shape[0], LANES - dt.shape[1]), w.dtype)]
    return jnp.concatenate(parts, axis=1).astype(BF16)


def _inproj_body(h_ref, sh_ref, sc_ref, w_ref, *rest, rope):
    tabs, o_refs = (rest[:2], rest[2:]) if rope else ((), rest)
    x = h_ref[0]
    mu = jnp.mean(x, -1, keepdims=True)
    xc = x - mu
    var = jnp.mean(xc * xc, -1, keepdims=True)
    xm = (xc * lax.rsqrt(var + LN_EPS) * (1.0 + sc_ref[0]) + sh_ref[0]).astype(BF16)
    off = 0
    for seg, (o_ref, n) in enumerate(zip(o_refs, IN_SEGS)):
        for j in range(0, n, IN_CHUNK):
            w = min(IN_CHUNK, n - j)
            r = _dot(xm, w_ref[:, off + j:off + j + w])
            if rope and seg == SEG_Q:
                r = _rope(r, tabs[0][...], tabs[1][...]) * (ATT_HEADDIM ** -0.5 * LOG2E)
            if rope and seg == SEG_KV:
                o_ref[0, :, :ATT_KV] = _rope(r[:, :ATT_KV], tabs[0][:, :ATT_KV], tabs[1][:, :ATT_KV]).astype(o_ref.dtype)
                o_ref[0, :, ATT_KV:] = r[:, ATT_KV:].astype(o_ref.dtype)
            else:
                o_ref[0, :, j:j + w] = r.astype(o_ref.dtype)
        off += n


def _in_projection(h, shift, scale, w_packed, rope_tables=None):
    b, L, d = h.shape
    tm = min(L, PROJ_ROWS)
    n_all = w_packed.shape[1]
    row = lambda bi, i: (bi, i, 0)
    vec = lambda bi, i: (bi, 0, 0)
    rope = rope_tables is not None
    tab_specs = [pl.BlockSpec((tm, ATT_WIDTH), lambda bi, i: (i, 0))] * 2 if rope else []
    return pl.pallas_call(
        functools.partial(_inproj_body, rope=rope),
        grid=(b, L // tm),
        in_specs=[pl.BlockSpec((1, tm, d), row), pl.BlockSpec((1, 1, d), vec), pl.BlockSpec((1, 1, d), vec),
                  pl.BlockSpec((d, n_all), lambda bi, i: (0, 0))] + tab_specs,
        out_specs=[pl.BlockSpec((1, tm, n), row) for n in IN_SEGS],
        out_shape=[jax.ShapeDtypeStruct((b, L, n), dt) for n, dt in zip(IN_SEGS, IN_DTYPES)],
        compiler_params=_cp("parallel", "arbitrary"),
        name="in_projection",
    )(h, shift, scale, w_packed, *(rope_tables or ()))


def _dwconv_body(u_ref, w_ref, b_ref, o_ref, *, act):
    x = u_ref[0].astype(F32)
    L = x.shape[0]
    row = lax.broadcasted_iota(jnp.int32, x.shape, 0)
    prev = jnp.where(row == 0, 0.0, pltpu.roll(x, 1, 0))
    nxt = jnp.where(row == L - 1, 0.0, pltpu.roll(x, L - 1, 0))
    y = prev * w_ref[0:1, :] + x * w_ref[1:2, :] + nxt * w_ref[2:3, :] + b_ref[...]
    if act:
        y = _silu(y)
    o_ref[0, 0] = y.astype(o_ref.dtype)


def _dwconv(u, w, bias, *, act, split):
    b, L, c = u.shape
    tc = 256
    per = split // tc
    return pl.pallas_call(
        functools.partial(_dwconv_body, act=act),
        grid=(b, c // tc),
        in_specs=[pl.BlockSpec((1, L, tc), lambda bi, j: (bi, 0, j)),
                  pl.BlockSpec((3, tc), lambda bi, j: (0, j)),
                  pl.BlockSpec((1, tc), lambda bi, j: (0, j))],
        out_specs=pl.BlockSpec((1, 1, L, tc), lambda bi, j: (j // per, bi, 0, j % per)),
        out_shape=jax.ShapeDtypeStruct((c // split, b, L, split), BF16),
        compiler_params=_cp("parallel", "arbitrary"),
        name="dwconv",
    )(u, w, bias.reshape(1, c))


def _filter_features(L):
    t = jnp.linspace(0.0, 1.0, L, dtype=F32)[:, None]
    w = 2.0 * math.pi * jnp.arange(L, dtype=F32)[:, None] / L
    f = jnp.linspace(1e-4, HY_BANDS - 1, HY_BANDS, dtype=F32)[None]
    z = jnp.concatenate([t, jnp.cos(f * w), -jnp.sin(f * w)], -1)
    return jnp.pad(z, ((0, 0), (0, LANES - HY_EMB)))


def _pad_to(a, rows, cols):
    return jnp.pad(a, ((0, rows - a.shape[0]), (0, cols - a.shape[1])))


def _filter_body(z_ref, w1_ref, w2_ref, w3_ref, b_ref, fr_ref, wo_ref, ad_ref, o_ref, *, nblk):
    i = pl.program_id(0)

    @pl.when(i < nblk)
    def _():
        z = z_ref[...]
        fr = fr_ref[...]
        h = jnp.sin(fr * (_dot(z, w1_ref[...], HI) + b_ref[0:1, :]))
        h = jnp.sin(fr * (_dot(h, w2_ref[...], HI) + b_ref[1:2, :]))
        h = jnp.sin(fr * (_dot(h, w3_ref[...], HI) + b_ref[2:3, :]))
        win = jnp.exp(-z[:, 0:1] * ad_ref[...])
        h_hi = h.astype(BF16)
        h_lo = (h - h_hi.astype(F32)).astype(BF16)
        for j in range(4):
            cols = slice(j * HY_WIDTH, (j + 1) * HY_WIDTH)
            taps = _dot(h_hi, wo_ref[0, :, cols]) + _dot(h_lo, wo_ref[0, :, cols]) + _dot(h_hi, wo_ref[1, :, cols])
            o_ref[:, cols] = taps * win

    @pl.when(i == nblk)
    def _():
        o_ref[...] = jnp.zeros_like(o_ref)


def _hyena_filter_taps(L, w1, b1, w2, b2, w3, b3, freq, w_out):
    z = _filter_features(L)
    hp = LANES
    bias = jnp.stack([jnp.pad(b, (0, hp - b.shape[0])) for b in (b1, b2, b3)])
    bias = jnp.pad(bias, ((0, 5), (0, 0)))
    fr = jnp.pad(freq, (0, hp - freq.shape[0])).reshape(1, hp)
    max_decay = math.log(HY_DECAY_TARGET) / HY_FAST_PCT
    min_decay = math.log(HY_DECAY_TARGET) / HY_SLOW_PCT
    absd = jnp.abs(jnp.linspace(min_decay, max_decay, HY_WIDTH, dtype=F32)).reshape(1, HY_WIDTH)
    tl = FILTER_ROWS
    nblk = L // tl
    n = 4 * HY_WIDTH
    full = lambda i: (0, 0)
    return pl.pallas_call(
        functools.partial(_filter_body, nblk=nblk),
        grid=(nblk + 1,),
        in_specs=[pl.BlockSpec((tl, hp), lambda i: (jnp.minimum(i, nblk - 1), 0)),
                  pl.BlockSpec((hp, hp), full), pl.BlockSpec((hp, hp), full), pl.BlockSpec((hp, hp), full),
                  pl.BlockSpec((8, hp), full), pl.BlockSpec((1, hp), full),
                  pl.BlockSpec((2, hp, n), lambda i: (0, 0, 0)), pl.BlockSpec((1, HY_WIDTH), full)],
        out_specs=pl.BlockSpec((tl, n), lambda i: (i, 0)),
        out_shape=jax.ShapeDtypeStruct((L + tl, n), F32),
        compiler_params=_cp("arbitrary"),
        name="hyena_filter",
    )(z, _pad_to(w1, hp, hp), _pad_to(w2, hp, hp), _pad_to(w3, hp, hp), bias, fr, _split2(_pad_to(w_out, hp, n)),
      absd)


def _hy_cfg(L):
    n2 = 128 if L >= 2048 else 16
    n1 = 2 * L // n2
    k1n = n1 // 2 + 1
    jp = -(-2 * k1n // 16) * 16
    pitch = n2 + 8
    return dict(L=L, n2=n2, n1=n1, nh=n1 // 2, k1n=k1n, jp=jp, pitch=pitch)


@functools.lru_cache(maxsize=None)
def _dft_tables(L):
    cfg = _hy_cfg(L)
    n, n1, n2, nh, k1n, jp = 2 * L, cfg["n1"], cfg["n2"], cfg["nh"], cfg["k1n"], cfg["jp"]
    a_n1 = np.arange(n1)
    a_k1 = np.arange(k1n)
    th = 2 * np.pi * np.outer(a_k1, a_n1) / n1
    f1 = np.zeros((jp, n1))
    f1[0:2 * k1n:2] = np.cos(th)
    f1[1:2 * k1n:2] = -np.sin(th)
    a_n2 = np.arange(n2)
    m1 = np.zeros((k1n, 2 * n2, 2 * n2))
    for k in range(k1n):
        f = np.exp(-2j * np.pi * (np.outer(a_n2, a_n2) / n2 + a_n2[None, :] * k / n))
        m1[k] = np.block([[f.real, -f.imag], [f.imag, f.real]])
    ck = np.full(k1n, 2.0)
    ck[0] = 1.0
    ck[-1] = 1.0
    th6 = 2 * np.pi * np.outer(np.arange(nh), a_k1) / n1
    f6 = np.zeros((nh, jp))
    f6[:, 0:2 * k1n:2] = ck * np.cos(th6) / n
    f6[:, 1:2 * k1n:2] = -ck * np.sin(th6) / n
    as32 = lambda a: np.asarray(a, np.float32)
    return dict(f1=as32(f1), m1=as32(m1), f6=as32(f6))


def _first_stage(src_ref, a_scr, f1, cfg):
    n2, nh, jp, pitch = cfg["n2"], cfg["nh"], cfg["jp"], cfg["pitch"]
    bt = src_ref.shape[0]

    def step(i, carry):
        xs = jnp.concatenate([src_ref[t, pl.ds(2 * i + u, nh, stride=pitch), :] for t in range(bt) for u in range(2)], 1)
        r = _dot(f1, xs.astype(BF16))
        for t in range(bt):
            for u in range(2):
                lo = (2 * t + u) * LANES
                a_scr[t, pl.ds(2 * i + u, jp, stride=pitch), :] = r[:, lo:lo + LANES]
        return carry

    lax.fori_loop(0, n2 // 2, step, 0, unroll=min(16 // bt, n2 // 2))


def _k1_rows(a_scr, k, cfg):
    n2, pitch = cfg["n2"], cfg["pitch"]
    base = pl.multiple_of(2 * k * pitch, 8)
    parts = [jnp.concatenate([a_scr[t, pl.ds(base, n2), :], a_scr[t, pl.ds(base + pitch, n2), :]], 0)
             for t in range(a_scr.shape[0])]
    return base, jnp.concatenate(parts, 1)


def _split2(table):
    t = jnp.asarray(table)
    hi = t.astype(BF16)
    return jnp.stack([hi, (t - hi.astype(F32)).astype(BF16)])


def _spectrum_body(kf_ref, kb_ref, f1_ref, m1_ref, o_ref, a_scr, *, cfg):
    n2, nh, jp, pitch = cfg["n2"], cfg["nh"], cfg["jp"], cfg["pitch"]

    def dot3(m_hi, m_lo, x):
        x_hi = x.astype(BF16)
        x_lo = (x - x_hi.astype(F32)).astype(BF16)
        return _dot(m_hi, x_hi) + _dot(m_lo, x_hi) + _dot(m_hi, x_lo)

    def step(i, carry):
        cols = []
        for t in range(2):
            n = 2 * i + t
            cols.append(jnp.concatenate([kf_ref[pl.ds(n, nh, stride=n2), :],
                                         kb_ref[pl.ds(n2 - n, nh, stride=n2), :]], 0))
        r = dot3(f1_ref[0], f1_ref[1], jnp.concatenate(cols, 1))
        a_scr[0, pl.ds(2 * i, jp, stride=pitch), :] = r[:, :LANES]
        a_scr[0, pl.ds(2 * i + 1, jp, stride=pitch), :] = r[:, LANES:]
        return carry

    lax.fori_loop(0, n2 // 2, step, 0, unroll=min(16, n2 // 2))
    lag0 = pl.ds(0, jp, stride=pitch)
    a_scr[0, lag0, :] = a_scr[0, lag0, :] + f1_ref[0, :, 0:1].astype(F32) * kb_ref[0:1, :]

    def mid(k, carry):
        _, a = _k1_rows(a_scr, k, cfg)
        o_ref[0, k] = dot3(m1_ref[0, k], m1_ref[1, k], a).astype(BF16)
        return carry

    lax.fori_loop(0, cfg["k1n"], mid, 0, unroll=SECOND_STAGE_UNROLL)


def _hyena_spectrum(L, taps):
    c = HY_WIDTH
    cfg = _hy_cfg(L)
    tb = _dft_tables(L)
    n1, n2, nh, k1n = cfg["n1"], cfg["n2"], cfg["nh"], cfg["k1n"]
    f1 = _split2(np.concatenate([tb["f1"][:, :nh], tb["f1"][:, n1 - 1:nh - 1:-1]], 1))
    m1 = _split2(tb["m1"])
    nct = c // LANES
    rows = taps.shape[0]
    return pl.pallas_call(
        functools.partial(_spectrum_body, cfg=cfg),
        grid=(2, nct),
        in_specs=[pl.BlockSpec((rows, LANES), lambda cv, j: (0, 2 * cv * nct + j)),
                  pl.BlockSpec((rows, LANES), lambda cv, j: (0, (2 * cv + 1) * nct + j)),
                  pl.BlockSpec(f1.shape, lambda cv, j: (0, 0, 0)),
                  pl.BlockSpec(m1.shape, lambda cv, j: (0, 0, 0, 0))],
        out_specs=pl.BlockSpec((1, k1n, 2 * n2, LANES), lambda cv, j: (cv, 0, 0, j)),
        out_shape=jax.ShapeDtypeStruct((2, k1n, 2 * n2, c), BF16),
        scratch_shapes=[pltpu.VMEM((1, cfg["jp"] * cfg["pitch"], LANES), F32)],
        compiler_params=_cp("arbitrary", "arbitrary"),
        name="hyena_spectrum",
    )(taps, taps, f1, m1)


def _short_conv(u_ref, w_ref, b_ref, which):
    x = u_ref[...].astype(F32)
    L = x.shape[0]
    edge = lax.broadcasted_iota(jnp.int32, (8, x.shape[1]), 0)
    prev = pltpu.roll(x, 1, 0)
    prev = jnp.concatenate([jnp.where(edge == 0, 0.0, prev[:8]), prev[8:]], 0)
    nxt = pltpu.roll(x, L - 1, 0)
    nxt = jnp.concatenate([nxt[:L - 8], jnp.where(edge == 7, 0.0, nxt[L - 8:])], 0)
    sel = slice(which, which + 1)
    return prev * w_ref[0, sel, :] + x * w_ref[1, sel, :] + nxt * w_ref[2, sel, :] + b_ref[sel, :]


def _long_conv(src_scr, a_scr, y_scr, w_scr, f1_ref, f6_ref, m1_ref, kf_ref, conv, cfg):
    n2, nh, jp, pitch = cfg["n2"], cfg["nh"], cfg["jp"], cfg["pitch"]
    bt = src_scr.shape[0]
    _first_stage(src_scr, a_scr, f1_ref[...], cfg)

    def forward(k, carry):
        _, a = _k1_rows(a_scr, k, cfg)
        x = _dot(m1_ref[k], a.astype(BF16))
        kk = kf_ref[conv, k].astype(F32)
        kr, ki = kk[:n2], kk[n2:]
        cols = []
        for t in range(bt):
            xr, xi = x[:n2, t * LANES:(t + 1) * LANES], x[n2:, t * LANES:(t + 1) * LANES]
            cols.append(jnp.concatenate([xr * kr - xi * ki, xr * ki + xi * kr], 0))
        w_scr[k] = jnp.concatenate(cols, 1).astype(BF16)
        return carry

    def inverse(k, carry):
        base = pl.multiple_of(2 * k * pitch, 8)
        b = lax.dot_general(m1_ref[k], w_scr[k], (((0,), (0,)), ((), ())), preferred_element_type=F32)
        for t in range(bt):
            a_scr[t, pl.ds(base, n2), :] = b[:n2, t * LANES:(t + 1) * LANES]
            a_scr[t, pl.ds(base + pitch, n2), :] = b[n2:, t * LANES:(t + 1) * LANES]
        return carry

    lax.fori_loop(0, cfg["k1n"], forward, 0, unroll=SECOND_STAGE_UNROLL)
    lax.fori_loop(0, cfg["k1n"], inverse, 0, unroll=SECOND_STAGE_UNROLL)
    f6 = f6_ref[...]

    def last(i, carry):
        bs = jnp.concatenate([a_scr[t, pl.ds(2 * i + u, jp, stride=pitch), :] for t in range(bt) for u in range(2)], 1)
        y = _dot(f6, bs.astype(BF16))
        for t in range(bt):
            for u in range(2):
                lo = (2 * t + u) * LANES
                y_scr[t, pl.ds(2 * i + u, nh, stride=pitch), :] = y[:, lo:lo + LANES]
        return carry

    lax.fori_loop(0, n2 // 2, last, 0, unroll=min(16 // bt, n2 // 2))


def _hyena_body(v_ref, x1_ref, x2_ref, g_ref, cw_ref, cb_ref, hb_ref, f1_ref, f6_ref, m1_ref, kf_ref,
                o_ref, s_scr, a_scr, y_scr, w_scr, *, cfg):
    tabs = (w_scr, f1_ref, f6_ref, m1_ref, kf_ref)
    n2, nh, pitch = cfg["n2"], cfg["nh"], cfg["pitch"]
    slots = range(s_scr.shape[0])

    def put(scr, t, val):
        for n1 in range(nh):
            scr[t, n1 * pitch:n1 * pitch + n2, :] = val[n1 * n2:(n1 + 1) * n2]

    def get(scr, t):
        return jnp.concatenate([scr[t, n1 * pitch:n1 * pitch + n2, :] for n1 in range(nh)], 0)

    for t in slots:
        put(s_scr, t, _short_conv(v_ref.at[t], cw_ref, cb_ref, 0))
    _long_conv(s_scr, a_scr, y_scr, *tabs, 0, cfg)
    for t in slots:
        put(s_scr, t, _short_conv(x1_ref.at[t], cw_ref, cb_ref, 1) * (get(y_scr, t) + get(s_scr, t) * hb_ref[0:1, :]))
    _long_conv(s_scr, a_scr, y_scr, *tabs, 1, cfg)
    for t in slots:
        y = _short_conv(x2_ref.at[t], cw_ref, cb_ref, 2) * (get(y_scr, t) + get(s_scr, t) * hb_ref[1:2, :])
        o_ref[t] = (y * _silu(g_ref[t].astype(F32))).astype(o_ref.dtype)


def _hyena(u_hy3, u_gate, conv_w, conv_b, kf, hy_bias):
    b, L, _ = u_gate.shape
    c = HY_WIDTH
    cfg = _hy_cfg(L)
    tb = _dft_tables(L)
    nct = c // LANES
    bt = max(t for t in (1, 2, 4, 8) if b % t == 0 and t * L <= max(L, HY_STEP_ROWS))
    col = lambda off: pl.BlockSpec((bt, L, LANES), lambda j, bi: (bi, 0, off * nct + j))
    full = lambda a: pl.BlockSpec(a.shape, lambda j, bi: (0,) * a.ndim)
    f1 = jnp.asarray(tb["f1"][:, :cfg["nh"]]).astype(BF16)
    f6 = jnp.asarray(tb["f6"]).astype(BF16)
    m1 = jnp.asarray(tb["m1"]).astype(BF16)
    return pl.pallas_call(
        functools.partial(_hyena_body, cfg=cfg),
        grid=(nct, b // bt),
        in_specs=[col(0), col(1), col(2), col(0),
                  pl.BlockSpec((3, 3, LANES), lambda j, bi: (0, 0, j)),
                  pl.BlockSpec((3, LANES), lambda j, bi: (0, j)),
                  pl.BlockSpec((2, LANES), lambda j, bi: (0, j)),
                  full(f1), full(f6), full(m1),
                  pl.BlockSpec((2, cfg["k1n"], 2 * cfg["n2"], LANES), lambda j, bi: (0, 0, 0, j),
                               pipeline_mode=pl.Buffered(1))],
        out_specs=col(0),
        out_shape=jax.ShapeDtypeStruct((b, L, c), BF16),
        scratch_shapes=[pltpu.VMEM((bt, cfg["nh"] * cfg["pitch"], LANES), F32),
                        pltpu.VMEM((bt, cfg["jp"] * cfg["pitch"], LANES), F32),
                        pltpu.VMEM((bt, cfg["nh"] * cfg["pitch"], LANES), F32),
                        pltpu.VMEM((cfg["k1n"], 2 * cfg["n2"], bt * LANES), BF16)],
        compiler_params=_cp("arbitrary", "arbitrary"),
        name="hyena",
    )(u_hy3, u_hy3, u_hy3, u_gate, conv_w.reshape(3, 3, c), conv_b.reshape(3, c), hy_bias, f1, f6, m1, kf)


SSD_CPS = 4
SSD_SPS = 4
SPLIT_STRIDE = 2 * SSM_HEADS


def _pack3(x):
    hi = x.astype(BF16).astype(F32)
    r1 = x - hi
    mid = r1.astype(BF16).astype(F32)
    lo = (r1 - mid).astype(BF16).astype(F32)
    return (hi + pltpu.roll(mid, SPLIT_STRIDE, 1) + pltpu.roll(lo, 2 * SPLIT_STRIDE, 1)).astype(BF16)


def _unpack3(x3, used):
    return jnp.where(used, x3 + pltpu.roll(x3, LANES - SPLIT_STRIDE, 1) + pltpu.roll(x3, LANES - 2 * SPLIT_STRIDE, 1), 0.0)


@functools.lru_cache(maxsize=None)
def _ssd_spread_tables():
    col = np.zeros((LANES, 2 * SSM_HEADS * LANES), np.float32)
    head = np.zeros((LANES, 2 * SSM_WIDTH), np.float32)
    for c in range(2 * SSM_HEADS):
        d, h = divmod(c, SSM_HEADS)
        for piece in range(3):
            col[c + piece * SPLIT_STRIDE, c * LANES:(c + 1) * LANES] = 1.0
            lo = d * SSM_WIDTH + h * SSM_HEADDIM
            head[c + piece * SPLIT_STRIDE, lo:lo + SSM_HEADDIM] = 1.0
    return col, head


def _ssd_chunk_body(xbc_ref, dtr_ref, dtb_ref, a_ref, ecol_ref, ehead_ref, yd_ref, cs_ref, ex_ref, et_ref):
    q = SSM_CHUNK
    hpg = SSM_HEADS // SSM_GROUPS
    gw = SSM_WIDTH // SSM_GROUPS
    li = lax.broadcasted_iota(jnp.int32, (q, q), 0)
    si = lax.broadcasted_iota(jnp.int32, (q, q), 1)
    below = li > si
    diag = li == si
    fwd_lane = si < SSM_HEADS
    used = si < 2 * SSM_HEADS
    tril = (li >= si).astype(BF16)
    triu = (li <= si).astype(BF16)
    chunks = range(cs_ref.shape[1])
    rows = [slice(c * q, (c + 1) * q) for c in chunks]
    ehead = ehead_ref[...]

    raws = [dtr_ref[0, r, :] + dtb_ref[...] for r in rows]
    dts = [jnp.where(used, jnp.maximum(x, 0.0) + jnp.log1p(jnp.exp(-jnp.abs(x))), 0.0) for x in raws]
    da3 = [_pack3(dt * a_ref[...]) for dt in dts]
    acs = [jnp.where(fwd_lane, _unpack3(_dot(tril, x), used), _unpack3(_dot(triu, x), used)) for x in da3]
    tots = [jnp.where(fwd_lane[0:1], a[q - 1:q, :], a[0:1, :]) for a in acs]
    ws = [dt * jnp.exp(t - a) for dt, t, a in zip(dts, tots, acs)]
    acs3 = [_pack3(a) for a in acs]
    colb = [_dot(x, ecol_ref[...]) for x in acs3]
    wx = [_dot(_pack3(w), ehead) for w in ws]
    for c in chunks:
        ex_ref[0, rows[c], :] = jnp.exp(_dot(acs3[c], ehead)).astype(BF16)
        et_ref[0, c] = jnp.exp(_dot(_pack3(jnp.broadcast_to(tots[c], (8, LANES))), ehead))
    rowt = [(a - jnp.where(dt > 0.0, jnp.log(dt), -BIG)).T for a, dt in zip(acs, dts)]
    dsum = [(dt + pltpu.roll(dt, LANES - SSM_HEADS, 1)).T for dt in dts]
    xbc = [xbc_ref[0, r, :] for r in rows]
    xsb = [x[:, :SSM_WIDTH].astype(BF16) for x in xbc]
    bgs = [[x[:, SSM_WIDTH + g * SSM_STATE:SSM_WIDTH + (g + 1) * SSM_STATE] for g in range(SSM_GROUPS)] for x in xbc]
    cgs = [[x[:, SSM_WIDTH + (SSM_GROUPS + g) * SSM_STATE:SSM_WIDTH + (SSM_GROUPS + g + 1) * SSM_STATE]
            for g in range(SSM_GROUPS)] for x in xbc]
    gmat = [[_dot_t(cgs[c][g].astype(BF16), bgs[c][g].astype(BF16)) for g in range(SSM_GROUPS)] for c in chunks]
    mats = []
    for c in chunks:
        for h in range(SSM_HEADS):
            hb = SSM_HEADS + h
            arg = jnp.where(below, colb[c][:, h * q:(h + 1) * q] - rowt[c][h:h + 1, :],
                            colb[c][:, hb * q:(hb + 1) * q] - rowt[c][hb:hb + 1, :])
            dec = jnp.where(diag, dsum[c][h:h + 1, :], jnp.exp(arg))
            mats.append((gmat[c][h // hpg] * dec).astype(BF16))
    for c in chunks:
        for h in range(SSM_HEADS):
            lo = h * SSM_HEADDIM
            yd_ref[0, rows[c], lo:lo + SSM_HEADDIM] = _dot(mats[c * SSM_HEADS + h],
                                                             xsb[c][:, lo:lo + SSM_HEADDIM]).astype(yd_ref.dtype)
    for c in chunks:
        for g in range(SSM_GROUPS):
            bgt = bgs[c][g].astype(F32).T.astype(BF16)
            xg = xbc[c][:, g * gw:(g + 1) * gw].astype(F32)
            for d in range(2):
                lo = d * SSM_WIDTH + g * gw
                cs_ref[0, c, d, g] = _dot(bgt, (xg * wx[c][:, lo:lo + gw]).astype(BF16)).astype(cs_ref.dtype)


def _ssd_chunks(xbc, dt_raw, dt_bias_row, a_row):
    b, L, _ = xbc.shape
    nc = L // SSM_CHUNK
    cps = math.gcd(nc, SSD_CPS)
    rows = cps * SSM_CHUNK
    gw = SSM_WIDTH // SSM_GROUPS
    blk = lambda bi, i: (bi, i, 0)
    full2 = lambda bi, i: (0, 0)
    ecol, ehead = (jnp.asarray(t).astype(BF16) for t in _ssd_spread_tables())
    return pl.pallas_call(
        _ssd_chunk_body,
        grid=(b, nc // cps),
        in_specs=[pl.BlockSpec((1, rows, SSM_CONV_DIM), blk), pl.BlockSpec((1, rows, LANES), blk),
                  pl.BlockSpec((1, LANES), full2), pl.BlockSpec((1, LANES), full2),
                  pl.BlockSpec(ecol.shape, full2), pl.BlockSpec(ehead.shape, full2)],
        out_specs=[pl.BlockSpec((1, rows, SSM_WIDTH), blk),
                   pl.BlockSpec((1, cps, 2, SSM_GROUPS, SSM_STATE, gw), lambda bi, i: (bi, i, 0, 0, 0, 0)),
                   pl.BlockSpec((1, rows, 2 * SSM_WIDTH), blk),
                   pl.BlockSpec((1, cps, 8, 2 * SSM_WIDTH), lambda bi, i: (bi, i, 0, 0))],
        out_shape=[jax.ShapeDtypeStruct((b, L, SSM_WIDTH), BF16),
                   jax.ShapeDtypeStruct((b, nc, 2, SSM_GROUPS, SSM_STATE, gw), BF16),
                   jax.ShapeDtypeStruct((b, L, 2 * SSM_WIDTH), BF16),
                   jax.ShapeDtypeStruct((b, nc, 8, 2 * SSM_WIDTH), F32)],
        compiler_params=_cp("parallel", "arbitrary"),
        name="ssd_chunks",
    )(xbc, dt_raw, dt_bias_row, a_row, ecol, ehead)


def _ssd_state_body(cf_ref, cb_ref, xf_ref, xb_ref, ef_ref, eb_ref, sf_ref, sb_ref, init_ref, yf_ref, yb_ref, fin_ref,
                    st_ref, *, nsteps):
    ci = pl.program_id(1)

    @pl.when(ci == 0)
    def _():
        st_ref[...] = init_ref[0]

    q = SSM_CHUNK
    gw = SSM_WIDTH // SSM_GROUPS
    sps = sf_ref.shape[1]
    for s in range(sps):
        dirs = ((cf_ref, xf_ref, ef_ref, sf_ref, yf_ref, s), (cb_ref, xb_ref, eb_ref, sb_ref, yb_ref, sps - 1 - s))
        for d, (c_ref, x_ref, e_ref, s_ref, y_ref, j) in enumerate(dirs):
            rows = slice(j * q, (j + 1) * q)
            cmat = c_ref[0, rows, :].astype(BF16)
            for g in range(SSM_GROUPS):
                cols = slice(g * gw, (g + 1) * gw)
                st = st_ref[d, g]
                y_ref[0, rows, cols] = (_dot(cmat[:, g * SSM_STATE:(g + 1) * SSM_STATE], st.astype(BF16))
                                        * x_ref[0, rows, cols].astype(F32)).astype(y_ref.dtype)
                st_ref[d, g] = st * e_ref[0, j, 0:1, cols] + s_ref[0, j, 0, g].astype(F32)

    @pl.when(ci == nsteps - 1)
    def _():
        fin_ref[0] = st_ref[...]


def _ssd_states(xbc, ex, et, cs, init):
    b, L, _ = xbc.shape
    nc = L // SSM_CHUNK
    gw = SSM_WIDTH // SSM_GROUPS
    c_col = SSM_CONV_DIM // (SSM_GROUPS * SSM_STATE) - 1
    st_shape = (2, SSM_GROUPS, SSM_STATE, gw)
    st_spec = pl.BlockSpec((1,) + st_shape, lambda bi, c: (bi, 0, 0, 0, 0))
    sps = math.gcd(nc, SSD_SPS)
    nsteps = nc // sps
    rows = sps * SSM_CHUNK
    cs_blk = (1, sps, 1, SSM_GROUPS, SSM_STATE, gw)
    fwd = lambda *tail: (lambda bi, c: (bi, c) + tail)
    bwd = lambda *tail: (lambda bi, c: (bi, nsteps - 1 - c) + tail)
    return pl.pallas_call(
        functools.partial(_ssd_state_body, nsteps=nsteps),
        grid=(b, nsteps),
        in_specs=[pl.BlockSpec((1, rows, SSM_GROUPS * SSM_STATE), fwd(c_col)),
                  pl.BlockSpec((1, rows, SSM_GROUPS * SSM_STATE), bwd(c_col)),
                  pl.BlockSpec((1, rows, SSM_WIDTH), fwd(0)),
                  pl.BlockSpec((1, rows, SSM_WIDTH), bwd(1)),
                  pl.BlockSpec((1, sps, 8, SSM_WIDTH), fwd(0, 0)),
                  pl.BlockSpec((1, sps, 8, SSM_WIDTH), bwd(0, 1)),
                  pl.BlockSpec(cs_blk, fwd(0, 0, 0, 0)),
                  pl.BlockSpec(cs_blk, bwd(1, 0, 0, 0)),
                  st_spec],
        out_specs=[pl.BlockSpec((1, rows, SSM_WIDTH), fwd(0)),
                   pl.BlockSpec((1, rows, SSM_WIDTH), bwd(0)),
                   st_spec],
        out_shape=[jax.ShapeDtypeStruct((b, L, SSM_WIDTH), BF16), jax.ShapeDtypeStruct((b, L, SSM_WIDTH), BF16),
                   jax.ShapeDtypeStruct((b,) + st_shape, F32)],
        scratch_shapes=[pltpu.VMEM(st_shape, F32)],
        compiler_params=_cp("parallel", "arbitrary"),
        name="ssd_states",
    )(xbc, xbc, ex, ex, et, et, cs, cs, init)


def _ssd(xbc, dt_raw, dt_bias_row, a_row, init):
    y_diag, cs, ex, et = _ssd_chunks(xbc, dt_raw, dt_bias_row, a_row)
    y_f, y_b, fin = _ssd_states(xbc, ex, et, cs, init)
    return (y_diag, y_f, y_b), fin


def _rope_tables(L):
    rows = L // GRID_W
    row = jnp.broadcast_to(jnp.arange(rows)[:, None], (rows, GRID_W)).reshape(L)
    col = jnp.broadcast_to(jnp.arange(GRID_W)[None, :], (rows, GRID_W)).reshape(L)
    nf = ATT_HEADDIM // 4
    inv = ROPE_BASE ** (-jnp.arange(nf, dtype=F32) / nf)
    ar = row.astype(F32)[:, None] * inv
    ac = col.astype(F32)[:, None] * inv
    cos = jnp.concatenate([jnp.cos(ar), jnp.cos(ar), jnp.cos(ac), jnp.cos(ac)], -1)
    sin = jnp.concatenate([-jnp.sin(ar), jnp.sin(ar), -jnp.sin(ac), jnp.sin(ac)], -1)
    return jnp.tile(cos, (1, ATT_HEADS)), jnp.tile(sin, (1, ATT_HEADS))


def _rope(x, cos, sin):
    w = x.shape[-1]
    quarter = ATT_HEADDIM // 4
    lane = lax.broadcasted_iota(jnp.int32, x.shape, x.ndim - 1)
    partner = jnp.where((lane // quarter) % 2 == 0, pltpu.roll(x, w - quarter, x.ndim - 1),
                        pltpu.roll(x, quarter, x.ndim - 1))
    return x * cos + partner * sin


def _wattn_body(sink_ref, bias_ref, q_ref, kp_ref, kc_ref, kn_ref, vp_ref, vc_ref, vn_ref, kx_ref, vx_ref, z_ref,
                o_ref):
    hd = ATT_HEADDIM
    low_half = lax.broadcasted_iota(jnp.int32, (WINDOW, 2 * hd), 1) < hd
    q = q_ref[0]
    bias = bias_ref[0]
    z = z_ref[0].astype(F32)
    k_all, v_ext = [], []
    for g in range(ATT_KV_HEADS):
        ks = slice(g * hd, (g + 1) * hd)
        k_all.append(jnp.concatenate([r[0, :, ks] for r in (kp_ref, kc_ref, kn_ref, kx_ref)], 0).astype(BF16))
        v_all = jnp.concatenate([r[0, :, ks] for r in (vp_ref, vc_ref, vn_ref, vx_ref)], 0).astype(BF16)
        ones = jnp.ones_like(v_all)
        v_ext.append((jnp.concatenate([v_all, ones], 1), jnp.concatenate([ones, v_all], 1)))
    heads = range(ATT_HEADS)
    sinks = [sink_ref[h] * LOG2E for h in heads]
    scores = [_dot_t(q[:, h * hd:(h + 1) * hd].astype(BF16), k_all[h // ATT_GROUP]) + bias for h in heads]
    maxes = [jnp.maximum(jnp.max(s, -1, keepdims=True), sk) for s, sk in zip(scores, sinks)]
    probs = [jnp.exp2(s - m).astype(BF16) for s, m in zip(scores, maxes)]
    exts = [_dot(p, v_ext[h // ATT_GROUP][h % 2]) for h, p in zip(heads, probs)]
    outs = [e / (pltpu.roll(e, hd, 1) + jnp.exp2(sk - m)) for e, sk, m in zip(exts, sinks, maxes)]
    for pair in range(ATT_HEADS // 2):
        cs = slice(2 * pair * hd, (2 * pair + 2) * hd)
        o_ref[0, :, cs] = (jnp.where(low_half, outs[2 * pair], outs[2 * pair + 1]) * _silu(z[:, cs])).astype(o_ref.dtype)


def _window_attention(q_rot, u_kv, uc_kv, sinks, z_a):
    b, L, _ = q_rot.shape
    lc = uc_kv.shape[1]
    nb = L // WINDOW
    hd2 = ATT_KV
    cur = lambda bi, i: (bi, i, 0)
    prv = lambda bi, i: (bi, jnp.maximum(i - 1, 0), 0)
    nxt = lambda bi, i: (bi, jnp.minimum(i + 1, nb - 1), 0)
    vcur = lambda bi, i: (bi, i, 1)
    vprv = lambda bi, i: (bi, jnp.maximum(i - 1, 0), 1)
    vnxt = lambda bi, i: (bi, jnp.minimum(i + 1, nb - 1), 1)
    kblk = lambda f: pl.BlockSpec((1, WINDOW, hd2), f)
    nk = 3 * WINDOW + lc
    row = np.arange(WINDOW)[:, None]
    col = np.arange(nk)[None, :]
    in_prev = (col < WINDOW) & (col >= row)
    in_next = (col >= 2 * WINDOW) & (col < 3 * WINDOW) & (col - 2 * WINDOW <= row)
    always = ((col >= WINDOW) & (col < 2 * WINDOW)) | (col >= 3 * WINDOW)
    kinds = [always | in_next, always | in_prev | in_next, always | in_prev]
    if nb == 1:
        kinds = [always] * 3
    bias = jnp.asarray(np.where(np.stack(kinds), 0.0, NEG).astype(np.float32))
    kind = lambda bi, i: (jnp.where(i == 0, 0, jnp.where(i == nb - 1, 2, 1)), 0, 0)
    return pl.pallas_call(
        _wattn_body,
        grid=(b, nb),
        in_specs=[pl.BlockSpec(memory_space=pltpu.SMEM),
                  pl.BlockSpec((1, WINDOW, nk), kind),
                  pl.BlockSpec((1, WINDOW, ATT_WIDTH), cur),
                  kblk(prv), kblk(cur), kblk(nxt), kblk(vprv), kblk(vcur), kblk(vnxt),
                  pl.BlockSpec((1, lc, hd2), lambda bi, i: (bi, 0, 0)),
                  pl.BlockSpec((1, lc, hd2), lambda bi, i: (bi, 0, 1)),
                  pl.BlockSpec((1, WINDOW, ATT_WIDTH), cur)],
        out_specs=pl.BlockSpec((1, WINDOW, ATT_WIDTH), cur),
        out_shape=jax.ShapeDtypeStruct((b, L, ATT_WIDTH), BF16),
        compiler_params=_cp("parallel", "arbitrary"),
        name="window_attention",
    )(sinks, bias, q_rot, u_kv, u_kv, u_kv, u_kv, u_kv, u_kv, uc_kv, uc_kv, z_a)


def _cattn_body(sink_ref, q_ref, k_ref, v_ref, z_ref, o_ref):
    hd = ATT_HEADDIM
    low_half = lax.broadcasted_iota(jnp.int32, (q_ref.shape[1], 2 * hd), 1) < hd
    q = (q_ref[0].astype(F32) * (hd ** -0.5 * LOG2E)).astype(BF16)
    z = z_ref[0].astype(F32)
    k_all, v_ext = [], []
    for g in range(ATT_KV_HEADS):
        ks = slice(g * hd, (g + 1) * hd)
        k_all.append(k_ref[0, :, ks])
        v = v_ref[0, :, ks]
        ones = jnp.ones_like(v)
        v_ext.append((jnp.concatenate([v, ones], 1), jnp.concatenate([ones, v], 1)))
    heads = range(ATT_HEADS)
    sinks = [sink_ref[h] * LOG2E for h in heads]
    scores = [_dot_t(q[:, h * hd:(h + 1) * hd], k_all[h // ATT_GROUP]) for h in heads]
    maxes = [jnp.maximum(jnp.max(s, -1, keepdims=True), sk) for s, sk in zip(scores, sinks)]
    probs = [jnp.exp2(s - m).astype(BF16) for s, m in zip(scores, maxes)]
    exts = [_dot(p, v_ext[h // ATT_GROUP][h % 2]) for h, p in zip(heads, probs)]
    outs = [e / (pltpu.roll(e, hd, 1) + jnp.exp2(sk - m)) for e, sk, m in zip(exts, sinks, maxes)]
    for pair in range(ATT_HEADS // 2):
        cs = slice(2 * pair * hd, (2 * pair + 2) * hd)
        o_ref[0, :, cs] = (jnp.where(low_half, outs[2 * pair], outs[2 * pair + 1]) * _silu(z[:, cs])).astype(o_ref.dtype)


def _ctx_attention(uc_q, uc_kv, sinks, z_ac):
    b, lc, _ = uc_q.shape
    blk = lambda bi: (bi, 0, 0)
    return pl.pallas_call(
        _cattn_body,
        grid=(b,),
        in_specs=[pl.BlockSpec(memory_space=pltpu.SMEM),
                  pl.BlockSpec((1, lc, ATT_WIDTH), blk),
                  pl.BlockSpec((1, lc, ATT_KV), lambda bi: (bi, 0, 0)),
                  pl.BlockSpec((1, lc, ATT_KV), lambda bi: (bi, 0, 1)),
                  pl.BlockSpec((1, lc, ATT_WIDTH), blk)],
        out_specs=pl.BlockSpec((1, lc, ATT_WIDTH), blk),
        out_shape=jax.ShapeDtypeStruct((b, lc, ATT_WIDTH), BF16),
        compiler_params=_cp("parallel"),
        name="ctx_attention",
    )(sinks, uc_q, uc_kv, uc_kv, z_ac)


def _out_body(h_ref, g_ref, yhy_ref, yd_ref, yf_ref, yb_ref, xs_ref, zs_ref, yat_ref, dsk_ref, nw_ref, w_ref,
              lg_ref, lb_ref, o_ref):
    gw = SSM_WIDTH // SSM_GROUPS
    y_scan = yd_ref[0].astype(F32) + yf_ref[0].astype(F32) + yb_ref[0].astype(F32)
    ys = (y_scan + xs_ref[0].astype(F32) * dsk_ref[...]) * _silu(zs_ref[0].astype(F32))
    parts = [yhy_ref[0].astype(BF16)]
    for g in range(SSM_GROUPS):
        seg = ys[:, g * gw:(g + 1) * gw]
        seg = seg * lax.rsqrt(jnp.mean(seg * seg, -1, keepdims=True) + RMS_EPS) * nw_ref[:, g * gw:(g + 1) * gw]
        parts.append(seg.astype(BF16))
    parts.append(yat_ref[0].astype(BF16))
    acc = _dot(jnp.concatenate(parts, 1), w_ref[...])
    r = DEEPNORM_ALPHA * h_ref[0] + g_ref[0] * acc
    mu = jnp.mean(r, -1, keepdims=True)
    rc = r - mu
    var = jnp.mean(rc * rc, -1, keepdims=True)
    o_ref[0] = rc * lax.rsqrt(var + LN_EPS) * lg_ref[...] + lb_ref[...]


def _out_projection(h, gate_mod, y_hy, y_ssd, xbc, z_s, y_at, d_skip, norm_w, w_out, ln_g, ln_b):
    b, L, d = h.shape
    tm = min(L, PROJ_ROWS)
    row = lambda bi, i: (bi, i, 0)
    vec = lambda bi, i: (bi, 0, 0)
    full = lambda bi, i: (0, 0)
    w512 = pl.BlockSpec((1, tm, SSM_WIDTH), row)
    return pl.pallas_call(
        _out_body,
        grid=(b, L // tm),
        in_specs=[pl.BlockSpec((1, tm, d), row), pl.BlockSpec((1, 1, d), vec),
                  w512, w512, w512, w512, w512, w512, w512,
                  pl.BlockSpec((1, SSM_WIDTH), full), pl.BlockSpec((1, SSM_WIDTH), full),
                  pl.BlockSpec(w_out.shape, full), pl.BlockSpec((1, d), full), pl.BlockSpec((1, d), full)],
        out_specs=pl.BlockSpec((1, tm, d), row),
        out_shape=jax.ShapeDtypeStruct((b, L, d), F32),
        compiler_params=_cp("parallel", "arbitrary"),
        name="out_projection",
    )(h, gate_mod, y_hy, *y_ssd, xbc, z_s, y_at, d_skip, norm_w, w_out.astype(BF16),
      ln_g.reshape(1, d), ln_b.reshape(1, d))


def _sequence_front(h, shift, scale, w_packed, ssm_conv_w, ssm_conv_b, rope_tables=None):
    u_hy3, u_hyg, u_xbc, u_zs, u_q, u_kv, u_za, u_dt = _in_projection(h, shift, scale, w_packed, rope_tables)
    xbc = _dwconv(u_xbc, ssm_conv_w, ssm_conv_b, act=True, split=SSM_CONV_DIM)[0]
    return dict(hy3=u_hy3, hy_gate=u_hyg, xbc=xbc, z_s=u_zs, q=u_q, kv=u_kv, z_a=u_za, dt=u_dt)


def kernel(x, c, ctx, c_ctx, w_mod, b_mod, w_in, hy_conv_w, hy_conv_b, hy_f_w1, hy_f_b1, hy_f_w2, hy_f_b2,
           hy_f_w3, hy_f_b3, hy_f_freq, hy_f_wout, hy_bias, ssm_conv_w, ssm_conv_b, ssm_dt_bias, ssm_a_log,
           ssm_d, ssm_norm_w, attn_sinks, w_out, ln_g, ln_b):
    b, L, d = x.shape
    lc = ctx.shape[1]
    cos, sin = _rope_tables(L)
    cc = jnp.concatenate([c, c_ctx[None], jnp.zeros((16 - b - 1, d), F32)], 0)
    zero_state = jnp.zeros((b, 2, SSM_GROUPS, SSM_STATE, SSM_WIDTH // SSM_GROUPS), F32)
    h_lat, h_ctx = x, ctx
    for i in range(DEPTH):
        ctx_needed = i < DEPTH - 1
        mod = _modulation(cc, w_mod[i], b_mod[i])
        sh, sc, g = (mod[:b, None, j * d:(j + 1) * d] for j in range(3))
        sh_c, sc_c, g_c = (jnp.broadcast_to(mod[b:b + 1, None, j * d:(j + 1) * d], (b, 1, d)) for j in range(3))
        w_packed = _pack_w_in(w_in[i])
        lat = _sequence_front(h_lat, sh, sc, w_packed, ssm_conv_w[i], ssm_conv_b[i], (cos, sin))
        cx = _sequence_front(h_ctx, sh_c, sc_c, w_packed, ssm_conv_w[i], ssm_conv_b[i])

        dt_bias_row = jnp.pad(ssm_dt_bias[i].reshape(1, -1), ((0, 0), (0, LANES - 2 * SSM_HEADS)))
        a_row = jnp.pad(-jnp.exp(ssm_a_log[i]).reshape(1, -1), ((0, 0), (0, LANES - 2 * SSM_HEADS)))
        ys_c, s_c = _ssd(cx["xbc"], cx["dt"], dt_bias_row, a_row, zero_state)
        ys, _ = _ssd(lat["xbc"], lat["dt"], dt_bias_row, a_row, s_c)

        filt = (hy_f_w1[i], hy_f_b1[i], hy_f_w2[i], hy_f_b2[i], hy_f_w3[i], hy_f_b3[i], hy_f_freq[i], hy_f_wout[i])
        kf = _hyena_spectrum(L, _hyena_filter_taps(L, *filt))
        y_hy = _hyena(lat["hy3"], lat["hy_gate"], hy_conv_w[i], hy_conv_b[i], kf, hy_bias[i])

        y_at = _window_attention(lat["q"], lat["kv"], cx["kv"], attn_sinks[i], lat["z_a"])

        d_skip = jnp.repeat(ssm_d[i], SSM_HEADDIM).reshape(1, SSM_WIDTH)
        norm_w = ssm_norm_w[i].reshape(1, SSM_WIDTH)
        new_lat = _out_projection(h_lat, g, y_hy, ys, lat["xbc"], lat["z_s"], y_at, d_skip, norm_w,
                                  w_out[i], ln_g[i], ln_b[i])
        if ctx_needed:
            kf_c = _hyena_spectrum(lc, _hyena_filter_taps(lc, *filt))
            y_hy_c = _hyena(cx["hy3"], cx["hy_gate"], hy_conv_w[i], hy_conv_b[i], kf_c, hy_bias[i])
            y_at_c = _ctx_attention(cx["q"], cx["kv"], attn_sinks[i], cx["z_a"])
            h_ctx = _out_projection(h_ctx, g_c, y_hy_c, ys_c, cx["xbc"], cx["z_s"], y_at_c, d_skip,
                                    norm_w, w_out[i], ln_g[i], ln_b[i])
        h_lat = new_lat
    return h_lat
```
